```python
import jax, jax.numpy as jnp
from jax import lax
import numpy as np

D_MODEL = 1024
BATCH = 8
SEQ = 2048
DEPTH = 1
DEC_BATCH = 128
DEC_SEQ = 8
PAST_LEN = 16384
PAGE_SIZE = 128

MIX_WIDTH = D_MODEL
C_CONV = MIX_WIDTH // 2
N_CONV_GROUPS = 8
CONV_W = 31
C_GMLP = MIX_WIDTH - C_CONV
N_GMLP_HEADS = 8
GMLP_HEAD_DIM = C_GMLP // N_GMLP_HEADS
CHUNK = 128
N_EXPERTS = 32
TOP_K = 4
D_FF_EXPERT = D_MODEL
SWIGLU_LIMIT = 7.0
SWIGLU_ALPHA = 1.702
PLE_DIM = 256
MOE_BLOCK = 128
LN_EPS = 1e-5
DEEPNORM_ALPHA = (2.0 * DEPTH) ** 0.25
DEEPNORM_BETA = (8.0 * DEPTH) ** -0.25

kernel_name = "hybrid_conv_gmlp_moe_decoder_step"


def layer_norm(x, g, b):
    xf = x.astype(jnp.float32)
    mu = jnp.mean(xf, axis=-1, keepdims=True)
    var = jnp.mean(jnp.square(xf - mu), axis=-1, keepdims=True)
    y = (xf - mu) * lax.rsqrt(var + LN_EPS)
    return (y * g.astype(jnp.float32) + b.astype(jnp.float32)).astype(x.dtype)


def group_norm_tokenwise(x, g, b, n_groups):
    bsz, length, c = x.shape
    xf = x.astype(jnp.float32).reshape(bsz, length, n_groups, c // n_groups)
    mu = jnp.mean(xf, axis=-1, keepdims=True)
    var = jnp.mean(jnp.square(xf - mu), axis=-1, keepdims=True)
    y = ((xf - mu) * lax.rsqrt(var + LN_EPS)).reshape(bsz, length, c)
    return (y * g.astype(jnp.float32) + b.astype(jnp.float32)).astype(x.dtype)


def conformer_conv(a_val, a_gate, conv_prev, conv_w, gn_g, gn_b):
    u = a_val * jax.nn.sigmoid(a_gate)
    u_ext = jnp.concatenate([conv_prev.astype(u.dtype), u], axis=1)
    y = lax.conv_general_dilated(
        u_ext, conv_w[:, None, :].astype(u.dtype), window_strides=(1,), padding="VALID",
        dimension_numbers=("NWC", "WIO", "NWC"), feature_group_count=C_CONV)
    y = jax.nn.silu(group_norm_tokenwise(y, gn_g, gn_b, N_CONV_GROUPS))
    return y, u_ext[:, -(CONV_W - 1):]


def chunk_spatial_gate(z_u, z_v, vn_g, vn_b, w_s, b_s):
    u = jax.nn.gelu(z_u)
    v = layer_norm(jax.nn.gelu(z_v), vn_g, vn_b)
    bsz, length, c = v.shape
    n_chunks = -(-length // CHUNK)
    pad = n_chunks * CHUNK - length
    vp = jnp.pad(v, ((0, 0), (0, pad), (0, 0))).reshape(bsz, n_chunks, CHUNK, N_GMLP_HEADS, GMLP_HEAD_DIM)
    causal = jnp.tril(jnp.ones((CHUNK, CHUNK), dtype=bool))
    w = jnp.where(causal[None], w_s, 0).astype(v.dtype)
    mixed = jnp.einsum("hts,bnshd->bnthd", w, vp) + b_s.T.astype(v.dtype)[None, None, :, :, None]
    mixed = mixed.reshape(bsz, n_chunks * CHUNK, c)[:, :length]
    last_start = ((length - 1) // CHUNK) * CHUNK
    return u * mixed, v[:, last_start:]


def token_mixers(x, conv_prev, w_in, conv_w, gn_g, gn_b, vn_g, vn_b, w_s, b_s, w_out):
    z = x @ w_in
    a_val, a_gate, g_u, g_v = jnp.split(z, [C_CONV, 2 * C_CONV, 2 * C_CONV + C_GMLP], axis=-1)
    y_a, conv_state = conformer_conv(a_val, a_gate, conv_prev, conv_w, gn_g, gn_b)
    y_b, v_state = chunk_spatial_gate(g_u, g_v, vn_g, vn_b, w_s, b_s)
    y = jnp.concatenate([y_a, y_b], axis=-1) @ w_out
    return y, conv_state, v_state


def moe(x, w_r, b_r, w_up, b_up, w_down, b_down):
    bsz, length, d = x.shape
    n_tok = bsz * length
    xt = x.reshape(n_tok, d)
    logits = xt.astype(jnp.float32) @ w_r.astype(jnp.float32) + b_r.astype(jnp.float32)
    top_vals, top_idx = lax.top_k(logits, TOP_K)
    gates = jax.nn.softmax(top_vals, axis=-1).astype(x.dtype)
    n_assign = n_tok * TOP_K
    flat_e = top_idx.reshape(-1).astype(jnp.int32)
    flat_tok = jnp.repeat(jnp.arange(n_tok, dtype=jnp.int32), TOP_K)
    flat_g = gates.reshape(-1)
    order = jnp.argsort(flat_e)
    se, stok, sg = flat_e[order], flat_tok[order], flat_g[order]
    counts = jnp.bincount(flat_e, length=N_EXPERTS).astype(jnp.int32)
    starts = jnp.cumsum(counts) - counts
    padded = (counts + MOE_BLOCK - 1) // MOE_BLOCK * MOE_BLOCK
    p_ends = jnp.cumsum(padded)
    p_starts = p_ends - padded
    dest = p_starts[se] + (jnp.arange(n_assign, dtype=jnp.int32) - starts[se])
    n_blocks = -(-n_assign // MOE_BLOCK) + N_EXPERTS
    cap = n_blocks * MOE_BLOCK
    buf_tok = jnp.full((cap,), n_tok, jnp.int32).at[dest].set(stok)
    buf_g = jnp.zeros((cap,), x.dtype).at[dest].set(sg)
    block_e = jnp.minimum(
        jnp.searchsorted(p_ends, jnp.arange(n_blocks, dtype=jnp.int32) * MOE_BLOCK, side="right"),
        N_EXPERTS - 1)
    x_pad = jnp.concatenate([xt, jnp.zeros((1, d), x.dtype)], axis=0)

    def expert_block(args):
        tok, e = args
        h = x_pad[tok] @ w_up[e] + b_up[e]
        h_glu = jnp.minimum(h[:, :D_FF_EXPERT], SWIGLU_LIMIT)
        h_lin = jnp.clip(h[:, D_FF_EXPERT:], -SWIGLU_LIMIT, SWIGLU_LIMIT)
        act = h_glu * jax.nn.sigmoid(SWIGLU_ALPHA * h_glu) * (h_lin + 1.0)
        return act @ w_down[e] + b_down[e]

    y_blocks = lax.map(expert_block, (buf_tok.reshape(n_blocks, MOE_BLOCK), block_e))
    y = jnp.zeros((n_tok + 1, d), x.dtype).at[buf_tok].add(buf_g[:, None] * y_blocks.reshape(cap, d))
    return y[:n_tok].reshape(bsz, length, d)


def decoder_layer(x, p, conv_prev, w_in, conv_w, gn_g, gn_b, vn_g, vn_b, w_s, b_s, w_out,
                  ln1_g, ln1_b, w_router, b_router, w_up, b_up, w_down, b_down,
                  w_ple, w_ple_gate, ln2_g, ln2_b):
    mix, conv_state, v_state = token_mixers(x, conv_prev, w_in, conv_w, gn_g, gn_b, vn_g, vn_b, w_s, b_s, w_out)
    h = layer_norm(DEEPNORM_ALPHA * x + mix, ln1_g, ln1_b)
    ple = (p.astype(x.dtype) @ w_ple) * jax.nn.sigmoid(h @ w_ple_gate)
    out = layer_norm(DEEPNORM_ALPHA * h + moe(h, w_router, b_router, w_up, b_up, w_down, b_down) + ple, ln2_g, ln2_b)
    return out, conv_state, v_state


def setup_inputs(seed: int = 0) -> dict:
    key = jax.random.key(seed)
    ks = jax.random.split(key, 28)
    f32 = jnp.float32
    nrm = lambda k, shape, s: jax.random.normal(k, shape, f32) * s
    n_in = 2 * C_CONV + 2 * C_GMLP
    return {
        "x_prompt": nrm(ks[0], (BATCH, SEQ, D_MODEL), 1.0),
        "x_sample": nrm(ks[1], (DEC_BATCH, DEC_SEQ, D_MODEL), 1.0),
        "cache_conv": nrm(ks[2], (DEPTH, DEC_BATCH, CONV_W - 1, C_CONV), 0.5),
        "p_prompt": nrm(ks[3], (DEPTH, BATCH, SEQ, PLE_DIM), 1.0),
        "p_sample": nrm(ks[4], (DEPTH, DEC_BATCH, DEC_SEQ, PLE_DIM), 1.0),
        "ln_in_g": 1.0 + nrm(ks[5], (D_MODEL,), 0.02),
        "ln_in_b": nrm(ks[6], (D_MODEL,), 0.02),
        "w_in": nrm(ks[7], (DEPTH, D_MODEL, n_in), D_MODEL ** -0.5),
        "conv_w": nrm(ks[8], (DEPTH, CONV_W, C_CONV), CONV_W ** -0.5),
        "gn_g": 1.0 + nrm(ks[9], (DEPTH, C_CONV), 0.02),
        "gn_b": nrm(ks[10], (DEPTH, C_CONV), 0.02),
        "vn_g": 1.0 + nrm(ks[11], (DEPTH, C_GMLP), 0.02),
        "vn_b": nrm(ks[12], (DEPTH, C_GMLP), 0.02),
        "w_spatial": nrm(ks[13], (DEPTH, N_GMLP_HEADS, CHUNK, CHUNK), CHUNK ** -0.5),
        "b_spatial": 1.0 + nrm(ks[14], (DEPTH, N_GMLP_HEADS, CHUNK), 0.02),
        "w_out": nrm(ks[15], (DEPTH, C_CONV + C_GMLP, D_MODEL), DEEPNORM_BETA * (C_CONV + C_GMLP) ** -0.5),
        "ln1_g": 1.0 + nrm(ks[16], (DEPTH, D_MODEL), 0.02),
        "ln1_b": nrm(ks[17], (DEPTH, D_MODEL), 0.02),
        "w_router": nrm(ks[18], (DEPTH, D_MODEL, N_EXPERTS), D_MODEL ** -0.5),
        "b_router": nrm(ks[19], (DEPTH, N_EXPERTS), 0.01),
        "w_up": nrm(ks[20], (DEPTH, N_EXPERTS, D_MODEL, 2 * D_FF_EXPERT), D_MODEL ** -0.5),
        "b_up": nrm(ks[21], (DEPTH, N_EXPERTS, 2 * D_FF_EXPERT), 0.02),
        "w_down": nrm(ks[22], (DEPTH, N_EXPERTS, D_FF_EXPERT, D_MODEL), DEEPNORM_BETA * D_FF_EXPERT ** -0.5),
        "b_down": nrm(ks[23], (DEPTH, N_EXPERTS, D_MODEL), 0.02),
        "w_ple": nrm(ks[24], (DEPTH, PLE_DIM, D_MODEL), DEEPNORM_BETA * PLE_DIM ** -0.5),
        "w_ple_gate": nrm(ks[25], (DEPTH, D_MODEL, D_MODEL), D_MODEL ** -0.5),
        "ln2_g": 1.0 + nrm(ks[26], (DEPTH, D_MODEL), 0.02),
        "ln2_b": nrm(ks[27], (DEPTH, D_MODEL), 0.02),
    }


def reference(x_prompt, x_sample, cache_conv, p_prompt, p_sample, ln_in_g, ln_in_b, w_in, conv_w,
              gn_g, gn_b, vn_g, vn_b, w_spatial, b_spatial, w_out, ln1_g, ln1_b, w_router, b_router,
              w_up, b_up, w_down, b_down, w_ple, w_ple_gate, ln2_g, ln2_b):
    xp = layer_norm(x_prompt, ln_in_g, ln_in_b)
    xs = layer_norm(x_sample, ln_in_g, ln_in_b)
    conv_zero = jnp.zeros((x_prompt.shape[0], CONV_W - 1, C_CONV), xp.dtype)
    conv_p_list, conv_s_list, v_p_list, v_s_list = [], [], [], []
    for i in range(DEPTH):
        lp = (w_in[i], conv_w[i], gn_g[i], gn_b[i], vn_g[i], vn_b[i], w_spatial[i], b_spatial[i], w_out[i],
              ln1_g[i], ln1_b[i], w_router[i], b_router[i], w_up[i], b_up[i], w_down[i], b_down[i],
              w_ple[i], w_ple_gate[i], ln2_g[i], ln2_b[i])
        xp, conv_p, v_p = decoder_layer(xp, p_prompt[i], conv_zero, *lp)
        xs, conv_s, v_s = decoder_layer(xs, p_sample[i], cache_conv[i], *lp)
        conv_p_list.append(conv_p)
        conv_s_list.append(conv_s)
        v_p_list.append(v_p)
        v_s_list.append(v_s)
    conv_state_prompt = jnp.stack(conv_p_list, axis=0)
    conv_state_sample = jnp.stack(conv_s_list, axis=0)
    chunk_v_prompt = jnp.stack(v_p_list, axis=0)
    chunk_v_sample = jnp.stack(v_s_list, axis=0)
    return (xp, xs, conv_state_prompt, conv_state_sample, chunk_v_prompt, chunk_v_sample)
```

```python
import functools

import jax
import jax.numpy as jnp
from jax import lax
from jax.experimental import pallas as pl
from jax.experimental.pallas import tpu as pltpu

F32 = jnp.float32
BF16 = jnp.bfloat16

D_MODEL = 1024
C_CONV = 512
C_GMLP = 512
N_CONV_GROUPS = 8
CONV_W = 31
N_HEADS = 8
HEAD_DIM = C_GMLP // N_HEADS
CHUNK = 128
N_EXPERTS = 32
TOP_K = 4
D_FF = 1024
PLE_DIM = 256
SWIGLU_LIMIT = 7.0
SWIGLU_ALPHA = 1.702
LN_EPS = 1e-5
DEPTH = 1
DEEPNORM_ALPHA = (2.0 * DEPTH) ** 0.25

LANES = 128
SUBLANES = 8
MXU_DIM = 256
VMEM_LIMIT = 48 * 1024 * 1024

T_PROMPT = 256
HIST = 32
SEQS_PER_TILE = 32
SEQ_CHUNK = 4
T_RANK = 512
T_ROWS = 256
TM = 256


def _dot(a, b):
    return jnp.dot(a, b, preferred_element_type=F32)


def _layer_norm(x, g, b):
    mu = jnp.mean(x, axis=-1, keepdims=True)
    xc = x - mu
    var = jnp.mean(xc * xc, axis=-1, keepdims=True)
    return xc * lax.rsqrt(var + LN_EPS) * g + b


def _split_bf16(a):
    hi = a.astype(BF16)
    lo = (a - hi.astype(F32)).astype(BF16)
    return hi, lo


def _group_mean(a, gmat_ref):
    hi, lo = _split_bf16(a)
    g = gmat_ref[...]
    outs = []
    for s in range(C_CONV // MXU_DIM):
        sl = slice(MXU_DIM * s, MXU_DIM * (s + 1))
        outs.append(_dot(hi[:, sl], g) + _dot(lo[:, sl], g))
    return jnp.concatenate(outs, axis=1)


def _group_norm_silu(y, gmat_ref, gn_g, gn_b):
    mu = _group_mean(y, gmat_ref)
    yc = y - mu
    var = _group_mean(yc * yc, gmat_ref)
    yn = yc * lax.rsqrt(var + LN_EPS) * gn_g + gn_b
    return yn * jax.nn.sigmoid(yn)


def _front(x, w):
    xn = _layer_norm(x, w["ln_in_g"][...], w["ln_in_b"][...])
    z = _dot(xn.astype(BF16), w["w_in"][...])
    a_val = z[:, 0:C_CONV]
    a_gate = z[:, C_CONV:2 * C_CONV]
    g_u = z[:, 2 * C_CONV:2 * C_CONV + C_GMLP]
    g_v = z[:, 2 * C_CONV + C_GMLP:]
    u = a_val * jax.nn.sigmoid(a_gate)
    ug = jax.nn.gelu(g_u)
    v = _layer_norm(jax.nn.gelu(g_v), w["vn_g"][...], w["vn_b"][...])
    return xn, u, ug, v


def _tail(xn, y_a, y_b, p, w, h_ref, base_ref, idx_ref, gate_ref):
    mix = _dot(y_a.astype(BF16), w["w_out"][0:C_CONV, :]) + _dot(y_b.astype(BF16), w["w_out"][C_CONV:, :])
    h = _layer_norm(DEEPNORM_ALPHA * xn + mix, w["ln1_g"][...], w["ln1_b"][...])
    hb, h_lo = _split_bf16(h)
    ple = _dot(p.astype(BF16), w["w_ple"][...]) * jax.nn.sigmoid(_dot(hb, w["w_gate"][...]))
    h_ref[...] = h
    base_ref[...] = DEEPNORM_ALPHA * h + ple

    wr_hi = w["wr_hi"][...]
    logits = _dot(hb, wr_hi) + _dot(h_lo, wr_hi) + _dot(hb, w["wr_lo"][...]) + w["b_r"][...]
    lane = lax.broadcasted_iota(jnp.int32, logits.shape, 1)
    lane_f = lane.astype(F32)
    vals = jnp.where(lane < N_EXPERTS, logits, -jnp.inf)
    tops, ids = [], []
    for _ in range(TOP_K):
        m = jnp.max(vals, axis=-1, keepdims=True)
        i = jnp.min(jnp.where(vals == m, lane_f, float(LANES)), axis=-1, keepdims=True)
        vals = jnp.where(lane_f == i, -jnp.inf, vals)
        tops.append(m)
        ids.append(i)
    exps = [jnp.exp(m - tops[0]) for m in tops]
    denom = exps[0] + exps[1] + exps[2] + exps[3]
    idx_out = jnp.zeros(logits.shape, F32)
    gate_out = jnp.zeros(logits.shape, F32)
    for k in range(TOP_K):
        idx_out = jnp.where(lane == k, ids[k], idx_out)
        gate_out = jnp.where(lane == k, exps[k] / denom, gate_out)
    idx_ref[...] = idx_out.astype(jnp.int32)
    gate_ref[...] = gate_out


_WEIGHT_NAMES = ("ln_in_g", "ln_in_b", "w_in", "vn_g", "vn_b", "gmat", "gn_g", "gn_b", "w_out", "ln1_g", "ln1_b",
                 "w_gate", "w_ple", "wr_hi", "wr_lo", "b_r")


def _prompt_branch(j, x_ref, p_ref, w, cw_ref, ws_ref, bs_ref, outs, tail_ref, vch_ref, ubuf, yscr, last_j):
    t = T_PROMPT
    xn, u, ug, v = _front(x_ref[...], w)

    @pl.when(j == 0)
    def _():
        ubuf[0:HIST, :] = jnp.zeros((HIST, C_CONV), F32)

    @pl.when(j > 0)
    def _():
        ubuf[0:HIST, :] = ubuf[t:t + HIST, :]

    ubuf[HIST:HIST + t, :] = u
    rows = 32
    first = HIST - (CONV_W - 1)
    for c in range(t // rows):
        acc = jnp.zeros((rows, C_CONV), F32)
        for k in range(CONV_W):
            acc = acc + cw_ref[pl.ds(k, 1), :] * ubuf[pl.ds(first + k + c * rows, rows), :]
        yscr[c * rows:(c + 1) * rows, :] = acc
    y_a = _group_norm_silu(yscr[...], w["gmat"], w["gn_g"][...], w["gn_b"][...])

    lane = lax.broadcasted_iota(jnp.int32, (CHUNK, LANES), 1)
    mixed_chunks = []
    for c in range(t // CHUNK):
        vc = v[c * CHUNK:(c + 1) * CHUNK, :]
        parts = []
        for q in range(N_HEADS // 2):
            vp = vc[:, q * LANES:(q + 1) * LANES]
            rhs = jnp.concatenate([jnp.where(lane < HEAD_DIM, vp, 0.0), jnp.where(lane >= HEAD_DIM, vp, 0.0)],
                                  axis=0).astype(BF16)
            parts.append(_dot(ws_ref[q], rhs))
        mixed_chunks.append(jnp.concatenate(parts, axis=1) + bs_ref[...])
    y_b = ug * jnp.concatenate(mixed_chunks, axis=0)

    _tail(xn, y_a, y_b, p_ref[...], w, *outs)

    @pl.when(j == last_j)
    def _():
        tail_ref[...] = u[t - HIST:, :]
        vch_ref[...] = v[t - CHUNK:, :]


def _sample_branch(x_ref, p_ref, cpad_ref, w, cw_ref, convu_ref, gw_ref, b8_ref, outs, u_out_ref, v_out_ref,
                   uscr, vscr, yscr, mscr):
    xn, u, ug, v = _front(x_ref[...], w)
    u_out_ref[...] = u
    v_out_ref[...] = v
    uscr[...] = u
    vscr[...] = v
    first = HIST - (CONV_W - 1)
    rows = SEQ_CHUNK * SUBLANES

    def step(c, carry):
        r0 = pl.multiple_of(c * rows, rows)
        s0 = c * SEQ_CHUNK
        u3 = uscr[pl.ds(r0, rows), :].reshape(SEQ_CHUNK, SUBLANES, C_CONV)
        v3 = vscr[pl.ds(r0, rows), :].reshape(SEQ_CHUNK, SUBLANES, C_GMLP)
        acc = jnp.zeros((SEQ_CHUNK, SUBLANES, C_CONV), F32)
        for k in range(CONV_W):
            acc = acc + cw_ref[pl.ds(k, 1), :][None] * cpad_ref[pl.ds(s0, SEQ_CHUNK), pl.ds(first + k, SUBLANES), :]
        mix = jnp.zeros((SEQ_CHUNK, SUBLANES, C_GMLP), F32) + b8_ref[...][None]
        for s in range(SUBLANES):
            acc = acc + u3[:, s:s + 1, :] * convu_ref[s][None]
            mix = mix + v3[:, s:s + 1, :] * gw_ref[s][None]
        yscr[pl.ds(r0, rows), :] = acc.reshape(rows, C_CONV)
        mscr[pl.ds(r0, rows), :] = mix.reshape(rows, C_GMLP)
        return carry

    lax.fori_loop(0, SEQS_PER_TILE // SEQ_CHUNK, step, 0)
    y_a = _group_norm_silu(yscr[...], w["gmat"], w["gn_g"][...], w["gn_b"][...])
    y_b = ug * mscr[...]
    _tail(xn, y_a, y_b, p_ref[...], w, *outs)


def _mixer_body(*refs, n_prompt_tiles, tiles_per_seq):
    n_w = len(_WEIGHT_NAMES)
    xp_ref, pp_ref, xs_ref, ps_ref, cpad_ref = refs[:5]
    w = dict(zip(_WEIGHT_NAMES, refs[5:5 + n_w]))
    cw_ref, ws_ref, bs_ref, convu_ref, gw_ref, b8_ref = refs[5 + n_w:11 + n_w]
    outs = refs[11 + n_w:15 + n_w]
    tail_ref, vch_ref, u_out_ref, v_out_ref = refs[15 + n_w:19 + n_w]
    ubuf, yscr, uscr, vscr, mscr = refs[19 + n_w:]
    step = pl.program_id(0)

    @pl.when(step < n_prompt_tiles)
    def _():
        _prompt_branch(lax.rem(step, tiles_per_seq), xp_ref, pp_ref, w, cw_ref, ws_ref, bs_ref, outs, tail_ref,
                       vch_ref, ubuf, yscr, tiles_per_seq - 1)

    @pl.when(step >= n_prompt_tiles)
    def _():
        _sample_branch(xs_ref, ps_ref, cpad_ref, w, cw_ref, convu_ref, gw_ref, b8_ref, outs, u_out_ref, v_out_ref,
                       uscr, vscr, yscr, mscr)


def _rank_body(idx_ref, tri_ref, rank_ref, counts_ref, carry):
    i = pl.program_id(0)

    @pl.when(i == 0)
    def _():
        carry[...] = jnp.zeros(carry.shape, F32)

    idx = idx_ref[...]
    lane = lax.broadcasted_iota(jnp.int32, idx.shape, 1)
    hots = [lane == idx[:, k:k + 1] for k in range(TOP_K)]
    multi = jnp.zeros(idx.shape, F32)
    for hot in hots:
        multi = multi + hot.astype(F32)
    before = _dot(tri_ref[...], multi.astype(BF16)) + carry[...]
    rank = jnp.zeros(idx.shape, F32)
    for k, hot in enumerate(hots):
        rk = jnp.sum(jnp.where(hot, before, 0.0), axis=-1, keepdims=True)
        rank = jnp.where(lane == k, rk, rank)
    rank_ref[...] = rank.astype(jnp.int32)
    carry[...] = carry[...] + jnp.sum(multi, axis=0, keepdims=True)
    counts_ref[...] = carry[...].astype(jnp.int32)


def _pad_bits():
    b = TM // 2
    while b >= SUBLANES:
        yield b
        b //= 2


def _dispatch_body(pad_off_ref, pad_n_ref, n_used_ref, dest_ref, h_ref, xs_ref, zbuf, sem, zsem, *, n_blocks):
    def row_copy(r, d):
        return pltpu.make_async_copy(h_ref.at[pl.ds(r, 1), :], xs_ref.at[pl.ds(d, 1), :], sem)

    def issue(r, carry):
        for k in range(TOP_K):
            row_copy(r, dest_ref[0, r * TOP_K + k]).start()
        return carry

    lax.fori_loop(0, T_ROWS, issue, 0, unroll=8)

    @pl.when(pl.program_id(0) == 0)
    def _():
        zbuf[...] = jnp.zeros(zbuf.shape, F32)

        def zero_copy(off, b):
            return pltpu.make_async_copy(zbuf.at[pl.ds(0, b), :], xs_ref.at[pl.ds(off, b), :], zsem)

        def start_or_wait(cond, cp, wait):
            @pl.when(cond)
            def _():
                if wait:
                    cp.wait()
                else:
                    cp.start()

        for wait in (False, True):
            for e in range(N_EXPERTS):
                n = pad_n_ref[e]
                start = pad_off_ref[e]
                head = jnp.minimum((-start) & (SUBLANES - 1), n)
                for r in range(SUBLANES - 1):
                    start_or_wait(r < head, zero_copy(start + r, 1), wait)
                body = n - head
                for b in _pad_bits():
                    off = pl.multiple_of(start + head + (body & ~(2 * b - 1)), SUBLANES)
                    start_or_wait((body & b) != 0, zero_copy(off, b), wait)

        half = TM // 2

        def tail_block(wait):
            def go(blk, carry):
                for s in range(TM // half):
                    cp = zero_copy(pl.multiple_of(blk * TM + s * half, half), half)
                    cp.wait() if wait else cp.start()
                return carry
            return go

        lax.fori_loop(n_used_ref[0], n_blocks, tail_block(False), 0)
        lax.fori_loop(n_used_ref[0], n_blocks, tail_block(True), 0)

    for k in range(TOP_K):
        pltpu.make_async_copy(h_ref, xs_ref.at[pl.ds(0, T_ROWS), :], sem).wait()


def _expert_body(blk_e_ref, n_used_ref, x_ref, wu_ref, bu_ref, wd_ref, bd_ref, y_ref):
    @pl.when(pl.program_id(0) < n_used_ref[0])
    def _():
        hcat = _dot(x_ref[...].astype(BF16), wu_ref[...]) + bu_ref[...]
        h_glu = jnp.minimum(hcat[:, :D_FF], SWIGLU_LIMIT)
        h_lin = jnp.clip(hcat[:, D_FF:], -SWIGLU_LIMIT, SWIGLU_LIMIT)
        act = h_glu * jax.nn.sigmoid(SWIGLU_ALPHA * h_glu) * (h_lin + 1.0)
        y_ref[...] = _dot(act.astype(BF16), wd_ref[...]) + bd_ref[...]

    @pl.when(pl.program_id(0) >= n_used_ref[0])
    def _():
        y_ref[...] = jnp.zeros(y_ref.shape, F32)


def _combine_body(dest_ref, y_ref, base_ref, gate_ref, g_ref, b_ref, outp_ref, outs_ref, ybuf, sem, *, n_prompt_tiles):
    def row_copy(r, k, d):
        return pltpu.make_async_copy(y_ref.at[pl.ds(d, 1), :], ybuf.at[k, pl.ds(r, 1), :], sem)

    def issue(r, carry):
        for k in range(TOP_K):
            row_copy(r, k, dest_ref[0, r * TOP_K + k]).start()
        return carry

    lax.fori_loop(0, T_ROWS, issue, 0, unroll=8)
    for k in range(TOP_K):
        pltpu.make_async_copy(y_ref.at[pl.ds(0, T_ROWS), :], ybuf.at[k], sem).wait()

    gate = gate_ref[...]
    acc = base_ref[...]
    for k in range(TOP_K):
        acc = acc + gate[:, k:k + 1] * ybuf[k]
    out = _layer_norm(acc, g_ref[...], b_ref[...])
    i = pl.program_id(0)

    @pl.when(i < n_prompt_tiles)
    def _():
        outp_ref[...] = out

    @pl.when(i >= n_prompt_tiles)
    def _():
        outs_ref[...] = out


def _full(shape):
    return pl.BlockSpec(shape, lambda *_: (0,) * len(shape))


def kernel(x_prompt, x_sample, cache_conv, p_prompt, p_sample, ln_in_g, ln_in_b, w_in, conv_w, gn_g, gn_b, vn_g, vn_b,
           w_spatial, b_spatial, w_out, ln1_g, ln1_b, w_router, b_router, w_up, b_up, w_down, b_down, w_ple,
           w_ple_gate, ln2_g, ln2_b):
    batch, seq, _ = x_prompt.shape
    dec_batch, dec_seq, _ = x_sample.shape
    assert w_in.shape[0] == DEPTH and dec_seq == SUBLANES and seq % T_PROMPT == 0
    n_prompt = batch * seq
    n_sample = dec_batch * dec_seq
    n_tok = n_prompt + n_sample
    t_s = SEQS_PER_TILE * dec_seq
    assert n_prompt % T_ROWS == 0 and n_sample % T_ROWS == 0 and n_tok % T_RANK == 0 and n_sample % t_s == 0

    row = lambda a: a.reshape(1, -1).astype(F32)
    gidx = jnp.arange(MXU_DIM) // (C_CONV // N_CONV_GROUPS)
    gmat = jnp.where(gidx[:, None] == gidx[None, :], 1.0 / (C_CONV // N_CONV_GROUPS), 0.0).astype(BF16)
    wr_pad = jnp.pad(w_router[0].astype(F32), ((0, 0), (0, LANES - N_EXPERTS)))
    wr_hi = wr_pad.astype(BF16)
    wr_lo = (wr_pad - wr_hi.astype(F32)).astype(BF16)
    weights = dict(
        ln_in_g=row(ln_in_g), ln_in_b=row(ln_in_b), w_in=w_in[0].astype(BF16), vn_g=row(vn_g[0]), vn_b=row(vn_b[0]),
        gmat=gmat, gn_g=row(gn_g[0]), gn_b=row(gn_b[0]), w_out=w_out[0].astype(BF16), ln1_g=row(ln1_g[0]),
        ln1_b=row(ln1_b[0]), w_gate=w_ple_gate[0].astype(BF16), w_ple=w_ple[0].astype(BF16), wr_hi=wr_hi, wr_lo=wr_lo,
        b_r=jnp.pad(row(b_router[0]), ((0, 0), (0, LANES - N_EXPERTS))))
    w_list = [weights[n] for n in _WEIGHT_NAMES]
    w_specs = [_full(a.shape) for a in w_list]
    cw = jnp.pad(conv_w[0].astype(F32), ((0, HIST - CONV_W), (0, 0)))
    causal = jnp.tril(jnp.ones((CHUNK, CHUNK), bool))
    ws_m = jnp.where(causal[None], w_spatial[0], 0.0)
    ws_cat = jnp.concatenate([ws_m[0::2], ws_m[1::2]], axis=2).astype(BF16)
    bs_full = jnp.repeat(b_spatial[0].T.astype(F32), HEAD_DIM, axis=1)
    s_i = jnp.arange(SUBLANES)[:, None]
    t_i = jnp.arange(SUBLANES)[None, :]
    tap = jnp.clip(CONV_W - 1 - t_i + s_i, 0, CONV_W - 1)
    convu = jnp.where((s_i <= t_i)[:, :, None], conv_w[0].astype(F32)[tap], 0.0)
    gw8 = jnp.transpose(ws_m[:, :SUBLANES, :SUBLANES], (2, 1, 0))
    gw8 = jnp.repeat(gw8.astype(F32), HEAD_DIM, axis=2)
    b8 = bs_full[:SUBLANES]
    cpad = jnp.pad(cache_conv[0].astype(F32), ((0, 0), (HIST - (CONV_W - 1), SUBLANES), (0, 0)))

    cparams = lambda sem: pltpu.CompilerParams(dimension_semantics=sem, vmem_limit_bytes=VMEM_LIMIT)

    nj = seq // T_PROMPT
    npt = n_prompt // T_PROMPT
    assert t_s == T_PROMPT
    pstep = lambda i: jnp.minimum(i, npt - 1)
    sstep = lambda i: jnp.maximum(i - npt, 0)
    tok_block = lambda width: pl.BlockSpec((T_PROMPT, width), lambda i: (i, 0))
    tables = [cw, ws_cat, bs_full, convu, gw8, b8]
    h_all, base_all, idx_all, gate_all, u_tail, v_chunk, u_s, v_s = pl.pallas_call(
        functools.partial(_mixer_body, n_prompt_tiles=npt, tiles_per_seq=nj),
        grid=(npt + n_sample // t_s,),
        in_specs=[pl.BlockSpec((None, T_PROMPT, D_MODEL), lambda i: (pstep(i) // nj, pstep(i) % nj, 0)),
                  pl.BlockSpec((None, T_PROMPT, PLE_DIM), lambda i: (pstep(i) // nj, pstep(i) % nj, 0)),
                  pl.BlockSpec((t_s, D_MODEL), lambda i: (sstep(i), 0)),
                  pl.BlockSpec((t_s, PLE_DIM), lambda i: (sstep(i), 0)),
                  pl.BlockSpec((SEQS_PER_TILE,) + cpad.shape[1:], lambda i: (sstep(i), 0, 0))]
                 + w_specs + [_full(a.shape) for a in tables],
        out_specs=[tok_block(D_MODEL), tok_block(D_MODEL), tok_block(LANES), tok_block(LANES),
                   pl.BlockSpec((None, HIST, C_CONV), lambda i: (pstep(i) // nj, 0, 0)),
                   pl.BlockSpec((None, CHUNK, C_GMLP), lambda i: (pstep(i) // nj, 0, 0)),
                   pl.BlockSpec((t_s, C_CONV), lambda i: (sstep(i), 0)),
                   pl.BlockSpec((t_s, C_GMLP), lambda i: (sstep(i), 0))],
        out_shape=(jax.ShapeDtypeStruct((n_tok, D_MODEL), F32), jax.ShapeDtypeStruct((n_tok, D_MODEL), F32),
                   jax.ShapeDtypeStruct((n_tok, LANES), jnp.int32), jax.ShapeDtypeStruct((n_tok, LANES), F32),
                   jax.ShapeDtypeStruct((batch, HIST, C_CONV), F32), jax.ShapeDtypeStruct((batch, CHUNK, C_GMLP), F32),
                   jax.ShapeDtypeStruct((n_sample, C_CONV), F32), jax.ShapeDtypeStruct((n_sample, C_GMLP), F32)),
        scratch_shapes=[pltpu.VMEM((T_PROMPT + HIST, C_CONV), F32)] + [pltpu.VMEM((T_PROMPT, C_CONV), F32)] * 4,
        compiler_params=cparams(("arbitrary",)),
        name="mixer",
    )(x_prompt, p_prompt[0], x_sample.reshape(n_sample, D_MODEL), p_sample[0].reshape(n_sample, PLE_DIM), cpad,
      *w_list, *tables)

    tri = (jnp.arange(T_RANK)[:, None] > jnp.arange(T_RANK)[None, :]).astype(BF16)
    rank_all, counts = pl.pallas_call(
        _rank_body,
        grid=(n_tok // T_RANK,),
        in_specs=[pl.BlockSpec((T_RANK, LANES), lambda i: (i, 0)), _full(tri.shape)],
        out_specs=[pl.BlockSpec((T_RANK, LANES), lambda i: (i, 0)), _full((1, LANES))],
        out_shape=(jax.ShapeDtypeStruct((n_tok, LANES), jnp.int32), jax.ShapeDtypeStruct((1, LANES), jnp.int32)),
        scratch_shapes=[pltpu.VMEM((1, LANES), F32)],
        compiler_params=cparams(("arbitrary",)),
        name="rank",
    )(idx_all, tri)

    n_assign = n_tok * TOP_K
    n_blocks = n_assign // TM + N_EXPERTS
    cap = n_blocks * TM
    cnt = counts[0, :N_EXPERTS]
    padded = (cnt + TM - 1) // TM * TM
    p_end = jnp.cumsum(padded)
    p_start = p_end - padded
    dest = (p_start[idx_all[:, :TOP_K]] + rank_all[:, :TOP_K]).astype(jnp.int32)
    dest_tiles = dest.reshape(n_tok // T_ROWS, 1, T_ROWS * TOP_K)
    blk_e = jnp.minimum(jnp.searchsorted(p_end, jnp.arange(n_blocks, dtype=jnp.int32) * TM, side="right"),
                        N_EXPERTS - 1).astype(jnp.int32)
    n_used = (p_end[-1:] // TM).astype(jnp.int32)
    pad_off = (p_start + cnt).astype(jnp.int32)
    pad_n = (padded - cnt).astype(jnp.int32)

    dest_spec = pl.BlockSpec((None, 1, T_ROWS * TOP_K), lambda i, *_: (i, 0, 0), memory_space=pltpu.SMEM)

    x_sorted = pl.pallas_call(
        functools.partial(_dispatch_body, n_blocks=n_blocks),
        grid_spec=pltpu.PrefetchScalarGridSpec(
            num_scalar_prefetch=3,
            grid=(n_tok // T_ROWS,),
            in_specs=[dest_spec, pl.BlockSpec((T_ROWS, D_MODEL), lambda i, *_: (i, 0))],
            out_specs=pl.BlockSpec(memory_space=pl.ANY),
            scratch_shapes=[pltpu.VMEM((TM // 2, D_MODEL), F32), pltpu.SemaphoreType.DMA, pltpu.SemaphoreType.DMA]),
        out_shape=jax.ShapeDtypeStruct((cap, D_MODEL), F32),
        compiler_params=cparams(("arbitrary",)),
        name="dispatch",
    )(pad_off, pad_n, n_used, dest_tiles, h_all)

    last = lambda i, n_used_ref: jnp.minimum(i, n_used_ref[0] - 1)
    y_sorted = pl.pallas_call(
        _expert_body,
        grid_spec=pltpu.PrefetchScalarGridSpec(
            num_scalar_prefetch=2,
            grid=(n_blocks,),
            in_specs=[pl.BlockSpec((TM, D_MODEL), lambda i, be, nu: (last(i, nu), 0)),
                      pl.BlockSpec((None, D_MODEL, 2 * D_FF), lambda i, be, nu: (be[last(i, nu)], 0, 0)),
                      pl.BlockSpec((None, 1, 2 * D_FF), lambda i, be, nu: (be[last(i, nu)], 0, 0)),
                      pl.BlockSpec((None, D_FF, D_MODEL), lambda i, be, nu: (be[last(i, nu)], 0, 0)),
                      pl.BlockSpec((None, 1, D_MODEL), lambda i, be, nu: (be[last(i, nu)], 0, 0))],
            out_specs=pl.BlockSpec((TM, D_MODEL), lambda i, be, nu: (i, 0))),
        out_shape=jax.ShapeDtypeStruct((cap, D_MODEL), F32),
        compiler_params=cparams(("arbitrary",)),
        name="experts",
    )(blk_e, n_used, x_sorted, w_up[0].astype(BF16), b_up[0].astype(F32)[:, None, :], w_down[0].astype(BF16),
      b_down[0].astype(F32)[:, None, :])

    assert T_ROWS == T_PROMPT
    out_p, out_s = pl.pallas_call(
        functools.partial(_combine_body, n_prompt_tiles=npt),
        grid=(n_tok // T_ROWS,),
        in_specs=[dest_spec, pl.BlockSpec(memory_space=pl.ANY),
                  pl.BlockSpec((T_ROWS, D_MODEL), lambda i: (i, 0)), pl.BlockSpec((T_ROWS, LANES), lambda i: (i, 0)),
                  _full((1, D_MODEL)), _full((1, D_MODEL))],
        out_specs=[pl.BlockSpec((T_ROWS, D_MODEL), lambda i: (jnp.minimum(i, npt - 1), 0)),
                   pl.BlockSpec((T_ROWS, D_MODEL), lambda i: (jnp.maximum(i - npt, 0), 0))],
        out_shape=(jax.ShapeDtypeStruct((n_prompt, D_MODEL), F32), jax.ShapeDtypeStruct((n_sample, D_MODEL), F32)),
        scratch_shapes=[pltpu.VMEM((TOP_K, T_ROWS, D_MODEL), F32), pltpu.SemaphoreType.DMA],
        compiler_params=cparams(("arbitrary",)),
        name="combine",
    )(dest_tiles, y_sorted, base_all, gate_all, row(ln2_g[0]), row(ln2_b[0]))

    y_prompt = out_p.reshape(batch, seq, D_MODEL)
    y_sample = out_s.reshape(dec_batch, dec_seq, D_MODEL)
    conv_state_prompt = u_tail[None, :, HIST - (CONV_W - 1):, :]
    u_s3 = u_s.reshape(dec_batch, dec_seq, C_CONV)
    conv_state_sample = jnp.concatenate([cache_conv[0][:, dec_seq:, :].astype(F32), u_s3], axis=1)[None]
    chunk_v_prompt = v_chunk[None]
    chunk_v_sample = v_s.reshape(1, dec_batch, dec_seq, C_GMLP)
    return (y_prompt, y_sample, conv_state_prompt, conv_state_sample, chunk_v_prompt, chunk_v_sample)
```

```python
import functools

import jax
import jax.numpy as jnp
from jax import lax
from jax.experimental import pallas as pl
from jax.experimental.pallas import tpu as pltpu

F32 = jnp.float32
BF16 = jnp.bfloat16

D_MODEL = 1024
C_CONV = 512
C_GMLP = 512
N_CONV_GROUPS = 8
CONV_W = 31
N_HEADS = 8
HEAD_DIM = C_GMLP // N_HEADS
CHUNK = 128
N_EXPERTS = 32
TOP_K = 4
D_FF = 1024
PLE_DIM = 256
SWIGLU_LIMIT = 7.0
SWIGLU_ALPHA = 1.702
LN_EPS = 1e-5
DEPTH = 1
DEEPNORM_ALPHA = (2.0 * DEPTH) ** 0.25

LANES = 128
SUBLANES = 8
MXU_DIM = 256
VMEM_LIMIT = 48 * 1024 * 1024
EXPERT_VMEM_LIMIT = 56 * 1024 * 1024

T_PROMPT = 256
HIST = 32
SEQS_PER_TILE = 32
SEQ_CHUNK = 4
T_RANK = 512
T_ROWS = 256
TM = 256


def _dot(a, b):
    return jnp.dot(a, b, preferred_element_type=F32)


def _layer_norm(x, g, b):
    mu = jnp.mean(x, axis=-1, keepdims=True)
    xc = x - mu
    var = jnp.mean(xc * xc, axis=-1, keepdims=True)
    return xc * lax.rsqrt(var + LN_EPS) * g + b


def _split_bf16(a):
    hi = a.astype(BF16)
    lo = (a - hi.astype(F32)).astype(BF16)
    return hi, lo


def _group_mean(a, gmat_ref):
    hi, lo = _split_bf16(a)
    g = gmat_ref[...]
    outs = []
    for s in range(C_CONV // MXU_DIM):
        sl = slice(MXU_DIM * s, MXU_DIM * (s + 1))
        outs.append(_dot(hi[:, sl], g) + _dot(lo[:, sl], g))
    return jnp.concatenate(outs, axis=1)


def _group_norm_silu(y, gmat_ref, gn_g, gn_b):
    mu = _group_mean(y, gmat_ref)
    yc = y - mu
    var = _group_mean(yc * yc, gmat_ref)
    yn = yc * lax.rsqrt(var + LN_EPS) * gn_g + gn_b
    return yn * jax.nn.sigmoid(yn)


def _front(x, w):
    xn = _layer_norm(x, w["ln_in_g"][...], w["ln_in_b"][...])
    z = _dot(xn.astype(BF16), w["w_in"][...])
    a_val = z[:, 0:C_CONV]
    a_gate = z[:, C_CONV:2 * C_CONV]
    g_u = z[:, 2 * C_CONV:2 * C_CONV + C_GMLP]
    g_v = z[:, 2 * C_CONV + C_GMLP:]
    u = a_val * jax.nn.sigmoid(a_gate)
    ug = jax.nn.gelu(g_u)
    v = _layer_norm(jax.nn.gelu(g_v), w["vn_g"][...], w["vn_b"][...])
    return xn, u, ug, v


def _tail(xn, y_a, y_b, p, w, h_ref, base_ref, idx_ref, gate_ref):
    mix = _dot(y_a.astype(BF16), w["w_out"][0:C_CONV, :]) + _dot(y_b.astype(BF16), w["w_out"][C_CONV:, :])
    h = _layer_norm(DEEPNORM_ALPHA * xn + mix, w["ln1_g"][...], w["ln1_b"][...])
    hb, h_lo = _split_bf16(h)
    ple = _dot(p.astype(BF16), w["w_ple"][...]) * jax.nn.sigmoid(_dot(hb, w["w_gate"][...]))
    h_ref[...] = h
    base_ref[...] = DEEPNORM_ALPHA * h + ple

    wr_hi = w["wr_hi"][...]
    logits = _dot(hb, wr_hi) + _dot(h_lo, wr_hi) + _dot(hb, w["wr_lo"][...]) + w["b_r"][...]
    lane = lax.broadcasted_iota(jnp.int32, logits.shape, 1)
    lane_f = lane.astype(F32)
    vals = jnp.where(lane < N_EXPERTS, logits, -jnp.inf)
    tops, ids = [], []
    for _ in range(TOP_K):
        m = jnp.max(vals, axis=-1, keepdims=True)
        i = jnp.min(jnp.where(vals == m, lane_f, float(LANES)), axis=-1, keepdims=True)
        vals = jnp.where(lane_f == i, -jnp.inf, vals)
        tops.append(m)
        ids.append(i)
    exps = [jnp.exp(m - tops[0]) for m in tops]
    denom = exps[0] + exps[1] + exps[2] + exps[3]
    idx_out = jnp.zeros(logits.shape, F32)
    gate_out = jnp.zeros(logits.shape, F32)
    for k in range(TOP_K):
        idx_out = jnp.where(lane == k, ids[k], idx_out)
        gate_out = jnp.where(lane == k, exps[k] / denom, gate_out)
    idx_ref[...] = idx_out.astype(jnp.int32)
    gate_ref[...] = gate_out


_WEIGHT_NAMES = ("ln_in_g", "ln_in_b", "w_in", "vn_g", "vn_b", "gmat", "gn_g", "gn_b", "w_out", "ln1_g", "ln1_b",
                 "w_gate", "w_ple", "wr_hi", "wr_lo", "b_r")


def _prompt_branch(j, x_ref, p_ref, w, cw_ref, ws_ref, bs_ref, outs, tail_ref, vch_ref, ubuf, yscr, last_j):
    t = T_PROMPT
    xn, u, ug, v = _front(x_ref[...], w)

    @pl.when(j == 0)
    def _():
        ubuf[0:HIST, :] = jnp.zeros((HIST, C_CONV), F32)

    @pl.when(j > 0)
    def _():
        ubuf[0:HIST, :] = ubuf[t:t + HIST, :]

    ubuf[HIST:HIST + t, :] = u
    rows = 32
    first = HIST - (CONV_W - 1)
    for c in range(t // rows):
        acc = None
        for r in range(SUBLANES):
            taps = [k for k in range(CONV_W) if (first + k) % SUBLANES == r]
            win = ubuf[pl.ds(first + taps[0] + c * rows, rows + SUBLANES * (len(taps) - 1)), :]
            for n, k in enumerate(taps):
                term = cw_ref[pl.ds(k, 1), :] * win[n * SUBLANES:n * SUBLANES + rows, :]
                acc = term if acc is None else acc + term
        yscr[c * rows:(c + 1) * rows, :] = acc
    y_a = _group_norm_silu(yscr[...], w["gmat"], w["gn_g"][...], w["gn_b"][...])

    lane = lax.broadcasted_iota(jnp.int32, (CHUNK, LANES), 1)
    mixed_chunks = []
    for c in range(t // CHUNK):
        vc = v[c * CHUNK:(c + 1) * CHUNK, :]
        parts = []
        for q in range(N_HEADS // 2):
            vp = vc[:, q * LANES:(q + 1) * LANES]
            rhs = jnp.concatenate([jnp.where(lane < HEAD_DIM, vp, 0.0), jnp.where(lane >= HEAD_DIM, vp, 0.0)],
                                  axis=0).astype(BF16)
            parts.append(_dot(ws_ref[q], rhs))
        mixed_chunks.append(jnp.concatenate(parts, axis=1) + bs_ref[...])
    y_b = ug * jnp.concatenate(mixed_chunks, axis=0)

    _tail(xn, y_a, y_b, p_ref[...], w, *outs)

    @pl.when(j == last_j)
    def _():
        tail_ref[...] = u[t - HIST:, :]
        vch_ref[...] = v[t - CHUNK:, :]


def _sample_branch(x_ref, p_ref, cpad_ref, w, cw_ref, convu_ref, gw_ref, b8_ref, outs, u_out_ref, v_out_ref,
                   uscr, vscr, yscr, mscr):
    xn, u, ug, v = _front(x_ref[...], w)
    u_out_ref[...] = u
    v_out_ref[...] = v
    uscr[...] = u
    vscr[...] = v
    first = HIST - (CONV_W - 1)
    rows = SEQ_CHUNK * SUBLANES

    def step(c, carry):
        r0 = pl.multiple_of(c * rows, rows)
        s0 = c * SEQ_CHUNK
        u3 = uscr[pl.ds(r0, rows), :].reshape(SEQ_CHUNK, SUBLANES, C_CONV)
        v3 = vscr[pl.ds(r0, rows), :].reshape(SEQ_CHUNK, SUBLANES, C_GMLP)
        acc = jnp.zeros((SEQ_CHUNK, SUBLANES, C_CONV), F32)
        for k in range(CONV_W):
            acc = acc + cw_ref[pl.ds(k, 1), :][None] * cpad_ref[pl.ds(s0, SEQ_CHUNK), pl.ds(first + k, SUBLANES), :]
        mix = jnp.zeros((SEQ_CHUNK, SUBLANES, C_GMLP), F32) + b8_ref[...][None]
        for s in range(SUBLANES):
            acc = acc + u3[:, s:s + 1, :] * convu_ref[s][None]
            mix = mix + v3[:, s:s + 1, :] * gw_ref[s][None]
        yscr[pl.ds(r0, rows), :] = acc.reshape(rows, C_CONV)
        mscr[pl.ds(r0, rows), :] = mix.reshape(rows, C_GMLP)
        return carry

    lax.fori_loop(0, SEQS_PER_TILE // SEQ_CHUNK, step, 0)
    y_a = _group_norm_silu(yscr[...], w["gmat"], w["gn_g"][...], w["gn_b"][...])
    y_b = ug * mscr[...]
    _tail(xn, y_a, y_b, p_ref[...], w, *outs)


def _mixer_body(*refs, n_prompt_tiles, tiles_per_seq):
    n_w = len(_WEIGHT_NAMES)
    xp_ref, pp_ref, xs_ref, ps_ref, cpad_ref = refs[:5]
    w = dict(zip(_WEIGHT_NAMES, refs[5:5 + n_w]))
    cw_ref, ws_ref, bs_ref, convu_ref, gw_ref, b8_ref = refs[5 + n_w:11 + n_w]
    outs = refs[11 + n_w:15 + n_w]
    tail_ref, vch_ref, u_out_ref, v_out_ref = refs[15 + n_w:19 + n_w]
    ubuf, yscr, uscr, vscr, mscr = refs[19 + n_w:]
    step = pl.program_id(0)

    @pl.when(step < n_prompt_tiles)
    def _():
        _prompt_branch(lax.rem(step, tiles_per_seq), xp_ref, pp_ref, w, cw_ref, ws_ref, bs_ref, outs, tail_ref,
                       vch_ref, ubuf, yscr, tiles_per_seq - 1)

    @pl.when(step >= n_prompt_tiles)
    def _():
        _sample_branch(xs_ref, ps_ref, cpad_ref, w, cw_ref, convu_ref, gw_ref, b8_ref, outs, u_out_ref, v_out_ref,
                       uscr, vscr, yscr, mscr)


def _rank_body(idx_ref, tri_ref, rank_ref, counts_ref, carry):
    i = pl.program_id(0)

    @pl.when(i == 0)
    def _():
        carry[...] = jnp.zeros(carry.shape, F32)

    idx = idx_ref[...]
    lane = lax.broadcasted_iota(jnp.int32, idx.shape, 1)
    hots = [lane == idx[:, k:k + 1] for k in range(TOP_K)]
    multi = jnp.zeros(idx.shape, F32)
    for hot in hots:
        multi = multi + hot.astype(F32)
    before = _dot(tri_ref[...], multi.astype(BF16)) + carry[...]
    rank = jnp.zeros(idx.shape, F32)
    for k, hot in enumerate(hots):
        rk = jnp.sum(jnp.where(hot, before, 0.0), axis=-1, keepdims=True)
        rank = jnp.where(lane == k, rk, rank)
    rank_ref[...] = rank.astype(jnp.int32)
    carry[...] = carry[...] + jnp.sum(multi, axis=0, keepdims=True)
    counts_ref[...] = carry[...].astype(jnp.int32)


def _pad_bits():
    b = TM // 2
    while b >= SUBLANES:
        yield b
        b //= 2


def _dispatch_body(pad_off_ref, pad_n_ref, n_used_ref, dest_ref, h_ref, xs_ref, zbuf, sem, zsem, *, n_blocks):
    def issue(g, carry):
        for j in range(SUBLANES):
            for k in range(TOP_K):
                d = dest_ref[0, g * (SUBLANES * TOP_K) + j * TOP_K + k]
                pltpu.make_async_copy(h_ref.at[g, pl.ds(j, 1), :], xs_ref.at[pl.ds(d, 1), :], sem).start(priority=k % 2)
        return carry

    lax.fori_loop(0, T_ROWS // SUBLANES, issue, 0)

    @pl.when(pl.program_id(0) == 0)
    def _():
        zbuf[...] = jnp.zeros(zbuf.shape, F32)

        def zero_copy(off, b):
            return pltpu.make_async_copy(zbuf.at[pl.ds(0, b), :], xs_ref.at[pl.ds(off, b), :], zsem)

        def start_or_wait(cond, cp, wait):
            @pl.when(cond)
            def _():
                if wait:
                    cp.wait()
                else:
                    cp.start()

        for wait in (False, True):
            for e in range(N_EXPERTS):
                n = pad_n_ref[e]
                start = pad_off_ref[e]
                head = jnp.minimum((-start) & (SUBLANES - 1), n)
                for r in range(SUBLANES - 1):
                    start_or_wait(r < head, zero_copy(start + r, 1), wait)
                body = n - head
                for b in _pad_bits():
                    off = pl.multiple_of(start + head + (body & ~(2 * b - 1)), SUBLANES)
                    start_or_wait((body & b) != 0, zero_copy(off, b), wait)

        half = TM // 2

        def tail_block(wait):
            def go(blk, carry):
                for s in range(TM // half):
                    cp = zero_copy(pl.multiple_of(blk * TM + s * half, half), half)
                    cp.wait() if wait else cp.start()
                return carry
            return go

        lax.fori_loop(n_used_ref[0], n_blocks, tail_block(False), 0)
        lax.fori_loop(n_used_ref[0], n_blocks, tail_block(True), 0)

    for _ in range(T_ROWS * TOP_K // zbuf.shape[0]):
        pltpu.make_async_copy(zbuf, xs_ref.at[pl.ds(0, zbuf.shape[0]), :], sem).wait()


def _expert_body(blk_e_ref, first_ref, slot_ref, next_ref, n_used_ref, x_ref, bu_ref, bd_ref, wu_hbm, wd_hbm, y_ref,
                 wu_f32, wd_f32, wu_bf, wd_bf, sems):
    i = pl.program_id(0)

    def weight_copies(e, s):
        return (pltpu.make_async_copy(wu_hbm.at[e], wu_f32.at[s], sems.at[s]),
                pltpu.make_async_copy(wd_hbm.at[e], wd_f32.at[s], sems.at[s]))

    @pl.when(i < n_used_ref[0])
    def _():
        s = slot_ref[i]

        @pl.when(first_ref[i] == 1)
        def _():
            @pl.when(i == 0)
            def _():
                for cp in weight_copies(blk_e_ref[i], s):
                    cp.start()

            for cp in weight_copies(blk_e_ref[i], s):
                cp.wait()
            nxt = next_ref[i]

            @pl.when(nxt >= 0)
            def _():
                for cp in weight_copies(nxt, 1 - s):
                    cp.start()

            chunk = D_MODEL // SUBLANES

            def cast_rows(c, carry):
                r = pl.multiple_of(c * chunk, chunk)
                wu_bf[pl.ds(r, chunk), :] = wu_f32[s, pl.ds(r, chunk), :].astype(BF16)
                wd_bf[pl.ds(r, chunk), :] = wd_f32[s, pl.ds(r, chunk), :].astype(BF16)
                return carry

            lax.fori_loop(0, SUBLANES, cast_rows, 0)

        hcat = _dot(x_ref[...].astype(BF16), wu_bf[...]) + bu_ref[...]
        h_glu = jnp.minimum(hcat[:, :D_FF], SWIGLU_LIMIT)
        h_lin = jnp.clip(hcat[:, D_FF:], -SWIGLU_LIMIT, SWIGLU_LIMIT)
        act = h_glu * jax.nn.sigmoid(SWIGLU_ALPHA * h_glu) * (h_lin + 1.0)
        y_ref[...] = _dot(act.astype(BF16), wd_bf[...]) + bd_ref[...]

    @pl.when(pl.program_id(0) >= n_used_ref[0])
    def _():
        y_ref[...] = jnp.zeros(y_ref.shape, F32)


def _combine_body(dest_ref, dest_next_ref, y_ref, base_ref, gate_ref, g_ref, b_ref, outp_ref, outs_ref, ybuf, sems, *,
                  n_prompt_tiles, n_tiles):
    i = pl.program_id(0)
    groups = T_ROWS // SUBLANES

    def issue_tile(d_ref, slot):
        def issue(g, carry):
            for j in range(SUBLANES):
                for k in range(TOP_K):
                    d = d_ref[0, g * (SUBLANES * TOP_K) + j * TOP_K + k]
                    pltpu.make_async_copy(y_ref.at[d >> 3, pl.ds(d & (SUBLANES - 1), 1), :],
                                          ybuf.at[slot, k, g, pl.ds(j, 1), :], sems.at[slot]).start(priority=k % 2)
            return carry

        lax.fori_loop(0, groups, issue, 0)

    slot = lax.rem(i, 2)

    @pl.when(i == 0)
    def _():
        issue_tile(dest_ref, 0)

    @pl.when(i + 1 < n_tiles)
    def _():
        issue_tile(dest_next_ref, 1 - slot)

    for k in range(TOP_K):
        pltpu.make_async_copy(y_ref.at[pl.ds(0, groups)], ybuf.at[slot, k], sems.at[slot]).wait()

    gate = gate_ref[...]
    acc = base_ref[...]
    for k in range(TOP_K):
        acc = acc + gate[:, k:k + 1] * ybuf[slot, k].reshape(T_ROWS, D_MODEL)
    out = _layer_norm(acc, g_ref[...], b_ref[...])

    @pl.when(i < n_prompt_tiles)
    def _():
        outp_ref[...] = out

    @pl.when(i >= n_prompt_tiles)
    def _():
        outs_ref[...] = out


def _full(shape):
    return pl.BlockSpec(shape, lambda *_: (0,) * len(shape))


def kernel(x_prompt, x_sample, cache_conv, p_prompt, p_sample, ln_in_g, ln_in_b, w_in, conv_w, gn_g, gn_b, vn_g, vn_b,
           w_spatial, b_spatial, w_out, ln1_g, ln1_b, w_router, b_router, w_up, b_up, w_down, b_down, w_ple,
           w_ple_gate, ln2_g, ln2_b):
    batch, seq, _ = x_prompt.shape
    dec_batch, dec_seq, _ = x_sample.shape
    assert w_in.shape[0] == DEPTH and dec_seq == SUBLANES and seq % T_PROMPT == 0
    n_prompt = batch * seq
    n_sample = dec_batch * dec_seq
    n_tok = n_prompt + n_sample
    t_s = SEQS_PER_TILE * dec_seq
    assert n_prompt % T_ROWS == 0 and n_sample % T_ROWS == 0 and n_tok % T_RANK == 0 and n_sample % t_s == 0

    row = lambda a: a.reshape(1, -1).astype(F32)
    gidx = jnp.arange(MXU_DIM) // (C_CONV // N_CONV_GROUPS)
    gmat = jnp.where(gidx[:, None] == gidx[None, :], 1.0 / (C_CONV // N_CONV_GROUPS), 0.0).astype(BF16)
    wr_pad = jnp.pad(w_router[0].astype(F32), ((0, 0), (0, LANES - N_EXPERTS)))
    wr_hi = wr_pad.astype(BF16)
    wr_lo = (wr_pad - wr_hi.astype(F32)).astype(BF16)
    weights = dict(
        ln_in_g=row(ln_in_g), ln_in_b=row(ln_in_b), w_in=w_in[0].astype(BF16), vn_g=row(vn_g[0]), vn_b=row(vn_b[0]),
        gmat=gmat, gn_g=row(gn_g[0]), gn_b=row(gn_b[0]), w_out=w_out[0].astype(BF16), ln1_g=row(ln1_g[0]),
        ln1_b=row(ln1_b[0]), w_gate=w_ple_gate[0].astype(BF16), w_ple=w_ple[0].astype(BF16), wr_hi=wr_hi, wr_lo=wr_lo,
        b_r=jnp.pad(row(b_router[0]), ((0, 0), (0, LANES - N_EXPERTS))))
    w_list = [weights[n] for n in _WEIGHT_NAMES]
    w_specs = [_full(a.shape) for a in w_list]
    cw = jnp.pad(conv_w[0].astype(F32), ((0, HIST - CONV_W), (0, 0)))
    causal = jnp.tril(jnp.ones((CHUNK, CHUNK), bool))
    ws_m = jnp.where(causal[None], w_spatial[0], 0.0)
    ws_cat = jnp.concatenate([ws_m[0::2], ws_m[1::2]], axis=2).astype(BF16)
    bs_full = jnp.repeat(b_spatial[0].T.astype(F32), HEAD_DIM, axis=1)
    s_i = jnp.arange(SUBLANES)[:, None]
    t_i = jnp.arange(SUBLANES)[None, :]
    tap = jnp.clip(CONV_W - 1 - t_i + s_i, 0, CONV_W - 1)
    convu = jnp.where((s_i <= t_i)[:, :, None], conv_w[0].astype(F32)[tap], 0.0)
    gw8 = jnp.transpose(ws_m[:, :SUBLANES, :SUBLANES], (2, 1, 0))
    gw8 = jnp.repeat(gw8.astype(F32), HEAD_DIM, axis=2)
    b8 = bs_full[:SUBLANES]
    cpad = jnp.pad(cache_conv[0].astype(F32), ((0, 0), (HIST - (CONV_W - 1), SUBLANES), (0, 0)))

    cparams = lambda sem: pltpu.CompilerParams(dimension_semantics=sem, vmem_limit_bytes=VMEM_LIMIT)

    nj = seq // T_PROMPT
    npt = n_prompt // T_PROMPT
    assert t_s == T_PROMPT
    pstep = lambda i: jnp.minimum(i, npt - 1)
    sstep = lambda i: jnp.maximum(i - npt, 0)
    tok_block = lambda width: pl.BlockSpec((T_PROMPT, width), lambda i: (i, 0))
    tables = [cw, ws_cat, bs_full, convu, gw8, b8]
    h_all, base_all, idx_all, gate_all, u_tail, v_chunk, u_s, v_s = pl.pallas_call(
        functools.partial(_mixer_body, n_prompt_tiles=npt, tiles_per_seq=nj),
        grid=(npt + n_sample // t_s,),
        in_specs=[pl.BlockSpec((None, T_PROMPT, D_MODEL), lambda i: (pstep(i) // nj, pstep(i) % nj, 0)),
                  pl.BlockSpec((None, T_PROMPT, PLE_DIM), lambda i: (pstep(i) // nj, pstep(i) % nj, 0)),
                  pl.BlockSpec((t_s, D_MODEL), lambda i: (sstep(i), 0)),
                  pl.BlockSpec((t_s, PLE_DIM), lambda i: (sstep(i), 0)),
                  pl.BlockSpec((SEQS_PER_TILE,) + cpad.shape[1:], lambda i: (sstep(i), 0, 0))]
                 + w_specs + [_full(a.shape) for a in tables],
        out_specs=[tok_block(D_MODEL), tok_block(D_MODEL), tok_block(LANES), tok_block(LANES),
                   pl.BlockSpec((None, HIST, C_CONV), lambda i: (pstep(i) // nj, 0, 0)),
                   pl.BlockSpec((None, CHUNK, C_GMLP), lambda i: (pstep(i) // nj, 0, 0)),
                   pl.BlockSpec((t_s, C_CONV), lambda i: (sstep(i), 0)),
                   pl.BlockSpec((t_s, C_GMLP), lambda i: (sstep(i), 0))],
        out_shape=(jax.ShapeDtypeStruct((n_tok, D_MODEL), F32), jax.ShapeDtypeStruct((n_tok, D_MODEL), F32),
                   jax.ShapeDtypeStruct((n_tok, LANES), jnp.int32), jax.ShapeDtypeStruct((n_tok, LANES), F32),
                   jax.ShapeDtypeStruct((batch, HIST, C_CONV), F32), jax.ShapeDtypeStruct((batch, CHUNK, C_GMLP), F32),
                   jax.ShapeDtypeStruct((n_sample, C_CONV), F32), jax.ShapeDtypeStruct((n_sample, C_GMLP), F32)),
        scratch_shapes=[pltpu.VMEM((T_PROMPT + HIST, C_CONV), F32)] + [pltpu.VMEM((T_PROMPT, C_CONV), F32)] * 4,
        compiler_params=cparams(("arbitrary",)),
        name="mixer",
    )(x_prompt, p_prompt[0], x_sample.reshape(n_sample, D_MODEL), p_sample[0].reshape(n_sample, PLE_DIM), cpad,
      *w_list, *tables)

    tri = (jnp.arange(T_RANK)[:, None] > jnp.arange(T_RANK)[None, :]).astype(BF16)
    rank_all, counts = pl.pallas_call(
        _rank_body,
        grid=(n_tok // T_RANK,),
        in_specs=[pl.BlockSpec((T_RANK, LANES), lambda i: (i, 0)), _full(tri.shape)],
        out_specs=[pl.BlockSpec((T_RANK, LANES), lambda i: (i, 0)), _full((1, LANES))],
        out_shape=(jax.ShapeDtypeStruct((n_tok, LANES), jnp.int32), jax.ShapeDtypeStruct((1, LANES), jnp.int32)),
        scratch_shapes=[pltpu.VMEM((1, LANES), F32)],
        compiler_params=cparams(("arbitrary",)),
        name="rank",
    )(idx_all, tri)

    n_assign = n_tok * TOP_K
    n_blocks = n_assign // TM + N_EXPERTS
    cap = n_blocks * TM
    cnt = counts[0, :N_EXPERTS]
    padded = (cnt + TM - 1) // TM * TM
    p_end = jnp.cumsum(padded)
    p_start = p_end - padded
    dest = (p_start[idx_all[:, :TOP_K]] + rank_all[:, :TOP_K]).astype(jnp.int32)
    dest_tiles = dest.reshape(n_tok // T_ROWS, 1, T_ROWS * TOP_K)
    blk_row = jnp.arange(n_blocks, dtype=jnp.int32) * TM
    blk_e = jnp.minimum(jnp.sum(p_end[None, :] <= blk_row[:, None], axis=1), N_EXPERTS - 1).astype(jnp.int32)
    n_used = (p_end[-1:] // TM).astype(jnp.int32)
    pad_off = (p_start + cnt).astype(jnp.int32)
    pad_n = (padded - cnt).astype(jnp.int32)
    e_ids = jnp.arange(N_EXPERTS, dtype=jnp.int32)
    used = cnt > 0
    slot_e = (jnp.cumsum(used.astype(jnp.int32)) - 1) & 1
    later_used = jnp.where(used[None, :] & (e_ids[None, :] > e_ids[:, None]), e_ids[None, :], N_EXPERTS)
    next_e = jnp.min(later_used, axis=1)
    next_e = jnp.where(next_e < N_EXPERTS, next_e, -1).astype(jnp.int32)
    blk_first = ((blk_row == p_start[blk_e]) & (blk_row < p_end[-1])).astype(jnp.int32)
    blk_slot = slot_e[blk_e].astype(jnp.int32)
    blk_next = next_e[blk_e]

    dest_spec = pl.BlockSpec((None, 1, T_ROWS * TOP_K), lambda i, *_: (i, 0, 0), memory_space=pltpu.SMEM)

    x_sorted = pl.pallas_call(
        functools.partial(_dispatch_body, n_blocks=n_blocks),
        grid_spec=pltpu.PrefetchScalarGridSpec(
            num_scalar_prefetch=3,
            grid=(n_tok // T_ROWS,),
            in_specs=[dest_spec, pl.BlockSpec((T_ROWS // SUBLANES, SUBLANES, D_MODEL), lambda i, *_: (i, 0, 0))],
            out_specs=pl.BlockSpec(memory_space=pl.ANY),
            scratch_shapes=[pltpu.VMEM((TM // 2, D_MODEL), F32), pltpu.SemaphoreType.DMA, pltpu.SemaphoreType.DMA]),
        out_shape=jax.ShapeDtypeStruct((cap, D_MODEL), F32),
        compiler_params=cparams(("arbitrary",)),
        name="dispatch",
    )(pad_off, pad_n, n_used, dest_tiles, h_all.reshape(n_tok // SUBLANES, SUBLANES, D_MODEL))

    last = lambda i, nu: jnp.minimum(i, nu[0] - 1)
    y_sorted = pl.pallas_call(
        _expert_body,
        grid_spec=pltpu.PrefetchScalarGridSpec(
            num_scalar_prefetch=5,
            grid=(n_blocks,),
            in_specs=[pl.BlockSpec((TM, D_MODEL), lambda i, be, bf, bs, bn, nu: (last(i, nu), 0)),
                      pl.BlockSpec((None, 1, 2 * D_FF), lambda i, be, bf, bs, bn, nu: (be[last(i, nu)], 0, 0)),
                      pl.BlockSpec((None, 1, D_MODEL), lambda i, be, bf, bs, bn, nu: (be[last(i, nu)], 0, 0)),
                      pl.BlockSpec(memory_space=pl.ANY), pl.BlockSpec(memory_space=pl.ANY)],
            out_specs=pl.BlockSpec((TM, D_MODEL), lambda i, be, bf, bs, bn, nu: (i, 0)),
            scratch_shapes=[pltpu.VMEM((2, D_MODEL, 2 * D_FF), F32), pltpu.VMEM((2, D_FF, D_MODEL), F32),
                            pltpu.VMEM((D_MODEL, 2 * D_FF), BF16), pltpu.VMEM((D_FF, D_MODEL), BF16),
                            pltpu.SemaphoreType.DMA((2,))]),
        out_shape=jax.ShapeDtypeStruct((cap, D_MODEL), F32),
        compiler_params=pltpu.CompilerParams(dimension_semantics=("arbitrary",), vmem_limit_bytes=EXPERT_VMEM_LIMIT),
        name="experts",
    )(blk_e, blk_first, blk_slot, blk_next, n_used, x_sorted, b_up[0].astype(F32)[:, None, :],
      b_down[0].astype(F32)[:, None, :], w_up[0].astype(F32), w_down[0].astype(F32))

    assert T_ROWS == T_PROMPT
    n_tiles = n_tok // T_ROWS
    dest_next_spec = pl.BlockSpec((None, 1, T_ROWS * TOP_K), lambda i: (jnp.minimum(i + 1, n_tiles - 1), 0, 0),
                                  memory_space=pltpu.SMEM)
    out_p, out_s = pl.pallas_call(
        functools.partial(_combine_body, n_prompt_tiles=npt, n_tiles=n_tiles),
        grid=(n_tiles,),
        in_specs=[dest_spec, dest_next_spec, pl.BlockSpec(memory_space=pl.ANY),
                  pl.BlockSpec((T_ROWS, D_MODEL), lambda i: (i, 0)), pl.BlockSpec((T_ROWS, LANES), lambda i: (i, 0)),
                  _full((1, D_MODEL)), _full((1, D_MODEL))],
        out_specs=[pl.BlockSpec((T_ROWS, D_MODEL), lambda i: (jnp.minimum(i, npt - 1), 0)),
                   pl.BlockSpec((T_ROWS, D_MODEL), lambda i: (jnp.maximum(i - npt, 0), 0))],
        out_shape=(jax.ShapeDtypeStruct((n_prompt, D_MODEL), F32), jax.ShapeDtypeStruct((n_sample, D_MODEL), F32)),
        scratch_shapes=[pltpu.VMEM((2, TOP_K, T_ROWS // SUBLANES, SUBLANES, D_MODEL), F32),
                        pltpu.SemaphoreType.DMA((2,))],
        compiler_params=cparams(("arbitrary",)),
        name="combine",
    )(dest_tiles, dest_tiles, y_sorted.reshape(cap // SUBLANES, SUBLANES, D_MODEL), base_all, gate_all,
      row(ln2_g[0]), row(ln2_b[0]))

    y_prompt = out_p.reshape(batch, seq, D_MODEL)
    y_sample = out_s.reshape(dec_batch, dec_seq, D_MODEL)
    conv_state_prompt = u_tail[None, :, HIST - (CONV_W - 1):, :]
    u_s3 = u_s.reshape(dec_batch, dec_seq, C_CONV)
    conv_state_sample = jnp.concatenate([cache_conv[0][:, dec_seq:, :].astype(F32), u_s3], axis=1)[None]
    chunk_v_prompt = v_chunk[None]
    chunk_v_sample = v_s.reshape(1, dec_batch, dec_seq, C_GMLP)
    return (y_prompt, y_sample, conv_state_prompt, conv_state_sample, chunk_v_prompt, chunk_v_sample)
```

```python
import functools

import jax
import jax.numpy as jnp
from jax import lax
from jax.experimental import pallas as pl
from jax.experimental.pallas import tpu as pltpu

F32 = jnp.float32
BF16 = jnp.bfloat16

D_MODEL = 1024
C_CONV = 512
C_GMLP = 512
N_CONV_GROUPS = 8
CONV_W = 31
N_HEADS = 8
HEAD_DIM = C_GMLP // N_HEADS
CHUNK = 128
N_EXPERTS = 32
TOP_K = 4
D_FF = 1024
PLE_DIM = 256
SWIGLU_LIMIT = 7.0
SWIGLU_ALPHA = 1.702
LN_EPS = 1e-5
DEPTH = 1
DEEPNORM_ALPHA = (2.0 * DEPTH) ** 0.25

LANES = 128
SUBLANES = 8
MXU_DIM = 256
VMEM_LIMIT = 48 * 1024 * 1024
EXPERT_VMEM_LIMIT = 56 * 1024 * 1024

T_PROMPT = 256
HIST = 32
CONV_STRIDE = 4
SEQS_PER_TILE = 32
SEQ_CHUNK = 4
T_RANK = 512
T_ROWS = 256
TM = 256


def _dot(a, b):
    return jnp.dot(a, b, preferred_element_type=F32)


def _layer_norm(x, g, b):
    mu = jnp.mean(x, axis=-1, keepdims=True)
    xc = x - mu
    var = jnp.mean(xc * xc, axis=-1, keepdims=True)
    return xc * lax.rsqrt(var + LN_EPS) * g + b


def _split_bf16(a):
    hi = a.astype(BF16)
    lo = (a - hi.astype(F32)).astype(BF16)
    return hi, lo


def _group_mean(a, gmat_ref):
    hi, lo = _split_bf16(a)
    g = gmat_ref[...]
    outs = []
    for s in range(C_CONV // MXU_DIM):
        sl = slice(MXU_DIM * s, MXU_DIM * (s + 1))
        outs.append(_dot(hi[:, sl], g) + _dot(lo[:, sl], g))
    return jnp.concatenate(outs, axis=1)


def _group_norm_silu(y, gmat_ref, gn_g, gn_b):
    mu = _group_mean(y, gmat_ref)
    yc = y - mu
    var = _group_mean(yc * yc, gmat_ref)
    yn = yc * lax.rsqrt(var + LN_EPS) * gn_g + gn_b
    return yn * jax.nn.sigmoid(yn)


def _store_token_tiles(ref, val):
    n = val.shape[0]
    for c in range(D_MODEL // LANES):
        ref[pl.ds(c, n, stride=SUBLANES), :] = val[:, c * LANES:(c + 1) * LANES]


def _load_token_tiles(ref, n, lead=()):
    parts = [ref[lead + (pl.ds(c, n, stride=SUBLANES), slice(None))] for c in range(D_MODEL // LANES)]
    return jnp.concatenate(parts, axis=1)


def _front(x, w):
    xn = _layer_norm(x, w["ln_in_g"][...], w["ln_in_b"][...])
    z = _dot(xn.astype(BF16), w["w_in"][...])
    a_val = z[:, 0:C_CONV]
    a_gate = z[:, C_CONV:2 * C_CONV]
    g_u = z[:, 2 * C_CONV:2 * C_CONV + C_GMLP]
    g_v = z[:, 2 * C_CONV + C_GMLP:]
    u = a_val * jax.nn.sigmoid(a_gate)
    ug = jax.nn.gelu(g_u)
    v = _layer_norm(jax.nn.gelu(g_v), w["vn_g"][...], w["vn_b"][...])
    return xn, u, ug, v


def _tail(xn, y_a, y_b, p, w, h_ref, base_ref, idx_ref, gate_ref):
    mix = _dot(y_a.astype(BF16), w["w_out"][0:C_CONV, :]) + _dot(y_b.astype(BF16), w["w_out"][C_CONV:, :])
    h = _layer_norm(DEEPNORM_ALPHA * xn + mix, w["ln1_g"][...], w["ln1_b"][...])
    hb, h_lo = _split_bf16(h)
    ple = _dot(p.astype(BF16), w["w_ple"][...]) * jax.nn.sigmoid(_dot(hb, w["w_gate"][...]))
    _store_token_tiles(h_ref, h)
    base_ref[...] = DEEPNORM_ALPHA * h + ple

    wr_hi = w["wr_hi"][...]
    logits = _dot(hb, wr_hi) + _dot(h_lo, wr_hi) + _dot(hb, w["wr_lo"][...]) + w["b_r"][...]
    lane = lax.broadcasted_iota(jnp.int32, logits.shape, 1)
    lane_f = lane.astype(F32)
    vals = jnp.where(lane < N_EXPERTS, logits, -jnp.inf)
    tops, ids = [], []
    for _ in range(TOP_K):
        m = jnp.max(vals, axis=-1, keepdims=True)
        i = jnp.min(jnp.where(vals == m, lane_f, float(LANES)), axis=-1, keepdims=True)
        vals = jnp.where(lane_f == i, -jnp.inf, vals)
        tops.append(m)
        ids.append(i)
    exps = [jnp.exp(m - tops[0]) for m in tops]
    denom = exps[0] + exps[1] + exps[2] + exps[3]
    idx_out = jnp.zeros(logits.shape, F32)
    gate_out = jnp.zeros(logits.shape, F32)
    for k in range(TOP_K):
        idx_out = jnp.where(lane == k, ids[k], idx_out)
        gate_out = jnp.where(lane == k, exps[k] / denom, gate_out)
    idx_ref[...] = idx_out.astype(jnp.int32)
    gate_ref[...] = gate_out


_WEIGHT_NAMES = ("ln_in_g", "ln_in_b", "w_in", "vn_g", "vn_b", "gmat", "gn_g", "gn_b", "w_out", "ln1_g", "ln1_b",
                 "w_gate", "w_ple", "wr_hi", "wr_lo", "b_r")


def _prompt_branch(j, x_ref, p_ref, w, cwb_ref, ws_ref, bs_ref, outs, tail_ref, vch_ref, ubuf, yslab, last_j):
    t = T_PROMPT
    xn, u, ug, v = _front(x_ref[...], w)

    n_slabs = C_CONV // LANES

    @pl.when(j == 0)
    def _():
        ubuf[:, 0:HIST, :] = jnp.zeros((n_slabs, HIST, LANES), F32)

    @pl.when(j > 0)
    def _():
        ubuf[:, 0:HIST, :] = ubuf[:, t:t + HIST, :]

    for s in range(n_slabs):
        ubuf[s, HIST:HIST + t, :] = u[:, s * LANES:(s + 1) * LANES]
    rows = CONV_STRIDE * SUBLANES
    first = HIST - (CONV_W - 1)
    for s in range(n_slabs):
        for c in range(t // rows):
            accs = [None] * CONV_STRIDE
            for shift in range(CONV_STRIDE + CONV_W - 1):
                win = ubuf[s, pl.ds(first + c * rows + shift, SUBLANES, stride=CONV_STRIDE), :]
                for ph in range(CONV_STRIDE):
                    k = shift - ph
                    if 0 <= k < CONV_W:
                        term = cwb_ref[k, s] * win
                        accs[ph] = term if accs[ph] is None else accs[ph] + term
            for ph in range(CONV_STRIDE):
                yslab[s, pl.ds(c * rows + ph, SUBLANES, stride=CONV_STRIDE), :] = accs[ph]
    y_conv = jnp.concatenate([yslab[s] for s in range(n_slabs)], axis=1)
    y_a = _group_norm_silu(y_conv, w["gmat"], w["gn_g"][...], w["gn_b"][...])

    lane = lax.broadcasted_iota(jnp.int32, (CHUNK, LANES), 1)
    mixed_chunks = []
    for c in range(t // CHUNK):
        vc = v[c * CHUNK:(c + 1) * CHUNK, :]
        parts = []
        for q in range(N_HEADS // 2):
            vp = vc[:, q * LANES:(q + 1) * LANES]
            rhs = jnp.concatenate([jnp.where(lane < HEAD_DIM, vp, 0.0), jnp.where(lane >= HEAD_DIM, vp, 0.0)],
                                  axis=0).astype(BF16)
            parts.append(_dot(ws_ref[q], rhs))
        mixed_chunks.append(jnp.concatenate(parts, axis=1) + bs_ref[...])
    y_b = ug * jnp.concatenate(mixed_chunks, axis=0)

    _tail(xn, y_a, y_b, p_ref[...], w, *outs)

    @pl.when(j == last_j)
    def _():
        tail_ref[...] = u[t - HIST:, :]
        vch_ref[...] = v[t - CHUNK:, :]


def _sample_branch(x_ref, p_ref, cpad_ref, w, cw_ref, convu_ref, gw_ref, b8_ref, outs, u_out_ref, v_out_ref,
                   uscr, vscr, yscr, mscr):
    xn, u, ug, v = _front(x_ref[...], w)
    u_out_ref[...] = u
    v_out_ref[...] = v
    uscr[...] = u
    vscr[...] = v
    first = HIST - (CONV_W - 1)
    rows = SEQ_CHUNK * SUBLANES

    def step(c, carry):
        r0 = pl.multiple_of(c * rows, rows)
        s0 = c * SEQ_CHUNK
        u3 = uscr[pl.ds(r0, rows), :].reshape(SEQ_CHUNK, SUBLANES, C_CONV)
        v3 = vscr[pl.ds(r0, rows), :].reshape(SEQ_CHUNK, SUBLANES, C_GMLP)
        acc = jnp.zeros((SEQ_CHUNK, SUBLANES, C_CONV), F32)
        for k in range(CONV_W):
            acc = acc + cw_ref[pl.ds(k, 1), :][None] * cpad_ref[pl.ds(s0, SEQ_CHUNK), pl.ds(first + k, SUBLANES), :]
        mix = jnp.zeros((SEQ_CHUNK, SUBLANES, C_GMLP), F32) + b8_ref[...][None]
        for s in range(SUBLANES):
            acc = acc + u3[:, s:s + 1, :] * convu_ref[s][None]
            mix = mix + v3[:, s:s + 1, :] * gw_ref[s][None]
        yscr[pl.ds(r0, rows), :] = acc.reshape(rows, C_CONV)
        mscr[pl.ds(r0, rows), :] = mix.reshape(rows, C_GMLP)
        return carry

    lax.fori_loop(0, SEQS_PER_TILE // SEQ_CHUNK, step, 0)
    y_a = _group_norm_silu(yscr[...], w["gmat"], w["gn_g"][...], w["gn_b"][...])
    y_b = ug * mscr[...]
    _tail(xn, y_a, y_b, p_ref[...], w, *outs)


def _mixer_body(*refs, n_prompt_tiles, tiles_per_seq):
    n_w = len(_WEIGHT_NAMES)
    xp_ref, pp_ref, xs_ref, ps_ref, cpad_ref = refs[:5]
    w = dict(zip(_WEIGHT_NAMES, refs[5:5 + n_w]))
    cw_ref, cwb_ref, ws_ref, bs_ref, convu_ref, gw_ref, b8_ref = refs[5 + n_w:12 + n_w]
    outs = refs[12 + n_w:16 + n_w]
    tail_ref, vch_ref, u_out_ref, v_out_ref = refs[16 + n_w:20 + n_w]
    ubuf, yslab, yscr, uscr, vscr, mscr = refs[20 + n_w:]
    step = pl.program_id(0)

    @pl.when(step < n_prompt_tiles)
    def _():
        _prompt_branch(lax.rem(step, tiles_per_seq), xp_ref, pp_ref, w, cwb_ref, ws_ref, bs_ref, outs, tail_ref,
                       vch_ref, ubuf, yslab, tiles_per_seq - 1)

    @pl.when(step >= n_prompt_tiles)
    def _():
        _sample_branch(xs_ref, ps_ref, cpad_ref, w, cw_ref, convu_ref, gw_ref, b8_ref, outs, u_out_ref, v_out_ref,
                       uscr, vscr, yscr, mscr)


def _rank_body(idx_ref, tri_ref, rank_ref, counts_ref, carry):
    i = pl.program_id(0)

    @pl.when(i == 0)
    def _():
        carry[...] = jnp.zeros(carry.shape, F32)

    idx = idx_ref[...]
    lane = lax.broadcasted_iota(jnp.int32, idx.shape, 1)
    hots = [lane == idx[:, k:k + 1] for k in range(TOP_K)]
    multi = jnp.zeros(idx.shape, F32)
    for hot in hots:
        multi = multi + hot.astype(F32)
    before = _dot(tri_ref[...], multi.astype(BF16)) + carry[...]
    rank = jnp.zeros(idx.shape, F32)
    for k, hot in enumerate(hots):
        rk = jnp.sum(jnp.where(hot, before, 0.0), axis=-1, keepdims=True)
        rank = jnp.where(lane == k, rk, rank)
    rank_ref[...] = rank.astype(jnp.int32)
    carry[...] = carry[...] + jnp.sum(multi, axis=0, keepdims=True)
    counts_ref[...] = carry[...].astype(jnp.int32)


def _pad_bits():
    b = TM // 2
    while b >= 1:
        yield b
        b //= 2


def _token_rows(t, n=1):
    return pl.ds(pl.multiple_of(t * SUBLANES, SUBLANES), n * SUBLANES)


def _dispatch_body(pad_off_ref, pad_n_ref, n_used_ref, dest_ref, h_ref, xs_ref, zbuf, sem, zsem, *, n_blocks):
    i = pl.program_id(0)

    group = SUBLANES

    def issue(g, carry):
        dests = [dest_ref[0, g * (group * TOP_K) + c] for c in range(group * TOP_K)]
        for j in range(group):
            src = h_ref.at[_token_rows(i * T_ROWS + g * group + j), :]
            for k in range(TOP_K):
                pltpu.make_async_copy(src, xs_ref.at[_token_rows(dests[j * TOP_K + k]), :], sem).start(priority=k % 2)
        return carry

    lax.fori_loop(0, T_ROWS // group, issue, 0)

    @pl.when(i == 0)
    def _():
        zbuf[...] = jnp.zeros(zbuf.shape, F32)

        def zero_copy(off, b):
            return pltpu.make_async_copy(zbuf.at[pl.ds(0, b * SUBLANES), :], xs_ref.at[_token_rows(off, b), :], zsem)

        def start_or_wait(cond, cp, wait):
            @pl.when(cond)
            def _():
                if wait:
                    cp.wait()
                else:
                    cp.start()

        for wait in (False, True):
            for e in range(N_EXPERTS):
                n = pad_n_ref[e]
                for b in _pad_bits():
                    start_or_wait((n & b) != 0, zero_copy(pad_off_ref[e] + (n & ~(2 * b - 1)), b), wait)

        half = TM // 2

        def tail_block(wait):
            def go(blk, carry):
                for s in range(TM // half):
                    cp = zero_copy(blk * TM + s * half, half)
                    cp.wait() if wait else cp.start()
                return carry
            return go

        lax.fori_loop(n_used_ref[0], n_blocks, tail_block(False), 0)
        lax.fori_loop(n_used_ref[0], n_blocks, tail_block(True), 0)

    for _ in range(TOP_K):
        pltpu.make_async_copy(h_ref.at[_token_rows(0, T_ROWS), :], xs_ref.at[_token_rows(0, T_ROWS), :], sem).wait()


def _expert_body(blk_e_ref, first_ref, slot_ref, next_ref, n_used_ref, x_ref, bu_ref, bd_ref, wu_hbm, wd_hbm, y_ref,
                 wu_f32, wd_f32, wu_bf, wd_bf, sems):
    i = pl.program_id(0)

    def weight_copies(e, s):
        return (pltpu.make_async_copy(wu_hbm.at[e], wu_f32.at[s], sems.at[s]),
                pltpu.make_async_copy(wd_hbm.at[e], wd_f32.at[s], sems.at[s]))

    @pl.when(i < n_used_ref[0])
    def _():
        s = slot_ref[i]

        @pl.when(first_ref[i] == 1)
        def _():
            @pl.when(i == 0)
            def _():
                for cp in weight_copies(blk_e_ref[i], s):
                    cp.start()

            for cp in weight_copies(blk_e_ref[i], s):
                cp.wait()
            nxt = next_ref[i]

            @pl.when(nxt >= 0)
            def _():
                for cp in weight_copies(nxt, 1 - s):
                    cp.start()

            chunk = D_MODEL // SUBLANES

            def cast_rows(c, carry):
                r = pl.multiple_of(c * chunk, chunk)
                wu_bf[pl.ds(r, chunk), :] = wu_f32[s, pl.ds(r, chunk), :].astype(BF16)
                wd_bf[pl.ds(r, chunk), :] = wd_f32[s, pl.ds(r, chunk), :].astype(BF16)
                return carry

            lax.fori_loop(0, SUBLANES, cast_rows, 0)

        hcat = _dot(_load_token_tiles(x_ref, TM).astype(BF16), wu_bf[...]) + bu_ref[...]
        h_glu = jnp.minimum(hcat[:, :D_FF], SWIGLU_LIMIT)
        h_lin = jnp.clip(hcat[:, D_FF:], -SWIGLU_LIMIT, SWIGLU_LIMIT)
        act = h_glu * jax.nn.sigmoid(SWIGLU_ALPHA * h_glu) * (h_lin + 1.0)
        _store_token_tiles(y_ref, _dot(act.astype(BF16), wd_bf[...]) + bd_ref[...])

    @pl.when(pl.program_id(0) >= n_used_ref[0])
    def _():
        y_ref[...] = jnp.zeros(y_ref.shape, F32)


def _combine_body(dest_ref, dest_next_ref, y_ref, base_ref, gate_ref, g_ref, b_ref, outp_ref, outs_ref, ybuf, sems, *,
                  n_prompt_tiles, n_tiles):
    i = pl.program_id(0)

    def issue_tile(d_ref, slot):
        def issue(r, carry):
            for k in range(TOP_K):
                d = d_ref[0, r * TOP_K + k]
                pltpu.make_async_copy(y_ref.at[_token_rows(d), :], ybuf.at[slot * TOP_K + k, _token_rows(r), :],
                                      sems.at[slot]).start(priority=k % 2)
            return carry

        lax.fori_loop(0, T_ROWS, issue, 0, unroll=8)

    slot = lax.rem(i, 2)

    @pl.when(i == 0)
    def _():
        issue_tile(dest_ref, 0)

    @pl.when(i + 1 < n_tiles)
    def _():
        issue_tile(dest_next_ref, 1 - slot)

    for k in range(TOP_K):
        pltpu.make_async_copy(y_ref.at[_token_rows(0, T_ROWS), :], ybuf.at[slot * TOP_K + k], sems.at[slot]).wait()

    gate = gate_ref[...]
    acc = base_ref[...]
    for k in range(TOP_K):
        acc = acc + gate[:, k:k + 1] * _load_token_tiles(ybuf, T_ROWS, lead=(slot * TOP_K + k,))
    out = _layer_norm(acc, g_ref[...], b_ref[...])

    @pl.when(i < n_prompt_tiles)
    def _():
        outp_ref[...] = out

    @pl.when(i >= n_prompt_tiles)
    def _():
        outs_ref[...] = out


def _full(shape):
    return pl.BlockSpec(shape, lambda *_: (0,) * len(shape))


def kernel(x_prompt, x_sample, cache_conv, p_prompt, p_sample, ln_in_g, ln_in_b, w_in, conv_w, gn_g, gn_b, vn_g, vn_b,
           w_spatial, b_spatial, w_out, ln1_g, ln1_b, w_router, b_router, w_up, b_up, w_down, b_down, w_ple,
           w_ple_gate, ln2_g, ln2_b):
    batch, seq, _ = x_prompt.shape
    dec_batch, dec_seq, _ = x_sample.shape
    assert w_in.shape[0] == DEPTH and dec_seq == SUBLANES and seq % T_PROMPT == 0
    n_prompt = batch * seq
    n_sample = dec_batch * dec_seq
    n_tok = n_prompt + n_sample
    t_s = SEQS_PER_TILE * dec_seq
    assert n_prompt % T_ROWS == 0 and n_sample % T_ROWS == 0 and n_tok % T_RANK == 0 and n_sample % t_s == 0

    row = lambda a: a.reshape(1, -1).astype(F32)
    gidx = jnp.arange(MXU_DIM) // (C_CONV // N_CONV_GROUPS)
    gmat = jnp.where(gidx[:, None] == gidx[None, :], 1.0 / (C_CONV // N_CONV_GROUPS), 0.0).astype(BF16)
    wr_pad = jnp.pad(w_router[0].astype(F32), ((0, 0), (0, LANES - N_EXPERTS)))
    wr_hi = wr_pad.astype(BF16)
    wr_lo = (wr_pad - wr_hi.astype(F32)).astype(BF16)
    weights = dict(
        ln_in_g=row(ln_in_g), ln_in_b=row(ln_in_b), w_in=w_in[0].astype(BF16), vn_g=row(vn_g[0]), vn_b=row(vn_b[0]),
        gmat=gmat, gn_g=row(gn_g[0]), gn_b=row(gn_b[0]), w_out=w_out[0].astype(BF16), ln1_g=row(ln1_g[0]),
        ln1_b=row(ln1_b[0]), w_gate=w_ple_gate[0].astype(BF16), w_ple=w_ple[0].astype(BF16), wr_hi=wr_hi, wr_lo=wr_lo,
        b_r=jnp.pad(row(b_router[0]), ((0, 0), (0, LANES - N_EXPERTS))))
    w_list = [weights[n] for n in _WEIGHT_NAMES]
    w_specs = [_full(a.shape) for a in w_list]
    cw = jnp.pad(conv_w[0].astype(F32), ((0, HIST - CONV_W), (0, 0)))
    causal = jnp.tril(jnp.ones((CHUNK, CHUNK), bool))
    ws_m = jnp.where(causal[None], w_spatial[0], 0.0)
    ws_cat = jnp.concatenate([ws_m[0::2], ws_m[1::2]], axis=2).astype(BF16)
    bs_full = jnp.repeat(b_spatial[0].T.astype(F32), HEAD_DIM, axis=1)
    s_i = jnp.arange(SUBLANES)[:, None]
    t_i = jnp.arange(SUBLANES)[None, :]
    tap = jnp.clip(CONV_W - 1 - t_i + s_i, 0, CONV_W - 1)
    convu = jnp.where((s_i <= t_i)[:, :, None], conv_w[0].astype(F32)[tap], 0.0)
    gw8 = jnp.transpose(ws_m[:, :SUBLANES, :SUBLANES], (2, 1, 0))
    gw8 = jnp.repeat(gw8.astype(F32), HEAD_DIM, axis=2)
    b8 = bs_full[:SUBLANES]
    cpad = jnp.pad(cache_conv[0].astype(F32), ((0, 0), (HIST - (CONV_W - 1), SUBLANES), (0, 0)))

    cparams = lambda sem: pltpu.CompilerParams(dimension_semantics=sem, vmem_limit_bytes=VMEM_LIMIT)

    nj = seq // T_PROMPT
    npt = n_prompt // T_PROMPT
    assert t_s == T_PROMPT
    pstep = lambda i: jnp.minimum(i, npt - 1)
    sstep = lambda i: jnp.maximum(i - npt, 0)
    tok_block = lambda width: pl.BlockSpec((T_PROMPT, width), lambda i: (i, 0))
    cwb = jnp.broadcast_to(conv_w[0].astype(F32).reshape(CONV_W, C_CONV // LANES, 1, LANES),
                           (CONV_W, C_CONV // LANES, SUBLANES, LANES))
    tables = [cw, cwb, ws_cat, bs_full, convu, gw8, b8]
    h_all, base_all, idx_all, gate_all, u_tail, v_chunk, u_s, v_s = pl.pallas_call(
        functools.partial(_mixer_body, n_prompt_tiles=npt, tiles_per_seq=nj),
        grid=(npt + n_sample // t_s,),
        in_specs=[pl.BlockSpec((None, T_PROMPT, D_MODEL), lambda i: (pstep(i) // nj, pstep(i) % nj, 0)),
                  pl.BlockSpec((None, T_PROMPT, PLE_DIM), lambda i: (pstep(i) // nj, pstep(i) % nj, 0)),
                  pl.BlockSpec((t_s, D_MODEL), lambda i: (sstep(i), 0)),
                  pl.BlockSpec((t_s, PLE_DIM), lambda i: (sstep(i), 0)),
                  pl.BlockSpec((SEQS_PER_TILE,) + cpad.shape[1:], lambda i: (sstep(i), 0, 0))]
                 + w_specs + [_full(a.shape) for a in tables],
        out_specs=[pl.BlockSpec((T_PROMPT * SUBLANES, LANES), lambda i: (i, 0)),
                   tok_block(D_MODEL), tok_block(LANES), tok_block(LANES),
                   pl.BlockSpec((None, HIST, C_CONV), lambda i: (pstep(i) // nj, 0, 0)),
                   pl.BlockSpec((None, CHUNK, C_GMLP), lambda i: (pstep(i) // nj, 0, 0)),
                   pl.BlockSpec((t_s, C_CONV), lambda i: (sstep(i), 0)),
                   pl.BlockSpec((t_s, C_GMLP), lambda i: (sstep(i), 0))],
        out_shape=(jax.ShapeDtypeStruct((n_tok * SUBLANES, LANES), F32), jax.ShapeDtypeStruct((n_tok, D_MODEL), F32),
                   jax.ShapeDtypeStruct((n_tok, LANES), jnp.int32), jax.ShapeDtypeStruct((n_tok, LANES), F32),
                   jax.ShapeDtypeStruct((batch, HIST, C_CONV), F32), jax.ShapeDtypeStruct((batch, CHUNK, C_GMLP), F32),
                   jax.ShapeDtypeStruct((n_sample, C_CONV), F32), jax.ShapeDtypeStruct((n_sample, C_GMLP), F32)),
        scratch_shapes=[pltpu.VMEM((C_CONV // LANES, T_PROMPT + HIST, LANES), F32),
                        pltpu.VMEM((C_CONV // LANES, T_PROMPT, LANES), F32)]
                       + [pltpu.VMEM((T_PROMPT, C_CONV), F32)] * 4,
        compiler_params=cparams(("arbitrary",)),
        name="mixer",
    )(x_prompt, p_prompt[0], x_sample.reshape(n_sample, D_MODEL), p_sample[0].reshape(n_sample, PLE_DIM), cpad,
      *w_list, *tables)

    tri = (jnp.arange(T_RANK)[:, None] > jnp.arange(T_RANK)[None, :]).astype(BF16)
    rank_all, counts = pl.pallas_call(
        _rank_body,
        grid=(n_tok // T_RANK,),
        in_specs=[pl.BlockSpec((T_RANK, LANES), lambda i: (i, 0)), _full(tri.shape)],
        out_specs=[pl.BlockSpec((T_RANK, LANES), lambda i: (i, 0)), _full((1, LANES))],
        out_shape=(jax.ShapeDtypeStruct((n_tok, LANES), jnp.int32), jax.ShapeDtypeStruct((1, LANES), jnp.int32)),
        scratch_shapes=[pltpu.VMEM((1, LANES), F32)],
        compiler_params=cparams(("arbitrary",)),
        name="rank",
    )(idx_all, tri)

    n_assign = n_tok * TOP_K
    n_blocks = n_assign // TM + N_EXPERTS
    cap = n_blocks * TM
    cnt = counts[0, :N_EXPERTS]
    padded = (cnt + TM - 1) // TM * TM
    p_end = jnp.cumsum(padded)
    p_start = p_end - padded
    dest = (p_start[idx_all[:, :TOP_K]] + rank_all[:, :TOP_K]).astype(jnp.int32)
    dest_tiles = dest.reshape(n_tok // T_ROWS, 1, T_ROWS * TOP_K)
    blk_row = jnp.arange(n_blocks, dtype=jnp.int32) * TM
    blk_e = jnp.minimum(jnp.sum(p_end[None, :] <= blk_row[:, None], axis=1), N_EXPERTS - 1).astype(jnp.int32)
    n_used = (p_end[-1:] // TM).astype(jnp.int32)
    pad_off = (p_start + cnt).astype(jnp.int32)
    pad_n = (padded - cnt).astype(jnp.int32)
    e_ids = jnp.arange(N_EXPERTS, dtype=jnp.int32)
    used = cnt > 0
    slot_e = (jnp.cumsum(used.astype(jnp.int32)) - 1) & 1
    later_used = jnp.where(used[None, :] & (e_ids[None, :] > e_ids[:, None]), e_ids[None, :], N_EXPERTS)
    next_e = jnp.min(later_used, axis=1)
    next_e = jnp.where(next_e < N_EXPERTS, next_e, -1).astype(jnp.int32)
    blk_first = ((blk_row == p_start[blk_e]) & (blk_row < p_end[-1])).astype(jnp.int32)
    blk_slot = slot_e[blk_e].astype(jnp.int32)
    blk_next = next_e[blk_e]

    dest_spec = pl.BlockSpec((None, 1, T_ROWS * TOP_K), lambda i, *_: (i, 0, 0), memory_space=pltpu.SMEM)

    x_sorted = pl.pallas_call(
        functools.partial(_dispatch_body, n_blocks=n_blocks),
        grid_spec=pltpu.PrefetchScalarGridSpec(
            num_scalar_prefetch=3,
            grid=(n_tok // T_ROWS,),
            in_specs=[dest_spec, pl.BlockSpec(memory_space=pl.ANY)],
            out_specs=pl.BlockSpec(memory_space=pl.ANY),
            scratch_shapes=[pltpu.VMEM((TM // 2 * SUBLANES, LANES), F32), pltpu.SemaphoreType.DMA,
                            pltpu.SemaphoreType.DMA]),
        out_shape=jax.ShapeDtypeStruct((cap * SUBLANES, LANES), F32),
        compiler_params=cparams(("arbitrary",)),
        name="dispatch",
    )(pad_off, pad_n, n_used, dest_tiles, h_all)

    last = lambda i, nu: jnp.minimum(i, nu[0] - 1)
    y_sorted = pl.pallas_call(
        _expert_body,
        grid_spec=pltpu.PrefetchScalarGridSpec(
            num_scalar_prefetch=5,
            grid=(n_blocks,),
            in_specs=[pl.BlockSpec((TM * SUBLANES, LANES), lambda i, be, bf, bs, bn, nu: (last(i, nu), 0)),
                      pl.BlockSpec((None, 1, 2 * D_FF), lambda i, be, bf, bs, bn, nu: (be[last(i, nu)], 0, 0)),
                      pl.BlockSpec((None, 1, D_MODEL), lambda i, be, bf, bs, bn, nu: (be[last(i, nu)], 0, 0)),
                      pl.BlockSpec(memory_space=pl.ANY), pl.BlockSpec(memory_space=pl.ANY)],
            out_specs=pl.BlockSpec((TM * SUBLANES, LANES), lambda i, be, bf, bs, bn, nu: (i, 0)),
            scratch_shapes=[pltpu.VMEM((2, D_MODEL, 2 * D_FF), F32), pltpu.VMEM((2, D_FF, D_MODEL), F32),
                            pltpu.VMEM((D_MODEL, 2 * D_FF), BF16), pltpu.VMEM((D_FF, D_MODEL), BF16),
                            pltpu.SemaphoreType.DMA((2,))]),
        out_shape=jax.ShapeDtypeStruct((cap * SUBLANES, LANES), F32),
        compiler_params=pltpu.CompilerParams(dimension_semantics=("arbitrary",), vmem_limit_bytes=EXPERT_VMEM_LIMIT),
        name="experts",
    )(blk_e, blk_first, blk_slot, blk_next, n_used, x_sorted, b_up[0].astype(F32)[:, None, :],
      b_down[0].astype(F32)[:, None, :], w_up[0].astype(F32), w_down[0].astype(F32))

    assert T_ROWS == T_PROMPT
    n_tiles = n_tok // T_ROWS
    dest_next_spec = pl.BlockSpec((None, 1, T_ROWS * TOP_K), lambda i: (jnp.minimum(i + 1, n_tiles - 1), 0, 0),
                                  memory_space=pltpu.SMEM)
    out_p, out_s = pl.pallas_call(
        functools.partial(_combine_body, n_prompt_tiles=npt, n_tiles=n_tiles),
        grid=(n_tiles,),
        in_specs=[dest_spec, dest_next_spec, pl.BlockSpec(memory_space=pl.ANY),
                  pl.BlockSpec((T_ROWS, D_MODEL), lambda i: (i, 0)), pl.BlockSpec((T_ROWS, LANES), lambda i: (i, 0)),
                  _full((1, D_MODEL)), _full((1, D_MODEL))],
        out_specs=[pl.BlockSpec((T_ROWS, D_MODEL), lambda i: (jnp.minimum(i, npt - 1), 0)),
                   pl.BlockSpec((T_ROWS, D_MODEL), lambda i: (jnp.maximum(i - npt, 0), 0))],
        out_shape=(jax.ShapeDtypeStruct((n_prompt, D_MODEL), F32), jax.ShapeDtypeStruct((n_sample, D_MODEL), F32)),
        scratch_shapes=[pltpu.VMEM((2 * TOP_K, T_ROWS * SUBLANES, LANES), F32), pltpu.SemaphoreType.DMA((2,))],
        compiler_params=cparams(("arbitrary",)),
        name="combine",
    )(dest_tiles, dest_tiles, y_sorted, base_all, gate_all, row(ln2_g[0]), row(ln2_b[0]))

    y_prompt = out_p.reshape(batch, seq, D_MODEL)
    y_sample = out_s.reshape(dec_batch, dec_seq, D_MODEL)
    conv_state_prompt = u_tail[None, :, HIST - (CONV_W - 1):, :]
    u_s3 = u_s.reshape(dec_batch, dec_seq, C_CONV)
    conv_state_sample = jnp.concatenate([cache_conv[0][:, dec_seq:, :].astype(F32), u_s3], axis=1)[None]
    chunk_v_prompt = v_chunk[None]
    chunk_v_sample = v_s.reshape(1, dec_batch, dec_seq, C_GMLP)
    return (y_prompt, y_sample, conv_state_prompt, conv_state_sample, chunk_v_prompt, chunk_v_sample)
```

```python
import functools

import jax
import jax.numpy as jnp
from jax import lax
from jax.experimental import pallas as pl
from jax.experimental.pallas import tpu as pltpu

F32 = jnp.float32
BF16 = jnp.bfloat16

D_MODEL = 1024
C_CONV = 512
C_GMLP = 512
N_CONV_GROUPS = 8
CONV_W = 31
N_HEADS = 8
HEAD_DIM = C_GMLP // N_HEADS
CHUNK = 128
N_EXPERTS = 32
TOP_K = 4
D_FF = 1024
PLE_DIM = 256
SWIGLU_LIMIT = 7.0
SWIGLU_ALPHA = 1.702
LN_EPS = 1e-5
DEPTH = 1
DEEPNORM_ALPHA = (2.0 * DEPTH) ** 0.25

LANES = 128
SUBLANES = 8
MXU_DIM = 256
VMEM_LIMIT = 48 * 1024 * 1024
EXPERT_VMEM_LIMIT = 56 * 1024 * 1024

T_PROMPT = 256
HIST = 32
CONV_STRIDE = 4
SEQS_PER_TILE = 32
SEQ_CHUNK = 4
T_RANK = 512
T_ROWS = 256
TM = 256


def _dot(a, b):
    return jnp.dot(a, b, preferred_element_type=F32)


def _layer_norm(x, g, b):
    mu = jnp.mean(x, axis=-1, keepdims=True)
    xc = x - mu
    var = jnp.mean(xc * xc, axis=-1, keepdims=True)
    return xc * lax.rsqrt(var + LN_EPS) * g + b


def _split_bf16(a):
    hi = a.astype(BF16)
    lo = (a - hi.astype(F32)).astype(BF16)
    return hi, lo


def _group_mean(a, gmat_ref):
    hi, lo = _split_bf16(a)
    g = gmat_ref[...]
    outs = []
    for s in range(C_CONV // MXU_DIM):
        sl = slice(MXU_DIM * s, MXU_DIM * (s + 1))
        outs.append(_dot(hi[:, sl], g) + _dot(lo[:, sl], g))
    return jnp.concatenate(outs, axis=1)


def _group_norm_silu(y, gmat_ref, gn_g, gn_b):
    mu = _group_mean(y, gmat_ref)
    yc = y - mu
    var = _group_mean(yc * yc, gmat_ref)
    yn = yc * lax.rsqrt(var + LN_EPS) * gn_g + gn_b
    return yn * jax.nn.sigmoid(yn)


def _store_token_tiles(ref, val):
    n = val.shape[0]
    for c in range(D_MODEL // LANES):
        ref[pl.ds(c, n, stride=SUBLANES), :] = val[:, c * LANES:(c + 1) * LANES]


def _load_token_tiles(ref, n, lead=()):
    parts = [ref[lead + (pl.ds(c, n, stride=SUBLANES), slice(None))] for c in range(D_MODEL // LANES)]
    return jnp.concatenate(parts, axis=1)


def _front(x, w):
    xn = _layer_norm(x, w["ln_in_g"][...], w["ln_in_b"][...])
    z = _dot(xn.astype(BF16), w["w_in"][...])
    a_val = z[:, 0:C_CONV]
    a_gate = z[:, C_CONV:2 * C_CONV]
    g_u = z[:, 2 * C_CONV:2 * C_CONV + C_GMLP]
    g_v = z[:, 2 * C_CONV + C_GMLP:]
    u = a_val * jax.nn.sigmoid(a_gate)
    ug = jax.nn.gelu(g_u)
    v = _layer_norm(jax.nn.gelu(g_v), w["vn_g"][...], w["vn_b"][...])
    return xn, u, ug, v


def _tail(xn, y_a, y_b, p, w, h_ref, base_ref, idx_ref, gate_ref):
    mix = _dot(y_a.astype(BF16), w["w_out"][0:C_CONV, :]) + _dot(y_b.astype(BF16), w["w_out"][C_CONV:, :])
    h = _layer_norm(DEEPNORM_ALPHA * xn + mix, w["ln1_g"][...], w["ln1_b"][...])
    hb, h_lo = _split_bf16(h)
    ple = _dot(p.astype(BF16), w["w_ple"][...]) * jax.nn.sigmoid(_dot(hb, w["w_gate"][...]))
    _store_token_tiles(h_ref, h)
    base_ref[...] = DEEPNORM_ALPHA * h + ple

    wr_hi = w["wr_hi"][...]
    logits = _dot(hb, wr_hi) + _dot(h_lo, wr_hi) + _dot(hb, w["wr_lo"][...]) + w["b_r"][...]
    lane = lax.broadcasted_iota(jnp.int32, logits.shape, 1)
    lane_f = lane.astype(F32)
    vals = jnp.where(lane < N_EXPERTS, logits, -jnp.inf)
    tops, ids = [], []
    for _ in range(TOP_K):
        m = jnp.max(vals, axis=-1, keepdims=True)
        i = jnp.min(jnp.where(vals == m, lane_f, float(LANES)), axis=-1, keepdims=True)
        vals = jnp.where(lane_f == i, -jnp.inf, vals)
        tops.append(m)
        ids.append(i)
    exps = [jnp.exp(m - tops[0]) for m in tops]
    denom = exps[0] + exps[1] + exps[2] + exps[3]
    idx_out = jnp.zeros(logits.shape, F32)
    gate_out = jnp.zeros(logits.shape, F32)
    for k in range(TOP_K):
        idx_out = jnp.where(lane == k, ids[k], idx_out)
        gate_out = jnp.where(lane == k, exps[k] / denom, gate_out)
    idx_ref[...] = idx_out.astype(jnp.int32)
    gate_ref[...] = gate_out


_WEIGHT_NAMES = ("ln_in_g", "ln_in_b", "w_in", "vn_g", "vn_b", "gmat", "gn_g", "gn_b", "w_out", "ln1_g", "ln1_b",
                 "w_gate", "w_ple", "wr_hi", "wr_lo", "b_r")


def _prompt_branch(j, x_ref, p_ref, w, cwb_ref, ws_ref, bs_ref, outs, tail_ref, vch_ref, ubuf, yslab, last_j):
    t = T_PROMPT
    xn, u, ug, v = _front(x_ref[...], w)

    n_slabs = C_CONV // LANES

    @pl.when(j == 0)
    def _():
        ubuf[:, 0:HIST, :] = jnp.zeros((n_slabs, HIST, LANES), F32)

    @pl.when(j > 0)
    def _():
        ubuf[:, 0:HIST, :] = ubuf[:, t:t + HIST, :]

    for s in range(n_slabs):
        ubuf[s, HIST:HIST + t, :] = u[:, s * LANES:(s + 1) * LANES]
    rows = CONV_STRIDE * SUBLANES
    first = HIST - (CONV_W - 1)
    for s in range(n_slabs):
        for c in range(t // rows):
            accs = [None] * CONV_STRIDE
            for shift in range(CONV_STRIDE + CONV_W - 1):
                win = ubuf[s, pl.ds(first + c * rows + shift, SUBLANES, stride=CONV_STRIDE), :]
                for ph in range(CONV_STRIDE):
                    k = shift - ph
                    if 0 <= k < CONV_W:
                        term = cwb_ref[k, s] * win
                        accs[ph] = term if accs[ph] is None else accs[ph] + term
            for ph in range(CONV_STRIDE):
                yslab[s, pl.ds(c * rows + ph, SUBLANES, stride=CONV_STRIDE), :] = accs[ph]
    y_conv = jnp.concatenate([yslab[s] for s in range(n_slabs)], axis=1)
    y_a = _group_norm_silu(y_conv, w["gmat"], w["gn_g"][...], w["gn_b"][...])

    lane = lax.broadcasted_iota(jnp.int32, (CHUNK, LANES), 1)
    mixed_chunks = []
    for c in range(t // CHUNK):
        vc = v[c * CHUNK:(c + 1) * CHUNK, :]
        parts = []
        for q in range(N_HEADS // 2):
            vp = vc[:, q * LANES:(q + 1) * LANES]
            rhs = jnp.concatenate([jnp.where(lane < HEAD_DIM, vp, 0.0), jnp.where(lane >= HEAD_DIM, vp, 0.0)],
                                  axis=0).astype(BF16)
            parts.append(_dot(ws_ref[q], rhs))
        mixed_chunks.append(jnp.concatenate(parts, axis=1) + bs_ref[...])
    y_b = ug * jnp.concatenate(mixed_chunks, axis=0)

    _tail(xn, y_a, y_b, p_ref[...], w, *outs)

    @pl.when(j == last_j)
    def _():
        tail_ref[...] = u[t - HIST:, :]
        vch_ref[...] = v[t - CHUNK:, :]


def _sample_branch(x_ref, p_ref, cpad_ref, w, cw_ref, convu_ref, gw_ref, b8_ref, outs, u_out_ref, v_out_ref,
                   uscr, vscr, yscr, mscr):
    xn, u, ug, v = _front(x_ref[...], w)
    u_out_ref[...] = u
    v_out_ref[...] = v
    uscr[...] = u
    vscr[...] = v
    first = HIST - (CONV_W - 1)
    rows = SEQ_CHUNK * SUBLANES

    def step(c, carry):
        r0 = pl.multiple_of(c * rows, rows)
        s0 = c * SEQ_CHUNK
        u3 = uscr[pl.ds(r0, rows), :].reshape(SEQ_CHUNK, SUBLANES, C_CONV)
        v3 = vscr[pl.ds(r0, rows), :].reshape(SEQ_CHUNK, SUBLANES, C_GMLP)
        acc = jnp.zeros((SEQ_CHUNK, SUBLANES, C_CONV), F32)
        for k in range(CONV_W):
            acc = acc + cw_ref[pl.ds(k, 1), :][None] * cpad_ref[pl.ds(s0, SEQ_CHUNK), pl.ds(first + k, SUBLANES), :]
        mix = jnp.zeros((SEQ_CHUNK, SUBLANES, C_GMLP), F32) + b8_ref[...][None]
        for s in range(SUBLANES):
            acc = acc + u3[:, s:s + 1, :] * convu_ref[s][None]
            mix = mix + v3[:, s:s + 1, :] * gw_ref[s][None]
        yscr[pl.ds(r0, rows), :] = acc.reshape(rows, C_CONV)
        mscr[pl.ds(r0, rows), :] = mix.reshape(rows, C_GMLP)
        return carry

    lax.fori_loop(0, SEQS_PER_TILE // SEQ_CHUNK, step, 0)
    y_a = _group_norm_silu(yscr[...], w["gmat"], w["gn_g"][...], w["gn_b"][...])
    y_b = ug * mscr[...]
    _tail(xn, y_a, y_b, p_ref[...], w, *outs)


def _mixer_body(*refs, n_prompt_tiles, tiles_per_seq):
    n_w = len(_WEIGHT_NAMES)
    xp_ref, pp_ref, xs_ref, ps_ref, cpad_ref = refs[:5]
    w = dict(zip(_WEIGHT_NAMES, refs[5:5 + n_w]))
    cw_ref, cwb_ref, ws_ref, bs_ref, convu_ref, gw_ref, b8_ref = refs[5 + n_w:12 + n_w]
    outs = refs[12 + n_w:16 + n_w]
    tail_ref, vch_ref, u_out_ref, v_out_ref = refs[16 + n_w:20 + n_w]
    ubuf, yslab, yscr, uscr, vscr, mscr = refs[20 + n_w:]
    step = pl.program_id(0)

    @pl.when(step < n_prompt_tiles)
    def _():
        _prompt_branch(lax.rem(step, tiles_per_seq), xp_ref, pp_ref, w, cwb_ref, ws_ref, bs_ref, outs, tail_ref,
                       vch_ref, ubuf, yslab, tiles_per_seq - 1)

    @pl.when(step >= n_prompt_tiles)
    def _():
        _sample_branch(xs_ref, ps_ref, cpad_ref, w, cw_ref, convu_ref, gw_ref, b8_ref, outs, u_out_ref, v_out_ref,
                       uscr, vscr, yscr, mscr)


def _rank_body(idx_ref, tri_ref, rank_ref, counts_ref, carry):
    i = pl.program_id(0)

    @pl.when(i == 0)
    def _():
        carry[...] = jnp.zeros(carry.shape, F32)

    idx = idx_ref[...]
    lane = lax.broadcasted_iota(jnp.int32, idx.shape, 1)
    hots = [lane == idx[:, k:k + 1] for k in range(TOP_K)]
    multi = jnp.zeros(idx.shape, F32)
    for hot in hots:
        multi = multi + hot.astype(F32)
    before = _dot(tri_ref[...], multi.astype(BF16)) + carry[...]
    rank = jnp.zeros(idx.shape, F32)
    for k, hot in enumerate(hots):
        rk = jnp.sum(jnp.where(hot, before, 0.0), axis=-1, keepdims=True)
        rank = jnp.where(lane == k, rk, rank)
    rank_ref[...] = rank.astype(jnp.int32)
    carry[...] = carry[...] + jnp.sum(multi, axis=0, keepdims=True)
    counts_ref[...] = carry[...].astype(jnp.int32)


def _pad_bits():
    b = TM // 2
    while b >= 1:
        yield b
        b //= 2


def _token_rows(t, n=1):
    return pl.ds(pl.multiple_of(t * SUBLANES, SUBLANES), n * SUBLANES)


def _dispatch_body(pad_off_ref, pad_n_ref, n_used_ref, dest_ref, h_ref, xs_ref, zbuf, sem, zsem, *, n_blocks):
    i = pl.program_id(0)

    def issue(r, carry):
        src = h_ref.at[_token_rows(r), :]
        for k in range(TOP_K):
            d = dest_ref[0, r * TOP_K + k]
            pltpu.make_async_copy(src, xs_ref.at[_token_rows(d), :], sem).start(priority=k % 2)
        return carry

    lax.fori_loop(0, T_ROWS, issue, 0, unroll=8)

    @pl.when(i == 0)
    def _():
        zbuf[...] = jnp.zeros(zbuf.shape, F32)

        def zero_copy(off, b):
            return pltpu.make_async_copy(zbuf.at[pl.ds(0, b * SUBLANES), :], xs_ref.at[_token_rows(off, b), :], zsem)

        def start_or_wait(cond, cp, wait):
            @pl.when(cond)
            def _():
                if wait:
                    cp.wait()
                else:
                    cp.start()

        for wait in (False, True):
            for e in range(N_EXPERTS):
                n = pad_n_ref[e]
                for b in _pad_bits():
                    start_or_wait((n & b) != 0, zero_copy(pad_off_ref[e] + (n & ~(2 * b - 1)), b), wait)

        half = TM // 2

        def tail_block(wait):
            def go(blk, carry):
                for s in range(TM // half):
                    cp = zero_copy(blk * TM + s * half, half)
                    cp.wait() if wait else cp.start()
                return carry
            return go

        lax.fori_loop(n_used_ref[0], n_blocks, tail_block(False), 0)
        lax.fori_loop(n_used_ref[0], n_blocks, tail_block(True), 0)

    for _ in range(TOP_K):
        pltpu.make_async_copy(h_ref, xs_ref.at[_token_rows(0, T_ROWS), :], sem).wait()


def _expert_body(blk_e_ref, first_ref, slot_ref, next_ref, n_used_ref, x_ref, bu_ref, bd_ref, wu_hbm, wd_hbm, y_ref,
                 wu_f32, wd_f32, wu_bf, wd_bf, sems):
    i = pl.program_id(0)

    def weight_copies(e, s):
        return (pltpu.make_async_copy(wu_hbm.at[e], wu_f32.at[s], sems.at[s]),
                pltpu.make_async_copy(wd_hbm.at[e], wd_f32.at[s], sems.at[s]))

    @pl.when(i < n_used_ref[0])
    def _():
        s = slot_ref[i]

        @pl.when(first_ref[i] == 1)
        def _():
            @pl.when(i == 0)
            def _():
                for cp in weight_copies(blk_e_ref[i], s):
                    cp.start()

            for cp in weight_copies(blk_e_ref[i], s):
                cp.wait()
            nxt = next_ref[i]

            @pl.when(nxt >= 0)
            def _():
                for cp in weight_copies(nxt, 1 - s):
                    cp.start()

            chunk = D_MODEL // SUBLANES

            def cast_rows(c, carry):
                r = pl.multiple_of(c * chunk, chunk)
                wu_bf[pl.ds(r, chunk), :] = wu_f32[s, pl.ds(r, chunk), :].astype(BF16)
                wd_bf[pl.ds(r, chunk), :] = wd_f32[s, pl.ds(r, chunk), :].astype(BF16)
                return carry

            lax.fori_loop(0, SUBLANES, cast_rows, 0)

        hcat = _dot(_load_token_tiles(x_ref, TM).astype(BF16), wu_bf[...]) + bu_ref[...]
        h_glu = jnp.minimum(hcat[:, :D_FF], SWIGLU_LIMIT)
        h_lin = jnp.clip(hcat[:, D_FF:], -SWIGLU_LIMIT, SWIGLU_LIMIT)
        act = h_glu * jax.nn.sigmoid(SWIGLU_ALPHA * h_glu) * (h_lin + 1.0)
        _store_token_tiles(y_ref, _dot(act.astype(BF16), wd_bf[...]) + bd_ref[...])

    @pl.when(pl.program_id(0) >= n_used_ref[0])
    def _():
        y_ref[...] = jnp.zeros(y_ref.shape, F32)


def _combine_body(dest_ref, dest_next_ref, y_ref, base_ref, gate_ref, g_ref, b_ref, outp_ref, outs_ref, ybuf, sems, *,
                  n_prompt_tiles, n_tiles):
    i = pl.program_id(0)

    def issue_tile(d_ref, slot):
        def issue(r, carry):
            for k in range(TOP_K):
                d = d_ref[0, r * TOP_K + k]
                pltpu.make_async_copy(y_ref.at[_token_rows(d), :], ybuf.at[slot * TOP_K + k, _token_rows(r), :],
                                      sems.at[slot]).start(priority=k % 2)
            return carry

        lax.fori_loop(0, T_ROWS, issue, 0, unroll=8)

    slot = lax.rem(i, 2)

    @pl.when(i == 0)
    def _():
        issue_tile(dest_ref, 0)

    @pl.when(i + 1 < n_tiles)
    def _():
        issue_tile(dest_next_ref, 1 - slot)

    for k in range(TOP_K):
        pltpu.make_async_copy(y_ref.at[_token_rows(0, T_ROWS), :], ybuf.at[slot * TOP_K + k], sems.at[slot]).wait()

    gate = gate_ref[...]
    acc = base_ref[...]
    for k in range(TOP_K):
        acc = acc + gate[:, k:k + 1] * _load_token_tiles(ybuf, T_ROWS, lead=(slot * TOP_K + k,))
    out = _layer_norm(acc, g_ref[...], b_ref[...])

    @pl.when(i < n_prompt_tiles)
    def _():
        outp_ref[...] = out

    @pl.when(i >= n_prompt_tiles)
    def _():
        outs_ref[...] = out


def _full(shape):
    return pl.BlockSpec(shape, lambda *_: (0,) * len(shape))


def kernel(x_prompt, x_sample, cache_conv, p_prompt, p_sample, ln_in_g, ln_in_b, w_in, conv_w, gn_g, gn_b, vn_g, vn_b,
           w_spatial, b_spatial, w_out, ln1_g, ln1_b, w_router, b_router, w_up, b_up, w_down, b_down, w_ple,
           w_ple_gate, ln2_g, ln2_b):
    batch, seq, _ = x_prompt.shape
    dec_batch, dec_seq, _ = x_sample.shape
    assert w_in.shape[0] == DEPTH and dec_seq == SUBLANES and seq % T_PROMPT == 0
    n_prompt = batch * seq
    n_sample = dec_batch * dec_seq
    n_tok = n_prompt + n_sample
    t_s = SEQS_PER_TILE * dec_seq
    assert n_prompt % T_ROWS == 0 and n_sample % T_ROWS == 0 and n_tok % T_RANK == 0 and n_sample % t_s == 0

    row = lambda a: a.reshape(1, -1).astype(F32)
    gidx = jnp.arange(MXU_DIM) // (C_CONV // N_CONV_GROUPS)
    gmat = jnp.where(gidx[:, None] == gidx[None, :], 1.0 / (C_CONV // N_CONV_GROUPS), 0.0).astype(BF16)
    wr_pad = jnp.pad(w_router[0].astype(F32), ((0, 0), (0, LANES - N_EXPERTS)))
    wr_hi = wr_pad.astype(BF16)
    wr_lo = (wr_pad - wr_hi.astype(F32)).astype(BF16)
    weights = dict(
        ln_in_g=row(ln_in_g), ln_in_b=row(ln_in_b), w_in=w_in[0].astype(BF16), vn_g=row(vn_g[0]), vn_b=row(vn_b[0]),
        gmat=gmat, gn_g=row(gn_g[0]), gn_b=row(gn_b[0]), w_out=w_out[0].astype(BF16), ln1_g=row(ln1_g[0]),
        ln1_b=row(ln1_b[0]), w_gate=w_ple_gate[0].astype(BF16), w_ple=w_ple[0].astype(BF16), wr_hi=wr_hi, wr_lo=wr_lo,
        b_r=jnp.pad(row(b_router[0]), ((0, 0), (0, LANES - N_EXPERTS))))
    w_list = [weights[n] for n in _WEIGHT_NAMES]
    w_specs = [_full(a.shape) for a in w_list]
    cw = jnp.pad(conv_w[0].astype(F32), ((0, HIST - CONV_W), (0, 0)))
    causal = jnp.tril(jnp.ones((CHUNK, CHUNK), bool))
    ws_m = jnp.where(causal[None], w_spatial[0], 0.0)
    ws_cat = jnp.concatenate([ws_m[0::2], ws_m[1::2]], axis=2).astype(BF16)
    bs_full = jnp.repeat(b_spatial[0].T.astype(F32), HEAD_DIM, axis=1)
    s_i = jnp.arange(SUBLANES)[:, None]
    t_i = jnp.arange(SUBLANES)[None, :]
    tap = jnp.clip(CONV_W - 1 - t_i + s_i, 0, CONV_W - 1)
    convu = jnp.where((s_i <= t_i)[:, :, None], conv_w[0].astype(F32)[tap], 0.0)
    gw8 = jnp.transpose(ws_m[:, :SUBLANES, :SUBLANES], (2, 1, 0))
    gw8 = jnp.repeat(gw8.astype(F32), HEAD_DIM, axis=2)
    b8 = bs_full[:SUBLANES]
    cpad = jnp.pad(cache_conv[0].astype(F32), ((0, 0), (HIST - (CONV_W - 1), SUBLANES), (0, 0)))

    cparams = lambda sem: pltpu.CompilerParams(dimension_semantics=sem, vmem_limit_bytes=VMEM_LIMIT)

    nj = seq // T_PROMPT
    npt = n_prompt // T_PROMPT
    assert t_s == T_PROMPT
    pstep = lambda i: jnp.minimum(i, npt - 1)
    sstep = lambda i: jnp.maximum(i - npt, 0)
    tok_block = lambda width: pl.BlockSpec((T_PROMPT, width), lambda i: (i, 0))
    cwb = jnp.broadcast_to(conv_w[0].astype(F32).reshape(CONV_W, C_CONV // LANES, 1, LANES),
                           (CONV_W, C_CONV // LANES, SUBLANES, LANES))
    tables = [cw, cwb, ws_cat, bs_full, convu, gw8, b8]
    h_all, base_all, idx_all, gate_all, u_tail, v_chunk, u_s, v_s = pl.pallas_call(
        functools.partial(_mixer_body, n_prompt_tiles=npt, tiles_per_seq=nj),
        grid=(npt + n_sample // t_s,),
        in_specs=[pl.BlockSpec((None, T_PROMPT, D_MODEL), lambda i: (pstep(i) // nj, pstep(i) % nj, 0)),
                  pl.BlockSpec((None, T_PROMPT, PLE_DIM), lambda i: (pstep(i) // nj, pstep(i) % nj, 0)),
                  pl.BlockSpec((t_s, D_MODEL), lambda i: (sstep(i), 0)),
                  pl.BlockSpec((t_s, PLE_DIM), lambda i: (sstep(i), 0)),
                  pl.BlockSpec((SEQS_PER_TILE,) + cpad.shape[1:], lambda i: (sstep(i), 0, 0))]
                 + w_specs + [_full(a.shape) for a in tables],
        out_specs=[pl.BlockSpec((T_PROMPT * SUBLANES, LANES), lambda i: (i, 0)),
                   tok_block(D_MODEL), tok_block(LANES), tok_block(LANES),
                   pl.BlockSpec((None, HIST, C_CONV), lambda i: (pstep(i) // nj, 0, 0)),
                   pl.BlockSpec((None, CHUNK, C_GMLP), lambda i: (pstep(i) // nj, 0, 0)),
                   pl.BlockSpec((t_s, C_CONV), lambda i: (sstep(i), 0)),
                   pl.BlockSpec((t_s, C_GMLP), lambda i: (sstep(i), 0))],
        out_shape=(jax.ShapeDtypeStruct((n_tok * SUBLANES, LANES), F32), jax.ShapeDtypeStruct((n_tok, D_MODEL), F32),
                   jax.ShapeDtypeStruct((n_tok, LANES), jnp.int32), jax.ShapeDtypeStruct((n_tok, LANES), F32),
                   jax.ShapeDtypeStruct((batch, HIST, C_CONV), F32), jax.ShapeDtypeStruct((batch, CHUNK, C_GMLP), F32),
                   jax.ShapeDtypeStruct((n_sample, C_CONV), F32), jax.ShapeDtypeStruct((n_sample, C_GMLP), F32)),
        scratch_shapes=[pltpu.VMEM((C_CONV // LANES, T_PROMPT + HIST, LANES), F32),
                        pltpu.VMEM((C_CONV // LANES, T_PROMPT, LANES), F32)]
                       + [pltpu.VMEM((T_PROMPT, C_CONV), F32)] * 4,
        compiler_params=cparams(("arbitrary",)),
        name="mixer",
    )(x_prompt, p_prompt[0], x_sample.reshape(n_sample, D_MODEL), p_sample[0].reshape(n_sample, PLE_DIM), cpad,
      *w_list, *tables)

    tri = (jnp.arange(T_RANK)[:, None] > jnp.arange(T_RANK)[None, :]).astype(BF16)
    rank_all, counts = pl.pallas_call(
        _rank_body,
        grid=(n_tok // T_RANK,),
        in_specs=[pl.BlockSpec((T_RANK, LANES), lambda i: (i, 0)), _full(tri.shape)],
        out_specs=[pl.BlockSpec((T_RANK, LANES), lambda i: (i, 0)), _full((1, LANES))],
        out_shape=(jax.ShapeDtypeStruct((n_tok, LANES), jnp.int32), jax.ShapeDtypeStruct((1, LANES), jnp.int32)),
        scratch_shapes=[pltpu.VMEM((1, LANES), F32)],
        compiler_params=cparams(("arbitrary",)),
        name="rank",
    )(idx_all, tri)

    n_assign = n_tok * TOP_K
    n_blocks = n_assign // TM + N_EXPERTS
    cap = n_blocks * TM
    cnt = counts[0, :N_EXPERTS]
    padded = (cnt + TM - 1) // TM * TM
    p_end = jnp.cumsum(padded)
    p_start = p_end - padded
    dest = (p_start[idx_all[:, :TOP_K]] + rank_all[:, :TOP_K]).astype(jnp.int32)
    dest_tiles = dest.reshape(n_tok // T_ROWS, 1, T_ROWS * TOP_K)
    blk_row = jnp.arange(n_blocks, dtype=jnp.int32) * TM
    blk_e = jnp.minimum(jnp.sum(p_end[None, :] <= blk_row[:, None], axis=1), N_EXPERTS - 1).astype(jnp.int32)
    n_used = (p_end[-1:] // TM).astype(jnp.int32)
    pad_off = (p_start + cnt).astype(jnp.int32)
    pad_n = (padded - cnt).astype(jnp.int32)
    e_ids = jnp.arange(N_EXPERTS, dtype=jnp.int32)
    used = cnt > 0
    slot_e = (jnp.cumsum(used.astype(jnp.int32)) - 1) & 1
    later_used = jnp.where(used[None, :] & (e_ids[None, :] > e_ids[:, None]), e_ids[None, :], N_EXPERTS)
    next_e = jnp.min(later_used, axis=1)
    next_e = jnp.where(next_e < N_EXPERTS, next_e, -1).astype(jnp.int32)
    blk_first = ((blk_row == p_start[blk_e]) & (blk_row < p_end[-1])).astype(jnp.int32)
    blk_slot = slot_e[blk_e].astype(jnp.int32)
    blk_next = next_e[blk_e]

    dest_spec = pl.BlockSpec((None, 1, T_ROWS * TOP_K), lambda i, *_: (i, 0, 0), memory_space=pltpu.SMEM)

    x_sorted = pl.pallas_call(
        functools.partial(_dispatch_body, n_blocks=n_blocks),
        grid_spec=pltpu.PrefetchScalarGridSpec(
            num_scalar_prefetch=3,
            grid=(n_tok // T_ROWS,),
            in_specs=[dest_spec, pl.BlockSpec((T_ROWS * SUBLANES, LANES), lambda i, *_: (i, 0))],
            out_specs=pl.BlockSpec(memory_space=pl.ANY),
            scratch_shapes=[pltpu.VMEM((TM // 2 * SUBLANES, LANES), F32), pltpu.SemaphoreType.DMA,
                            pltpu.SemaphoreType.DMA]),
        out_shape=jax.ShapeDtypeStruct((cap * SUBLANES, LANES), F32),
        compiler_params=cparams(("arbitrary",)),
        name="dispatch",
    )(pad_off, pad_n, n_used, dest_tiles, h_all)

    last = lambda i, nu: jnp.minimum(i, nu[0] - 1)
    y_sorted = pl.pallas_call(
        _expert_body,
        grid_spec=pltpu.PrefetchScalarGridSpec(
            num_scalar_prefetch=5,
            grid=(n_blocks,),
            in_specs=[pl.BlockSpec((TM * SUBLANES, LANES), lambda i, be, bf, bs, bn, nu: (last(i, nu), 0)),
                      pl.BlockSpec((None, 1, 2 * D_FF), lambda i, be, bf, bs, bn, nu: (be[last(i, nu)], 0, 0)),
                      pl.BlockSpec((None, 1, D_MODEL), lambda i, be, bf, bs, bn, nu: (be[last(i, nu)], 0, 0)),
                      pl.BlockSpec(memory_space=pl.ANY), pl.BlockSpec(memory_space=pl.ANY)],
            out_specs=pl.BlockSpec((TM * SUBLANES, LANES), lambda i, be, bf, bs, bn, nu: (i, 0)),
            scratch_shapes=[pltpu.VMEM((2, D_MODEL, 2 * D_FF), F32), pltpu.VMEM((2, D_FF, D_MODEL), F32),
                            pltpu.VMEM((D_MODEL, 2 * D_FF), BF16), pltpu.VMEM((D_FF, D_MODEL), BF16),
                            pltpu.SemaphoreType.DMA((2,))]),
        out_shape=jax.ShapeDtypeStruct((cap * SUBLANES, LANES), F32),
        compiler_params=pltpu.CompilerParams(dimension_semantics=("arbitrary",), vmem_limit_bytes=EXPERT_VMEM_LIMIT),
        name="experts",
    )(blk_e, blk_first, blk_slot, blk_next, n_used, x_sorted, b_up[0].astype(F32)[:, None, :],
      b_down[0].astype(F32)[:, None, :], w_up[0].astype(F32), w_down[0].astype(F32))

    assert T_ROWS == T_PROMPT
    n_tiles = n_tok // T_ROWS
    dest_next_spec = pl.BlockSpec((None, 1, T_ROWS * TOP_K), lambda i: (jnp.minimum(i + 1, n_tiles - 1), 0, 0),
                                  memory_space=pltpu.SMEM)
    out_p, out_s = pl.pallas_call(
        functools.partial(_combine_body, n_prompt_tiles=npt, n_tiles=n_tiles),
        grid=(n_tiles,),
        in_specs=[dest_spec, dest_next_spec, pl.BlockSpec(memory_space=pl.ANY),
                  pl.BlockSpec((T_ROWS, D_MODEL), lambda i: (i, 0)), pl.BlockSpec((T_ROWS, LANES), lambda i: (i, 0)),
                  _full((1, D_MODEL)), _full((1, D_MODEL))],
        out_specs=[pl.BlockSpec((T_ROWS, D_MODEL), lambda i: (jnp.minimum(i, npt - 1), 0)),
                   pl.BlockSpec((T_ROWS, D_MODEL), lambda i: (jnp.maximum(i - npt, 0), 0))],
        out_shape=(jax.ShapeDtypeStruct((n_prompt, D_MODEL), F32), jax.ShapeDtypeStruct((n_sample, D_MODEL), F32)),
        scratch_shapes=[pltpu.VMEM((2 * TOP_K, T_ROWS * SUBLANES, LANES), F32), pltpu.SemaphoreType.DMA((2,))],
        compiler_params=cparams(("arbitrary",)),
        name="combine",
    )(dest_tiles, dest_tiles, y_sorted, base_all, gate_all, row(ln2_g[0]), row(ln2_b[0]))

    y_prompt = out_p.reshape(batch, seq, D_MODEL)
    y_sample = out_s.reshape(dec_batch, dec_seq, D_MODEL)
    conv_state_prompt = u_tail[None, :, HIST - (CONV_W - 1):, :]
    u_s3 = u_s.reshape(dec_batch, dec_seq, C_CONV)
    conv_state_sample = jnp.concatenate([cache_conv[0][:, dec_seq:, :].astype(F32), u_s3], axis=1)[None]
    chunk_v_prompt = v_chunk[None]
    chunk_v_sample = v_s.reshape(1, dec_batch, dec_seq, C_GMLP)
    return (y_prompt, y_sample, conv_state_prompt, conv_state_sample, chunk_v_prompt, chunk_v_sample)
```

```python
import functools

import jax
import jax.numpy as jnp
from jax import lax
from jax.experimental import pallas as pl
from jax.experimental.pallas import tpu as pltpu

F32 = jnp.float32
BF16 = jnp.bfloat16

D_MODEL = 1024
C_CONV = 512
C_GMLP = 512
N_CONV_GROUPS = 8
CONV_W = 31
N_HEADS = 8
HEAD_DIM = C_GMLP // N_HEADS
CHUNK = 128
N_EXPERTS = 32
TOP_K = 4
D_FF = 1024
PLE_DIM = 256
SWIGLU_LIMIT = 7.0
SWIGLU_ALPHA = 1.702
LN_EPS = 1e-5
DEPTH = 1
DEEPNORM_ALPHA = (2.0 * DEPTH) ** 0.25

LANES = 128
SUBLANES = 8
MXU_DIM = 256
VMEM_LIMIT = 48 * 1024 * 1024
BIG_VMEM_LIMIT = 56 * 1024 * 1024

T_PROMPT = 512
HIST = 32
CONV_STRIDE = 4
SEQS_PER_TILE = 64
SEQ_CHUNK = 4
T_RANK = 512
T_ROWS = 256
TM = 256


def _dot(a, b):
    return jnp.dot(a, b, preferred_element_type=F32)


def _layer_norm(x, g, b):
    mu = jnp.mean(x, axis=-1, keepdims=True)
    xc = x - mu
    var = jnp.mean(xc * xc, axis=-1, keepdims=True)
    return xc * lax.rsqrt(var + LN_EPS) * g + b


def _split_bf16(a):
    hi = a.astype(BF16)
    lo = (a - hi.astype(F32)).astype(BF16)
    return hi, lo


def _group_mean(a, gmat_ref):
    hi, lo = _split_bf16(a)
    g = gmat_ref[...]
    outs = []
    for s in range(C_CONV // MXU_DIM):
        sl = slice(MXU_DIM * s, MXU_DIM * (s + 1))
        outs.append(_dot(hi[:, sl], g) + _dot(lo[:, sl], g))
    return jnp.concatenate(outs, axis=1)


def _group_norm_silu(y, gmat_ref, gn_g, gn_b):
    mu = _group_mean(y, gmat_ref)
    yc = y - mu
    var = _group_mean(yc * yc, gmat_ref)
    yn = yc * lax.rsqrt(var + LN_EPS) * gn_g + gn_b
    return yn * jax.nn.sigmoid(yn)


def _store_token_tiles(ref, val):
    n = val.shape[0]
    for c in range(D_MODEL // LANES):
        ref[pl.ds(c, n, stride=SUBLANES), :] = val[:, c * LANES:(c + 1) * LANES]


def _load_token_tiles(ref, n, lead=()):
    parts = [ref[lead + (pl.ds(c, n, stride=SUBLANES), slice(None))] for c in range(D_MODEL // LANES)]
    return jnp.concatenate(parts, axis=1)


def _front(x, w):
    xn = _layer_norm(x, w["ln_in_g"][...], w["ln_in_b"][...])
    z = _dot(xn.astype(BF16), w["w_in"][...])
    a_val = z[:, 0:C_CONV]
    a_gate = z[:, C_CONV:2 * C_CONV]
    g_u = z[:, 2 * C_CONV:2 * C_CONV + C_GMLP]
    g_v = z[:, 2 * C_CONV + C_GMLP:]
    u = a_val * jax.nn.sigmoid(a_gate)
    ug = jax.nn.gelu(g_u)
    v = _layer_norm(jax.nn.gelu(g_v), w["vn_g"][...], w["vn_b"][...])
    return xn, u, ug, v


def _tail(xn, y_a, y_b, p, w, h_ref, base_ref, idx_ref, gate_ref):
    mix = _dot(y_a.astype(BF16), w["w_out"][0:C_CONV, :]) + _dot(y_b.astype(BF16), w["w_out"][C_CONV:, :])
    h = _layer_norm(DEEPNORM_ALPHA * xn + mix, w["ln1_g"][...], w["ln1_b"][...])
    hb, h_lo = _split_bf16(h)
    ple = _dot(p.astype(BF16), w["w_ple"][...]) * jax.nn.sigmoid(_dot(hb, w["w_gate"][...]))
    _store_token_tiles(h_ref, h)
    base_ref[...] = DEEPNORM_ALPHA * h + ple

    wr_hi = w["wr_hi"][...]
    logits = _dot(hb, wr_hi) + _dot(h_lo, wr_hi) + _dot(hb, w["wr_lo"][...]) + w["b_r"][...]
    lane = lax.broadcasted_iota(jnp.int32, logits.shape, 1)
    lane_f = lane.astype(F32)
    vals = jnp.where(lane < N_EXPERTS, logits, -jnp.inf)
    tops, ids = [], []
    for _ in range(TOP_K):
        m = jnp.max(vals, axis=-1, keepdims=True)
        i = jnp.min(jnp.where(vals == m, lane_f, float(LANES)), axis=-1, keepdims=True)
        vals = jnp.where(lane_f == i, -jnp.inf, vals)
        tops.append(m)
        ids.append(i)
    exps = [jnp.exp(m - tops[0]) for m in tops]
    denom = exps[0] + exps[1] + exps[2] + exps[3]
    idx_out = jnp.zeros(logits.shape, F32)
    gate_out = jnp.zeros(logits.shape, F32)
    for k in range(TOP_K):
        idx_out = jnp.where(lane == k, ids[k], idx_out)
        gate_out = jnp.where(lane == k, exps[k] / denom, gate_out)
    idx_ref[...] = idx_out.astype(jnp.int32)
    gate_ref[...] = gate_out


_WEIGHT_NAMES = ("ln_in_g", "ln_in_b", "w_in", "vn_g", "vn_b", "gmat", "gn_g", "gn_b", "w_out", "ln1_g", "ln1_b",
                 "w_gate", "w_ple", "wr_hi", "wr_lo", "b_r")


def _prompt_branch(j, x_ref, p_ref, w, cwb_ref, ws_ref, bs_ref, outs, tail_ref, vch_ref, ubuf, yslab, last_j):
    t = T_PROMPT
    xn, u, ug, v = _front(x_ref[...], w)

    n_slabs = C_CONV // LANES

    @pl.when(j == 0)
    def _():
        ubuf[:, 0:HIST, :] = jnp.zeros((n_slabs, HIST, LANES), F32)

    @pl.when(j > 0)
    def _():
        ubuf[:, 0:HIST, :] = ubuf[:, t:t + HIST, :]

    for s in range(n_slabs):
        ubuf[s, HIST:HIST + t, :] = u[:, s * LANES:(s + 1) * LANES]
    rows = CONV_STRIDE * SUBLANES
    first = HIST - (CONV_W - 1)
    for s in range(n_slabs):
        for c in range(t // rows):
            accs = [None] * CONV_STRIDE
            for shift in range(CONV_STRIDE + CONV_W - 1):
                win = ubuf[s, pl.ds(first + c * rows + shift, SUBLANES, stride=CONV_STRIDE), :]
                for ph in range(CONV_STRIDE):
                    k = shift - ph
                    if 0 <= k < CONV_W:
                        term = cwb_ref[k, s] * win
                        accs[ph] = term if accs[ph] is None else accs[ph] + term
            for ph in range(CONV_STRIDE):
                yslab[s, pl.ds(c * rows + ph, SUBLANES, stride=CONV_STRIDE), :] = accs[ph]
    y_conv = jnp.concatenate([yslab[s] for s in range(n_slabs)], axis=1)
    y_a = _group_norm_silu(y_conv, w["gmat"], w["gn_g"][...], w["gn_b"][...])

    lane = lax.broadcasted_iota(jnp.int32, (CHUNK, LANES), 1)
    mixed_chunks = []
    for c in range(t // CHUNK):
        vc = v[c * CHUNK:(c + 1) * CHUNK, :]
        parts = []
        for q in range(N_HEADS // 2):
            vp = vc[:, q * LANES:(q + 1) * LANES]
            rhs = jnp.concatenate([jnp.where(lane < HEAD_DIM, vp, 0.0), jnp.where(lane >= HEAD_DIM, vp, 0.0)],
                                  axis=0).astype(BF16)
            parts.append(_dot(ws_ref[q], rhs))
        mixed_chunks.append(jnp.concatenate(parts, axis=1) + bs_ref[...])
    y_b = ug * jnp.concatenate(mixed_chunks, axis=0)

    _tail(xn, y_a, y_b, p_ref[...], w, *outs)

    @pl.when(j == last_j)
    def _():
        tail_ref[...] = u[t - HIST:, :]
        vch_ref[...] = v[t - CHUNK:, :]


def _sample_branch(x_ref, p_ref, cpad_ref, w, cw_ref, convu_ref, gw_ref, b8_ref, outs, u_out_ref, v_out_ref,
                   uscr, vscr, yscr, mscr):
    xn, u, ug, v = _front(x_ref[...], w)
    u_out_ref[...] = u
    v_out_ref[...] = v
    uscr[...] = u
    vscr[...] = v
    first = HIST - (CONV_W - 1)
    rows = SEQ_CHUNK * SUBLANES

    def step(c, carry):
        r0 = pl.multiple_of(c * rows, rows)
        s0 = c * SEQ_CHUNK
        u3 = uscr[pl.ds(r0, rows), :].reshape(SEQ_CHUNK, SUBLANES, C_CONV)
        v3 = vscr[pl.ds(r0, rows), :].reshape(SEQ_CHUNK, SUBLANES, C_GMLP)
        acc = jnp.zeros((SEQ_CHUNK, SUBLANES, C_CONV), F32)
        for k in range(CONV_W):
            acc = acc + cw_ref[pl.ds(k, 1), :][None] * cpad_ref[pl.ds(s0, SEQ_CHUNK), pl.ds(first + k, SUBLANES), :]
        mix = jnp.zeros((SEQ_CHUNK, SUBLANES, C_GMLP), F32) + b8_ref[...][None]
        for s in range(SUBLANES):
            acc = acc + u3[:, s:s + 1, :] * convu_ref[s][None]
            mix = mix + v3[:, s:s + 1, :] * gw_ref[s][None]
        yscr[pl.ds(r0, rows), :] = acc.reshape(rows, C_CONV)
        mscr[pl.ds(r0, rows), :] = mix.reshape(rows, C_GMLP)
        return carry

    lax.fori_loop(0, SEQS_PER_TILE // SEQ_CHUNK, step, 0)
    y_a = _group_norm_silu(yscr[...], w["gmat"], w["gn_g"][...], w["gn_b"][...])
    y_b = ug * mscr[...]
    _tail(xn, y_a, y_b, p_ref[...], w, *outs)


def _mixer_body(*refs, n_prompt_tiles, tiles_per_seq):
    n_w = len(_WEIGHT_NAMES)
    xp_ref, pp_ref, xs_ref, ps_ref, cpad_ref = refs[:5]
    w = dict(zip(_WEIGHT_NAMES, refs[5:5 + n_w]))
    cw_ref, cwb_ref, ws_ref, bs_ref, convu_ref, gw_ref, b8_ref = refs[5 + n_w:12 + n_w]
    outs = refs[12 + n_w:16 + n_w]
    tail_ref, vch_ref, u_out_ref, v_out_ref = refs[16 + n_w:20 + n_w]
    ubuf, yslab, yscr, uscr, vscr, mscr = refs[20 + n_w:]
    step = pl.program_id(0)

    @pl.when(step < n_prompt_tiles)
    def _():
        _prompt_branch(lax.rem(step, tiles_per_seq), xp_ref, pp_ref, w, cwb_ref, ws_ref, bs_ref, outs, tail_ref,
                       vch_ref, ubuf, yslab, tiles_per_seq - 1)

    @pl.when(step >= n_prompt_tiles)
    def _():
        _sample_branch(xs_ref, ps_ref, cpad_ref, w, cw_ref, convu_ref, gw_ref, b8_ref, outs, u_out_ref, v_out_ref,
                       uscr, vscr, yscr, mscr)


def _rank_body(idx_ref, tri_ref, rank_ref, counts_ref, carry):
    i = pl.program_id(0)

    @pl.when(i == 0)
    def _():
        carry[...] = jnp.zeros(carry.shape, F32)

    idx = idx_ref[...]
    lane = lax.broadcasted_iota(jnp.int32, idx.shape, 1)
    hots = [lane == idx[:, k:k + 1] for k in range(TOP_K)]
    multi = jnp.zeros(idx.shape, F32)
    for hot in hots:
        multi = multi + hot.astype(F32)
    before = _dot(tri_ref[...], multi.astype(BF16)) + carry[...]
    rank = jnp.zeros(idx.shape, F32)
    for k, hot in enumerate(hots):
        rk = jnp.sum(jnp.where(hot, before, 0.0), axis=-1, keepdims=True)
        rank = jnp.where(lane == k, rk, rank)
    rank_ref[...] = rank.astype(jnp.int32)
    carry[...] = carry[...] + jnp.sum(multi, axis=0, keepdims=True)
    counts_ref[...] = carry[...].astype(jnp.int32)


def _pad_bits():
    b = TM // 2
    while b >= 1:
        yield b
        b //= 2


def _token_rows(t, n=1):
    return pl.ds(pl.multiple_of(t * SUBLANES, SUBLANES), n * SUBLANES)


def _dispatch_body(pad_off_ref, pad_n_ref, n_used_ref, dest_ref, h_ref, xs_ref, zbuf, sem, zsem, *, n_blocks):
    i = pl.program_id(0)

    def issue(r, carry):
        src = h_ref.at[_token_rows(r), :]
        for k in range(TOP_K):
            d = dest_ref[0, r * TOP_K + k]
            pltpu.make_async_copy(src, xs_ref.at[_token_rows(d), :], sem).start(priority=k % 2)
        return carry

    lax.fori_loop(0, T_ROWS, issue, 0, unroll=8)

    @pl.when(i == 0)
    def _():
        zbuf[...] = jnp.zeros(zbuf.shape, F32)

        def zero_copy(off, b):
            return pltpu.make_async_copy(zbuf.at[pl.ds(0, b * SUBLANES), :], xs_ref.at[_token_rows(off, b), :], zsem)

        def start_or_wait(cond, cp, wait):
            @pl.when(cond)
            def _():
                if wait:
                    cp.wait()
                else:
                    cp.start()

        for wait in (False, True):
            for e in range(N_EXPERTS):
                n = pad_n_ref[e]
                for b in _pad_bits():
                    start_or_wait((n & b) != 0, zero_copy(pad_off_ref[e] + (n & ~(2 * b - 1)), b), wait)

        half = TM // 2

        def tail_block(wait):
            def go(blk, carry):
                for s in range(TM // half):
                    cp = zero_copy(blk * TM + s * half, half)
                    cp.wait() if wait else cp.start()
                return carry
            return go

        lax.fori_loop(n_used_ref[0], n_blocks, tail_block(False), 0)
        lax.fori_loop(n_used_ref[0], n_blocks, tail_block(True), 0)

    for _ in range(TOP_K):
        pltpu.make_async_copy(h_ref, xs_ref.at[_token_rows(0, T_ROWS), :], sem).wait()


def _expert_body(blk_e_ref, first_ref, slot_ref, next_ref, n_used_ref, x_ref, bu_ref, bd_ref, wu_hbm, wd_hbm, y_ref,
                 wu_f32, wd_f32, wu_bf, wd_bf, sems):
    i = pl.program_id(0)

    def weight_copies(e, s):
        return (pltpu.make_async_copy(wu_hbm.at[e], wu_f32.at[s], sems.at[s]),
                pltpu.make_async_copy(wd_hbm.at[e], wd_f32.at[s], sems.at[s]))

    @pl.when(i < n_used_ref[0])
    def _():
        s = slot_ref[i]

        @pl.when(first_ref[i] == 1)
        def _():
            @pl.when(i == 0)
            def _():
                for cp in weight_copies(blk_e_ref[i], s):
                    cp.start()

            for cp in weight_copies(blk_e_ref[i], s):
                cp.wait()
            nxt = next_ref[i]

            @pl.when(nxt >= 0)
            def _():
                for cp in weight_copies(nxt, 1 - s):
                    cp.start()

            chunk = D_MODEL // SUBLANES

            def cast_rows(c, carry):
                r = pl.multiple_of(c * chunk, chunk)
                wu_bf[pl.ds(r, chunk), :] = wu_f32[s, pl.ds(r, chunk), :].astype(BF16)
                wd_bf[pl.ds(r, chunk), :] = wd_f32[s, pl.ds(r, chunk), :].astype(BF16)
                return carry

            lax.fori_loop(0, SUBLANES, cast_rows, 0)

        hcat = _dot(_load_token_tiles(x_ref, TM).astype(BF16), wu_bf[...]) + bu_ref[...]
        h_glu = jnp.minimum(hcat[:, :D_FF], SWIGLU_LIMIT)
        h_lin = jnp.clip(hcat[:, D_FF:], -SWIGLU_LIMIT, SWIGLU_LIMIT)
        act = h_glu * jax.nn.sigmoid(SWIGLU_ALPHA * h_glu) * (h_lin + 1.0)
        _store_token_tiles(y_ref, _dot(act.astype(BF16), wd_bf[...]) + bd_ref[...])

    @pl.when(pl.program_id(0) >= n_used_ref[0])
    def _():
        y_ref[...] = jnp.zeros(y_ref.shape, F32)


def _combine_body(dest_ref, dest_next_ref, y_ref, base_ref, gate_ref, g_ref, b_ref, outp_ref, outs_ref, ybuf, sems, *,
                  n_prompt_tiles, n_tiles):
    i = pl.program_id(0)

    def issue_tile(d_ref, slot):
        def issue(r, carry):
            for k in range(TOP_K):
                d = d_ref[0, r * TOP_K + k]
                pltpu.make_async_copy(y_ref.at[_token_rows(d), :], ybuf.at[slot * TOP_K + k, _token_rows(r), :],
                                      sems.at[slot]).start(priority=k % 2)
            return carry

        lax.fori_loop(0, T_ROWS, issue, 0, unroll=8)

    slot = lax.rem(i, 2)

    @pl.when(i == 0)
    def _():
        issue_tile(dest_ref, 0)

    @pl.when(i + 1 < n_tiles)
    def _():
        issue_tile(dest_next_ref, 1 - slot)

    for k in range(TOP_K):
        pltpu.make_async_copy(y_ref.at[_token_rows(0, T_ROWS), :], ybuf.at[slot * TOP_K + k], sems.at[slot]).wait()

    gate = gate_ref[...]
    acc = base_ref[...]
    for k in range(TOP_K):
        acc = acc + gate[:, k:k + 1] * _load_token_tiles(ybuf, T_ROWS, lead=(slot * TOP_K + k,))
    out = _layer_norm(acc, g_ref[...], b_ref[...])

    @pl.when(i < n_prompt_tiles)
    def _():
        outp_ref[...] = out

    @pl.when(i >= n_prompt_tiles)
    def _():
        outs_ref[...] = out


def _full(shape):
    return pl.BlockSpec(shape, lambda *_: (0,) * len(shape), pipeline_mode=pl.Buffered(1))


def kernel(x_prompt, x_sample, cache_conv, p_prompt, p_sample, ln_in_g, ln_in_b, w_in, conv_w, gn_g, gn_b, vn_g, vn_b,
           w_spatial, b_spatial, w_out, ln1_g, ln1_b, w_router, b_router, w_up, b_up, w_down, b_down, w_ple,
           w_ple_gate, ln2_g, ln2_b):
    batch, seq, _ = x_prompt.shape
    dec_batch, dec_seq, _ = x_sample.shape
    assert w_in.shape[0] == DEPTH and dec_seq == SUBLANES and seq % T_PROMPT == 0
    n_prompt = batch * seq
    n_sample = dec_batch * dec_seq
    n_tok = n_prompt + n_sample
    t_s = SEQS_PER_TILE * dec_seq
    assert n_prompt % T_ROWS == 0 and n_sample % T_ROWS == 0 and n_tok % T_RANK == 0 and n_sample % t_s == 0

    row = lambda a: a.reshape(1, -1).astype(F32)
    gidx = jnp.arange(MXU_DIM) // (C_CONV // N_CONV_GROUPS)
    gmat = jnp.where(gidx[:, None] == gidx[None, :], 1.0 / (C_CONV // N_CONV_GROUPS), 0.0).astype(BF16)
    wr_pad = jnp.pad(w_router[0].astype(F32), ((0, 0), (0, LANES - N_EXPERTS)))
    wr_hi = wr_pad.astype(BF16)
    wr_lo = (wr_pad - wr_hi.astype(F32)).astype(BF16)
    weights = dict(
        ln_in_g=row(ln_in_g), ln_in_b=row(ln_in_b), w_in=w_in[0].astype(BF16), vn_g=row(vn_g[0]), vn_b=row(vn_b[0]),
        gmat=gmat, gn_g=row(gn_g[0]), gn_b=row(gn_b[0]), w_out=w_out[0].astype(BF16), ln1_g=row(ln1_g[0]),
        ln1_b=row(ln1_b[0]), w_gate=w_ple_gate[0].astype(BF16), w_ple=w_ple[0].astype(BF16), wr_hi=wr_hi, wr_lo=wr_lo,
        b_r=jnp.pad(row(b_router[0]), ((0, 0), (0, LANES - N_EXPERTS))))
    w_list = [weights[n] for n in _WEIGHT_NAMES]
    w_specs = [_full(a.shape) for a in w_list]
    cw = jnp.pad(conv_w[0].astype(F32), ((0, HIST - CONV_W), (0, 0)))
    causal = jnp.tril(jnp.ones((CHUNK, CHUNK), bool))
    ws_m = jnp.where(causal[None], w_spatial[0], 0.0)
    ws_cat = jnp.concatenate([ws_m[0::2], ws_m[1::2]], axis=2).astype(BF16)
    bs_full = jnp.repeat(b_spatial[0].T.astype(F32), HEAD_DIM, axis=1)
    s_i = jnp.arange(SUBLANES)[:, None]
    t_i = jnp.arange(SUBLANES)[None, :]
    tap = jnp.clip(CONV_W - 1 - t_i + s_i, 0, CONV_W - 1)
    convu = jnp.where((s_i <= t_i)[:, :, None], conv_w[0].astype(F32)[tap], 0.0)
    gw8 = jnp.transpose(ws_m[:, :SUBLANES, :SUBLANES], (2, 1, 0))
    gw8 = jnp.repeat(gw8.astype(F32), HEAD_DIM, axis=2)
    b8 = bs_full[:SUBLANES]
    cpad = jnp.pad(cache_conv[0].astype(F32), ((0, 0), (HIST - (CONV_W - 1), SUBLANES), (0, 0)))

    cparams = lambda sem: pltpu.CompilerParams(dimension_semantics=sem, vmem_limit_bytes=VMEM_LIMIT)

    nj = seq // T_PROMPT
    npt = n_prompt // T_PROMPT
    assert t_s == T_PROMPT
    pstep = lambda i: jnp.minimum(i, npt - 1)
    sstep = lambda i: jnp.maximum(i - npt, 0)
    tok_block = lambda width: pl.BlockSpec((T_PROMPT, width), lambda i: (i, 0))
    cwb = jnp.broadcast_to(conv_w[0].astype(F32).reshape(CONV_W, C_CONV // LANES, 1, LANES),
                           (CONV_W, C_CONV // LANES, SUBLANES, LANES))
    tables = [cw, cwb, ws_cat, bs_full, convu, gw8, b8]
    h_all, base_all, idx_all, gate_all, u_tail, v_chunk, u_s, v_s = pl.pallas_call(
        functools.partial(_mixer_body, n_prompt_tiles=npt, tiles_per_seq=nj),
        grid=(npt + n_sample // t_s,),
        in_specs=[pl.BlockSpec((None, T_PROMPT, D_MODEL), lambda i: (pstep(i) // nj, pstep(i) % nj, 0)),
                  pl.BlockSpec((None, T_PROMPT, PLE_DIM), lambda i: (pstep(i) // nj, pstep(i) % nj, 0)),
                  pl.BlockSpec((t_s, D_MODEL), lambda i: (sstep(i), 0)),
                  pl.BlockSpec((t_s, PLE_DIM), lambda i: (sstep(i), 0)),
                  pl.BlockSpec((SEQS_PER_TILE,) + cpad.shape[1:], lambda i: (sstep(i), 0, 0))]
                 + w_specs + [_full(a.shape) for a in tables],
        out_specs=[pl.BlockSpec((T_PROMPT * SUBLANES, LANES), lambda i: (i, 0)),
                   tok_block(D_MODEL), tok_block(LANES), tok_block(LANES),
                   pl.BlockSpec((None, HIST, C_CONV), lambda i: (pstep(i) // nj, 0, 0)),
                   pl.BlockSpec((None, CHUNK, C_GMLP), lambda i: (pstep(i) // nj, 0, 0)),
                   pl.BlockSpec((t_s, C_CONV), lambda i: (sstep(i), 0)),
                   pl.BlockSpec((t_s, C_GMLP), lambda i: (sstep(i), 0))],
        out_shape=(jax.ShapeDtypeStruct((n_tok * SUBLANES, LANES), F32), jax.ShapeDtypeStruct((n_tok, D_MODEL), F32),
                   jax.ShapeDtypeStruct((n_tok, LANES), jnp.int32), jax.ShapeDtypeStruct((n_tok, LANES), F32),
                   jax.ShapeDtypeStruct((batch, HIST, C_CONV), F32), jax.ShapeDtypeStruct((batch, CHUNK, C_GMLP), F32),
                   jax.ShapeDtypeStruct((n_sample, C_CONV), F32), jax.ShapeDtypeStruct((n_sample, C_GMLP), F32)),
        scratch_shapes=[pltpu.VMEM((C_CONV // LANES, T_PROMPT + HIST, LANES), F32),
                        pltpu.VMEM((C_CONV // LANES, T_PROMPT, LANES), F32)]
                       + [pltpu.VMEM((T_PROMPT, C_CONV), F32)] * 4,
        compiler_params=pltpu.CompilerParams(dimension_semantics=("arbitrary",), vmem_limit_bytes=BIG_VMEM_LIMIT),
        name="mixer",
    )(x_prompt, p_prompt[0], x_sample.reshape(n_sample, D_MODEL), p_sample[0].reshape(n_sample, PLE_DIM), cpad,
      *w_list, *tables)

    tri = (jnp.arange(T_RANK)[:, None] > jnp.arange(T_RANK)[None, :]).astype(BF16)
    rank_all, counts = pl.pallas_call(
        _rank_body,
        grid=(n_tok // T_RANK,),
        in_specs=[pl.BlockSpec((T_RANK, LANES), lambda i: (i, 0)), _full(tri.shape)],
        out_specs=[pl.BlockSpec((T_RANK, LANES), lambda i: (i, 0)), _full((1, LANES))],
        out_shape=(jax.ShapeDtypeStruct((n_tok, LANES), jnp.int32), jax.ShapeDtypeStruct((1, LANES), jnp.int32)),
        scratch_shapes=[pltpu.VMEM((1, LANES), F32)],
        compiler_params=cparams(("arbitrary",)),
        name="rank",
    )(idx_all, tri)

    n_assign = n_tok * TOP_K
    n_blocks = n_assign // TM + N_EXPERTS
    cap = n_blocks * TM
    cnt = counts[0, :N_EXPERTS]
    padded = (cnt + TM - 1) // TM * TM
    p_end = jnp.cumsum(padded)
    p_start = p_end - padded
    e_ids = jnp.arange(N_EXPERTS, dtype=jnp.int32)
    lookup = lambda table, ids: jnp.sum(jnp.where(ids[..., None] == e_ids, table, 0), axis=-1)
    dest = (lookup(p_start, idx_all[:, :TOP_K]) + rank_all[:, :TOP_K]).astype(jnp.int32)
    dest_tiles = dest.reshape(n_tok // T_ROWS, 1, T_ROWS * TOP_K)
    blk_row = jnp.arange(n_blocks, dtype=jnp.int32) * TM
    blk_e = jnp.minimum(jnp.sum(p_end[None, :] <= blk_row[:, None], axis=1), N_EXPERTS - 1).astype(jnp.int32)
    n_used = (p_end[-1:] // TM).astype(jnp.int32)
    pad_off = (p_start + cnt).astype(jnp.int32)
    pad_n = (padded - cnt).astype(jnp.int32)
    used = cnt > 0
    slot_e = (jnp.cumsum(used.astype(jnp.int32)) - 1) & 1
    later_used = jnp.where(used[None, :] & (e_ids[None, :] > e_ids[:, None]), e_ids[None, :], N_EXPERTS)
    next_e = jnp.min(later_used, axis=1)
    next_e = jnp.where(next_e < N_EXPERTS, next_e, -1).astype(jnp.int32)
    blk_first = ((blk_row == lookup(p_start, blk_e)) & (blk_row < p_end[-1])).astype(jnp.int32)
    blk_slot = lookup(slot_e, blk_e).astype(jnp.int32)
    blk_next = lookup(next_e, blk_e).astype(jnp.int32)

    dest_spec = pl.BlockSpec((None, 1, T_ROWS * TOP_K), lambda i, *_: (i, 0, 0), memory_space=pltpu.SMEM)

    x_sorted = pl.pallas_call(
        functools.partial(_dispatch_body, n_blocks=n_blocks),
        grid_spec=pltpu.PrefetchScalarGridSpec(
            num_scalar_prefetch=3,
            grid=(n_tok // T_ROWS,),
            in_specs=[dest_spec, pl.BlockSpec((T_ROWS * SUBLANES, LANES), lambda i, *_: (i, 0))],
            out_specs=pl.BlockSpec(memory_space=pl.ANY),
            scratch_shapes=[pltpu.VMEM((TM // 2 * SUBLANES, LANES), F32), pltpu.SemaphoreType.DMA,
                            pltpu.SemaphoreType.DMA]),
        out_shape=jax.ShapeDtypeStruct((cap * SUBLANES, LANES), F32),
        compiler_params=cparams(("arbitrary",)),
        name="dispatch",
    )(pad_off, pad_n, n_used, dest_tiles, h_all)

    last = lambda i, nu: jnp.minimum(i, nu[0] - 1)
    y_sorted = pl.pallas_call(
        _expert_body,
        grid_spec=pltpu.PrefetchScalarGridSpec(
            num_scalar_prefetch=5,
            grid=(n_blocks,),
            in_specs=[pl.BlockSpec((TM * SUBLANES, LANES), lambda i, be, bf, bs, bn, nu: (last(i, nu), 0)),
                      pl.BlockSpec((None, 1, 2 * D_FF), lambda i, be, bf, bs, bn, nu: (be[last(i, nu)], 0, 0)),
                      pl.BlockSpec((None, 1, D_MODEL), lambda i, be, bf, bs, bn, nu: (be[last(i, nu)], 0, 0)),
                      pl.BlockSpec(memory_space=pl.ANY), pl.BlockSpec(memory_space=pl.ANY)],
            out_specs=pl.BlockSpec((TM * SUBLANES, LANES), lambda i, be, bf, bs, bn, nu: (i, 0)),
            scratch_shapes=[pltpu.VMEM((2, D_MODEL, 2 * D_FF), F32), pltpu.VMEM((2, D_FF, D_MODEL), F32),
                            pltpu.VMEM((D_MODEL, 2 * D_FF), BF16), pltpu.VMEM((D_FF, D_MODEL), BF16),
                            pltpu.SemaphoreType.DMA((2,))]),
        out_shape=jax.ShapeDtypeStruct((cap * SUBLANES, LANES), F32),
        compiler_params=pltpu.CompilerParams(dimension_semantics=("arbitrary",), vmem_limit_bytes=BIG_VMEM_LIMIT),
        name="experts",
    )(blk_e, blk_first, blk_slot, blk_next, n_used, x_sorted, b_up[0].astype(F32)[:, None, :],
      b_down[0].astype(F32)[:, None, :], w_up[0].astype(F32), w_down[0].astype(F32))

    npt = n_prompt // T_ROWS
    n_tiles = n_tok // T_ROWS
    dest_next_spec = pl.BlockSpec((None, 1, T_ROWS * TOP_K), lambda i: (jnp.minimum(i + 1, n_tiles - 1), 0, 0),
                                  memory_space=pltpu.SMEM)
    out_p, out_s = pl.pallas_call(
        functools.partial(_combine_body, n_prompt_tiles=npt, n_tiles=n_tiles),
        grid=(n_tiles,),
        in_specs=[dest_spec, dest_next_spec, pl.BlockSpec(memory_space=pl.ANY),
                  pl.BlockSpec((T_ROWS, D_MODEL), lambda i: (i, 0)), pl.BlockSpec((T_ROWS, LANES), lambda i: (i, 0)),
                  _full((1, D_MODEL)), _full((1, D_MODEL))],
        out_specs=[pl.BlockSpec((T_ROWS, D_MODEL), lambda i: (jnp.minimum(i, npt - 1), 0)),
                   pl.BlockSpec((T_ROWS, D_MODEL), lambda i: (jnp.maximum(i - npt, 0), 0))],
        out_shape=(jax.ShapeDtypeStruct((n_prompt, D_MODEL), F32), jax.ShapeDtypeStruct((n_sample, D_MODEL), F32)),
        scratch_shapes=[pltpu.VMEM((2 * TOP_K, T_ROWS * SUBLANES, LANES), F32), pltpu.SemaphoreType.DMA((2,))],
        compiler_params=cparams(("arbitrary",)),
        name="combine",
    )(dest_tiles, dest_tiles, y_sorted, base_all, gate_all, row(ln2_g[0]), row(ln2_b[0]))

    y_prompt = out_p.reshape(batch, seq, D_MODEL)
    y_sample = out_s.reshape(dec_batch, dec_seq, D_MODEL)
    conv_state_prompt = u_tail[None, :, HIST - (CONV_W - 1):, :]
    u_s3 = u_s.reshape(dec_batch, dec_seq, C_CONV)
    conv_state_sample = jnp.concatenate([cache_conv[0][:, dec_seq:, :].astype(F32), u_s3], axis=1)[None]
    chunk_v_prompt = v_chunk[None]
    chunk_v_sample = v_s.reshape(1, dec_batch, dec_seq, C_GMLP)
    return (y_prompt, y_sample, conv_state_prompt, conv_state_sample, chunk_v_prompt, chunk_v_sample)
```

```python
import functools

import jax
import jax.numpy as jnp
from jax import lax
from jax.experimental import pallas as pl
from jax.experimental.pallas import tpu as pltpu

F32 = jnp.float32
BF16 = jnp.bfloat16

D_MODEL = 1024
C_CONV = 512
C_GMLP = 512
N_CONV_GROUPS = 8
CONV_W = 31
N_HEADS = 8
HEAD_DIM = C_GMLP // N_HEADS
CHUNK = 128
N_EXPERTS = 32
TOP_K = 4
D_FF = 1024
PLE_DIM = 256
SWIGLU_LIMIT = 7.0
SWIGLU_ALPHA = 1.702
LN_EPS = 1e-5
DEPTH = 1
DEEPNORM_ALPHA = (2.0 * DEPTH) ** 0.25

LANES = 128
SUBLANES = 8
MXU_DIM = 256
VMEM_LIMIT = 48 * 1024 * 1024
BIG_VMEM_LIMIT = 56 * 1024 * 1024

T_PROMPT = 512
PROMPT_SPLIT = 2
HIST = 32
CONV_STRIDE = 4
SEQS_PER_TILE = 64
SEQ_CHUNK = 4
T_RANK = 512
T_DISPATCH = 1024
T_ROWS = 256
TM = 256


def _dot(a, b):
    return jnp.dot(a, b, preferred_element_type=F32)


def _layer_norm(x, g, b):
    mu = jnp.mean(x, axis=-1, keepdims=True)
    xc = x - mu
    var = jnp.mean(xc * xc, axis=-1, keepdims=True)
    return xc * lax.rsqrt(var + LN_EPS) * g + b


def _split_bf16(a):
    hi = a.astype(BF16)
    lo = (a - hi.astype(F32)).astype(BF16)
    return hi, lo


def _group_mean(a, gmat_ref):
    hi, lo = _split_bf16(a)
    g = gmat_ref[...]
    outs = []
    for s in range(C_CONV // MXU_DIM):
        sl = slice(MXU_DIM * s, MXU_DIM * (s + 1))
        outs.append(_dot(hi[:, sl], g) + _dot(lo[:, sl], g))
    return jnp.concatenate(outs, axis=1)


def _group_norm_silu(y, gmat_ref, gn_g, gn_b):
    mu = _group_mean(y, gmat_ref)
    yc = y - mu
    var = _group_mean(yc * yc, gmat_ref)
    yn = yc * lax.rsqrt(var + LN_EPS) * gn_g + gn_b
    return yn * jax.nn.sigmoid(yn)


def _store_token_tiles(ref, val, tok0=0):
    n = val.shape[0]
    for c in range(D_MODEL // LANES):
        ref[pl.ds(tok0 * SUBLANES + c, n, stride=SUBLANES), :] = val[:, c * LANES:(c + 1) * LANES]


def _load_token_tiles(ref, n, lead=()):
    parts = [ref[lead + (pl.ds(c, n, stride=SUBLANES), slice(None))] for c in range(D_MODEL // LANES)]
    return jnp.concatenate(parts, axis=1)


def _front(x, w):
    xn = _layer_norm(x, w["ln_in_g"][...], w["ln_in_b"][...])
    z = _dot(xn.astype(BF16), w["w_in"][...])
    a_val = z[:, 0:C_CONV]
    a_gate = z[:, C_CONV:2 * C_CONV]
    g_u = z[:, 2 * C_CONV:2 * C_CONV + C_GMLP]
    g_v = z[:, 2 * C_CONV + C_GMLP:]
    u = a_val * jax.nn.sigmoid(a_gate)
    ug = jax.nn.gelu(g_u)
    v = _layer_norm(jax.nn.gelu(g_v), w["vn_g"][...], w["vn_b"][...])
    return xn, u, ug, v


def _tail(xn, y_a, y_b, p, w, h_ref, base_ref, idx_ref, gate_ref, row0=0):
    rows = slice(row0, row0 + xn.shape[0])
    mix = _dot(y_a.astype(BF16), w["w_out"][0:C_CONV, :]) + _dot(y_b.astype(BF16), w["w_out"][C_CONV:, :])
    h = _layer_norm(DEEPNORM_ALPHA * xn + mix, w["ln1_g"][...], w["ln1_b"][...])
    hb, h_lo = _split_bf16(h)
    ple = _dot(p.astype(BF16), w["w_ple"][...]) * jax.nn.sigmoid(_dot(hb, w["w_gate"][...]))
    _store_token_tiles(h_ref, h, row0)
    base_ref[rows, :] = DEEPNORM_ALPHA * h + ple

    wr_hi = w["wr_hi"][...]
    logits = _dot(hb, wr_hi) + _dot(h_lo, wr_hi) + _dot(hb, w["wr_lo"][...]) + w["b_r"][...]
    lane = lax.broadcasted_iota(jnp.int32, logits.shape, 1)
    lane_f = lane.astype(F32)
    vals = jnp.where(lane < N_EXPERTS, logits, -jnp.inf)
    tops, ids = [], []
    for _ in range(TOP_K):
        m = jnp.max(vals, axis=-1, keepdims=True)
        i = jnp.min(jnp.where(vals == m, lane_f, float(LANES)), axis=-1, keepdims=True)
        vals = jnp.where(lane_f == i, -jnp.inf, vals)
        tops.append(m)
        ids.append(i)
    exps = [jnp.exp(m - tops[0]) for m in tops]
    denom = exps[0] + exps[1] + exps[2] + exps[3]
    idx_out = jnp.zeros(logits.shape, F32)
    gate_out = jnp.zeros(logits.shape, F32)
    for k in range(TOP_K):
        idx_out = jnp.where(lane == k, ids[k], idx_out)
        gate_out = jnp.where(lane == k, exps[k] / denom, gate_out)
    idx_ref[rows, :] = idx_out.astype(jnp.int32)
    gate_ref[rows, :] = gate_out


_WEIGHT_NAMES = ("ln_in_g", "ln_in_b", "w_in", "vn_g", "vn_b", "gmat", "gn_g", "gn_b", "w_out", "ln1_g", "ln1_b",
                 "w_gate", "w_ple", "wr_hi", "wr_lo", "b_r")


def _prompt_branch(j, x_ref, p_ref, w, cwb_ref, ws_ref, bs_ref, outs, tail_ref, vch_ref, ubuf, yslab, last_j):
    t = T_PROMPT
    n = t // PROMPT_SPLIT
    n_slabs = C_CONV // LANES

    @pl.when(j == 0)
    def _():
        ubuf[:, 0:HIST, :] = jnp.zeros((n_slabs, HIST, LANES), F32)

    @pl.when(j > 0)
    def _():
        ubuf[:, 0:HIST, :] = ubuf[:, t:t + HIST, :]

    def front(h):
        xn, u, ug, v = _front(x_ref[h * n:(h + 1) * n, :], w)
        for s in range(n_slabs):
            ubuf[s, HIST + h * n:HIST + (h + 1) * n, :] = u[:, s * LANES:(s + 1) * LANES]
        return xn, u, ug, v

    def conv(h):
        rows = CONV_STRIDE * SUBLANES
        first = HIST - (CONV_W - 1)
        for s in range(n_slabs):
            for c in range(h * n // rows, (h + 1) * n // rows):
                accs = [None] * CONV_STRIDE
                for shift in range(CONV_STRIDE + CONV_W - 1):
                    win = ubuf[s, pl.ds(first + c * rows + shift, SUBLANES, stride=CONV_STRIDE), :]
                    for ph in range(CONV_STRIDE):
                        k = shift - ph
                        if 0 <= k < CONV_W:
                            term = cwb_ref[k, s] * win
                            accs[ph] = term if accs[ph] is None else accs[ph] + term
                for ph in range(CONV_STRIDE):
                    yslab[s, pl.ds(c * rows + ph, SUBLANES, stride=CONV_STRIDE), :] = accs[ph]
        y_conv = jnp.concatenate([yslab[s, h * n:(h + 1) * n, :] for s in range(n_slabs)], axis=1)
        return _group_norm_silu(y_conv, w["gmat"], w["gn_g"][...], w["gn_b"][...])

    def spatial_gate(ug, v):
        lane = lax.broadcasted_iota(jnp.int32, (CHUNK, LANES), 1)
        mixed_chunks = []
        for c in range(n // CHUNK):
            vc = v[c * CHUNK:(c + 1) * CHUNK, :]
            parts = []
            for q in range(N_HEADS // 2):
                vp = vc[:, q * LANES:(q + 1) * LANES]
                rhs = jnp.concatenate([jnp.where(lane < HEAD_DIM, vp, 0.0), jnp.where(lane >= HEAD_DIM, vp, 0.0)],
                                      axis=0).astype(BF16)
                parts.append(_dot(ws_ref[q], rhs))
            mixed_chunks.append(jnp.concatenate(parts, axis=1) + bs_ref[...])
        return ug * jnp.concatenate(mixed_chunks, axis=0)

    def tail(h, f, y_a, y_b):
        _tail(f[0], y_a, y_b, p_ref[h * n:(h + 1) * n, :], w, *outs, row0=h * n)

    fronts = [front(0)]
    y_as = {}
    for h in range(PROMPT_SPLIT):
        if h + 1 < PROMPT_SPLIT:
            fronts.append(front(h + 1))
        y_as[h] = conv(h)
        if h > 0:
            tail(h - 1, fronts[h - 1], y_as.pop(h - 1), spatial_gate(fronts[h - 1][2], fronts[h - 1][3]))
    last = PROMPT_SPLIT - 1
    tail(last, fronts[last], y_as.pop(last), spatial_gate(fronts[last][2], fronts[last][3]))

    @pl.when(j == last_j)
    def _():
        tail_ref[...] = fronts[last][1][n - HIST:, :]
        vch_ref[...] = fronts[last][3][n - CHUNK:, :]


def _sample_branch(x_ref, p_ref, cpad_ref, w, cw_ref, convu_ref, gw_ref, b8_ref, outs, u_out_ref, v_out_ref,
                   uscr, vscr, yscr, mscr):
    xn, u, ug, v = _front(x_ref[...], w)
    u_out_ref[...] = u
    v_out_ref[...] = v
    uscr[...] = u
    vscr[...] = v
    first = HIST - (CONV_W - 1)
    rows = SEQ_CHUNK * SUBLANES

    def step(c, carry):
        r0 = pl.multiple_of(c * rows, rows)
        s0 = c * SEQ_CHUNK
        u3 = uscr[pl.ds(r0, rows), :].reshape(SEQ_CHUNK, SUBLANES, C_CONV)
        v3 = vscr[pl.ds(r0, rows), :].reshape(SEQ_CHUNK, SUBLANES, C_GMLP)
        acc = jnp.zeros((SEQ_CHUNK, SUBLANES, C_CONV), F32)
        for k in range(CONV_W):
            acc = acc + cw_ref[pl.ds(k, 1), :][None] * cpad_ref[pl.ds(s0, SEQ_CHUNK), pl.ds(first + k, SUBLANES), :]
        mix = jnp.zeros((SEQ_CHUNK, SUBLANES, C_GMLP), F32) + b8_ref[...][None]
        for s in range(SUBLANES):
            acc = acc + u3[:, s:s + 1, :] * convu_ref[s][None]
            mix = mix + v3[:, s:s + 1, :] * gw_ref[s][None]
        yscr[pl.ds(r0, rows), :] = acc.reshape(rows, C_CONV)
        mscr[pl.ds(r0, rows), :] = mix.reshape(rows, C_GMLP)
        return carry

    lax.fori_loop(0, SEQS_PER_TILE // SEQ_CHUNK, step, 0)
    y_a = _group_norm_silu(yscr[...], w["gmat"], w["gn_g"][...], w["gn_b"][...])
    y_b = ug * mscr[...]
    _tail(xn, y_a, y_b, p_ref[...], w, *outs)


def _mixer_body(*refs, n_prompt_tiles, tiles_per_seq):
    n_w = len(_WEIGHT_NAMES)
    xp_ref, pp_ref, xs_ref, ps_ref, cpad_ref = refs[:5]
    w = dict(zip(_WEIGHT_NAMES, refs[5:5 + n_w]))
    cw_ref, cwb_ref, ws_ref, bs_ref, convu_ref, gw_ref, b8_ref = refs[5 + n_w:12 + n_w]
    outs = refs[12 + n_w:16 + n_w]
    tail_ref, vch_ref, u_out_ref, v_out_ref = refs[16 + n_w:20 + n_w]
    ubuf, yslab, yscr, uscr, vscr, mscr = refs[20 + n_w:]
    step = pl.program_id(0)

    @pl.when(step < n_prompt_tiles)
    def _():
        _prompt_branch(lax.rem(step, tiles_per_seq), xp_ref, pp_ref, w, cwb_ref, ws_ref, bs_ref, outs, tail_ref,
                       vch_ref, ubuf, yslab, tiles_per_seq - 1)

    @pl.when(step >= n_prompt_tiles)
    def _():
        _sample_branch(xs_ref, ps_ref, cpad_ref, w, cw_ref, convu_ref, gw_ref, b8_ref, outs, u_out_ref, v_out_ref,
                       uscr, vscr, yscr, mscr)


def _rank_body(idx_ref, tri_ref, rank_ref, counts_ref, carry):
    i = pl.program_id(0)

    @pl.when(i == 0)
    def _():
        carry[...] = jnp.zeros(carry.shape, F32)

    idx = idx_ref[...]
    lane = lax.broadcasted_iota(jnp.int32, idx.shape, 1)
    hots = [lane == idx[:, k:k + 1] for k in range(TOP_K)]
    multi = jnp.zeros(idx.shape, F32)
    for hot in hots:
        multi = multi + hot.astype(F32)
    before = _dot(tri_ref[...], multi.astype(BF16)) + carry[...]
    rank = jnp.zeros(idx.shape, F32)
    for k, hot in enumerate(hots):
        rk = jnp.sum(jnp.where(hot, before, 0.0), axis=-1, keepdims=True)
        rank = jnp.where(lane == k, rk, rank)
    rank_ref[...] = rank.astype(jnp.int32)
    carry[...] = carry[...] + jnp.sum(multi, axis=0, keepdims=True)
    counts_ref[...] = carry[...].astype(jnp.int32)


def _pad_bits():
    b = TM // 2
    while b >= 1:
        yield b
        b //= 2


def _token_rows(t, n=1):
    return pl.ds(pl.multiple_of(t * SUBLANES, SUBLANES), n * SUBLANES)


def _dispatch_body(pad_off_ref, pad_n_ref, n_used_ref, dest_ref, h_ref, xs_ref, zbuf, sem, zsem, *, n_blocks):
    i = pl.program_id(0)

    def issue(r, carry):
        src = h_ref.at[_token_rows(r), :]
        for k in range(TOP_K):
            d = dest_ref[0, r * TOP_K + k]
            pltpu.make_async_copy(src, xs_ref.at[_token_rows(d), :], sem).start(priority=k % 2)
        return carry

    lax.fori_loop(0, T_DISPATCH, issue, 0, unroll=8)

    @pl.when(i == 0)
    def _():
        zbuf[...] = jnp.zeros(zbuf.shape, F32)

        def zero_copy(off, b):
            return pltpu.make_async_copy(zbuf.at[pl.ds(0, b * SUBLANES), :], xs_ref.at[_token_rows(off, b), :], zsem)

        def start_or_wait(cond, cp, wait):
            @pl.when(cond)
            def _():
                if wait:
                    cp.wait()
                else:
                    cp.start()

        for wait in (False, True):
            for e in range(N_EXPERTS):
                n = pad_n_ref[e]
                for b in _pad_bits():
                    start_or_wait((n & b) != 0, zero_copy(pad_off_ref[e] + (n & ~(2 * b - 1)), b), wait)

        half = TM // 2

        def tail_block(wait):
            def go(blk, carry):
                for s in range(TM // half):
                    cp = zero_copy(blk * TM + s * half, half)
                    cp.wait() if wait else cp.start()
                return carry
            return go

        lax.fori_loop(n_used_ref[0], n_blocks, tail_block(False), 0)
        lax.fori_loop(n_used_ref[0], n_blocks, tail_block(True), 0)

    for _ in range(TOP_K):
        pltpu.make_async_copy(h_ref, xs_ref.at[_token_rows(0, T_DISPATCH), :], sem).wait()


def _expert_body(blk_e_ref, first_ref, slot_ref, next_ref, n_used_ref, x_ref, bu_ref, bd_ref, wu_hbm, wd_hbm, y_ref,
                 wu_f32, wd_f32, wu_bf, wd_bf, sems):
    i = pl.program_id(0)

    def weight_copies(e, s):
        return (pltpu.make_async_copy(wu_hbm.at[e], wu_f32.at[s], sems.at[s]),
                pltpu.make_async_copy(wd_hbm.at[e], wd_f32.at[s], sems.at[s]))

    @pl.when(i < n_used_ref[0])
    def _():
        s = slot_ref[i]

        @pl.when(first_ref[i] == 1)
        def _():
            @pl.when(i == 0)
            def _():
                for cp in weight_copies(blk_e_ref[i], s):
                    cp.start()

            for cp in weight_copies(blk_e_ref[i], s):
                cp.wait()
            nxt = next_ref[i]

            @pl.when(nxt >= 0)
            def _():
                for cp in weight_copies(nxt, 1 - s):
                    cp.start()

            chunk = D_MODEL // SUBLANES

            def cast_rows(c, carry):
                r = pl.multiple_of(c * chunk, chunk)
                wu_bf[pl.ds(r, chunk), :] = wu_f32[s, pl.ds(r, chunk), :].astype(BF16)
                wd_bf[pl.ds(r, chunk), :] = wd_f32[s, pl.ds(r, chunk), :].astype(BF16)
                return carry

            lax.fori_loop(0, SUBLANES, cast_rows, 0)

        hcat = _dot(_load_token_tiles(x_ref, TM).astype(BF16), wu_bf[...]) + bu_ref[...]
        h_glu = jnp.minimum(hcat[:, :D_FF], SWIGLU_LIMIT)
        h_lin = jnp.clip(hcat[:, D_FF:], -SWIGLU_LIMIT, SWIGLU_LIMIT)
        act = h_glu * jax.nn.sigmoid(SWIGLU_ALPHA * h_glu) * (h_lin + 1.0)
        _store_token_tiles(y_ref, _dot(act.astype(BF16), wd_bf[...]) + bd_ref[...])

    @pl.when(pl.program_id(0) >= n_used_ref[0])
    def _():
        y_ref[...] = jnp.zeros(y_ref.shape, F32)


def _combine_body(dest_ref, dest_next_ref, y_ref, base_ref, gate_ref, g_ref, b_ref, outp_ref, outs_ref, ybuf, sems, *,
                  n_prompt_tiles, n_tiles):
    i = pl.program_id(0)

    def issue_tile(d_ref, slot):
        def issue(r, carry):
            for k in range(TOP_K):
                d = d_ref[0, r * TOP_K + k]
                pltpu.make_async_copy(y_ref.at[_token_rows(d), :], ybuf.at[slot * TOP_K + k, _token_rows(r), :],
                                      sems.at[slot]).start(priority=k % 2)
            return carry

        lax.fori_loop(0, T_ROWS, issue, 0, unroll=8)

    slot = lax.rem(i, 2)

    @pl.when(i == 0)
    def _():
        issue_tile(dest_ref, 0)

    @pl.when(i + 1 < n_tiles)
    def _():
        issue_tile(dest_next_ref, 1 - slot)

    for k in range(TOP_K):
        pltpu.make_async_copy(y_ref.at[_token_rows(0, T_ROWS), :], ybuf.at[slot * TOP_K + k], sems.at[slot]).wait()

    gate = gate_ref[...]
    acc = base_ref[...]
    for k in range(TOP_K):
        acc = acc + gate[:, k:k + 1] * _load_token_tiles(ybuf, T_ROWS, lead=(slot * TOP_K + k,))
    out = _layer_norm(acc, g_ref[...], b_ref[...])

    @pl.when(i < n_prompt_tiles)
    def _():
        outp_ref[...] = out

    @pl.when(i >= n_prompt_tiles)
    def _():
        outs_ref[...] = out


def _full(shape):
    return pl.BlockSpec(shape, lambda *_: (0,) * len(shape), pipeline_mode=pl.Buffered(1))


def kernel(x_prompt, x_sample, cache_conv, p_prompt, p_sample, ln_in_g, ln_in_b, w_in, conv_w, gn_g, gn_b, vn_g, vn_b,
           w_spatial, b_spatial, w_out, ln1_g, ln1_b, w_router, b_router, w_up, b_up, w_down, b_down, w_ple,
           w_ple_gate, ln2_g, ln2_b):
    batch, seq, _ = x_prompt.shape
    dec_batch, dec_seq, _ = x_sample.shape
    assert w_in.shape[0] == DEPTH and dec_seq == SUBLANES and seq % T_PROMPT == 0
    n_prompt = batch * seq
    n_sample = dec_batch * dec_seq
    n_tok = n_prompt + n_sample
    t_s = SEQS_PER_TILE * dec_seq
    assert n_prompt % T_ROWS == 0 and n_sample % T_ROWS == 0 and n_tok % T_RANK == 0 and n_sample % t_s == 0
    assert n_tok % T_DISPATCH == 0

    row = lambda a: a.reshape(1, -1).astype(F32)
    gidx = jnp.arange(MXU_DIM) // (C_CONV // N_CONV_GROUPS)
    gmat = jnp.where(gidx[:, None] == gidx[None, :], 1.0 / (C_CONV // N_CONV_GROUPS), 0.0).astype(BF16)
    wr_pad = jnp.pad(w_router[0].astype(F32), ((0, 0), (0, LANES - N_EXPERTS)))
    wr_hi = wr_pad.astype(BF16)
    wr_lo = (wr_pad - wr_hi.astype(F32)).astype(BF16)
    weights = dict(
        ln_in_g=row(ln_in_g), ln_in_b=row(ln_in_b), w_in=w_in[0].astype(BF16), vn_g=row(vn_g[0]), vn_b=row(vn_b[0]),
        gmat=gmat, gn_g=row(gn_g[0]), gn_b=row(gn_b[0]), w_out=w_out[0].astype(BF16), ln1_g=row(ln1_g[0]),
        ln1_b=row(ln1_b[0]), w_gate=w_ple_gate[0].astype(BF16), w_ple=w_ple[0].astype(BF16), wr_hi=wr_hi, wr_lo=wr_lo,
        b_r=jnp.pad(row(b_router[0]), ((0, 0), (0, LANES - N_EXPERTS))))
    w_list = [weights[n] for n in _WEIGHT_NAMES]
    w_specs = [_full(a.shape) for a in w_list]
    cw = jnp.pad(conv_w[0].astype(F32), ((0, HIST - CONV_W), (0, 0)))
    causal = jnp.tril(jnp.ones((CHUNK, CHUNK), bool))
    ws_m = jnp.where(causal[None], w_spatial[0], 0.0)
    ws_cat = jnp.concatenate([ws_m[0::2], ws_m[1::2]], axis=2).astype(BF16)
    bs_full = jnp.repeat(b_spatial[0].T.astype(F32), HEAD_DIM, axis=1)
    s_i = jnp.arange(SUBLANES)[:, None]
    t_i = jnp.arange(SUBLANES)[None, :]
    tap = jnp.clip(CONV_W - 1 - t_i + s_i, 0, CONV_W - 1)
    convu = jnp.where((s_i <= t_i)[:, :, None], conv_w[0].astype(F32)[tap], 0.0)
    gw8 = jnp.transpose(ws_m[:, :SUBLANES, :SUBLANES], (2, 1, 0))
    gw8 = jnp.repeat(gw8.astype(F32), HEAD_DIM, axis=2)
    b8 = bs_full[:SUBLANES]
    cpad = jnp.pad(cache_conv[0].astype(F32), ((0, 0), (HIST - (CONV_W - 1), SUBLANES), (0, 0)))

    cparams = lambda sem: pltpu.CompilerParams(dimension_semantics=sem, vmem_limit_bytes=VMEM_LIMIT)

    nj = seq // T_PROMPT
    npt = n_prompt // T_PROMPT
    assert t_s == T_PROMPT
    pstep = lambda i: jnp.minimum(i, npt - 1)
    sstep = lambda i: jnp.maximum(i - npt, 0)
    tok_block = lambda width: pl.BlockSpec((T_PROMPT, width), lambda i: (i, 0))
    cwb = jnp.broadcast_to(conv_w[0].astype(F32).reshape(CONV_W, C_CONV // LANES, 1, LANES),
                           (CONV_W, C_CONV // LANES, SUBLANES, LANES))
    tables = [cw, cwb, ws_cat, bs_full, convu, gw8, b8]
    h_all, base_all, idx_all, gate_all, u_tail, v_chunk, u_s, v_s = pl.pallas_call(
        functools.partial(_mixer_body, n_prompt_tiles=npt, tiles_per_seq=nj),
        grid=(npt + n_sample // t_s,),
        in_specs=[pl.BlockSpec((None, T_PROMPT, D_MODEL), lambda i: (pstep(i) // nj, pstep(i) % nj, 0)),
                  pl.BlockSpec((None, T_PROMPT, PLE_DIM), lambda i: (pstep(i) // nj, pstep(i) % nj, 0)),
                  pl.BlockSpec((t_s, D_MODEL), lambda i: (sstep(i), 0)),
                  pl.BlockSpec((t_s, PLE_DIM), lambda i: (sstep(i), 0)),
                  pl.BlockSpec((SEQS_PER_TILE,) + cpad.shape[1:], lambda i: (sstep(i), 0, 0),
                               pipeline_mode=pl.Buffered(1))]
                 + w_specs + [_full(a.shape) for a in tables],
        out_specs=[pl.BlockSpec((T_PROMPT * SUBLANES, LANES), lambda i: (i, 0)),
                   tok_block(D_MODEL), tok_block(LANES), tok_block(LANES),
                   pl.BlockSpec((None, HIST, C_CONV), lambda i: (pstep(i) // nj, 0, 0)),
                   pl.BlockSpec((None, CHUNK, C_GMLP), lambda i: (pstep(i) // nj, 0, 0)),
                   pl.BlockSpec((t_s, C_CONV), lambda i: (sstep(i), 0)),
                   pl.BlockSpec((t_s, C_GMLP), lambda i: (sstep(i), 0))],
        out_shape=(jax.ShapeDtypeStruct((n_tok * SUBLANES, LANES), F32), jax.ShapeDtypeStruct((n_tok, D_MODEL), F32),
                   jax.ShapeDtypeStruct((n_tok, LANES), jnp.int32), jax.ShapeDtypeStruct((n_tok, LANES), F32),
                   jax.ShapeDtypeStruct((batch, HIST, C_CONV), F32), jax.ShapeDtypeStruct((batch, CHUNK, C_GMLP), F32),
                   jax.ShapeDtypeStruct((n_sample, C_CONV), F32), jax.ShapeDtypeStruct((n_sample, C_GMLP), F32)),
        scratch_shapes=[pltpu.VMEM((C_CONV // LANES, T_PROMPT + HIST, LANES), F32),
                        pltpu.VMEM((C_CONV // LANES, T_PROMPT, LANES), F32)]
                       + [pltpu.VMEM((T_PROMPT, C_CONV), F32)] * 4,
        compiler_params=pltpu.CompilerParams(dimension_semantics=("arbitrary",), vmem_limit_bytes=BIG_VMEM_LIMIT),
        name="mixer",
    )(x_prompt, p_prompt[0], x_sample.reshape(n_sample, D_MODEL), p_sample[0].reshape(n_sample, PLE_DIM), cpad,
      *w_list, *tables)

    tri = (jnp.arange(T_RANK)[:, None] > jnp.arange(T_RANK)[None, :]).astype(BF16)
    rank_all, counts = pl.pallas_call(
        _rank_body,
        grid=(n_tok // T_RANK,),
        in_specs=[pl.BlockSpec((T_RANK, LANES), lambda i: (i, 0)), _full(tri.shape)],
        out_specs=[pl.BlockSpec((T_RANK, LANES), lambda i: (i, 0)), _full((1, LANES))],
        out_shape=(jax.ShapeDtypeStruct((n_tok, LANES), jnp.int32), jax.ShapeDtypeStruct((1, LANES), jnp.int32)),
        scratch_shapes=[pltpu.VMEM((1, LANES), F32)],
        compiler_params=cparams(("arbitrary",)),
        name="rank",
    )(idx_all, tri)

    n_assign = n_tok * TOP_K
    n_blocks = n_assign // TM + N_EXPERTS
    cap = n_blocks * TM
    cnt = counts[0, :N_EXPERTS]
    padded = (cnt + TM - 1) // TM * TM
    p_end = jnp.cumsum(padded)
    p_start = p_end - padded
    e_ids = jnp.arange(N_EXPERTS, dtype=jnp.int32)
    lookup = lambda table, ids: jnp.sum(jnp.where(ids[..., None] == e_ids, table, 0), axis=-1)
    dest = (lookup(p_start, idx_all[:, :TOP_K]) + rank_all[:, :TOP_K]).astype(jnp.int32)
    dest_tiles = dest.reshape(n_tok // T_ROWS, 1, T_ROWS * TOP_K)
    blk_row = jnp.arange(n_blocks, dtype=jnp.int32) * TM
    blk_e = jnp.minimum(jnp.sum(p_end[None, :] <= blk_row[:, None], axis=1), N_EXPERTS - 1).astype(jnp.int32)
    n_used = (p_end[-1:] // TM).astype(jnp.int32)
    pad_off = (p_start + cnt).astype(jnp.int32)
    pad_n = (padded - cnt).astype(jnp.int32)
    used = cnt > 0
    slot_e = (jnp.cumsum(used.astype(jnp.int32)) - 1) & 1
    later_used = jnp.where(used[None, :] & (e_ids[None, :] > e_ids[:, None]), e_ids[None, :], N_EXPERTS)
    next_e = jnp.min(later_used, axis=1)
    next_e = jnp.where(next_e < N_EXPERTS, next_e, -1).astype(jnp.int32)
    blk_first = ((blk_row == lookup(p_start, blk_e)) & (blk_row < p_end[-1])).astype(jnp.int32)
    blk_slot = lookup(slot_e, blk_e).astype(jnp.int32)
    blk_next = lookup(next_e, blk_e).astype(jnp.int32)

    dest_spec = pl.BlockSpec((None, 1, T_ROWS * TOP_K), lambda i, *_: (i, 0, 0), memory_space=pltpu.SMEM)

    x_sorted = pl.pallas_call(
        functools.partial(_dispatch_body, n_blocks=n_blocks),
        grid_spec=pltpu.PrefetchScalarGridSpec(
            num_scalar_prefetch=3,
            grid=(n_tok // T_DISPATCH,),
            in_specs=[pl.BlockSpec((None, 1, T_DISPATCH * TOP_K), lambda i, *_: (i, 0, 0), memory_space=pltpu.SMEM),
                      pl.BlockSpec((T_DISPATCH * SUBLANES, LANES), lambda i, *_: (i, 0))],
            out_specs=pl.BlockSpec(memory_space=pl.ANY),
            scratch_shapes=[pltpu.VMEM((TM // 2 * SUBLANES, LANES), F32), pltpu.SemaphoreType.DMA,
                            pltpu.SemaphoreType.DMA]),
        out_shape=jax.ShapeDtypeStruct((cap * SUBLANES, LANES), F32),
        compiler_params=cparams(("arbitrary",)),
        name="dispatch",
    )(pad_off, pad_n, n_used, dest.reshape(n_tok // T_DISPATCH, 1, T_DISPATCH * TOP_K), h_all)

    last = lambda i, nu: jnp.minimum(i, nu[0] - 1)
    y_sorted = pl.pallas_call(
        _expert_body,
        grid_spec=pltpu.PrefetchScalarGridSpec(
            num_scalar_prefetch=5,
            grid=(n_blocks,),
            in_specs=[pl.BlockSpec((TM * SUBLANES, LANES), lambda i, be, bf, bs, bn, nu: (last(i, nu), 0)),
                      pl.BlockSpec((None, 1, 2 * D_FF), lambda i, be, bf, bs, bn, nu: (be[last(i, nu)], 0, 0)),
                      pl.BlockSpec((None, 1, D_MODEL), lambda i, be, bf, bs, bn, nu: (be[last(i, nu)], 0, 0)),
                      pl.BlockSpec(memory_space=pl.ANY), pl.BlockSpec(memory_space=pl.ANY)],
            out_specs=pl.BlockSpec((TM * SUBLANES, LANES), lambda i, be, bf, bs, bn, nu: (i, 0)),
            scratch_shapes=[pltpu.VMEM((2, D_MODEL, 2 * D_FF), F32), pltpu.VMEM((2, D_FF, D_MODEL), F32),
                            pltpu.VMEM((D_MODEL, 2 * D_FF), BF16), pltpu.VMEM((D_FF, D_MODEL), BF16),
                            pltpu.SemaphoreType.DMA((2,))]),
        out_shape=jax.ShapeDtypeStruct((cap * SUBLANES, LANES), F32),
        compiler_params=pltpu.CompilerParams(dimension_semantics=("arbitrary",), vmem_limit_bytes=BIG_VMEM_LIMIT),
        name="experts",
    )(blk_e, blk_first, blk_slot, blk_next, n_used, x_sorted, b_up[0].astype(F32)[:, None, :],
      b_down[0].astype(F32)[:, None, :], w_up[0].astype(F32), w_down[0].astype(F32))

    npt = n_prompt // T_ROWS
    n_tiles = n_tok // T_ROWS
    dest_next_spec = pl.BlockSpec((None, 1, T_ROWS * TOP_K), lambda i: (jnp.minimum(i + 1, n_tiles - 1), 0, 0),
                                  memory_space=pltpu.SMEM)
    out_p, out_s = pl.pallas_call(
        functools.partial(_combine_body, n_prompt_tiles=npt, n_tiles=n_tiles),
        grid=(n_tiles,),
        in_specs=[dest_spec, dest_next_spec, pl.BlockSpec(memory_space=pl.ANY),
                  pl.BlockSpec((T_ROWS, D_MODEL), lambda i: (i, 0)), pl.BlockSpec((T_ROWS, LANES), lambda i: (i, 0)),
                  _full((1, D_MODEL)), _full((1, D_MODEL))],
        out_specs=[pl.BlockSpec((T_ROWS, D_MODEL), lambda i: (jnp.minimum(i, npt - 1), 0)),
                   pl.BlockSpec((T_ROWS, D_MODEL), lambda i: (jnp.maximum(i - npt, 0), 0))],
        out_shape=(jax.ShapeDtypeStruct((n_prompt, D_MODEL), F32), jax.ShapeDtypeStruct((n_sample, D_MODEL), F32)),
        scratch_shapes=[pltpu.VMEM((2 * TOP_K, T_ROWS * SUBLANES, LANES), F32), pltpu.SemaphoreType.DMA((2,))],
        compiler_params=cparams(("arbitrary",)),
        name="combine",
    )(dest_tiles, dest_tiles, y_sorted, base_all, gate_all, row(ln2_g[0]), row(ln2_b[0]))

    y_prompt = out_p.reshape(batch, seq, D_MODEL)
    y_sample = out_s.reshape(dec_batch, dec_seq, D_MODEL)
    conv_state_prompt = u_tail[None, :, HIST - (CONV_W - 1):, :]
    u_s3 = u_s.reshape(dec_batch, dec_seq, C_CONV)
    conv_state_sample = jnp.concatenate([cache_conv[0][:, dec_seq:, :].astype(F32), u_s3], axis=1)[None]
    chunk_v_prompt = v_chunk[None]
    chunk_v_sample = v_s.reshape(1, dec_batch, dec_seq, C_GMLP)
    return (y_prompt, y_sample, conv_state_prompt, conv_state_sample, chunk_v_prompt, chunk_v_sample)
```

```python
import functools

import jax
import jax.numpy as jnp
from jax import lax
from jax.experimental import pallas as pl
from jax.experimental.pallas import tpu as pltpu

F32 = jnp.float32
BF16 = jnp.bfloat16

D_MODEL = 1024
C_CONV = 512
C_GMLP = 512
N_CONV_GROUPS = 8
CONV_W = 31
N_HEADS = 8
HEAD_DIM = C_GMLP // N_HEADS
CHUNK = 128
N_EXPERTS = 32
TOP_K = 4
D_FF = 1024
PLE_DIM = 256
SWIGLU_LIMIT = 7.0
SWIGLU_ALPHA = 1.702
LN_EPS = 1e-5
DEPTH = 1
DEEPNORM_ALPHA = (2.0 * DEPTH) ** 0.25

LANES = 128
SUBLANES = 8
MXU_DIM = 256
VMEM_LIMIT = 48 * 1024 * 1024
BIG_VMEM_LIMIT = 56 * 1024 * 1024

T_PROMPT = 512
PROMPT_SPLIT = 2
HIST = 32
CONV_STRIDE = 4
SEQS_PER_TILE = 64
SEQ_CHUNK = 4
T_RANK = 512
T_DISPATCH = 1024
T_ROWS = 256
TM = 512
TM_CHAIN = 256


def _dot(a, b):
    return jnp.dot(a, b, preferred_element_type=F32)


def _layer_norm(x, g, b):
    mu = jnp.mean(x, axis=-1, keepdims=True)
    xc = x - mu
    var = jnp.mean(xc * xc, axis=-1, keepdims=True)
    return xc * lax.rsqrt(var + LN_EPS) * g + b


def _split_bf16(a):
    hi = a.astype(BF16)
    lo = (a - hi.astype(F32)).astype(BF16)
    return hi, lo


def _group_mean(a, gmat_ref):
    hi, lo = _split_bf16(a)
    g = gmat_ref[...]
    outs = []
    for s in range(C_CONV // MXU_DIM):
        sl = slice(MXU_DIM * s, MXU_DIM * (s + 1))
        outs.append(_dot(hi[:, sl], g) + _dot(lo[:, sl], g))
    return jnp.concatenate(outs, axis=1)


def _group_norm_silu(y, gmat_ref, gn_g, gn_b):
    mu = _group_mean(y, gmat_ref)
    yc = y - mu
    var = _group_mean(yc * yc, gmat_ref)
    yn = yc * lax.rsqrt(var + LN_EPS) * gn_g + gn_b
    return yn * jax.nn.sigmoid(yn)


def _store_token_tiles(ref, val, tok0=0):
    n = val.shape[0]
    for c in range(D_MODEL // LANES):
        ref[pl.ds(tok0 * SUBLANES + c, n, stride=SUBLANES), :] = val[:, c * LANES:(c + 1) * LANES]


def _load_token_tiles(ref, n, lead=(), tok0=0):
    parts = [ref[lead + (pl.ds(tok0 * SUBLANES + c, n, stride=SUBLANES), slice(None))]
             for c in range(D_MODEL // LANES)]
    return jnp.concatenate(parts, axis=1)


def _front(x, w):
    xn = _layer_norm(x, w["ln_in_g"][...], w["ln_in_b"][...])
    z = _dot(xn.astype(BF16), w["w_in"][...])
    a_val = z[:, 0:C_CONV]
    a_gate = z[:, C_CONV:2 * C_CONV]
    g_u = z[:, 2 * C_CONV:2 * C_CONV + C_GMLP]
    g_v = z[:, 2 * C_CONV + C_GMLP:]
    u = a_val * jax.nn.sigmoid(a_gate)
    ug = jax.nn.gelu(g_u)
    v = _layer_norm(jax.nn.gelu(g_v), w["vn_g"][...], w["vn_b"][...])
    return xn, u, ug, v


def _tail(xn, y_a, y_b, p, w, h_ref, base_ref, idx_ref, gate_ref, row0=0):
    rows = slice(row0, row0 + xn.shape[0])
    mix = _dot(y_a.astype(BF16), w["w_out"][0:C_CONV, :]) + _dot(y_b.astype(BF16), w["w_out"][C_CONV:, :])
    h = _layer_norm(DEEPNORM_ALPHA * xn + mix, w["ln1_g"][...], w["ln1_b"][...])
    hb, h_lo = _split_bf16(h)
    ple = _dot(p.astype(BF16), w["w_ple"][...]) * jax.nn.sigmoid(_dot(hb, w["w_gate"][...]))
    _store_token_tiles(h_ref, h, row0)
    base_ref[rows, :] = DEEPNORM_ALPHA * h + ple

    pair = _dot(hb, w["wr_pair"][...])
    logits = pair[:, :LANES] + pair[:, LANES:] + _dot(h_lo, w["wr_hi"][...]) + w["b_r"][...]
    lane = lax.broadcasted_iota(jnp.int32, logits.shape, 1)
    lane_f = lane.astype(F32)
    vals = jnp.where(lane < N_EXPERTS, logits, -jnp.inf)
    tops, ids = [], []
    for _ in range(TOP_K):
        m = jnp.max(vals, axis=-1, keepdims=True)
        i = jnp.min(jnp.where(vals == m, lane_f, float(LANES)), axis=-1, keepdims=True)
        vals = jnp.where(lane_f == i, -jnp.inf, vals)
        tops.append(m)
        ids.append(i)
    exps = [jnp.exp(m - tops[0]) for m in tops]
    denom = exps[0] + exps[1] + exps[2] + exps[3]
    idx_out = jnp.zeros(logits.shape, F32)
    gate_out = jnp.zeros(logits.shape, F32)
    for k in range(TOP_K):
        idx_out = jnp.where(lane == k, ids[k], idx_out)
        gate_out = jnp.where(lane == k, exps[k] / denom, gate_out)
    idx_ref[rows, :] = idx_out.astype(jnp.int32)
    gate_ref[rows, :] = gate_out


_WEIGHT_NAMES = ("ln_in_g", "ln_in_b", "w_in", "vn_g", "vn_b", "gmat", "gn_g", "gn_b", "w_out", "ln1_g", "ln1_b",
                 "w_gate", "w_ple", "wr_hi", "wr_pair", "b_r")


def _prompt_branch(j, x_ref, p_ref, w, cwb_ref, ws_ref, bs_ref, outs, tail_ref, vch_ref, ubuf, yslab, last_j):
    t = T_PROMPT
    n = t // PROMPT_SPLIT
    n_slabs = C_CONV // LANES

    @pl.when(j == 0)
    def _():
        ubuf[:, 0:HIST, :] = jnp.zeros((n_slabs, HIST, LANES), F32)

    @pl.when(j > 0)
    def _():
        ubuf[:, 0:HIST, :] = ubuf[:, t:t + HIST, :]

    def front(h):
        xn, u, ug, v = _front(x_ref[h * n:(h + 1) * n, :], w)
        for s in range(n_slabs):
            ubuf[s, HIST + h * n:HIST + (h + 1) * n, :] = u[:, s * LANES:(s + 1) * LANES]
        return xn, u, ug, v

    def conv(h):
        rows = CONV_STRIDE * SUBLANES
        first = HIST - (CONV_W - 1)
        for s in range(n_slabs):
            for c in range(h * n // rows, (h + 1) * n // rows):
                accs = [None] * CONV_STRIDE
                for shift in range(CONV_STRIDE + CONV_W - 1):
                    win = ubuf[s, pl.ds(first + c * rows + shift, SUBLANES, stride=CONV_STRIDE), :]
                    for ph in range(CONV_STRIDE):
                        k = shift - ph
                        if 0 <= k < CONV_W:
                            term = cwb_ref[k, s] * win
                            accs[ph] = term if accs[ph] is None else accs[ph] + term
                for ph in range(CONV_STRIDE):
                    yslab[s, pl.ds(c * rows + ph, SUBLANES, stride=CONV_STRIDE), :] = accs[ph]
        y_conv = jnp.concatenate([yslab[s, h * n:(h + 1) * n, :] for s in range(n_slabs)], axis=1)
        return _group_norm_silu(y_conv, w["gmat"], w["gn_g"][...], w["gn_b"][...])

    def spatial_gate(ug, v):
        lane = lax.broadcasted_iota(jnp.int32, (CHUNK, LANES), 1)
        mixed_chunks = []
        for c in range(n // CHUNK):
            vc = v[c * CHUNK:(c + 1) * CHUNK, :]
            parts = []
            for q in range(N_HEADS // 2):
                vp = vc[:, q * LANES:(q + 1) * LANES]
                rhs = jnp.concatenate([jnp.where(lane < HEAD_DIM, vp, 0.0), jnp.where(lane >= HEAD_DIM, vp, 0.0)],
                                      axis=0).astype(BF16)
                parts.append(_dot(ws_ref[q], rhs))
            mixed_chunks.append(jnp.concatenate(parts, axis=1) + bs_ref[...])
        return ug * jnp.concatenate(mixed_chunks, axis=0)

    def tail(h, f, y_a, y_b):
        _tail(f[0], y_a, y_b, p_ref[h * n:(h + 1) * n, :], w, *outs, row0=h * n)

    fronts = [front(0)]
    y_as = {}
    for h in range(PROMPT_SPLIT):
        if h + 1 < PROMPT_SPLIT:
            fronts.append(front(h + 1))
        y_as[h] = conv(h)
        if h > 0:
            tail(h - 1, fronts[h - 1], y_as.pop(h - 1), spatial_gate(fronts[h - 1][2], fronts[h - 1][3]))
    last = PROMPT_SPLIT - 1
    tail(last, fronts[last], y_as.pop(last), spatial_gate(fronts[last][2], fronts[last][3]))

    @pl.when(j == last_j)
    def _():
        tail_ref[...] = fronts[last][1][n - HIST:, :]
        vch_ref[...] = fronts[last][3][n - CHUNK:, :]


def _sample_branch(x_ref, p_ref, cpad_ref, w, cw_ref, convu_ref, gw_ref, b8_ref, outs, u_out_ref, v_out_ref,
                   uscr, vscr, yscr, mscr):
    xn, u, ug, v = _front(x_ref[...], w)
    u_out_ref[...] = u
    v_out_ref[...] = v
    uscr[...] = u
    vscr[...] = v
    first = HIST - (CONV_W - 1)
    rows = SEQ_CHUNK * SUBLANES

    def step(c, carry):
        r0 = pl.multiple_of(c * rows, rows)
        s0 = c * SEQ_CHUNK
        u3 = uscr[pl.ds(r0, rows), :].reshape(SEQ_CHUNK, SUBLANES, C_CONV)
        v3 = vscr[pl.ds(r0, rows), :].reshape(SEQ_CHUNK, SUBLANES, C_GMLP)
        acc = jnp.zeros((SEQ_CHUNK, SUBLANES, C_CONV), F32)
        for k in range(CONV_W):
            acc = acc + cw_ref[pl.ds(k, 1), :][None] * cpad_ref[pl.ds(s0, SEQ_CHUNK), pl.ds(first + k, SUBLANES), :]
        mix = jnp.zeros((SEQ_CHUNK, SUBLANES, C_GMLP), F32) + b8_ref[...][None]
        for s in range(SUBLANES):
            acc = acc + u3[:, s:s + 1, :] * convu_ref[s][None]
            mix = mix + v3[:, s:s + 1, :] * gw_ref[s][None]
        yscr[pl.ds(r0, rows), :] = acc.reshape(rows, C_CONV)
        mscr[pl.ds(r0, rows), :] = mix.reshape(rows, C_GMLP)
        return carry

    lax.fori_loop(0, SEQS_PER_TILE // SEQ_CHUNK, step, 0)
    y_a = _group_norm_silu(yscr[...], w["gmat"], w["gn_g"][...], w["gn_b"][...])
    y_b = ug * mscr[...]
    _tail(xn, y_a, y_b, p_ref[...], w, *outs)


def _mixer_body(*refs, n_prompt_tiles, tiles_per_seq):
    n_w = len(_WEIGHT_NAMES)
    xp_ref, pp_ref, xs_ref, ps_ref, cpad_ref = refs[:5]
    w = dict(zip(_WEIGHT_NAMES, refs[5:5 + n_w]))
    cw_ref, cwb_ref, ws_ref, bs_ref, convu_ref, gw_ref, b8_ref = refs[5 + n_w:12 + n_w]
    outs = refs[12 + n_w:16 + n_w]
    tail_ref, vch_ref, u_out_ref, v_out_ref = refs[16 + n_w:20 + n_w]
    ubuf, yslab, yscr, uscr, vscr, mscr = refs[20 + n_w:]
    step = pl.program_id(0)

    @pl.when(step < n_prompt_tiles)
    def _():
        _prompt_branch(lax.rem(step, tiles_per_seq), xp_ref, pp_ref, w, cwb_ref, ws_ref, bs_ref, outs, tail_ref,
                       vch_ref, ubuf, yslab, tiles_per_seq - 1)

    @pl.when(step >= n_prompt_tiles)
    def _():
        _sample_branch(xs_ref, ps_ref, cpad_ref, w, cw_ref, convu_ref, gw_ref, b8_ref, outs, u_out_ref, v_out_ref,
                       uscr, vscr, yscr, mscr)


def _rank_body(idx_ref, tri_ref, rank_ref, counts_ref, carry):
    i = pl.program_id(0)

    @pl.when(i == 0)
    def _():
        carry[...] = jnp.zeros(carry.shape, F32)

    idx = idx_ref[...]
    lane = lax.broadcasted_iota(jnp.int32, idx.shape, 1)
    hots = [lane == idx[:, k:k + 1] for k in range(TOP_K)]
    multi = jnp.zeros(idx.shape, F32)
    for hot in hots:
        multi = multi + hot.astype(F32)
    before = _dot(tri_ref[...], multi.astype(BF16)) + carry[...]
    rank = jnp.zeros(idx.shape, F32)
    for k, hot in enumerate(hots):
        rk = jnp.sum(jnp.where(hot, before, 0.0), axis=-1, keepdims=True)
        rank = jnp.where(lane == k, rk, rank)
    rank_ref[...] = rank.astype(jnp.int32)
    carry[...] = carry[...] + jnp.sum(multi, axis=0, keepdims=True)
    counts_ref[...] = carry[...].astype(jnp.int32)


def _pad_bits():
    b = TM // 2
    while b >= 1:
        yield b
        b //= 2


def _token_rows(t, n=1):
    return pl.ds(pl.multiple_of(t * SUBLANES, SUBLANES), n * SUBLANES)


def _dispatch_body(pad_off_ref, pad_n_ref, n_used_ref, dest_ref, h_ref, xs_ref, zbuf, sem, zsem, *, n_blocks):
    i = pl.program_id(0)

    def issue(r, carry):
        src = h_ref.at[_token_rows(r), :]
        for k in range(TOP_K):
            d = dest_ref[0, r * TOP_K + k]
            pltpu.make_async_copy(src, xs_ref.at[_token_rows(d), :], sem).start(priority=k % 2)
        return carry

    lax.fori_loop(0, T_DISPATCH, issue, 0, unroll=8)

    @pl.when(i == 0)
    def _():
        zbuf[...] = jnp.zeros(zbuf.shape, F32)

        def zero_copy(off, b):
            return pltpu.make_async_copy(zbuf.at[pl.ds(0, b * SUBLANES), :], xs_ref.at[_token_rows(off, b), :], zsem)

        def start_or_wait(cond, cp, wait):
            @pl.when(cond)
            def _():
                if wait:
                    cp.wait()
                else:
                    cp.start()

        for wait in (False, True):
            for e in range(N_EXPERTS):
                n = pad_n_ref[e]
                for b in _pad_bits():
                    start_or_wait((n & b) != 0, zero_copy(pad_off_ref[e] + (n & ~(2 * b - 1)), b), wait)

        half = TM // 2

        def tail_block(wait):
            def go(blk, carry):
                for s in range(TM // half):
                    cp = zero_copy(blk * TM + s * half, half)
                    cp.wait() if wait else cp.start()
                return carry
            return go

        lax.fori_loop(n_used_ref[0], n_blocks, tail_block(False), 0)
        lax.fori_loop(n_used_ref[0], n_blocks, tail_block(True), 0)

    for _ in range(TOP_K):
        pltpu.make_async_copy(h_ref, xs_ref.at[_token_rows(0, T_DISPATCH), :], sem).wait()


def _expert_body(blk_e_ref, first_ref, slot_ref, next_ref, rows_ref, n_used_ref, x_ref, bu_ref, bd_ref, wu_hbm, wd_hbm,
                 y_ref, wu_f32, wd_f32, wu_bf, wd_bf, sems):
    i = pl.program_id(0)

    def weight_copies(e, s):
        return (pltpu.make_async_copy(wu_hbm.at[e], wu_f32.at[s], sems.at[s]),
                pltpu.make_async_copy(wd_hbm.at[e], wd_f32.at[s], sems.at[s]))

    @pl.when(i < n_used_ref[0])
    def _():
        s = slot_ref[i]

        @pl.when(first_ref[i] == 1)
        def _():
            @pl.when(i == 0)
            def _():
                for cp in weight_copies(blk_e_ref[i], s):
                    cp.start()

            for cp in weight_copies(blk_e_ref[i], s):
                cp.wait()
            nxt = next_ref[i]

            @pl.when(nxt >= 0)
            def _():
                for cp in weight_copies(nxt, 1 - s):
                    cp.start()

            chunk = D_MODEL // SUBLANES

            def cast_rows(c, carry):
                r = pl.multiple_of(c * chunk, chunk)
                wu_bf[pl.ds(r, chunk), :] = wu_f32[s, pl.ds(r, chunk), :].astype(BF16)
                wd_bf[pl.ds(r, chunk), :] = wd_f32[s, pl.ds(r, chunk), :].astype(BF16)
                return carry

            lax.fori_loop(0, SUBLANES, cast_rows, 0)

        def chain(c):
            x = _load_token_tiles(x_ref, TM_CHAIN, tok0=c * TM_CHAIN)
            hcat = _dot(x.astype(BF16), wu_bf[...]) + bu_ref[...]
            h_glu = jnp.minimum(hcat[:, :D_FF], SWIGLU_LIMIT)
            h_lin = jnp.clip(hcat[:, D_FF:], -SWIGLU_LIMIT, SWIGLU_LIMIT)
            act = h_glu * jax.nn.sigmoid(SWIGLU_ALPHA * h_glu) * (h_lin + 1.0)
            _store_token_tiles(y_ref, _dot(act.astype(BF16), wd_bf[...]) + bd_ref[...], tok0=c * TM_CHAIN)

        n_chains = TM // TM_CHAIN
        live = (rows_ref[i] + TM_CHAIN - 1) // TM_CHAIN
        for m in range(1, n_chains + 1):
            @pl.when(live == m)
            def _():
                for c in range(m):
                    chain(c)
                if m < n_chains:
                    y_ref[m * TM_CHAIN * SUBLANES:, :] = jnp.zeros(((n_chains - m) * TM_CHAIN * SUBLANES, LANES), F32)

    @pl.when(pl.program_id(0) >= n_used_ref[0])
    def _():
        y_ref[...] = jnp.zeros(y_ref.shape, F32)


def _combine_body(dest_ref, dest_next_ref, y_ref, base_ref, gate_ref, g_ref, b_ref, outp_ref, outs_ref, ybuf, sems, *,
                  n_prompt_tiles, n_tiles):
    i = pl.program_id(0)

    def issue_tile(d_ref, slot):
        def issue(r, carry):
            for k in range(TOP_K):
                d = d_ref[0, r * TOP_K + k]
                pltpu.make_async_copy(y_ref.at[_token_rows(d), :], ybuf.at[slot * TOP_K + k, _token_rows(r), :],
                                      sems.at[slot]).start(priority=k % 2)
            return carry

        lax.fori_loop(0, T_ROWS, issue, 0, unroll=8)

    slot = lax.rem(i, 2)

    @pl.when(i == 0)
    def _():
        issue_tile(dest_ref, 0)

    @pl.when(i + 1 < n_tiles)
    def _():
        issue_tile(dest_next_ref, 1 - slot)

    for k in range(TOP_K):
        pltpu.make_async_copy(y_ref.at[_token_rows(0, T_ROWS), :], ybuf.at[slot * TOP_K + k], sems.at[slot]).wait()

    gate = gate_ref[...]
    acc = base_ref[...]
    for k in range(TOP_K):
        acc = acc + gate[:, k:k + 1] * _load_token_tiles(ybuf, T_ROWS, lead=(slot * TOP_K + k,))
    out = _layer_norm(acc, g_ref[...], b_ref[...])

    @pl.when(i < n_prompt_tiles)
    def _():
        outp_ref[...] = out

    @pl.when(i >= n_prompt_tiles)
    def _():
        outs_ref[...] = out


def _full(shape):
    return pl.BlockSpec(shape, lambda *_: (0,) * len(shape), pipeline_mode=pl.Buffered(1))


def kernel(x_prompt, x_sample, cache_conv, p_prompt, p_sample, ln_in_g, ln_in_b, w_in, conv_w, gn_g, gn_b, vn_g, vn_b,
           w_spatial, b_spatial, w_out, ln1_g, ln1_b, w_router, b_router, w_up, b_up, w_down, b_down, w_ple,
           w_ple_gate, ln2_g, ln2_b):
    batch, seq, _ = x_prompt.shape
    dec_batch, dec_seq, _ = x_sample.shape
    assert w_in.shape[0] == DEPTH and dec_seq == SUBLANES and seq % T_PROMPT == 0
    n_prompt = batch * seq
    n_sample = dec_batch * dec_seq
    n_tok = n_prompt + n_sample
    t_s = SEQS_PER_TILE * dec_seq
    assert n_prompt % T_ROWS == 0 and n_sample % T_ROWS == 0 and n_tok % T_RANK == 0 and n_sample % t_s == 0
    assert n_tok % T_DISPATCH == 0

    row = lambda a: a.reshape(1, -1).astype(F32)
    gidx = jnp.arange(MXU_DIM) // (C_CONV // N_CONV_GROUPS)
    gmat = jnp.where(gidx[:, None] == gidx[None, :], 1.0 / (C_CONV // N_CONV_GROUPS), 0.0).astype(BF16)
    wr_pad = jnp.pad(w_router[0].astype(F32), ((0, 0), (0, LANES - N_EXPERTS)))
    wr_hi = wr_pad.astype(BF16)
    wr_lo = (wr_pad - wr_hi.astype(F32)).astype(BF16)
    weights = dict(
        ln_in_g=row(ln_in_g), ln_in_b=row(ln_in_b), w_in=w_in[0].astype(BF16), vn_g=row(vn_g[0]), vn_b=row(vn_b[0]),
        gmat=gmat, gn_g=row(gn_g[0]), gn_b=row(gn_b[0]), w_out=w_out[0].astype(BF16), ln1_g=row(ln1_g[0]),
        ln1_b=row(ln1_b[0]), w_gate=w_ple_gate[0].astype(BF16), w_ple=w_ple[0].astype(BF16), wr_hi=wr_hi,
        wr_pair=jnp.concatenate([wr_hi, wr_lo], axis=1),
        b_r=jnp.pad(row(b_router[0]), ((0, 0), (0, LANES - N_EXPERTS))))
    w_list = [weights[n] for n in _WEIGHT_NAMES]
    w_specs = [_full(a.shape) for a in w_list]
    cw = jnp.pad(conv_w[0].astype(F32), ((0, HIST - CONV_W), (0, 0)))
    causal = jnp.tril(jnp.ones((CHUNK, CHUNK), bool))
    ws_m = jnp.where(causal[None], w_spatial[0], 0.0)
    ws_cat = jnp.concatenate([ws_m[0::2], ws_m[1::2]], axis=2).astype(BF16)
    bs_full = jnp.repeat(b_spatial[0].T.astype(F32), HEAD_DIM, axis=1)
    s_i = jnp.arange(SUBLANES)[:, None]
    t_i = jnp.arange(SUBLANES)[None, :]
    tap = jnp.clip(CONV_W - 1 - t_i + s_i, 0, CONV_W - 1)
    convu = jnp.where((s_i <= t_i)[:, :, None], conv_w[0].astype(F32)[tap], 0.0)
    gw8 = jnp.transpose(ws_m[:, :SUBLANES, :SUBLANES], (2, 1, 0))
    gw8 = jnp.repeat(gw8.astype(F32), HEAD_DIM, axis=2)
    b8 = bs_full[:SUBLANES]
    cpad = jnp.pad(cache_conv[0].astype(F32), ((0, 0), (HIST - (CONV_W - 1), SUBLANES), (0, 0)))

    cparams = lambda sem: pltpu.CompilerParams(dimension_semantics=sem, vmem_limit_bytes=VMEM_LIMIT)

    nj = seq // T_PROMPT
    npt = n_prompt // T_PROMPT
    assert t_s == T_PROMPT
    pstep = lambda i: jnp.minimum(i, npt - 1)
    sstep = lambda i: jnp.maximum(i - npt, 0)
    tok_block = lambda width: pl.BlockSpec((T_PROMPT, width), lambda i: (i, 0))
    cwb = jnp.broadcast_to(conv_w[0].astype(F32).reshape(CONV_W, C_CONV // LANES, 1, LANES),
                           (CONV_W, C_CONV // LANES, SUBLANES, LANES))
    tables = [cw, cwb, ws_cat, bs_full, convu, gw8, b8]
    h_all, base_all, idx_all, gate_all, u_tail, v_chunk, u_s, v_s = pl.pallas_call(
        functools.partial(_mixer_body, n_prompt_tiles=npt, tiles_per_seq=nj),
        grid=(npt + n_sample // t_s,),
        in_specs=[pl.BlockSpec((None, T_PROMPT, D_MODEL), lambda i: (pstep(i) // nj, pstep(i) % nj, 0)),
                  pl.BlockSpec((None, T_PROMPT, PLE_DIM), lambda i: (pstep(i) // nj, pstep(i) % nj, 0)),
                  pl.BlockSpec((t_s, D_MODEL), lambda i: (sstep(i), 0)),
                  pl.BlockSpec((t_s, PLE_DIM), lambda i: (sstep(i), 0)),
                  pl.BlockSpec((SEQS_PER_TILE,) + cpad.shape[1:], lambda i: (sstep(i), 0, 0),
                               pipeline_mode=pl.Buffered(1))]
                 + w_specs + [_full(a.shape) for a in tables],
        out_specs=[pl.BlockSpec((T_PROMPT * SUBLANES, LANES), lambda i: (i, 0)),
                   tok_block(D_MODEL), tok_block(LANES), tok_block(LANES),
                   pl.BlockSpec((None, HIST, C_CONV), lambda i: (pstep(i) // nj, 0, 0)),
                   pl.BlockSpec((None, CHUNK, C_GMLP), lambda i: (pstep(i) // nj, 0, 0)),
                   pl.BlockSpec((t_s, C_CONV), lambda i: (sstep(i), 0)),
                   pl.BlockSpec((t_s, C_GMLP), lambda i: (sstep(i), 0))],
        out_shape=(jax.ShapeDtypeStruct((n_tok * SUBLANES, LANES), F32), jax.ShapeDtypeStruct((n_tok, D_MODEL), F32),
                   jax.ShapeDtypeStruct((n_tok, LANES), jnp.int32), jax.ShapeDtypeStruct((n_tok, LANES), F32),
                   jax.ShapeDtypeStruct((batch, HIST, C_CONV), F32), jax.ShapeDtypeStruct((batch, CHUNK, C_GMLP), F32),
                   jax.ShapeDtypeStruct((n_sample, C_CONV), F32), jax.ShapeDtypeStruct((n_sample, C_GMLP), F32)),
        scratch_shapes=[pltpu.VMEM((C_CONV // LANES, T_PROMPT + HIST, LANES), F32),
                        pltpu.VMEM((C_CONV // LANES, T_PROMPT, LANES), F32)]
                       + [pltpu.VMEM((T_PROMPT, C_CONV), F32)] * 4,
        compiler_params=pltpu.CompilerParams(dimension_semantics=("arbitrary",), vmem_limit_bytes=BIG_VMEM_LIMIT),
        name="mixer",
    )(x_prompt, p_prompt[0], x_sample.reshape(n_sample, D_MODEL), p_sample[0].reshape(n_sample, PLE_DIM), cpad,
      *w_list, *tables)

    tri = (jnp.arange(T_RANK)[:, None] > jnp.arange(T_RANK)[None, :]).astype(BF16)
    rank_all, counts = pl.pallas_call(
        _rank_body,
        grid=(n_tok // T_RANK,),
        in_specs=[pl.BlockSpec((T_RANK, LANES), lambda i: (i, 0)), _full(tri.shape)],
        out_specs=[pl.BlockSpec((T_RANK, LANES), lambda i: (i, 0)), _full((1, LANES))],
        out_shape=(jax.ShapeDtypeStruct((n_tok, LANES), jnp.int32), jax.ShapeDtypeStruct((1, LANES), jnp.int32)),
        scratch_shapes=[pltpu.VMEM((1, LANES), F32)],
        compiler_params=cparams(("arbitrary",)),
        name="rank",
    )(idx_all, tri)

    n_assign = n_tok * TOP_K
    n_blocks = n_assign // TM + N_EXPERTS
    cap = n_blocks * TM
    cnt = counts[0, :N_EXPERTS]
    padded = (cnt + TM - 1) // TM * TM
    p_end = jnp.cumsum(padded)
    p_start = p_end - padded
    e_ids = jnp.arange(N_EXPERTS, dtype=jnp.int32)
    lookup = lambda table, ids: jnp.sum(jnp.where(ids[..., None] == e_ids, table, 0), axis=-1)
    dest = (lookup(p_start, idx_all[:, :TOP_K]) + rank_all[:, :TOP_K]).astype(jnp.int32)
    dest_tiles = dest.reshape(n_tok // T_ROWS, 1, T_ROWS * TOP_K)
    blk_row = jnp.arange(n_blocks, dtype=jnp.int32) * TM
    blk_e = jnp.minimum(jnp.sum(p_end[None, :] <= blk_row[:, None], axis=1), N_EXPERTS - 1).astype(jnp.int32)
    n_used = (p_end[-1:] // TM).astype(jnp.int32)
    pad_off = (p_start + cnt).astype(jnp.int32)
    pad_n = (padded - cnt).astype(jnp.int32)
    used = cnt > 0
    slot_e = (jnp.cumsum(used.astype(jnp.int32)) - 1) & 1
    later_used = jnp.where(used[None, :] & (e_ids[None, :] > e_ids[:, None]), e_ids[None, :], N_EXPERTS)
    next_e = jnp.min(later_used, axis=1)
    next_e = jnp.where(next_e < N_EXPERTS, next_e, -1).astype(jnp.int32)
    blk_first = ((blk_row == lookup(p_start, blk_e)) & (blk_row < p_end[-1])).astype(jnp.int32)
    blk_rows = jnp.clip(lookup(p_start + cnt, blk_e) - blk_row, 0, TM).astype(jnp.int32)
    blk_slot = lookup(slot_e, blk_e).astype(jnp.int32)
    blk_next = lookup(next_e, blk_e).astype(jnp.int32)

    dest_spec = pl.BlockSpec((None, 1, T_ROWS * TOP_K), lambda i, *_: (i, 0, 0), memory_space=pltpu.SMEM)

    x_sorted = pl.pallas_call(
        functools.partial(_dispatch_body, n_blocks=n_blocks),
        grid_spec=pltpu.PrefetchScalarGridSpec(
            num_scalar_prefetch=3,
            grid=(n_tok // T_DISPATCH,),
            in_specs=[pl.BlockSpec((None, 1, T_DISPATCH * TOP_K), lambda i, *_: (i, 0, 0), memory_space=pltpu.SMEM),
                      pl.BlockSpec((T_DISPATCH * SUBLANES, LANES), lambda i, *_: (i, 0))],
            out_specs=pl.BlockSpec(memory_space=pl.ANY),
            scratch_shapes=[pltpu.VMEM((TM // 2 * SUBLANES, LANES), F32), pltpu.SemaphoreType.DMA,
                            pltpu.SemaphoreType.DMA]),
        out_shape=jax.ShapeDtypeStruct((cap * SUBLANES, LANES), F32),
        compiler_params=cparams(("arbitrary",)),
        name="dispatch",
    )(pad_off, pad_n, n_used, dest.reshape(n_tok // T_DISPATCH, 1, T_DISPATCH * TOP_K), h_all)

    last = lambda i, nu: jnp.minimum(i, nu[0] - 1)
    y_sorted = pl.pallas_call(
        _expert_body,
        grid_spec=pltpu.PrefetchScalarGridSpec(
            num_scalar_prefetch=6,
            grid=(n_blocks,),
            in_specs=[pl.BlockSpec((TM * SUBLANES, LANES), lambda i, be, bf, bs, bn, br, nu: (last(i, nu), 0)),
                      pl.BlockSpec((None, 1, 2 * D_FF), lambda i, be, bf, bs, bn, br, nu: (be[last(i, nu)], 0, 0)),
                      pl.BlockSpec((None, 1, D_MODEL), lambda i, be, bf, bs, bn, br, nu: (be[last(i, nu)], 0, 0)),
                      pl.BlockSpec(memory_space=pl.ANY), pl.BlockSpec(memory_space=pl.ANY)],
            out_specs=pl.BlockSpec((TM * SUBLANES, LANES), lambda i, be, bf, bs, bn, br, nu: (i, 0)),
            scratch_shapes=[pltpu.VMEM((2, D_MODEL, 2 * D_FF), F32), pltpu.VMEM((2, D_FF, D_MODEL), F32),
                            pltpu.VMEM((D_MODEL, 2 * D_FF), BF16), pltpu.VMEM((D_FF, D_MODEL), BF16),
                            pltpu.SemaphoreType.DMA((2,))]),
        out_shape=jax.ShapeDtypeStruct((cap * SUBLANES, LANES), F32),
        compiler_params=pltpu.CompilerParams(dimension_semantics=("arbitrary",), vmem_limit_bytes=BIG_VMEM_LIMIT),
        name="experts",
    )(blk_e, blk_first, blk_slot, blk_next, blk_rows, n_used, x_sorted, b_up[0].astype(F32)[:, None, :],
      b_down[0].astype(F32)[:, None, :], w_up[0].astype(F32), w_down[0].astype(F32))

    npt = n_prompt // T_ROWS
    n_tiles = n_tok // T_ROWS
    dest_next_spec = pl.BlockSpec((None, 1, T_ROWS * TOP_K), lambda i: (jnp.minimum(i + 1, n_tiles - 1), 0, 0),
                                  memory_space=pltpu.SMEM)
    out_p, out_s = pl.pallas_call(
        functools.partial(_combine_body, n_prompt_tiles=npt, n_tiles=n_tiles),
        grid=(n_tiles,),
        in_specs=[dest_spec, dest_next_spec, pl.BlockSpec(memory_space=pl.ANY),
                  pl.BlockSpec((T_ROWS, D_MODEL), lambda i: (i, 0)), pl.BlockSpec((T_ROWS, LANES), lambda i: (i, 0)),
                  _full((1, D_MODEL)), _full((1, D_MODEL))],
        out_specs=[pl.BlockSpec((T_ROWS, D_MODEL), lambda i: (jnp.minimum(i, npt - 1), 0)),
                   pl.BlockSpec((T_ROWS, D_MODEL), lambda i: (jnp.maximum(i - npt, 0), 0))],
        out_shape=(jax.ShapeDtypeStruct((n_prompt, D_MODEL), F32), jax.ShapeDtypeStruct((n_sample, D_MODEL), F32)),
        scratch_shapes=[pltpu.VMEM((2 * TOP_K, T_ROWS * SUBLANES, LANES), F32), pltpu.SemaphoreType.DMA((2,))],
        compiler_params=cparams(("arbitrary",)),
        name="combine",
    )(dest_tiles, dest_tiles, y_sorted, base_all, gate_all, row(ln2_g[0]), row(ln2_b[0]))

    y_prompt = out_p.reshape(batch, seq, D_MODEL)
    y_sample = out_s.reshape(dec_batch, dec_seq, D_MODEL)
    conv_state_prompt = u_tail[None, :, HIST - (CONV_W - 1):, :]
    u_s3 = u_s.reshape(dec_batch, dec_seq, C_CONV)
    conv_state_sample = jnp.concatenate([cache_conv[0][:, dec_seq:, :].astype(F32), u_s3], axis=1)[None]
    chunk_v_prompt = v_chunk[None]
    chunk_v_sample = v_s.reshape(1, dec_batch, dec_seq, C_GMLP)
    return (y_prompt, y_sample, conv_state_prompt, conv_state_sample, chunk_v_prompt, chunk_v_sample)
```

```python
import functools

import jax
import jax.numpy as jnp
from jax import lax
from jax.experimental import pallas as pl
from jax.experimental.pallas import tpu as pltpu

F32 = jnp.float32
BF16 = jnp.bfloat16

D_MODEL = 1024
C_CONV = 512
C_GMLP = 512
N_CONV_GROUPS = 8
CONV_W = 31
N_HEADS = 8
HEAD_DIM = C_GMLP // N_HEADS
CHUNK = 128
N_EXPERTS = 32
TOP_K = 4
D_FF = 1024
PLE_DIM = 256
SWIGLU_LIMIT = 7.0
SWIGLU_ALPHA = 1.702
LN_EPS = 1e-5
DEPTH = 1
DEEPNORM_ALPHA = (2.0 * DEPTH) ** 0.25

LANES = 128
SUBLANES = 8
MXU_DIM = 256
VMEM_LIMIT = 48 * 1024 * 1024
BIG_VMEM_LIMIT = 56 * 1024 * 1024

T_PROMPT = 512
PROMPT_SPLIT = 2
HIST = 32
CONV_STRIDE = 4
SEQS_PER_TILE = 64
SEQ_CHUNK = 4
T_RANK = 512
T_DISPATCH = 1024
T_ROWS = 256
TM = 512
TM_CHAIN = 256


def _dot(a, b):
    return jnp.dot(a, b, preferred_element_type=F32)


def _layer_norm(x, g, b):
    mu = jnp.mean(x, axis=-1, keepdims=True)
    xc = x - mu
    var = jnp.mean(xc * xc, axis=-1, keepdims=True)
    return xc * lax.rsqrt(var + LN_EPS) * g + b


def _split_bf16(a):
    hi = a.astype(BF16)
    lo = (a - hi.astype(F32)).astype(BF16)
    return hi, lo


def _group_mean(a, gmat_ref):
    hi, lo = _split_bf16(a)
    g = gmat_ref[...]
    outs = []
    for s in range(C_CONV // MXU_DIM):
        sl = slice(MXU_DIM * s, MXU_DIM * (s + 1))
        outs.append(_dot(hi[:, sl], g) + _dot(lo[:, sl], g))
    return jnp.concatenate(outs, axis=1)


def _group_norm_silu(y, gmat_ref, gn_g, gn_b):
    mu = _group_mean(y, gmat_ref)
    yc = y - mu
    var = _group_mean(yc * yc, gmat_ref)
    yn = yc * lax.rsqrt(var + LN_EPS) * gn_g + gn_b
    return yn * jax.nn.sigmoid(yn)


def _store_token_tiles(ref, val, tok0=0):
    n = val.shape[0]
    for c in range(D_MODEL // LANES):
        ref[pl.ds(tok0 * SUBLANES + c, n, stride=SUBLANES), :] = val[:, c * LANES:(c + 1) * LANES]


def _load_token_tiles(ref, n, lead=(), tok0=0):
    parts = [ref[lead + (pl.ds(tok0 * SUBLANES + c, n, stride=SUBLANES), slice(None))]
             for c in range(D_MODEL // LANES)]
    return jnp.concatenate(parts, axis=1)


def _front(x, w):
    xn = _layer_norm(x, w["ln_in_g"][...], w["ln_in_b"][...])
    z = _dot(xn.astype(BF16), w["w_in"][...])
    a_val = z[:, 0:C_CONV]
    a_gate = z[:, C_CONV:2 * C_CONV]
    g_u = z[:, 2 * C_CONV:2 * C_CONV + C_GMLP]
    g_v = z[:, 2 * C_CONV + C_GMLP:]
    u = a_val * jax.nn.sigmoid(a_gate)
    ug = jax.nn.gelu(g_u)
    v = _layer_norm(jax.nn.gelu(g_v), w["vn_g"][...], w["vn_b"][...])
    return xn, u, ug, v


def _tail(xn, y_a, y_b, p, w, h_ref, base_ref, idx_ref, gate_ref, row0=0):
    rows = slice(row0, row0 + xn.shape[0])
    mix = _dot(y_a.astype(BF16), w["w_out"][0:C_CONV, :]) + _dot(y_b.astype(BF16), w["w_out"][C_CONV:, :])
    h = _layer_norm(DEEPNORM_ALPHA * xn + mix, w["ln1_g"][...], w["ln1_b"][...])
    hb, h_lo = _split_bf16(h)
    ple = _dot(p.astype(BF16), w["w_ple"][...]) * jax.nn.sigmoid(_dot(hb, w["w_gate"][...]))
    _store_token_tiles(h_ref, h, row0)
    base_ref[rows, :] = DEEPNORM_ALPHA * h + ple

    pair = _dot(hb, w["wr_pair"][...])
    logits = pair[:, :LANES] + pair[:, LANES:] + _dot(h_lo, w["wr_hi"][...]) + w["b_r"][...]
    lane = lax.broadcasted_iota(jnp.int32, logits.shape, 1)
    lane_f = lane.astype(F32)
    vals = jnp.where(lane < N_EXPERTS, logits, -jnp.inf)
    tops, ids = [], []
    for _ in range(TOP_K):
        m = jnp.max(vals, axis=-1, keepdims=True)
        i = jnp.min(jnp.where(vals == m, lane_f, float(LANES)), axis=-1, keepdims=True)
        vals = jnp.where(lane_f == i, -jnp.inf, vals)
        tops.append(m)
        ids.append(i)
    exps = [jnp.exp(m - tops[0]) for m in tops]
    denom = exps[0] + exps[1] + exps[2] + exps[3]
    idx_out = jnp.zeros(logits.shape, F32)
    gate_out = jnp.zeros(logits.shape, F32)
    for k in range(TOP_K):
        idx_out = jnp.where(lane == k, ids[k], idx_out)
        gate_out = jnp.where(lane == k, exps[k] / denom, gate_out)
    idx_ref[rows, :] = idx_out.astype(jnp.int32)
    gate_ref[rows, :] = gate_out


_WEIGHT_NAMES = ("ln_in_g", "ln_in_b", "w_in", "vn_g", "vn_b", "gmat", "gn_g", "gn_b", "w_out", "ln1_g", "ln1_b",
                 "w_gate", "w_ple", "wr_hi", "wr_pair", "b_r")


def _prompt_branch(j, x_ref, p_ref, w, cwb_ref, ws_ref, bs_ref, outs, tail_ref, vch_ref, ubuf, yslab, last_j):
    t = T_PROMPT
    n = t // PROMPT_SPLIT
    n_slabs = C_CONV // LANES

    @pl.when(j == 0)
    def _():
        ubuf[:, 0:HIST, :] = jnp.zeros((n_slabs, HIST, LANES), F32)

    @pl.when(j > 0)
    def _():
        ubuf[:, 0:HIST, :] = ubuf[:, t:t + HIST, :]

    def front(h):
        xn, u, ug, v = _front(x_ref[h * n:(h + 1) * n, :], w)
        for s in range(n_slabs):
            ubuf[s, HIST + h * n:HIST + (h + 1) * n, :] = u[:, s * LANES:(s + 1) * LANES]
        return xn, u, ug, v

    def conv(h):
        rows = CONV_STRIDE * SUBLANES
        first = HIST - (CONV_W - 1)
        for s in range(n_slabs):
            for c in range(h * n // rows, (h + 1) * n // rows):
                accs = [None] * CONV_STRIDE
                for shift in range(CONV_STRIDE + CONV_W - 1):
                    win = ubuf[s, pl.ds(first + c * rows + shift, SUBLANES, stride=CONV_STRIDE), :]
                    for ph in range(CONV_STRIDE):
                        k = shift - ph
                        if 0 <= k < CONV_W:
                            term = cwb_ref[k, s] * win
                            accs[ph] = term if accs[ph] is None else accs[ph] + term
                for ph in range(CONV_STRIDE):
                    yslab[s, pl.ds(c * rows + ph, SUBLANES, stride=CONV_STRIDE), :] = accs[ph]
        y_conv = jnp.concatenate([yslab[s, h * n:(h + 1) * n, :] for s in range(n_slabs)], axis=1)
        return _group_norm_silu(y_conv, w["gmat"], w["gn_g"][...], w["gn_b"][...])

    def spatial_gate(ug, v):
        lane = lax.broadcasted_iota(jnp.int32, (CHUNK, LANES), 1)
        mixed_chunks = []
        for c in range(n // CHUNK):
            vc = v[c * CHUNK:(c + 1) * CHUNK, :]
            parts = []
            for q in range(N_HEADS // 2):
                vp = vc[:, q * LANES:(q + 1) * LANES]
                rhs = jnp.concatenate([jnp.where(lane < HEAD_DIM, vp, 0.0), jnp.where(lane >= HEAD_DIM, vp, 0.0)],
                                      axis=0).astype(BF16)
                parts.append(_dot(ws_ref[q], rhs))
            mixed_chunks.append(jnp.concatenate(parts, axis=1) + bs_ref[...])
        return ug * jnp.concatenate(mixed_chunks, axis=0)

    def tail(h, f, y_a, y_b):
        _tail(f[0], y_a, y_b, p_ref[h * n:(h + 1) * n, :], w, *outs, row0=h * n)

    fronts = [front(0)]
    y_as = {}
    for h in range(PROMPT_SPLIT):
        if h + 1 < PROMPT_SPLIT:
            fronts.append(front(h + 1))
        y_as[h] = conv(h)
        if h > 0:
            tail(h - 1, fronts[h - 1], y_as.pop(h - 1), spatial_gate(fronts[h - 1][2], fronts[h - 1][3]))
    last = PROMPT_SPLIT - 1
    tail(last, fronts[last], y_as.pop(last), spatial_gate(fronts[last][2], fronts[last][3]))

    @pl.when(j == last_j)
    def _():
        tail_ref[...] = fronts[last][1][n - HIST:, :]
        vch_ref[...] = fronts[last][3][n - CHUNK:, :]


def _sample_branch(x_ref, p_ref, cpad_ref, w, cw_ref, convu_ref, gw_ref, b8_ref, outs, u_out_ref, v_out_ref,
                   uscr, vscr, yscr, mscr):
    xn, u, ug, v = _front(x_ref[...], w)
    u_out_ref[...] = u
    v_out_ref[...] = v
    uscr[...] = u
    vscr[...] = v
    first = HIST - (CONV_W - 1)
    rows = SEQ_CHUNK * SUBLANES

    def step(c, carry):
        r0 = pl.multiple_of(c * rows, rows)
        s0 = c * SEQ_CHUNK
        u3 = uscr[pl.ds(r0, rows), :].reshape(SEQ_CHUNK, SUBLANES, C_CONV)
        v3 = vscr[pl.ds(r0, rows), :].reshape(SEQ_CHUNK, SUBLANES, C_GMLP)
        acc = jnp.zeros((SEQ_CHUNK, SUBLANES, C_CONV), F32)
        for k in range(CONV_W):
            acc = acc + cw_ref[pl.ds(k, 1), :][None] * cpad_ref[pl.ds(s0, SEQ_CHUNK), pl.ds(first + k, SUBLANES), :]
        mix = jnp.zeros((SEQ_CHUNK, SUBLANES, C_GMLP), F32) + b8_ref[...][None]
        for s in range(SUBLANES):
            acc = acc + u3[:, s:s + 1, :] * convu_ref[s][None]
            mix = mix + v3[:, s:s + 1, :] * gw_ref[s][None]
        yscr[pl.ds(r0, rows), :] = acc.reshape(rows, C_CONV)
        mscr[pl.ds(r0, rows), :] = mix.reshape(rows, C_GMLP)
        return carry

    lax.fori_loop(0, SEQS_PER_TILE // SEQ_CHUNK, step, 0)
    y_a = _group_norm_silu(yscr[...], w["gmat"], w["gn_g"][...], w["gn_b"][...])
    y_b = ug * mscr[...]
    _tail(xn, y_a, y_b, p_ref[...], w, *outs)


def _mixer_body(*refs, n_prompt_tiles, tiles_per_seq):
    n_w = len(_WEIGHT_NAMES)
    xp_ref, pp_ref, xs_ref, ps_ref, cpad_ref = refs[:5]
    w = dict(zip(_WEIGHT_NAMES, refs[5:5 + n_w]))
    cw_ref, cwb_ref, ws_ref, bs_ref, convu_ref, gw_ref, b8_ref = refs[5 + n_w:12 + n_w]
    outs = refs[12 + n_w:16 + n_w]
    tail_ref, vch_ref, u_out_ref, v_out_ref = refs[16 + n_w:20 + n_w]
    ubuf, yslab, yscr, uscr, vscr, mscr = refs[20 + n_w:]
    step = pl.program_id(0)

    @pl.when(step < n_prompt_tiles)
    def _():
        _prompt_branch(lax.rem(step, tiles_per_seq), xp_ref, pp_ref, w, cwb_ref, ws_ref, bs_ref, outs, tail_ref,
                       vch_ref, ubuf, yslab, tiles_per_seq - 1)

    @pl.when(step >= n_prompt_tiles)
    def _():
        _sample_branch(xs_ref, ps_ref, cpad_ref, w, cw_ref, convu_ref, gw_ref, b8_ref, outs, u_out_ref, v_out_ref,
                       uscr, vscr, yscr, mscr)


def _rank_body(idx_ref, tri_ref, rank_ref, counts_ref, carry):
    i = pl.program_id(0)

    @pl.when(i == 0)
    def _():
        carry[...] = jnp.zeros(carry.shape, F32)

    idx = idx_ref[...]
    lane = lax.broadcasted_iota(jnp.int32, idx.shape, 1)
    hots = [lane == idx[:, k:k + 1] for k in range(TOP_K)]
    multi = jnp.zeros(idx.shape, F32)
    for hot in hots:
        multi = multi + hot.astype(F32)
    before = _dot(tri_ref[...], multi.astype(BF16)) + carry[...]
    rank = jnp.zeros(idx.shape, F32)
    for k, hot in enumerate(hots):
        rk = jnp.sum(jnp.where(hot, before, 0.0), axis=-1, keepdims=True)
        rank = jnp.where(lane == k, rk, rank)
    rank_ref[...] = rank.astype(jnp.int32)
    carry[...] = carry[...] + jnp.sum(multi, axis=0, keepdims=True)
    counts_ref[...] = carry[...].astype(jnp.int32)


def _pad_bits():
    b = TM // 2
    while b >= 1:
        yield b
        b //= 2


def _token_rows(t, n=1):
    return pl.ds(pl.multiple_of(t * SUBLANES, SUBLANES), n * SUBLANES)


def _dispatch_body(pad_off_ref, pad_n_ref, n_used_ref, dest_ref, h_ref, xs_ref, zbuf, sem, zsem, *, n_blocks):
    i = pl.program_id(0)

    def issue(r, carry):
        src = h_ref.at[_token_rows(r), :]
        for k in range(TOP_K):
            d = dest_ref[0, r * TOP_K + k]
            pltpu.make_async_copy(src, xs_ref.at[_token_rows(d), :], sem).start(priority=k % 2)
        return carry

    lax.fori_loop(0, T_DISPATCH, issue, 0, unroll=8)

    @pl.when(i == 0)
    def _():
        zbuf[...] = jnp.zeros(zbuf.shape, F32)

        def zero_copy(off, b):
            return pltpu.make_async_copy(zbuf.at[pl.ds(0, b * SUBLANES), :], xs_ref.at[_token_rows(off, b), :], zsem)

        def start_or_wait(cond, cp, wait):
            @pl.when(cond)
            def _():
                if wait:
                    cp.wait()
                else:
                    cp.start()

        for wait in (False, True):
            for e in range(N_EXPERTS):
                n = pad_n_ref[e]
                for b in _pad_bits():
                    start_or_wait((n & b) != 0, zero_copy(pad_off_ref[e] + (n & ~(2 * b - 1)), b), wait)

        half = TM // 2

        def tail_block(wait):
            def go(blk, carry):
                for s in range(TM // half):
                    cp = zero_copy(blk * TM + s * half, half)
                    cp.wait() if wait else cp.start()
                return carry
            return go

        lax.fori_loop(n_used_ref[0], n_blocks, tail_block(False), 0)
        lax.fori_loop(n_used_ref[0], n_blocks, tail_block(True), 0)

    for _ in range(TOP_K):
        pltpu.make_async_copy(h_ref, xs_ref.at[_token_rows(0, T_DISPATCH), :], sem).wait()


def _expert_body(blk_e_ref, first_ref, slot_ref, next_ref, rows_ref, n_used_ref, x_ref, bu_ref, bd_ref, wu_hbm, wd_hbm,
                 y_ref, wu_f32, wd_f32, wu_bf, wd_bf, sems):
    i = pl.program_id(0)

    def weight_copies(e, s):
        return (pltpu.make_async_copy(wu_hbm.at[e], wu_f32.at[s], sems.at[s]),
                pltpu.make_async_copy(wd_hbm.at[e], wd_f32.at[s], sems.at[s]))

    @pl.when(i < n_used_ref[0])
    def _():
        s = slot_ref[i]

        @pl.when(first_ref[i] == 1)
        def _():
            @pl.when(i == 0)
            def _():
                for cp in weight_copies(blk_e_ref[i], s):
                    cp.start()

            for cp in weight_copies(blk_e_ref[i], s):
                cp.wait()
            nxt = next_ref[i]

            @pl.when(nxt >= 0)
            def _():
                for cp in weight_copies(nxt, 1 - s):
                    cp.start()

            chunk = D_MODEL // SUBLANES

            def cast_rows(c, carry):
                r = pl.multiple_of(c * chunk, chunk)
                wu_bf[pl.ds(r, chunk), :] = wu_f32[s, pl.ds(r, chunk), :].astype(BF16)
                wd_bf[pl.ds(r, chunk), :] = wd_f32[s, pl.ds(r, chunk), :].astype(BF16)
                return carry

            lax.fori_loop(0, SUBLANES, cast_rows, 0)

        def chain(c):
            x = _load_token_tiles(x_ref, TM_CHAIN, tok0=c * TM_CHAIN)
            hcat = _dot(x.astype(BF16), wu_bf[...]) + bu_ref[...]
            h_glu = jnp.minimum(hcat[:, :D_FF], SWIGLU_LIMIT)
            h_lin = jnp.clip(hcat[:, D_FF:], -SWIGLU_LIMIT, SWIGLU_LIMIT)
            act = h_glu * jax.nn.sigmoid(SWIGLU_ALPHA * h_glu) * (h_lin + 1.0)
            _store_token_tiles(y_ref, _dot(act.astype(BF16), wd_bf[...]) + bd_ref[...], tok0=c * TM_CHAIN)

        n_chains = TM // TM_CHAIN
        live = (rows_ref[i] + TM_CHAIN - 1) // TM_CHAIN
        for m in range(1, n_chains + 1):
            @pl.when(live == m)
            def _():
                for c in range(m):
                    chain(c)
                if m < n_chains:
                    y_ref[m * TM_CHAIN * SUBLANES:, :] = jnp.zeros(((n_chains - m) * TM_CHAIN * SUBLANES, LANES), F32)

    @pl.when(pl.program_id(0) >= n_used_ref[0])
    def _():
        y_ref[...] = jnp.zeros(y_ref.shape, F32)


def _combine_body(dest_ref, dest_next_ref, y_ref, base_ref, gate_ref, g_ref, b_ref, outp_ref, outs_ref, ybuf, sems, *,
                  n_prompt_tiles, n_tiles):
    i = pl.program_id(0)
    n_col = D_MODEL // LANES

    def issue_tile(d_ref, slot):
        def issue(g, carry):
            for j in range(SUBLANES):
                for k in range(TOP_K):
                    d = d_ref[0, g * (SUBLANES * TOP_K) + j * TOP_K + k]
                    pltpu.make_async_copy(y_ref.at[_token_rows(d), :], ybuf.at[slot * TOP_K + k, g, :, j, :],
                                          sems.at[slot]).start(priority=k % 2)
            return carry

        lax.fori_loop(0, T_ROWS // SUBLANES, issue, 0)

    slot = lax.rem(i, 2)

    @pl.when(i == 0)
    def _():
        issue_tile(dest_ref, 0)

    @pl.when(i + 1 < n_tiles)
    def _():
        issue_tile(dest_next_ref, 1 - slot)

    for k in range(TOP_K):
        pltpu.make_async_copy(ybuf.at[slot * TOP_K + k], ybuf.at[slot * TOP_K + k], sems.at[slot]).wait()

    row_groups = []
    for g in range(T_ROWS // SUBLANES):
        rows = slice(g * SUBLANES, (g + 1) * SUBLANES)
        gate = gate_ref[rows, :]
        tiles = []
        for c in range(n_col):
            acc = base_ref[rows, c * LANES:(c + 1) * LANES]
            for k in range(TOP_K):
                acc = acc + gate[:, k:k + 1] * ybuf[slot * TOP_K + k, g, c]
            tiles.append(acc)
        row_groups.append(jnp.concatenate(tiles, axis=1))
    out = _layer_norm(jnp.concatenate(row_groups, axis=0), g_ref[...], b_ref[...])

    @pl.when(i < n_prompt_tiles)
    def _():
        outp_ref[...] = out

    @pl.when(i >= n_prompt_tiles)
    def _():
        outs_ref[...] = out


def _full(shape):
    return pl.BlockSpec(shape, lambda *_: (0,) * len(shape), pipeline_mode=pl.Buffered(1))


def kernel(x_prompt, x_sample, cache_conv, p_prompt, p_sample, ln_in_g, ln_in_b, w_in, conv_w, gn_g, gn_b, vn_g, vn_b,
           w_spatial, b_spatial, w_out, ln1_g, ln1_b, w_router, b_router, w_up, b_up, w_down, b_down, w_ple,
           w_ple_gate, ln2_g, ln2_b):
    batch, seq, _ = x_prompt.shape
    dec_batch, dec_seq, _ = x_sample.shape
    assert w_in.shape[0] == DEPTH and dec_seq == SUBLANES and seq % T_PROMPT == 0
    n_prompt = batch * seq
    n_sample = dec_batch * dec_seq
    n_tok = n_prompt + n_sample
    t_s = SEQS_PER_TILE * dec_seq
    assert n_prompt % T_ROWS == 0 and n_sample % T_ROWS == 0 and n_tok % T_RANK == 0 and n_sample % t_s == 0
    assert n_tok % T_DISPATCH == 0

    row = lambda a: a.reshape(1, -1).astype(F32)
    gidx = jnp.arange(MXU_DIM) // (C_CONV // N_CONV_GROUPS)
    gmat = jnp.where(gidx[:, None] == gidx[None, :], 1.0 / (C_CONV // N_CONV_GROUPS), 0.0).astype(BF16)
    wr_pad = jnp.pad(w_router[0].astype(F32), ((0, 0), (0, LANES - N_EXPERTS)))
    wr_hi = wr_pad.astype(BF16)
    wr_lo = (wr_pad - wr_hi.astype(F32)).astype(BF16)
    weights = dict(
        ln_in_g=row(ln_in_g), ln_in_b=row(ln_in_b), w_in=w_in[0].astype(BF16), vn_g=row(vn_g[0]), vn_b=row(vn_b[0]),
        gmat=gmat, gn_g=row(gn_g[0]), gn_b=row(gn_b[0]), w_out=w_out[0].astype(BF16), ln1_g=row(ln1_g[0]),
        ln1_b=row(ln1_b[0]), w_gate=w_ple_gate[0].astype(BF16), w_ple=w_ple[0].astype(BF16), wr_hi=wr_hi,
        wr_pair=jnp.concatenate([wr_hi, wr_lo], axis=1),
        b_r=jnp.pad(row(b_router[0]), ((0, 0), (0, LANES - N_EXPERTS))))
    w_list = [weights[n] for n in _WEIGHT_NAMES]
    w_specs = [_full(a.shape) for a in w_list]
    cw = jnp.pad(conv_w[0].astype(F32), ((0, HIST - CONV_W), (0, 0)))
    causal = jnp.tril(jnp.ones((CHUNK, CHUNK), bool))
    ws_m = jnp.where(causal[None], w_spatial[0], 0.0)
    ws_cat = jnp.concatenate([ws_m[0::2], ws_m[1::2]], axis=2).astype(BF16)
    bs_full = jnp.repeat(b_spatial[0].T.astype(F32), HEAD_DIM, axis=1)
    s_i = jnp.arange(SUBLANES)[:, None]
    t_i = jnp.arange(SUBLANES)[None, :]
    tap = jnp.clip(CONV_W - 1 - t_i + s_i, 0, CONV_W - 1)
    convu = jnp.where((s_i <= t_i)[:, :, None], conv_w[0].astype(F32)[tap], 0.0)
    gw8 = jnp.transpose(ws_m[:, :SUBLANES, :SUBLANES], (2, 1, 0))
    gw8 = jnp.repeat(gw8.astype(F32), HEAD_DIM, axis=2)
    b8 = bs_full[:SUBLANES]
    cpad = jnp.pad(cache_conv[0].astype(F32), ((0, 0), (HIST - (CONV_W - 1), SUBLANES), (0, 0)))

    cparams = lambda sem: pltpu.CompilerParams(dimension_semantics=sem, vmem_limit_bytes=VMEM_LIMIT)

    nj = seq // T_PROMPT
    npt = n_prompt // T_PROMPT
    assert t_s == T_PROMPT
    pstep = lambda i: jnp.minimum(i, npt - 1)
    sstep = lambda i: jnp.maximum(i - npt, 0)
    tok_block = lambda width: pl.BlockSpec((T_PROMPT, width), lambda i: (i, 0))
    cwb = jnp.broadcast_to(conv_w[0].astype(F32).reshape(CONV_W, C_CONV // LANES, 1, LANES),
                           (CONV_W, C_CONV // LANES, SUBLANES, LANES))
    tables = [cw, cwb, ws_cat, bs_full, convu, gw8, b8]
    h_all, base_all, idx_all, gate_all, u_tail, v_chunk, u_s, v_s = pl.pallas_call(
        functools.partial(_mixer_body, n_prompt_tiles=npt, tiles_per_seq=nj),
        grid=(npt + n_sample // t_s,),
        in_specs=[pl.BlockSpec((None, T_PROMPT, D_MODEL), lambda i: (pstep(i) // nj, pstep(i) % nj, 0)),
                  pl.BlockSpec((None, T_PROMPT, PLE_DIM), lambda i: (pstep(i) // nj, pstep(i) % nj, 0)),
                  pl.BlockSpec((t_s, D_MODEL), lambda i: (sstep(i), 0)),
                  pl.BlockSpec((t_s, PLE_DIM), lambda i: (sstep(i), 0)),
                  pl.BlockSpec((SEQS_PER_TILE,) + cpad.shape[1:], lambda i: (sstep(i), 0, 0),
                               pipeline_mode=pl.Buffered(1))]
                 + w_specs + [_full(a.shape) for a in tables],
        out_specs=[pl.BlockSpec((T_PROMPT * SUBLANES, LANES), lambda i: (i, 0)),
                   tok_block(D_MODEL), tok_block(LANES), tok_block(LANES),
                   pl.BlockSpec((None, HIST, C_CONV), lambda i: (pstep(i) // nj, 0, 0)),
                   pl.BlockSpec((None, CHUNK, C_GMLP), lambda i: (pstep(i) // nj, 0, 0)),
                   pl.BlockSpec((t_s, C_CONV), lambda i: (sstep(i), 0)),
                   pl.BlockSpec((t_s, C_GMLP), lambda i: (sstep(i), 0))],
        out_shape=(jax.ShapeDtypeStruct((n_tok * SUBLANES, LANES), F32), jax.ShapeDtypeStruct((n_tok, D_MODEL), F32),
                   jax.ShapeDtypeStruct((n_tok, LANES), jnp.int32), jax.ShapeDtypeStruct((n_tok, LANES), F32),
                   jax.ShapeDtypeStruct((batch, HIST, C_CONV), F32), jax.ShapeDtypeStruct((batch, CHUNK, C_GMLP), F32),
                   jax.ShapeDtypeStruct((n_sample, C_CONV), F32), jax.ShapeDtypeStruct((n_sample, C_GMLP), F32)),
        scratch_shapes=[pltpu.VMEM((C_CONV // LANES, T_PROMPT + HIST, LANES), F32),
                        pltpu.VMEM((C_CONV // LANES, T_PROMPT, LANES), F32)]
                       + [pltpu.VMEM((T_PROMPT, C_CONV), F32)] * 4,
        compiler_params=pltpu.CompilerParams(dimension_semantics=("arbitrary",), vmem_limit_bytes=BIG_VMEM_LIMIT),
        name="mixer",
    )(x_prompt, p_prompt[0], x_sample.reshape(n_sample, D_MODEL), p_sample[0].reshape(n_sample, PLE_DIM), cpad,
      *w_list, *tables)

    tri = (jnp.arange(T_RANK)[:, None] > jnp.arange(T_RANK)[None, :]).astype(BF16)
    rank_all, counts = pl.pallas_call(
        _rank_body,
        grid=(n_tok // T_RANK,),
        in_specs=[pl.BlockSpec((T_RANK, LANES), lambda i: (i, 0)), _full(tri.shape)],
        out_specs=[pl.BlockSpec((T_RANK, LANES), lambda i: (i, 0)), _full((1, LANES))],
        out_shape=(jax.ShapeDtypeStruct((n_tok, LANES), jnp.int32), jax.ShapeDtypeStruct((1, LANES), jnp.int32)),
        scratch_shapes=[pltpu.VMEM((1, LANES), F32)],
        compiler_params=cparams(("arbitrary",)),
        name="rank",
    )(idx_all, tri)

    n_assign = n_tok * TOP_K
    n_blocks = n_assign // TM + N_EXPERTS
    cap = n_blocks * TM
    cnt = counts[0, :N_EXPERTS]
    padded = (cnt + TM - 1) // TM * TM
    p_end = jnp.cumsum(padded)
    p_start = p_end - padded
    e_ids = jnp.arange(N_EXPERTS, dtype=jnp.int32)
    lookup = lambda table, ids: jnp.sum(jnp.where(ids[..., None] == e_ids, table, 0), axis=-1)
    dest = (lookup(p_start, idx_all[:, :TOP_K]) + rank_all[:, :TOP_K]).astype(jnp.int32)
    dest_tiles = dest.reshape(n_tok // T_ROWS, 1, T_ROWS * TOP_K)
    blk_row = jnp.arange(n_blocks, dtype=jnp.int32) * TM
    blk_e = jnp.minimum(jnp.sum(p_end[None, :] <= blk_row[:, None], axis=1), N_EXPERTS - 1).astype(jnp.int32)
    n_used = (p_end[-1:] // TM).astype(jnp.int32)
    pad_off = (p_start + cnt).astype(jnp.int32)
    pad_n = (padded - cnt).astype(jnp.int32)
    used = cnt > 0
    slot_e = (jnp.cumsum(used.astype(jnp.int32)) - 1) & 1
    later_used = jnp.where(used[None, :] & (e_ids[None, :] > e_ids[:, None]), e_ids[None, :], N_EXPERTS)
    next_e = jnp.min(later_used, axis=1)
    next_e = jnp.where(next_e < N_EXPERTS, next_e, -1).astype(jnp.int32)
    blk_first = ((blk_row == lookup(p_start, blk_e)) & (blk_row < p_end[-1])).astype(jnp.int32)
    blk_rows = jnp.clip(lookup(p_start + cnt, blk_e) - blk_row, 0, TM).astype(jnp.int32)
    blk_slot = lookup(slot_e, blk_e).astype(jnp.int32)
    blk_next = lookup(next_e, blk_e).astype(jnp.int32)

    dest_spec = pl.BlockSpec((None, 1, T_ROWS * TOP_K), lambda i, *_: (i, 0, 0), memory_space=pltpu.SMEM)

    x_sorted = pl.pallas_call(
        functools.partial(_dispatch_body, n_blocks=n_blocks),
        grid_spec=pltpu.PrefetchScalarGridSpec(
            num_scalar_prefetch=3,
            grid=(n_tok // T_DISPATCH,),
            in_specs=[pl.BlockSpec((None, 1, T_DISPATCH * TOP_K), lambda i, *_: (i, 0, 0), memory_space=pltpu.SMEM),
                      pl.BlockSpec((T_DISPATCH * SUBLANES, LANES), lambda i, *_: (i, 0))],
            out_specs=pl.BlockSpec(memory_space=pl.ANY),
            scratch_shapes=[pltpu.VMEM((TM // 2 * SUBLANES, LANES), F32), pltpu.SemaphoreType.DMA,
                            pltpu.SemaphoreType.DMA]),
        out_shape=jax.ShapeDtypeStruct((cap * SUBLANES, LANES), F32),
        compiler_params=cparams(("arbitrary",)),
        name="dispatch",
    )(pad_off, pad_n, n_used, dest.reshape(n_tok // T_DISPATCH, 1, T_DISPATCH * TOP_K), h_all)

    last = lambda i, nu: jnp.minimum(i, nu[0] - 1)
    y_sorted = pl.pallas_call(
        _expert_body,
        grid_spec=pltpu.PrefetchScalarGridSpec(
            num_scalar_prefetch=6,
            grid=(n_blocks,),
            in_specs=[pl.BlockSpec((TM * SUBLANES, LANES), lambda i, be, bf, bs, bn, br, nu: (last(i, nu), 0)),
                      pl.BlockSpec((None, 1, 2 * D_FF), lambda i, be, bf, bs, bn, br, nu: (be[last(i, nu)], 0, 0)),
                      pl.BlockSpec((None, 1, D_MODEL), lambda i, be, bf, bs, bn, br, nu: (be[last(i, nu)], 0, 0)),
                      pl.BlockSpec(memory_space=pl.ANY), pl.BlockSpec(memory_space=pl.ANY)],
            out_specs=pl.BlockSpec((TM * SUBLANES, LANES), lambda i, be, bf, bs, bn, br, nu: (i, 0)),
            scratch_shapes=[pltpu.VMEM((2, D_MODEL, 2 * D_FF), F32), pltpu.VMEM((2, D_FF, D_MODEL), F32),
                            pltpu.VMEM((D_MODEL, 2 * D_FF), BF16), pltpu.VMEM((D_FF, D_MODEL), BF16),
                            pltpu.SemaphoreType.DMA((2,))]),
        out_shape=jax.ShapeDtypeStruct((cap * SUBLANES, LANES), F32),
        compiler_params=pltpu.CompilerParams(dimension_semantics=("arbitrary",), vmem_limit_bytes=BIG_VMEM_LIMIT),
        name="experts",
    )(blk_e, blk_first, blk_slot, blk_next, blk_rows, n_used, x_sorted, b_up[0].astype(F32)[:, None, :],
      b_down[0].astype(F32)[:, None, :], w_up[0].astype(F32), w_down[0].astype(F32))

    npt = n_prompt // T_ROWS
    n_tiles = n_tok // T_ROWS
    dest_next_spec = pl.BlockSpec((None, 1, T_ROWS * TOP_K), lambda i: (jnp.minimum(i + 1, n_tiles - 1), 0, 0),
                                  memory_space=pltpu.SMEM)
    out_p, out_s = pl.pallas_call(
        functools.partial(_combine_body, n_prompt_tiles=npt, n_tiles=n_tiles),
        grid=(n_tiles,),
        in_specs=[dest_spec, dest_next_spec, pl.BlockSpec(memory_space=pl.ANY),
                  pl.BlockSpec((T_ROWS, D_MODEL), lambda i: (i, 0)), pl.BlockSpec((T_ROWS, LANES), lambda i: (i, 0)),
                  _full((1, D_MODEL)), _full((1, D_MODEL))],
        out_specs=[pl.BlockSpec((T_ROWS, D_MODEL), lambda i: (jnp.minimum(i, npt - 1), 0)),
                   pl.BlockSpec((T_ROWS, D_MODEL), lambda i: (jnp.maximum(i - npt, 0), 0))],
        out_shape=(jax.ShapeDtypeStruct((n_prompt, D_MODEL), F32), jax.ShapeDtypeStruct((n_sample, D_MODEL), F32)),
        scratch_shapes=[pltpu.VMEM((2 * TOP_K, T_ROWS // SUBLANES, D_MODEL // LANES, SUBLANES, LANES), F32),
                        pltpu.SemaphoreType.DMA((2,))],
        compiler_params=cparams(("arbitrary",)),
        name="combine",
    )(dest_tiles, dest_tiles, y_sorted, base_all, gate_all, row(ln2_g[0]), row(ln2_b[0]))

    y_prompt = out_p.reshape(batch, seq, D_MODEL)
    y_sample = out_s.reshape(dec_batch, dec_seq, D_MODEL)
    conv_state_prompt = u_tail[None, :, HIST - (CONV_W - 1):, :]
    u_s3 = u_s.reshape(dec_batch, dec_seq, C_CONV)
    conv_state_sample = jnp.concatenate([cache_conv[0][:, dec_seq:, :].astype(F32), u_s3], axis=1)[None]
    chunk_v_prompt = v_chunk[None]
    chunk_v_sample = v_s.reshape(1, dec_batch, dec_seq, C_GMLP)
    return (y_prompt, y_sample, conv_state_prompt, conv_state_sample, chunk_v_prompt, chunk_v_sample)
```

```python
import functools

import jax
import jax.numpy as jnp
from jax import lax
from jax.experimental import pallas as pl
from jax.experimental.pallas import tpu as pltpu

F32 = jnp.float32
BF16 = jnp.bfloat16

D_MODEL = 1024
C_CONV = 512
C_GMLP = 512
N_CONV_GROUPS = 8
CONV_W = 31
N_HEADS = 8
HEAD_DIM = C_GMLP // N_HEADS
CHUNK = 128
N_EXPERTS = 32
TOP_K = 4
D_FF = 1024
PLE_DIM = 256
SWIGLU_LIMIT = 7.0
SWIGLU_ALPHA = 1.702
LN_EPS = 1e-5
DEPTH = 1
DEEPNORM_ALPHA = (2.0 * DEPTH) ** 0.25

LANES = 128
SUBLANES = 8
MXU_DIM = 256
VMEM_LIMIT = 48 * 1024 * 1024
BIG_VMEM_LIMIT = 56 * 1024 * 1024

T_PROMPT = 512
PROMPT_SPLIT = 2
HIST = 32
CONV_STRIDE = 4
SEQS_PER_TILE = 64
SEQ_CHUNK = 4
T_RANK = 1024
T_DISPATCH = 1024
T_ROWS = 256
TM = 512
TM_CHAIN = 256


def _dot(a, b):
    return jnp.dot(a, b, preferred_element_type=F32)


def _layer_norm(x, g, b):
    mu = jnp.mean(x, axis=-1, keepdims=True)
    xc = x - mu
    var = jnp.mean(xc * xc, axis=-1, keepdims=True)
    return xc * lax.rsqrt(var + LN_EPS) * g + b


def _split_bf16(a):
    hi = a.astype(BF16)
    lo = (a - hi.astype(F32)).astype(BF16)
    return hi, lo


def _group_mean(a, gmat_ref):
    hi, lo = _split_bf16(a)
    g = gmat_ref[...]
    outs = []
    for s in range(C_CONV // MXU_DIM):
        sl = slice(MXU_DIM * s, MXU_DIM * (s + 1))
        outs.append(_dot(hi[:, sl], g) + _dot(lo[:, sl], g))
    return jnp.concatenate(outs, axis=1)


def _group_norm_silu(y, gmat_ref, gn_g, gn_b):
    mu = _group_mean(y, gmat_ref)
    yc = y - mu
    var = _group_mean(yc * yc, gmat_ref)
    yn = yc * lax.rsqrt(var + LN_EPS) * gn_g + gn_b
    return yn * jax.nn.sigmoid(yn)


def _store_token_tiles(ref, val, tok0=0):
    n = val.shape[0]
    for c in range(D_MODEL // LANES):
        ref[pl.ds(tok0 * SUBLANES + c, n, stride=SUBLANES), :] = val[:, c * LANES:(c + 1) * LANES]


def _load_token_tiles(ref, n, lead=(), tok0=0):
    parts = [ref[lead + (pl.ds(tok0 * SUBLANES + c, n, stride=SUBLANES), slice(None))]
             for c in range(D_MODEL // LANES)]
    return jnp.concatenate(parts, axis=1)


def _front(x, w):
    xn = _layer_norm(x, w["ln_in_g"][...], w["ln_in_b"][...])
    z = _dot(xn.astype(BF16), w["w_in"][...])
    a_val = z[:, 0:C_CONV]
    a_gate = z[:, C_CONV:2 * C_CONV]
    g_u = z[:, 2 * C_CONV:2 * C_CONV + C_GMLP]
    g_v = z[:, 2 * C_CONV + C_GMLP:]
    u = a_val * jax.nn.sigmoid(a_gate)
    ug = jax.nn.gelu(g_u)
    v = _layer_norm(jax.nn.gelu(g_v), w["vn_g"][...], w["vn_b"][...])
    return xn, u, ug, v


def _tail(xn, y_a, y_b, p, w, h_ref, base_ref, idx_ref, gate_ref, row0=0):
    rows = slice(row0, row0 + xn.shape[0])
    mix = _dot(y_a.astype(BF16), w["w_out"][0:C_CONV, :]) + _dot(y_b.astype(BF16), w["w_out"][C_CONV:, :])
    h = _layer_norm(DEEPNORM_ALPHA * xn + mix, w["ln1_g"][...], w["ln1_b"][...])
    hb, h_lo = _split_bf16(h)
    ple = _dot(p.astype(BF16), w["w_ple"][...]) * jax.nn.sigmoid(_dot(hb, w["w_gate"][...]))
    _store_token_tiles(h_ref, h, row0)
    base_ref[rows, :] = DEEPNORM_ALPHA * h + ple

    pair = _dot(hb, w["wr_pair"][...])
    logits = pair[:, :LANES] + pair[:, LANES:] + _dot(h_lo, w["wr_hi"][...]) + w["b_r"][...]
    lane = lax.broadcasted_iota(jnp.int32, logits.shape, 1)
    lane_f = lane.astype(F32)
    vals = jnp.where(lane < N_EXPERTS, logits, -jnp.inf)
    tops, ids = [], []
    for _ in range(TOP_K):
        m = jnp.max(vals, axis=-1, keepdims=True)
        i = jnp.min(jnp.where(vals == m, lane_f, float(LANES)), axis=-1, keepdims=True)
        vals = jnp.where(lane_f == i, -jnp.inf, vals)
        tops.append(m)
        ids.append(i)
    exps = [jnp.exp(m - tops[0]) for m in tops]
    denom = exps[0] + exps[1] + exps[2] + exps[3]
    idx_out = jnp.zeros(logits.shape, F32)
    gate_out = jnp.zeros(logits.shape, F32)
    for k in range(TOP_K):
        idx_out = jnp.where(lane == k, ids[k], idx_out)
        gate_out = jnp.where(lane == k, exps[k] / denom, gate_out)
    idx_ref[rows, :] = idx_out.astype(jnp.int32)
    gate_ref[rows, :] = gate_out


_WEIGHT_NAMES = ("ln_in_g", "ln_in_b", "w_in", "vn_g", "vn_b", "gmat", "gn_g", "gn_b", "w_out", "ln1_g", "ln1_b",
                 "w_gate", "w_ple", "wr_hi", "wr_pair", "b_r")


def _prompt_branch(j, x_ref, p_ref, w, cwb_ref, ws_ref, bs_ref, outs, tail_ref, vch_ref, ubuf, yslab, last_j):
    t = T_PROMPT
    n = t // PROMPT_SPLIT
    n_slabs = C_CONV // LANES

    @pl.when(j == 0)
    def _():
        ubuf[:, 0:HIST, :] = jnp.zeros((n_slabs, HIST, LANES), F32)

    @pl.when(j > 0)
    def _():
        ubuf[:, 0:HIST, :] = ubuf[:, t:t + HIST, :]

    def front(h):
        xn, u, ug, v = _front(x_ref[h * n:(h + 1) * n, :], w)
        for s in range(n_slabs):
            ubuf[s, HIST + h * n:HIST + (h + 1) * n, :] = u[:, s * LANES:(s + 1) * LANES]
        return xn, u, ug, v

    def conv(h):
        rows = CONV_STRIDE * SUBLANES
        first = HIST - (CONV_W - 1)
        for s in range(n_slabs):
            for c in range(h * n // rows, (h + 1) * n // rows):
                accs = [None] * CONV_STRIDE
                for shift in range(CONV_STRIDE + CONV_W - 1):
                    win = ubuf[s, pl.ds(first + c * rows + shift, SUBLANES, stride=CONV_STRIDE), :]
                    for ph in range(CONV_STRIDE):
                        k = shift - ph
                        if 0 <= k < CONV_W:
                            term = cwb_ref[k, s] * win
                            accs[ph] = term if accs[ph] is None else accs[ph] + term
                for ph in range(CONV_STRIDE):
                    yslab[s, pl.ds(c * rows + ph, SUBLANES, stride=CONV_STRIDE), :] = accs[ph]
        y_conv = jnp.concatenate([yslab[s, h * n:(h + 1) * n, :] for s in range(n_slabs)], axis=1)
        return _group_norm_silu(y_conv, w["gmat"], w["gn_g"][...], w["gn_b"][...])

    def spatial_gate(ug, v):
        lane = lax.broadcasted_iota(jnp.int32, (CHUNK, LANES), 1)
        mixed_chunks = []
        for c in range(n // CHUNK):
            vc = v[c * CHUNK:(c + 1) * CHUNK, :]
            parts = []
            for q in range(N_HEADS // 2):
                vp = vc[:, q * LANES:(q + 1) * LANES]
                rhs = jnp.concatenate([jnp.where(lane < HEAD_DIM, vp, 0.0), jnp.where(lane >= HEAD_DIM, vp, 0.0)],
                                      axis=0).astype(BF16)
                parts.append(_dot(ws_ref[q], rhs))
            mixed_chunks.append(jnp.concatenate(parts, axis=1) + bs_ref[...])
        return ug * jnp.concatenate(mixed_chunks, axis=0)

    def tail(h, f, y_a, y_b):
        _tail(f[0], y_a, y_b, p_ref[h * n:(h + 1) * n, :], w, *outs, row0=h * n)

    fronts = [front(0)]
    y_as = {}
    for h in range(PROMPT_SPLIT):
        if h + 1 < PROMPT_SPLIT:
            fronts.append(front(h + 1))
        y_as[h] = conv(h)
        if h > 0:
            tail(h - 1, fronts[h - 1], y_as.pop(h - 1), spatial_gate(fronts[h - 1][2], fronts[h - 1][3]))
    last = PROMPT_SPLIT - 1
    tail(last, fronts[last], y_as.pop(last), spatial_gate(fronts[last][2], fronts[last][3]))

    @pl.when(j == last_j)
    def _():
        tail_ref[...] = fronts[last][1][n - HIST:, :]
        vch_ref[...] = fronts[last][3][n - CHUNK:, :]


def _sample_branch(x_ref, p_ref, cpad_ref, w, cw_ref, convu_ref, gw_ref, b8_ref, outs, u_out_ref, v_out_ref,
                   uscr, vscr, yscr, mscr):
    xn, u, ug, v = _front(x_ref[...], w)
    u_out_ref[...] = u
    v_out_ref[...] = v
    uscr[...] = u
    vscr[...] = v
    first = HIST - (CONV_W - 1)
    rows = SEQ_CHUNK * SUBLANES

    def step(c, carry):
        r0 = pl.multiple_of(c * rows, rows)
        s0 = c * SEQ_CHUNK
        u3 = uscr[pl.ds(r0, rows), :].reshape(SEQ_CHUNK, SUBLANES, C_CONV)
        v3 = vscr[pl.ds(r0, rows), :].reshape(SEQ_CHUNK, SUBLANES, C_GMLP)
        acc = jnp.zeros((SEQ_CHUNK, SUBLANES, C_CONV), F32)
        for k in range(CONV_W):
            acc = acc + cw_ref[pl.ds(k, 1), :][None] * cpad_ref[pl.ds(s0, SEQ_CHUNK), pl.ds(first + k, SUBLANES), :]
        mix = jnp.zeros((SEQ_CHUNK, SUBLANES, C_GMLP), F32) + b8_ref[...][None]
        for s in range(SUBLANES):
            acc = acc + u3[:, s:s + 1, :] * convu_ref[s][None]
            mix = mix + v3[:, s:s + 1, :] * gw_ref[s][None]
        yscr[pl.ds(r0, rows), :] = acc.reshape(rows, C_CONV)
        mscr[pl.ds(r0, rows), :] = mix.reshape(rows, C_GMLP)
        return carry

    lax.fori_loop(0, SEQS_PER_TILE // SEQ_CHUNK, step, 0)
    y_a = _group_norm_silu(yscr[...], w["gmat"], w["gn_g"][...], w["gn_b"][...])
    y_b = ug * mscr[...]
    _tail(xn, y_a, y_b, p_ref[...], w, *outs)


def _mixer_body(*refs, n_prompt_tiles, tiles_per_seq):
    n_w = len(_WEIGHT_NAMES)
    xp_ref, pp_ref, xs_ref, ps_ref, cpad_ref = refs[:5]
    w = dict(zip(_WEIGHT_NAMES, refs[5:5 + n_w]))
    cw_ref, cwb_ref, ws_ref, bs_ref, convu_ref, gw_ref, b8_ref = refs[5 + n_w:12 + n_w]
    outs = refs[12 + n_w:16 + n_w]
    tail_ref, vch_ref, u_out_ref, v_out_ref = refs[16 + n_w:20 + n_w]
    ubuf, yslab, yscr, uscr, vscr, mscr = refs[20 + n_w:]
    step = pl.program_id(0)

    @pl.when(step < n_prompt_tiles)
    def _():
        _prompt_branch(lax.rem(step, tiles_per_seq), xp_ref, pp_ref, w, cwb_ref, ws_ref, bs_ref, outs, tail_ref,
                       vch_ref, ubuf, yslab, tiles_per_seq - 1)

    @pl.when(step >= n_prompt_tiles)
    def _():
        _sample_branch(xs_ref, ps_ref, cpad_ref, w, cw_ref, convu_ref, gw_ref, b8_ref, outs, u_out_ref, v_out_ref,
                       uscr, vscr, yscr, mscr)


def _rank_body(idx_ref, tri_ref, before_ref, counts_ref, carry):
    i = pl.program_id(0)

    @pl.when(i == 0)
    def _():
        carry[...] = jnp.zeros(carry.shape, F32)

    idx = idx_ref[...]
    lane = lax.broadcasted_iota(jnp.int32, idx.shape, 1)
    multi = jnp.zeros(idx.shape, F32)
    for k in range(TOP_K):
        multi = multi + (lane == idx[:, k:k + 1]).astype(F32)
    before_ref[...] = (_dot(tri_ref[...], multi.astype(BF16)) + carry[...]).astype(jnp.int32)
    carry[...] = carry[...] + jnp.sum(multi, axis=0, keepdims=True)
    counts_ref[...] = carry[...].astype(jnp.int32)


def _pad_bits():
    b = TM // 2
    while b >= 1:
        yield b
        b //= 2


def _token_rows(t, n=1):
    return pl.ds(pl.multiple_of(t * SUBLANES, SUBLANES), n * SUBLANES)


def _dispatch_body(pad_off_ref, pad_n_ref, n_used_ref, dest_ref, h_ref, xs_ref, zbuf, sem, zsem, *, n_blocks):
    i = pl.program_id(0)

    def issue(r, carry):
        src = h_ref.at[_token_rows(r), :]
        for k in range(TOP_K):
            d = dest_ref[0, r * TOP_K + k]
            pltpu.make_async_copy(src, xs_ref.at[_token_rows(d), :], sem).start(priority=k % 2)
        return carry

    lax.fori_loop(0, T_DISPATCH, issue, 0, unroll=8)

    @pl.when(i == 0)
    def _():
        zbuf[...] = jnp.zeros(zbuf.shape, F32)

        def zero_copy(off, b):
            return pltpu.make_async_copy(zbuf.at[pl.ds(0, b * SUBLANES), :], xs_ref.at[_token_rows(off, b), :], zsem)

        def start_or_wait(cond, cp, wait):
            @pl.when(cond)
            def _():
                if wait:
                    cp.wait()
                else:
                    cp.start()

        for wait in (False, True):
            for e in range(N_EXPERTS):
                n = pad_n_ref[e]
                for b in _pad_bits():
                    start_or_wait((n & b) != 0, zero_copy(pad_off_ref[e] + (n & ~(2 * b - 1)), b), wait)

        half = TM // 2

        def tail_block(wait):
            def go(blk, carry):
                for s in range(TM // half):
                    cp = zero_copy(blk * TM + s * half, half)
                    cp.wait() if wait else cp.start()
                return carry
            return go

        lax.fori_loop(n_used_ref[0], n_blocks, tail_block(False), 0)
        lax.fori_loop(n_used_ref[0], n_blocks, tail_block(True), 0)

    for _ in range(TOP_K):
        pltpu.make_async_copy(h_ref, xs_ref.at[_token_rows(0, T_DISPATCH), :], sem).wait()


def _expert_body(blk_e_ref, first_ref, slot_ref, next_ref, rows_ref, n_used_ref, x_ref, bu_ref, bd_ref, wu_hbm, wd_hbm,
                 y_ref, wu_f32, wd_f32, wu_bf, wd_bf, sems):
    i = pl.program_id(0)

    def weight_copies(e, s):
        return (pltpu.make_async_copy(wu_hbm.at[e], wu_f32.at[s], sems.at[s]),
                pltpu.make_async_copy(wd_hbm.at[e], wd_f32.at[s], sems.at[s]))

    @pl.when(i < n_used_ref[0])
    def _():
        s = slot_ref[i]

        @pl.when(first_ref[i] == 1)
        def _():
            @pl.when(i == 0)
            def _():
                for cp in weight_copies(blk_e_ref[i], s):
                    cp.start()

            for cp in weight_copies(blk_e_ref[i], s):
                cp.wait()
            nxt = next_ref[i]

            @pl.when(nxt >= 0)
            def _():
                for cp in weight_copies(nxt, 1 - s):
                    cp.start()

            chunk = D_MODEL // SUBLANES

            def cast_rows(c, carry):
                r = pl.multiple_of(c * chunk, chunk)
                wu_bf[pl.ds(r, chunk), :] = wu_f32[s, pl.ds(r, chunk), :].astype(BF16)
                wd_bf[pl.ds(r, chunk), :] = wd_f32[s, pl.ds(r, chunk), :].astype(BF16)
                return carry

            lax.fori_loop(0, SUBLANES, cast_rows, 0)

        def chain(c):
            x = _load_token_tiles(x_ref, TM_CHAIN, tok0=c * TM_CHAIN)
            hcat = _dot(x.astype(BF16), wu_bf[...]) + bu_ref[...]
            h_glu = jnp.minimum(hcat[:, :D_FF], SWIGLU_LIMIT)
            h_lin = jnp.clip(hcat[:, D_FF:], -SWIGLU_LIMIT, SWIGLU_LIMIT)
            act = h_glu * jax.nn.sigmoid(SWIGLU_ALPHA * h_glu) * (h_lin + 1.0)
            _store_token_tiles(y_ref, _dot(act.astype(BF16), wd_bf[...]) + bd_ref[...], tok0=c * TM_CHAIN)

        n_chains = TM // TM_CHAIN
        live = (rows_ref[i] + TM_CHAIN - 1) // TM_CHAIN
        for m in range(1, n_chains + 1):
            @pl.when(live == m)
            def _():
                for c in range(m):
                    chain(c)
                if m < n_chains:
                    y_ref[m * TM_CHAIN * SUBLANES:, :] = jnp.zeros(((n_chains - m) * TM_CHAIN * SUBLANES, LANES), F32)

    @pl.when(pl.program_id(0) >= n_used_ref[0])
    def _():
        y_ref[...] = jnp.zeros(y_ref.shape, F32)


def _combine_body(dest_ref, dest_next_ref, y_ref, base_ref, gate_ref, g_ref, b_ref, outp_ref, outs_ref, ybuf, sems, *,
                  n_prompt_tiles, n_tiles):
    i = pl.program_id(0)

    def issue_tile(d_ref, tile, slot):
        first = lax.rem(tile, T_DISPATCH // T_ROWS) * (T_ROWS * TOP_K)

        def issue(r, carry):
            for k in range(TOP_K):
                d = d_ref[0, first + r * TOP_K + k]
                pltpu.make_async_copy(y_ref.at[_token_rows(d), :], ybuf.at[slot * TOP_K + k, _token_rows(r), :],
                                      sems.at[slot]).start(priority=k % 2)
            return carry

        lax.fori_loop(0, T_ROWS, issue, 0, unroll=8)

    slot = lax.rem(i, 2)

    @pl.when(i == 0)
    def _():
        issue_tile(dest_ref, i, 0)

    @pl.when(i + 1 < n_tiles)
    def _():
        issue_tile(dest_next_ref, i + 1, 1 - slot)

    for k in range(TOP_K):
        pltpu.make_async_copy(y_ref.at[_token_rows(0, T_ROWS), :], ybuf.at[slot * TOP_K + k], sems.at[slot]).wait()

    gate = gate_ref[...]
    acc = base_ref[...]
    for k in range(TOP_K):
        acc = acc + gate[:, k:k + 1] * _load_token_tiles(ybuf, T_ROWS, lead=(slot * TOP_K + k,))
    out = _layer_norm(acc, g_ref[...], b_ref[...])

    @pl.when(i < n_prompt_tiles)
    def _():
        outp_ref[...] = out

    @pl.when(i >= n_prompt_tiles)
    def _():
        outs_ref[...] = out


def _full(shape):
    return pl.BlockSpec(shape, lambda *_: (0,) * len(shape), pipeline_mode=pl.Buffered(1))


def kernel(x_prompt, x_sample, cache_conv, p_prompt, p_sample, ln_in_g, ln_in_b, w_in, conv_w, gn_g, gn_b, vn_g, vn_b,
           w_spatial, b_spatial, w_out, ln1_g, ln1_b, w_router, b_router, w_up, b_up, w_down, b_down, w_ple,
           w_ple_gate, ln2_g, ln2_b):
    batch, seq, _ = x_prompt.shape
    dec_batch, dec_seq, _ = x_sample.shape
    assert w_in.shape[0] == DEPTH and dec_seq == SUBLANES and seq % T_PROMPT == 0
    n_prompt = batch * seq
    n_sample = dec_batch * dec_seq
    n_tok = n_prompt + n_sample
    t_s = SEQS_PER_TILE * dec_seq
    assert n_prompt % T_ROWS == 0 and n_sample % T_ROWS == 0 and n_tok % T_RANK == 0 and n_sample % t_s == 0
    assert n_tok % T_DISPATCH == 0

    row = lambda a: a.reshape(1, -1).astype(F32)
    gidx = jnp.arange(MXU_DIM) // (C_CONV // N_CONV_GROUPS)
    gmat = jnp.where(gidx[:, None] == gidx[None, :], 1.0 / (C_CONV // N_CONV_GROUPS), 0.0).astype(BF16)
    wr_pad = jnp.pad(w_router[0].astype(F32), ((0, 0), (0, LANES - N_EXPERTS)))
    wr_hi = wr_pad.astype(BF16)
    wr_lo = (wr_pad - wr_hi.astype(F32)).astype(BF16)
    weights = dict(
        ln_in_g=row(ln_in_g), ln_in_b=row(ln_in_b), w_in=w_in[0].astype(BF16), vn_g=row(vn_g[0]), vn_b=row(vn_b[0]),
        gmat=gmat, gn_g=row(gn_g[0]), gn_b=row(gn_b[0]), w_out=w_out[0].astype(BF16), ln1_g=row(ln1_g[0]),
        ln1_b=row(ln1_b[0]), w_gate=w_ple_gate[0].astype(BF16), w_ple=w_ple[0].astype(BF16), wr_hi=wr_hi,
        wr_pair=jnp.concatenate([wr_hi, wr_lo], axis=1),
        b_r=jnp.pad(row(b_router[0]), ((0, 0), (0, LANES - N_EXPERTS))))
    w_list = [weights[n] for n in _WEIGHT_NAMES]
    w_specs = [_full(a.shape) for a in w_list]
    cw = jnp.pad(conv_w[0].astype(F32), ((0, HIST - CONV_W), (0, 0)))
    causal = jnp.tril(jnp.ones((CHUNK, CHUNK), bool))
    ws_m = jnp.where(causal[None], w_spatial[0], 0.0)
    ws_cat = jnp.concatenate([ws_m[0::2], ws_m[1::2]], axis=2).astype(BF16)
    bs_full = jnp.repeat(b_spatial[0].T.astype(F32), HEAD_DIM, axis=1)
    s_i = jnp.arange(SUBLANES)[:, None]
    t_i = jnp.arange(SUBLANES)[None, :]
    tap = jnp.clip(CONV_W - 1 - t_i + s_i, 0, CONV_W - 1)
    convu = jnp.where((s_i <= t_i)[:, :, None], conv_w[0].astype(F32)[tap], 0.0)
    gw8 = jnp.transpose(ws_m[:, :SUBLANES, :SUBLANES], (2, 1, 0))
    gw8 = jnp.repeat(gw8.astype(F32), HEAD_DIM, axis=2)
    b8 = bs_full[:SUBLANES]
    cpad = jnp.pad(cache_conv[0].astype(F32), ((0, 0), (HIST - (CONV_W - 1), SUBLANES), (0, 0)))

    cparams = lambda sem: pltpu.CompilerParams(dimension_semantics=sem, vmem_limit_bytes=VMEM_LIMIT)

    nj = seq // T_PROMPT
    npt = n_prompt // T_PROMPT
    assert t_s == T_PROMPT
    pstep = lambda i: jnp.minimum(i, npt - 1)
    sstep = lambda i: jnp.maximum(i - npt, 0)
    tok_block = lambda width: pl.BlockSpec((T_PROMPT, width), lambda i: (i, 0))
    cwb = jnp.broadcast_to(conv_w[0].astype(F32).reshape(CONV_W, C_CONV // LANES, 1, LANES),
                           (CONV_W, C_CONV // LANES, SUBLANES, LANES))
    tables = [cw, cwb, ws_cat, bs_full, convu, gw8, b8]
    h_all, base_all, idx_all, gate_all, u_tail, v_chunk, u_s, v_s = pl.pallas_call(
        functools.partial(_mixer_body, n_prompt_tiles=npt, tiles_per_seq=nj),
        grid=(npt + n_sample // t_s,),
        in_specs=[pl.BlockSpec((None, T_PROMPT, D_MODEL), lambda i: (pstep(i) // nj, pstep(i) % nj, 0)),
                  pl.BlockSpec((None, T_PROMPT, PLE_DIM), lambda i: (pstep(i) // nj, pstep(i) % nj, 0)),
                  pl.BlockSpec((t_s, D_MODEL), lambda i: (sstep(i), 0)),
                  pl.BlockSpec((t_s, PLE_DIM), lambda i: (sstep(i), 0)),
                  pl.BlockSpec((SEQS_PER_TILE,) + cpad.shape[1:], lambda i: (sstep(i), 0, 0),
                               pipeline_mode=pl.Buffered(1))]
                 + w_specs + [_full(a.shape) for a in tables],
        out_specs=[pl.BlockSpec((T_PROMPT * SUBLANES, LANES), lambda i: (i, 0)),
                   tok_block(D_MODEL), tok_block(LANES), tok_block(LANES),
                   pl.BlockSpec((None, HIST, C_CONV), lambda i: (pstep(i) // nj, 0, 0)),
                   pl.BlockSpec((None, CHUNK, C_GMLP), lambda i: (pstep(i) // nj, 0, 0)),
                   pl.BlockSpec((t_s, C_CONV), lambda i: (sstep(i), 0)),
                   pl.BlockSpec((t_s, C_GMLP), lambda i: (sstep(i), 0))],
        out_shape=(jax.ShapeDtypeStruct((n_tok * SUBLANES, LANES), F32), jax.ShapeDtypeStruct((n_tok, D_MODEL), F32),
                   jax.ShapeDtypeStruct((n_tok, LANES), jnp.int32), jax.ShapeDtypeStruct((n_tok, LANES), F32),
                   jax.ShapeDtypeStruct((batch, HIST, C_CONV), F32), jax.ShapeDtypeStruct((batch, CHUNK, C_GMLP), F32),
                   jax.ShapeDtypeStruct((n_sample, C_CONV), F32), jax.ShapeDtypeStruct((n_sample, C_GMLP), F32)),
        scratch_shapes=[pltpu.VMEM((C_CONV // LANES, T_PROMPT + HIST, LANES), F32),
                        pltpu.VMEM((C_CONV // LANES, T_PROMPT, LANES), F32)]
                       + [pltpu.VMEM((T_PROMPT, C_CONV), F32)] * 4,
        compiler_params=pltpu.CompilerParams(dimension_semantics=("arbitrary",), vmem_limit_bytes=BIG_VMEM_LIMIT),
        name="mixer",
    )(x_prompt, p_prompt[0], x_sample.reshape(n_sample, D_MODEL), p_sample[0].reshape(n_sample, PLE_DIM), cpad,
      *w_list, *tables)

    tri = (jnp.arange(T_RANK)[:, None] > jnp.arange(T_RANK)[None, :]).astype(BF16)
    before_all, counts = pl.pallas_call(
        _rank_body,
        grid=(n_tok // T_RANK,),
        in_specs=[pl.BlockSpec((T_RANK, LANES), lambda i: (i, 0)), _full(tri.shape)],
        out_specs=[pl.BlockSpec((T_RANK, LANES), lambda i: (i, 0)), _full((1, LANES))],
        out_shape=(jax.ShapeDtypeStruct((n_tok, LANES), jnp.int32), jax.ShapeDtypeStruct((1, LANES), jnp.int32)),
        scratch_shapes=[pltpu.VMEM((1, LANES), F32)],
        compiler_params=cparams(("arbitrary",)),
        name="rank",
    )(idx_all, tri)

    n_assign = n_tok * TOP_K
    n_blocks = n_assign // TM + N_EXPERTS
    cap = n_blocks * TM
    cnt = counts[0, :N_EXPERTS]
    padded = (cnt + TM - 1) // TM * TM
    p_end = jnp.cumsum(padded)
    p_start = p_end - padded
    e_ids = jnp.arange(N_EXPERTS, dtype=jnp.int32)
    lookup = lambda table, ids: jnp.sum(jnp.where(ids[..., None] == e_ids, table, 0), axis=-1)
    slot_table = p_start[None, :] + before_all[:, :N_EXPERTS]
    dest = jnp.sum(jnp.where(idx_all[:, :TOP_K, None] == e_ids, slot_table[:, None, :], 0), axis=-1).astype(jnp.int32)
    dest_tiles = dest.reshape(n_tok // T_DISPATCH, 1, T_DISPATCH * TOP_K)
    blk_row = jnp.arange(n_blocks, dtype=jnp.int32) * TM
    blk_e = jnp.minimum(jnp.sum(p_end[None, :] <= blk_row[:, None], axis=1), N_EXPERTS - 1).astype(jnp.int32)
    n_used = (p_end[-1:] // TM).astype(jnp.int32)
    pad_off = (p_start + cnt).astype(jnp.int32)
    pad_n = (padded - cnt).astype(jnp.int32)
    used = cnt > 0
    slot_e = (jnp.cumsum(used.astype(jnp.int32)) - 1) & 1
    later_used = jnp.where(used[None, :] & (e_ids[None, :] > e_ids[:, None]), e_ids[None, :], N_EXPERTS)
    next_e = jnp.min(later_used, axis=1)
    next_e = jnp.where(next_e < N_EXPERTS, next_e, -1).astype(jnp.int32)
    blk_first = ((blk_row == lookup(p_start, blk_e)) & (blk_row < p_end[-1])).astype(jnp.int32)
    blk_rows = jnp.clip(lookup(p_start + cnt, blk_e) - blk_row, 0, TM).astype(jnp.int32)
    blk_slot = lookup(slot_e, blk_e).astype(jnp.int32)
    blk_next = lookup(next_e, blk_e).astype(jnp.int32)


    x_sorted = pl.pallas_call(
        functools.partial(_dispatch_body, n_blocks=n_blocks),
        grid_spec=pltpu.PrefetchScalarGridSpec(
            num_scalar_prefetch=3,
            grid=(n_tok // T_DISPATCH,),
            in_specs=[pl.BlockSpec((None, 1, T_DISPATCH * TOP_K), lambda i, *_: (i, 0, 0), memory_space=pltpu.SMEM),
                      pl.BlockSpec((T_DISPATCH * SUBLANES, LANES), lambda i, *_: (i, 0))],
            out_specs=pl.BlockSpec(memory_space=pl.ANY),
            scratch_shapes=[pltpu.VMEM((TM // 2 * SUBLANES, LANES), F32), pltpu.SemaphoreType.DMA,
                            pltpu.SemaphoreType.DMA]),
        out_shape=jax.ShapeDtypeStruct((cap * SUBLANES, LANES), F32),
        compiler_params=cparams(("arbitrary",)),
        name="dispatch",
    )(pad_off, pad_n, n_used, dest_tiles, h_all)

    last = lambda i, nu: jnp.minimum(i, nu[0] - 1)
    y_sorted = pl.pallas_call(
        _expert_body,
        grid_spec=pltpu.PrefetchScalarGridSpec(
            num_scalar_prefetch=6,
            grid=(n_blocks,),
            in_specs=[pl.BlockSpec((TM * SUBLANES, LANES), lambda i, be, bf, bs, bn, br, nu: (last(i, nu), 0)),
                      pl.BlockSpec((None, 1, 2 * D_FF), lambda i, be, bf, bs, bn, br, nu: (be[last(i, nu)], 0, 0)),
                      pl.BlockSpec((None, 1, D_MODEL), lambda i, be, bf, bs, bn, br, nu: (be[last(i, nu)], 0, 0)),
                      pl.BlockSpec(memory_space=pl.ANY), pl.BlockSpec(memory_space=pl.ANY)],
            out_specs=pl.BlockSpec((TM * SUBLANES, LANES), lambda i, be, bf, bs, bn, br, nu: (i, 0)),
            scratch_shapes=[pltpu.VMEM((2, D_MODEL, 2 * D_FF), F32), pltpu.VMEM((2, D_FF, D_MODEL), F32),
                            pltpu.VMEM((D_MODEL, 2 * D_FF), BF16), pltpu.VMEM((D_FF, D_MODEL), BF16),
                            pltpu.SemaphoreType.DMA((2,))]),
        out_shape=jax.ShapeDtypeStruct((cap * SUBLANES, LANES), F32),
        compiler_params=pltpu.CompilerParams(dimension_semantics=("arbitrary",), vmem_limit_bytes=BIG_VMEM_LIMIT),
        name="experts",
    )(blk_e, blk_first, blk_slot, blk_next, blk_rows, n_used, x_sorted, b_up[0].astype(F32)[:, None, :],
      b_down[0].astype(F32)[:, None, :], w_up[0].astype(F32), w_down[0].astype(F32))

    npt = n_prompt // T_ROWS
    n_tiles = n_tok // T_ROWS
    per = T_DISPATCH // T_ROWS
    dest_spec = pl.BlockSpec((None, 1, T_DISPATCH * TOP_K), lambda i: (i // per, 0, 0), memory_space=pltpu.SMEM)
    dest_next_spec = pl.BlockSpec((None, 1, T_DISPATCH * TOP_K),
                                  lambda i: (jnp.minimum(i + 1, n_tiles - 1) // per, 0, 0), memory_space=pltpu.SMEM)
    out_p, out_s = pl.pallas_call(
        functools.partial(_combine_body, n_prompt_tiles=npt, n_tiles=n_tiles),
        grid=(n_tiles,),
        in_specs=[dest_spec, dest_next_spec, pl.BlockSpec(memory_space=pl.ANY),
                  pl.BlockSpec((T_ROWS, D_MODEL), lambda i: (i, 0)), pl.BlockSpec((T_ROWS, LANES), lambda i: (i, 0)),
                  _full((1, D_MODEL)), _full((1, D_MODEL))],
        out_specs=[pl.BlockSpec((T_ROWS, D_MODEL), lambda i: (jnp.minimum(i, npt - 1), 0)),
                   pl.BlockSpec((T_ROWS, D_MODEL), lambda i: (jnp.maximum(i - npt, 0), 0))],
        out_shape=(jax.ShapeDtypeStruct((n_prompt, D_MODEL), F32), jax.ShapeDtypeStruct((n_sample, D_MODEL), F32)),
        scratch_shapes=[pltpu.VMEM((2 * TOP_K, T_ROWS * SUBLANES, LANES), F32), pltpu.SemaphoreType.DMA((2,))],
        compiler_params=cparams(("arbitrary",)),
        name="combine",
    )(dest_tiles, dest_tiles, y_sorted, base_all, gate_all, row(ln2_g[0]), row(ln2_b[0]))

    y_prompt = out_p.reshape(batch, seq, D_MODEL)
    y_sample = out_s.reshape(dec_batch, dec_seq, D_MODEL)
    conv_state_prompt = u_tail[None, :, HIST - (CONV_W - 1):, :]
    u_s3 = u_s.reshape(dec_batch, dec_seq, C_CONV)
    conv_state_sample = jnp.concatenate([cache_conv[0][:, dec_seq:, :].astype(F32), u_s3], axis=1)[None]
    chunk_v_prompt = v_chunk[None]
    chunk_v_sample = v_s.reshape(1, dec_batch, dec_seq, C_GMLP)
    return (y_prompt, y_sample, conv_state_prompt, conv_state_sample, chunk_v_prompt, chunk_v_sample)
```

```python
import functools

import jax
import jax.numpy as jnp
from jax import lax
from jax.experimental import pallas as pl
from jax.experimental.pallas import tpu as pltpu

F32 = jnp.float32
BF16 = jnp.bfloat16

D_MODEL = 1024
C_CONV = 512
C_GMLP = 512
N_CONV_GROUPS = 8
CONV_W = 31
N_HEADS = 8
HEAD_DIM = C_GMLP // N_HEADS
CHUNK = 128
N_EXPERTS = 32
TOP_K = 4
D_FF = 1024
PLE_DIM = 256
SWIGLU_LIMIT = 7.0
SWIGLU_ALPHA = 1.702
LN_EPS = 1e-5
DEPTH = 1
DEEPNORM_ALPHA = (2.0 * DEPTH) ** 0.25

LANES = 128
SUBLANES = 8
MXU_DIM = 256
VMEM_LIMIT = 48 * 1024 * 1024
BIG_VMEM_LIMIT = 56 * 1024 * 1024

T_PROMPT = 512
PROMPT_SPLIT = 2
HIST = 32
CONV_STRIDE = 4
SEQS_PER_TILE = 64
SEQ_CHUNK = 4
T_RANK = 1024
T_DISPATCH = 1024
T_ROWS = 512
TM = 512
TM_CHAIN = 256


def _dot(a, b):
    return jnp.dot(a, b, preferred_element_type=F32)


def _layer_norm(x, g, b):
    mu = jnp.mean(x, axis=-1, keepdims=True)
    xc = x - mu
    var = jnp.mean(xc * xc, axis=-1, keepdims=True)
    return xc * lax.rsqrt(var + LN_EPS) * g + b


def _split_bf16(a):
    hi = a.astype(BF16)
    lo = (a - hi.astype(F32)).astype(BF16)
    return hi, lo


def _group_mean(a, gmat_ref):
    hi, lo = _split_bf16(a)
    g = gmat_ref[...]
    outs = []
    for s in range(C_CONV // MXU_DIM):
        sl = slice(MXU_DIM * s, MXU_DIM * (s + 1))
        outs.append(_dot(hi[:, sl], g) + _dot(lo[:, sl], g))
    return jnp.concatenate(outs, axis=1)


def _group_norm_silu(y, gmat_ref, gn_g, gn_b):
    mu = _group_mean(y, gmat_ref)
    yc = y - mu
    var = _group_mean(yc * yc, gmat_ref)
    yn = yc * lax.rsqrt(var + LN_EPS) * gn_g + gn_b
    return yn * jax.nn.sigmoid(yn)


def _store_token_tiles(ref, val, tok0=0):
    n = val.shape[0]
    for c in range(D_MODEL // LANES):
        ref[pl.ds(tok0 * SUBLANES + c, n, stride=SUBLANES), :] = val[:, c * LANES:(c + 1) * LANES]


def _load_token_tiles(ref, n, lead=(), tok0=0):
    parts = [ref[lead + (pl.ds(tok0 * SUBLANES + c, n, stride=SUBLANES), slice(None))]
             for c in range(D_MODEL // LANES)]
    return jnp.concatenate(parts, axis=1)


def _front(x, w):
    xn = _layer_norm(x, w["ln_in_g"][...], w["ln_in_b"][...])
    z = _dot(xn.astype(BF16), w["w_in"][...])
    a_val = z[:, 0:C_CONV]
    a_gate = z[:, C_CONV:2 * C_CONV]
    g_u = z[:, 2 * C_CONV:2 * C_CONV + C_GMLP]
    g_v = z[:, 2 * C_CONV + C_GMLP:]
    u = a_val * jax.nn.sigmoid(a_gate)
    ug = jax.nn.gelu(g_u)
    v = _layer_norm(jax.nn.gelu(g_v), w["vn_g"][...], w["vn_b"][...])
    return xn, u, ug, v


def _tail(xn, y_a, y_b, p, w, h_ref, base_ref, idx_ref, gate_ref, row0=0):
    rows = slice(row0, row0 + xn.shape[0])
    mix = _dot(y_a.astype(BF16), w["w_out"][0:C_CONV, :]) + _dot(y_b.astype(BF16), w["w_out"][C_CONV:, :])
    h = _layer_norm(DEEPNORM_ALPHA * xn + mix, w["ln1_g"][...], w["ln1_b"][...])
    hb, h_lo = _split_bf16(h)
    ple = _dot(p.astype(BF16), w["w_ple"][...]) * jax.nn.sigmoid(_dot(hb, w["w_gate"][...]))
    _store_token_tiles(h_ref, h, row0)
    base_ref[rows, :] = DEEPNORM_ALPHA * h + ple

    pair = _dot(hb, w["wr_pair"][...])
    logits = pair[:, :LANES] + pair[:, LANES:] + _dot(h_lo, w["wr_hi"][...]) + w["b_r"][...]
    lane = lax.broadcasted_iota(jnp.int32, logits.shape, 1)
    lane_f = lane.astype(F32)
    vals = jnp.where(lane < N_EXPERTS, logits, -jnp.inf)
    tops, ids = [], []
    for _ in range(TOP_K):
        m = jnp.max(vals, axis=-1, keepdims=True)
        i = jnp.min(jnp.where(vals == m, lane_f, float(LANES)), axis=-1, keepdims=True)
        vals = jnp.where(lane_f == i, -jnp.inf, vals)
        tops.append(m)
        ids.append(i)
    exps = [jnp.exp(m - tops[0]) for m in tops]
    denom = exps[0] + exps[1] + exps[2] + exps[3]
    idx_out = jnp.zeros(logits.shape, F32)
    gate_out = jnp.zeros(logits.shape, F32)
    for k in range(TOP_K):
        idx_out = jnp.where(lane == k, ids[k], idx_out)
        gate_out = jnp.where(lane == k, exps[k] / denom, gate_out)
    idx_ref[rows, :] = idx_out.astype(jnp.int32)
    gate_ref[rows, :] = gate_out


_WEIGHT_NAMES = ("ln_in_g", "ln_in_b", "w_in", "vn_g", "vn_b", "gmat", "gn_g", "gn_b", "w_out", "ln1_g", "ln1_b",
                 "w_gate", "w_ple", "wr_hi", "wr_pair", "b_r")


def _prompt_branch(j, x_ref, p_ref, w, cwb_ref, ws_ref, bs_ref, outs, tail_ref, vch_ref, ubuf, yslab, last_j):
    t = T_PROMPT
    n = t // PROMPT_SPLIT
    n_slabs = C_CONV // LANES

    @pl.when(j == 0)
    def _():
        ubuf[:, 0:HIST, :] = jnp.zeros((n_slabs, HIST, LANES), F32)

    @pl.when(j > 0)
    def _():
        ubuf[:, 0:HIST, :] = ubuf[:, t:t + HIST, :]

    def front(h):
        xn, u, ug, v = _front(x_ref[h * n:(h + 1) * n, :], w)
        for s in range(n_slabs):
            ubuf[s, HIST + h * n:HIST + (h + 1) * n, :] = u[:, s * LANES:(s + 1) * LANES]
        return xn, u, ug, v

    def conv(h):
        rows = CONV_STRIDE * SUBLANES
        first = HIST - (CONV_W - 1)
        for s in range(n_slabs):
            for c in range(h * n // rows, (h + 1) * n // rows):
                accs = [None] * CONV_STRIDE
                for shift in range(CONV_STRIDE + CONV_W - 1):
                    win = ubuf[s, pl.ds(first + c * rows + shift, SUBLANES, stride=CONV_STRIDE), :]
                    for ph in range(CONV_STRIDE):
                        k = shift - ph
                        if 0 <= k < CONV_W:
                            term = cwb_ref[k, s] * win
                            accs[ph] = term if accs[ph] is None else accs[ph] + term
                for ph in range(CONV_STRIDE):
                    yslab[s, pl.ds(c * rows + ph, SUBLANES, stride=CONV_STRIDE), :] = accs[ph]
        y_conv = jnp.concatenate([yslab[s, h * n:(h + 1) * n, :] for s in range(n_slabs)], axis=1)
        return _group_norm_silu(y_conv, w["gmat"], w["gn_g"][...], w["gn_b"][...])

    def spatial_gate(ug, v):
        lane = lax.broadcasted_iota(jnp.int32, (CHUNK, LANES), 1)
        mixed_chunks = []
        for c in range(n // CHUNK):
            vc = v[c * CHUNK:(c + 1) * CHUNK, :]
            parts = []
            for q in range(N_HEADS // 2):
                vp = vc[:, q * LANES:(q + 1) * LANES]
                rhs = jnp.concatenate([jnp.where(lane < HEAD_DIM, vp, 0.0), jnp.where(lane >= HEAD_DIM, vp, 0.0)],
                                      axis=0).astype(BF16)
                parts.append(_dot(ws_ref[q], rhs))
            mixed_chunks.append(jnp.concatenate(parts, axis=1) + bs_ref[...])
        return ug * jnp.concatenate(mixed_chunks, axis=0)

    def tail(h, f, y_a, y_b):
        _tail(f[0], y_a, y_b, p_ref[h * n:(h + 1) * n, :], w, *outs, row0=h * n)

    fronts = [front(0)]
    y_as = {}
    for h in range(PROMPT_SPLIT):
        if h + 1 < PROMPT_SPLIT:
            fronts.append(front(h + 1))
        y_as[h] = conv(h)
        if h > 0:
            tail(h - 1, fronts[h - 1], y_as.pop(h - 1), spatial_gate(fronts[h - 1][2], fronts[h - 1][3]))
    last = PROMPT_SPLIT - 1
    tail(last, fronts[last], y_as.pop(last), spatial_gate(fronts[last][2], fronts[last][3]))

    @pl.when(j == last_j)
    def _():
        tail_ref[...] = fronts[last][1][n - HIST:, :]
        vch_ref[...] = fronts[last][3][n - CHUNK:, :]


def _sample_branch(x_ref, p_ref, cpad_ref, w, cw_ref, convu_ref, gw_ref, b8_ref, outs, u_out_ref, v_out_ref,
                   uscr, vscr, yscr, mscr):
    xn, u, ug, v = _front(x_ref[...], w)
    u_out_ref[...] = u
    v_out_ref[...] = v
    uscr[...] = u
    vscr[...] = v
    first = HIST - (CONV_W - 1)
    rows = SEQ_CHUNK * SUBLANES

    def step(c, carry):
        r0 = pl.multiple_of(c * rows, rows)
        s0 = c * SEQ_CHUNK
        u3 = uscr[pl.ds(r0, rows), :].reshape(SEQ_CHUNK, SUBLANES, C_CONV)
        v3 = vscr[pl.ds(r0, rows), :].reshape(SEQ_CHUNK, SUBLANES, C_GMLP)
        acc = jnp.zeros((SEQ_CHUNK, SUBLANES, C_CONV), F32)
        for k in range(CONV_W):
            acc = acc + cw_ref[pl.ds(k, 1), :][None] * cpad_ref[pl.ds(s0, SEQ_CHUNK), pl.ds(first + k, SUBLANES), :]
        mix = jnp.zeros((SEQ_CHUNK, SUBLANES, C_GMLP), F32) + b8_ref[...][None]
        for s in range(SUBLANES):
            acc = acc + u3[:, s:s + 1, :] * convu_ref[s][None]
            mix = mix + v3[:, s:s + 1, :] * gw_ref[s][None]
        yscr[pl.ds(r0, rows), :] = acc.reshape(rows, C_CONV)
        mscr[pl.ds(r0, rows), :] = mix.reshape(rows, C_GMLP)
        return carry

    lax.fori_loop(0, SEQS_PER_TILE // SEQ_CHUNK, step, 0)
    y_a = _group_norm_silu(yscr[...], w["gmat"], w["gn_g"][...], w["gn_b"][...])
    y_b = ug * mscr[...]
    _tail(xn, y_a, y_b, p_ref[...], w, *outs)


def _mixer_body(*refs, n_prompt_tiles, tiles_per_seq):
    n_w = len(_WEIGHT_NAMES)
    xp_ref, pp_ref, xs_ref, ps_ref, cpad_ref = refs[:5]
    w = dict(zip(_WEIGHT_NAMES, refs[5:5 + n_w]))
    cw_ref, cwb_ref, ws_ref, bs_ref, convu_ref, gw_ref, b8_ref = refs[5 + n_w:12 + n_w]
    outs = refs[12 + n_w:16 + n_w]
    tail_ref, vch_ref, u_out_ref, v_out_ref = refs[16 + n_w:20 + n_w]
    ubuf, yslab, yscr, uscr, vscr, mscr = refs[20 + n_w:]
    step = pl.program_id(0)

    @pl.when(step < n_prompt_tiles)
    def _():
        _prompt_branch(lax.rem(step, tiles_per_seq), xp_ref, pp_ref, w, cwb_ref, ws_ref, bs_ref, outs, tail_ref,
                       vch_ref, ubuf, yslab, tiles_per_seq - 1)

    @pl.when(step >= n_prompt_tiles)
    def _():
        _sample_branch(xs_ref, ps_ref, cpad_ref, w, cw_ref, convu_ref, gw_ref, b8_ref, outs, u_out_ref, v_out_ref,
                       uscr, vscr, yscr, mscr)


def _rank_body(idx_ref, tri_ref, before_ref, counts_ref, carry):
    i = pl.program_id(0)

    @pl.when(i == 0)
    def _():
        carry[...] = jnp.zeros(carry.shape, F32)

    idx = idx_ref[...]
    lane = lax.broadcasted_iota(jnp.int32, idx.shape, 1)
    multi = jnp.zeros(idx.shape, F32)
    for k in range(TOP_K):
        multi = multi + (lane == idx[:, k:k + 1]).astype(F32)
    before_ref[...] = (_dot(tri_ref[...], multi.astype(BF16)) + carry[...]).astype(jnp.int32)
    carry[...] = carry[...] + jnp.sum(multi, axis=0, keepdims=True)
    counts_ref[...] = carry[...].astype(jnp.int32)


def _pad_bits():
    b = TM // 2
    while b >= 1:
        yield b
        b //= 2


def _token_rows(t, n=1):
    return pl.ds(pl.multiple_of(t * SUBLANES, SUBLANES), n * SUBLANES)


def _dispatch_body(pad_off_ref, pad_n_ref, n_used_ref, dest_ref, h_ref, xs_ref, zbuf, sem, zsem, *, n_blocks):
    i = pl.program_id(0)

    def issue(r, carry):
        src = h_ref.at[_token_rows(r), :]
        for k in range(TOP_K):
            d = dest_ref[0, r * TOP_K + k]
            pltpu.make_async_copy(src, xs_ref.at[_token_rows(d), :], sem).start(priority=k % 2)
        return carry

    lax.fori_loop(0, T_DISPATCH, issue, 0, unroll=8)

    @pl.when(i == 0)
    def _():
        zbuf[...] = jnp.zeros(zbuf.shape, F32)

        def zero_copy(off, b):
            return pltpu.make_async_copy(zbuf.at[pl.ds(0, b * SUBLANES), :], xs_ref.at[_token_rows(off, b), :], zsem)

        def start_or_wait(cond, cp, wait):
            @pl.when(cond)
            def _():
                if wait:
                    cp.wait()
                else:
                    cp.start()

        for wait in (False, True):
            for e in range(N_EXPERTS):
                n = pad_n_ref[e]
                for b in _pad_bits():
                    start_or_wait((n & b) != 0, zero_copy(pad_off_ref[e] + (n & ~(2 * b - 1)), b), wait)

        half = TM // 2

        def tail_block(wait):
            def go(blk, carry):
                for s in range(TM // half):
                    cp = zero_copy(blk * TM + s * half, half)
                    cp.wait() if wait else cp.start()
                return carry
            return go

        lax.fori_loop(n_used_ref[0], n_blocks, tail_block(False), 0)
        lax.fori_loop(n_used_ref[0], n_blocks, tail_block(True), 0)

    for _ in range(TOP_K):
        pltpu.make_async_copy(h_ref, xs_ref.at[_token_rows(0, T_DISPATCH), :], sem).wait()


def _expert_body(blk_e_ref, first_ref, slot_ref, next_ref, rows_ref, n_used_ref, x_ref, bu_ref, bd_ref, wu_hbm, wd_hbm,
                 y_ref, wu_f32, wd_f32, wu_bf, wd_bf, sems):
    i = pl.program_id(0)

    def weight_copies(e, s):
        return (pltpu.make_async_copy(wu_hbm.at[e], wu_f32.at[s], sems.at[s]),
                pltpu.make_async_copy(wd_hbm.at[e], wd_f32.at[s], sems.at[s]))

    @pl.when(i < n_used_ref[0])
    def _():
        s = slot_ref[i]

        @pl.when(first_ref[i] == 1)
        def _():
            @pl.when(i == 0)
            def _():
                for cp in weight_copies(blk_e_ref[i], s):
                    cp.start()

            for cp in weight_copies(blk_e_ref[i], s):
                cp.wait()
            nxt = next_ref[i]

            @pl.when(nxt >= 0)
            def _():
                for cp in weight_copies(nxt, 1 - s):
                    cp.start()

            chunk = D_MODEL // SUBLANES

            def cast_rows(c, carry):
                r = pl.multiple_of(c * chunk, chunk)
                wu_bf[pl.ds(r, chunk), :] = wu_f32[s, pl.ds(r, chunk), :].astype(BF16)
                wd_bf[pl.ds(r, chunk), :] = wd_f32[s, pl.ds(r, chunk), :].astype(BF16)
                return carry

            lax.fori_loop(0, SUBLANES, cast_rows, 0)

        def chain(c):
            x = _load_token_tiles(x_ref, TM_CHAIN, tok0=c * TM_CHAIN)
            hcat = _dot(x.astype(BF16), wu_bf[...]) + bu_ref[...]
            h_glu = jnp.minimum(hcat[:, :D_FF], SWIGLU_LIMIT)
            h_lin = jnp.clip(hcat[:, D_FF:], -SWIGLU_LIMIT, SWIGLU_LIMIT)
            act = h_glu * jax.nn.sigmoid(SWIGLU_ALPHA * h_glu) * (h_lin + 1.0)
            _store_token_tiles(y_ref, _dot(act.astype(BF16), wd_bf[...]) + bd_ref[...], tok0=c * TM_CHAIN)

        n_chains = TM // TM_CHAIN
        live = (rows_ref[i] + TM_CHAIN - 1) // TM_CHAIN
        for m in range(1, n_chains + 1):
            @pl.when(live == m)
            def _():
                for c in range(m):
                    chain(c)
                if m < n_chains:
                    y_ref[m * TM_CHAIN * SUBLANES:, :] = jnp.zeros(((n_chains - m) * TM_CHAIN * SUBLANES, LANES), F32)

    @pl.when(pl.program_id(0) >= n_used_ref[0])
    def _():
        y_ref[...] = jnp.zeros(y_ref.shape, F32)


def _combine_body(dest_ref, dest_next_ref, y_ref, base_ref, gate_ref, g_ref, b_ref, outp_ref, outs_ref, ybuf, sems, *,
                  n_prompt_tiles, n_tiles):
    i = pl.program_id(0)

    def issue_tile(d_ref, tile, slot):
        first = lax.rem(tile, T_DISPATCH // T_ROWS) * (T_ROWS * TOP_K)

        def issue(r, carry):
            for k in range(TOP_K):
                d = d_ref[0, first + r * TOP_K + k]
                pltpu.make_async_copy(y_ref.at[_token_rows(d), :], ybuf.at[slot * TOP_K + k, _token_rows(r), :],
                                      sems.at[slot]).start(priority=k % 2)
            return carry

        lax.fori_loop(0, T_ROWS, issue, 0, unroll=8)

    slot = lax.rem(i, 2)

    @pl.when(i == 0)
    def _():
        issue_tile(dest_ref, i, 0)

    @pl.when(i + 1 < n_tiles)
    def _():
        issue_tile(dest_next_ref, i + 1, 1 - slot)

    for k in range(TOP_K):
        pltpu.make_async_copy(y_ref.at[_token_rows(0, T_ROWS), :], ybuf.at[slot * TOP_K + k], sems.at[slot]).wait()

    gate = gate_ref[...]
    acc = base_ref[...]
    for k in range(TOP_K):
        acc = acc + gate[:, k:k + 1] * _load_token_tiles(ybuf, T_ROWS, lead=(slot * TOP_K + k,))
    out = _layer_norm(acc, g_ref[...], b_ref[...])

    @pl.when(i < n_prompt_tiles)
    def _():
        outp_ref[...] = out

    @pl.when(i >= n_prompt_tiles)
    def _():
        outs_ref[...] = out


def _full(shape):
    return pl.BlockSpec(shape, lambda *_: (0,) * len(shape), pipeline_mode=pl.Buffered(1))


def kernel(x_prompt, x_sample, cache_conv, p_prompt, p_sample, ln_in_g, ln_in_b, w_in, conv_w, gn_g, gn_b, vn_g, vn_b,
           w_spatial, b_spatial, w_out, ln1_g, ln1_b, w_router, b_router, w_up, b_up, w_down, b_down, w_ple,
           w_ple_gate, ln2_g, ln2_b):
    batch, seq, _ = x_prompt.shape
    dec_batch, dec_seq, _ = x_sample.shape
    assert w_in.shape[0] == DEPTH and dec_seq == SUBLANES and seq % T_PROMPT == 0
    n_prompt = batch * seq
    n_sample = dec_batch * dec_seq
    n_tok = n_prompt + n_sample
    t_s = SEQS_PER_TILE * dec_seq
    assert n_prompt % T_ROWS == 0 and n_sample % T_ROWS == 0 and n_tok % T_RANK == 0 and n_sample % t_s == 0
    assert n_tok % T_DISPATCH == 0

    row = lambda a: a.reshape(1, -1).astype(F32)
    gidx = jnp.arange(MXU_DIM) // (C_CONV // N_CONV_GROUPS)
    gmat = jnp.where(gidx[:, None] == gidx[None, :], 1.0 / (C_CONV // N_CONV_GROUPS), 0.0).astype(BF16)
    wr_pad = jnp.pad(w_router[0].astype(F32), ((0, 0), (0, LANES - N_EXPERTS)))
    wr_hi = wr_pad.astype(BF16)
    wr_lo = (wr_pad - wr_hi.astype(F32)).astype(BF16)
    weights = dict(
        ln_in_g=row(ln_in_g), ln_in_b=row(ln_in_b), w_in=w_in[0].astype(BF16), vn_g=row(vn_g[0]), vn_b=row(vn_b[0]),
        gmat=gmat, gn_g=row(gn_g[0]), gn_b=row(gn_b[0]), w_out=w_out[0].astype(BF16), ln1_g=row(ln1_g[0]),
        ln1_b=row(ln1_b[0]), w_gate=w_ple_gate[0].astype(BF16), w_ple=w_ple[0].astype(BF16), wr_hi=wr_hi,
        wr_pair=jnp.concatenate([wr_hi, wr_lo], axis=1),
        b_r=jnp.pad(row(b_router[0]), ((0, 0), (0, LANES - N_EXPERTS))))
    w_list = [weights[n] for n in _WEIGHT_NAMES]
    w_specs = [_full(a.shape) for a in w_list]
    cw = jnp.pad(conv_w[0].astype(F32), ((0, HIST - CONV_W), (0, 0)))
    causal = jnp.tril(jnp.ones((CHUNK, CHUNK), bool))
    ws_m = jnp.where(causal[None], w_spatial[0], 0.0)
    ws_cat = jnp.concatenate([ws_m[0::2], ws_m[1::2]], axis=2).astype(BF16)
    bs_full = jnp.repeat(b_spatial[0].T.astype(F32), HEAD_DIM, axis=1)
    s_i = jnp.arange(SUBLANES)[:, None]
    t_i = jnp.arange(SUBLANES)[None, :]
    tap = jnp.clip(CONV_W - 1 - t_i + s_i, 0, CONV_W - 1)
    convu = jnp.where((s_i <= t_i)[:, :, None], conv_w[0].astype(F32)[tap], 0.0)
    gw8 = jnp.transpose(ws_m[:, :SUBLANES, :SUBLANES], (2, 1, 0))
    gw8 = jnp.repeat(gw8.astype(F32), HEAD_DIM, axis=2)
    b8 = bs_full[:SUBLANES]
    cpad = jnp.pad(cache_conv[0].astype(F32), ((0, 0), (HIST - (CONV_W - 1), SUBLANES), (0, 0)))

    cparams = lambda sem: pltpu.CompilerParams(dimension_semantics=sem, vmem_limit_bytes=VMEM_LIMIT)

    nj = seq // T_PROMPT
    npt = n_prompt // T_PROMPT
    assert t_s == T_PROMPT
    pstep = lambda i: jnp.minimum(i, npt - 1)
    sstep = lambda i: jnp.maximum(i - npt, 0)
    tok_block = lambda width: pl.BlockSpec((T_PROMPT, width), lambda i: (i, 0))
    cwb = jnp.broadcast_to(conv_w[0].astype(F32).reshape(CONV_W, C_CONV // LANES, 1, LANES),
                           (CONV_W, C_CONV // LANES, SUBLANES, LANES))
    tables = [cw, cwb, ws_cat, bs_full, convu, gw8, b8]
    h_all, base_all, idx_all, gate_all, u_tail, v_chunk, u_s, v_s = pl.pallas_call(
        functools.partial(_mixer_body, n_prompt_tiles=npt, tiles_per_seq=nj),
        grid=(npt + n_sample // t_s,),
        in_specs=[pl.BlockSpec((None, T_PROMPT, D_MODEL), lambda i: (pstep(i) // nj, pstep(i) % nj, 0)),
                  pl.BlockSpec((None, T_PROMPT, PLE_DIM), lambda i: (pstep(i) // nj, pstep(i) % nj, 0)),
                  pl.BlockSpec((t_s, D_MODEL), lambda i: (sstep(i), 0)),
                  pl.BlockSpec((t_s, PLE_DIM), lambda i: (sstep(i), 0)),
                  pl.BlockSpec((SEQS_PER_TILE,) + cpad.shape[1:], lambda i: (sstep(i), 0, 0),
                               pipeline_mode=pl.Buffered(1))]
                 + w_specs + [_full(a.shape) for a in tables],
        out_specs=[pl.BlockSpec((T_PROMPT * SUBLANES, LANES), lambda i: (i, 0)),
                   tok_block(D_MODEL), tok_block(LANES), tok_block(LANES),
                   pl.BlockSpec((None, HIST, C_CONV), lambda i: (pstep(i) // nj, 0, 0)),
                   pl.BlockSpec((None, CHUNK, C_GMLP), lambda i: (pstep(i) // nj, 0, 0)),
                   pl.BlockSpec((t_s, C_CONV), lambda i: (sstep(i), 0)),
                   pl.BlockSpec((t_s, C_GMLP), lambda i: (sstep(i), 0))],
        out_shape=(jax.ShapeDtypeStruct((n_tok * SUBLANES, LANES), F32), jax.ShapeDtypeStruct((n_tok, D_MODEL), F32),
                   jax.ShapeDtypeStruct((n_tok, LANES), jnp.int32), jax.ShapeDtypeStruct((n_tok, LANES), F32),
                   jax.ShapeDtypeStruct((batch, HIST, C_CONV), F32), jax.ShapeDtypeStruct((batch, CHUNK, C_GMLP), F32),
                   jax.ShapeDtypeStruct((n_sample, C_CONV), F32), jax.ShapeDtypeStruct((n_sample, C_GMLP), F32)),
        scratch_shapes=[pltpu.VMEM((C_CONV // LANES, T_PROMPT + HIST, LANES), F32),
                        pltpu.VMEM((C_CONV // LANES, T_PROMPT, LANES), F32)]
                       + [pltpu.VMEM((T_PROMPT, C_CONV), F32)] * 4,
        compiler_params=pltpu.CompilerParams(dimension_semantics=("arbitrary",), vmem_limit_bytes=BIG_VMEM_LIMIT),
        name="mixer",
    )(x_prompt, p_prompt[0], x_sample.reshape(n_sample, D_MODEL), p_sample[0].reshape(n_sample, PLE_DIM), cpad,
      *w_list, *tables)

    tri = (jnp.arange(T_RANK)[:, None] > jnp.arange(T_RANK)[None, :]).astype(BF16)
    before_all, counts = pl.pallas_call(
        _rank_body,
        grid=(n_tok // T_RANK,),
        in_specs=[pl.BlockSpec((T_RANK, LANES), lambda i: (i, 0)), _full(tri.shape)],
        out_specs=[pl.BlockSpec((T_RANK, LANES), lambda i: (i, 0)), _full((1, LANES))],
        out_shape=(jax.ShapeDtypeStruct((n_tok, LANES), jnp.int32), jax.ShapeDtypeStruct((1, LANES), jnp.int32)),
        scratch_shapes=[pltpu.VMEM((1, LANES), F32)],
        compiler_params=cparams(("arbitrary",)),
        name="rank",
    )(idx_all, tri)

    n_assign = n_tok * TOP_K
    n_blocks = n_assign // TM + N_EXPERTS
    cap = n_blocks * TM
    cnt = counts[0, :N_EXPERTS]
    padded = (cnt + TM - 1) // TM * TM
    p_end = jnp.cumsum(padded)
    p_start = p_end - padded
    e_ids = jnp.arange(N_EXPERTS, dtype=jnp.int32)
    lookup = lambda table, ids: jnp.sum(jnp.where(ids[..., None] == e_ids, table, 0), axis=-1)
    slot_table = p_start[None, :] + before_all[:, :N_EXPERTS]
    dest = jnp.sum(jnp.where(idx_all[:, :TOP_K, None] == e_ids, slot_table[:, None, :], 0), axis=-1).astype(jnp.int32)
    dest_tiles = dest.reshape(n_tok // T_DISPATCH, 1, T_DISPATCH * TOP_K)
    blk_row = jnp.arange(n_blocks, dtype=jnp.int32) * TM
    blk_e = jnp.minimum(jnp.sum(p_end[None, :] <= blk_row[:, None], axis=1), N_EXPERTS - 1).astype(jnp.int32)
    n_used = (p_end[-1:] // TM).astype(jnp.int32)
    pad_off = (p_start + cnt).astype(jnp.int32)
    pad_n = (padded - cnt).astype(jnp.int32)
    used = cnt > 0
    slot_e = (jnp.cumsum(used.astype(jnp.int32)) - 1) & 1
    later_used = jnp.where(used[None, :] & (e_ids[None, :] > e_ids[:, None]), e_ids[None, :], N_EXPERTS)
    next_e = jnp.min(later_used, axis=1)
    next_e = jnp.where(next_e < N_EXPERTS, next_e, -1).astype(jnp.int32)
    blk_first = ((blk_row == lookup(p_start, blk_e)) & (blk_row < p_end[-1])).astype(jnp.int32)
    blk_rows = jnp.clip(lookup(p_start + cnt, blk_e) - blk_row, 0, TM).astype(jnp.int32)
    blk_slot = lookup(slot_e, blk_e).astype(jnp.int32)
    blk_next = lookup(next_e, blk_e).astype(jnp.int32)


    x_sorted = pl.pallas_call(
        functools.partial(_dispatch_body, n_blocks=n_blocks),
        grid_spec=pltpu.PrefetchScalarGridSpec(
            num_scalar_prefetch=3,
            grid=(n_tok // T_DISPATCH,),
            in_specs=[pl.BlockSpec((None, 1, T_DISPATCH * TOP_K), lambda i, *_: (i, 0, 0), memory_space=pltpu.SMEM),
                      pl.BlockSpec((T_DISPATCH * SUBLANES, LANES), lambda i, *_: (i, 0))],
            out_specs=pl.BlockSpec(memory_space=pl.ANY),
            scratch_shapes=[pltpu.VMEM((TM // 2 * SUBLANES, LANES), F32), pltpu.SemaphoreType.DMA,
                            pltpu.SemaphoreType.DMA]),
        out_shape=jax.ShapeDtypeStruct((cap * SUBLANES, LANES), F32),
        compiler_params=cparams(("arbitrary",)),
        name="dispatch",
    )(pad_off, pad_n, n_used, dest_tiles, h_all)

    last = lambda i, nu: jnp.minimum(i, nu[0] - 1)
    y_sorted = pl.pallas_call(
        _expert_body,
        grid_spec=pltpu.PrefetchScalarGridSpec(
            num_scalar_prefetch=6,
            grid=(n_blocks,),
            in_specs=[pl.BlockSpec((TM * SUBLANES, LANES), lambda i, be, bf, bs, bn, br, nu: (last(i, nu), 0)),
                      pl.BlockSpec((None, 1, 2 * D_FF), lambda i, be, bf, bs, bn, br, nu: (be[last(i, nu)], 0, 0)),
                      pl.BlockSpec((None, 1, D_MODEL), lambda i, be, bf, bs, bn, br, nu: (be[last(i, nu)], 0, 0)),
                      pl.BlockSpec(memory_space=pl.ANY), pl.BlockSpec(memory_space=pl.ANY)],
            out_specs=pl.BlockSpec((TM * SUBLANES, LANES), lambda i, be, bf, bs, bn, br, nu: (i, 0)),
            scratch_shapes=[pltpu.VMEM((2, D_MODEL, 2 * D_FF), F32), pltpu.VMEM((2, D_FF, D_MODEL), F32),
                            pltpu.VMEM((D_MODEL, 2 * D_FF), BF16), pltpu.VMEM((D_FF, D_MODEL), BF16),
                            pltpu.SemaphoreType.DMA((2,))]),
        out_shape=jax.ShapeDtypeStruct((cap * SUBLANES, LANES), F32),
        compiler_params=pltpu.CompilerParams(dimension_semantics=("arbitrary",), vmem_limit_bytes=BIG_VMEM_LIMIT),
        name="experts",
    )(blk_e, blk_first, blk_slot, blk_next, blk_rows, n_used, x_sorted, b_up[0].astype(F32)[:, None, :],
      b_down[0].astype(F32)[:, None, :], w_up[0].astype(F32), w_down[0].astype(F32))

    npt = n_prompt // T_ROWS
    n_tiles = n_tok // T_ROWS
    per = T_DISPATCH // T_ROWS
    dest_spec = pl.BlockSpec((None, 1, T_DISPATCH * TOP_K), lambda i: (i // per, 0, 0), memory_space=pltpu.SMEM)
    dest_next_spec = pl.BlockSpec((None, 1, T_DISPATCH * TOP_K),
                                  lambda i: (jnp.minimum(i + 1, n_tiles - 1) // per, 0, 0), memory_space=pltpu.SMEM)
    out_p, out_s = pl.pallas_call(
        functools.partial(_combine_body, n_prompt_tiles=npt, n_tiles=n_tiles),
        grid=(n_tiles,),
        in_specs=[dest_spec, dest_next_spec, pl.BlockSpec(memory_space=pl.ANY),
                  pl.BlockSpec((T_ROWS, D_MODEL), lambda i: (i, 0)), pl.BlockSpec((T_ROWS, LANES), lambda i: (i, 0)),
                  _full((1, D_MODEL)), _full((1, D_MODEL))],
        out_specs=[pl.BlockSpec((T_ROWS, D_MODEL), lambda i: (jnp.minimum(i, npt - 1), 0)),
                   pl.BlockSpec((T_ROWS, D_MODEL), lambda i: (jnp.maximum(i - npt, 0), 0))],
        out_shape=(jax.ShapeDtypeStruct((n_prompt, D_MODEL), F32), jax.ShapeDtypeStruct((n_sample, D_MODEL), F32)),
        scratch_shapes=[pltpu.VMEM((2 * TOP_K, T_ROWS * SUBLANES, LANES), F32), pltpu.SemaphoreType.DMA((2,))],
        compiler_params=cparams(("arbitrary",)),
        name="combine",
    )(dest_tiles, dest_tiles, y_sorted, base_all, gate_all, row(ln2_g[0]), row(ln2_b[0]))

    y_prompt = out_p.reshape(batch, seq, D_MODEL)
    y_sample = out_s.reshape(dec_batch, dec_seq, D_MODEL)
    conv_state_prompt = u_tail[None, :, HIST - (CONV_W - 1):, :]
    u_s3 = u_s.reshape(dec_batch, dec_seq, C_CONV)
    conv_state_sample = jnp.concatenate([cache_conv[0][:, dec_seq:, :].astype(F32), u_s3], axis=1)[None]
    chunk_v_prompt = v_chunk[None]
    chunk_v_sample = v_s.reshape(1, dec_batch, dec_seq, C_GMLP)
    return (y_prompt, y_sample, conv_state_prompt, conv_state_sample, chunk_v_prompt, chunk_v_sample)
```

```python
import functools

import jax
import jax.numpy as jnp
from jax import lax
from jax.experimental import pallas as pl
from jax.experimental.pallas import tpu as pltpu

F32 = jnp.float32
BF16 = jnp.bfloat16

D_MODEL = 1024
C_CONV = 512
C_GMLP = 512
N_CONV_GROUPS = 8
CONV_W = 31
N_HEADS = 8
HEAD_DIM = C_GMLP // N_HEADS
CHUNK = 128
N_EXPERTS = 32
TOP_K = 4
D_FF = 1024
PLE_DIM = 256
SWIGLU_LIMIT = 7.0
SWIGLU_ALPHA = 1.702
LN_EPS = 1e-5
DEPTH = 1
DEEPNORM_ALPHA = (2.0 * DEPTH) ** 0.25

LANES = 128
SUBLANES = 8
MXU_DIM = 256
VMEM_LIMIT = 48 * 1024 * 1024
BIG_VMEM_LIMIT = 56 * 1024 * 1024

T_PROMPT = 512
PROMPT_SPLIT = 2
HIST = 32
CONV_STRIDE = 4
SEQS_PER_TILE = 64
SEQ_CHUNK = 4
T_RANK = 1024
T_DISPATCH = 1024
T_ROWS = 256
TM = 512
TM_CHAIN = 256


def _dot(a, b):
    return jnp.dot(a, b, preferred_element_type=F32)


def _layer_norm(x, g, b):
    mu = jnp.mean(x, axis=-1, keepdims=True)
    xc = x - mu
    var = jnp.mean(xc * xc, axis=-1, keepdims=True)
    return xc * lax.rsqrt(var + LN_EPS) * g + b


def _split_bf16(a):
    hi = a.astype(BF16)
    lo = (a - hi.astype(F32)).astype(BF16)
    return hi, lo


def _group_mean(a, gmat_ref):
    hi, lo = _split_bf16(a)
    g = gmat_ref[...]
    outs = []
    for s in range(C_CONV // MXU_DIM):
        sl = slice(MXU_DIM * s, MXU_DIM * (s + 1))
        outs.append(_dot(hi[:, sl], g) + _dot(lo[:, sl], g))
    return jnp.concatenate(outs, axis=1)


def _group_norm_silu(y, gmat_ref, gn_g, gn_b):
    mu = _group_mean(y, gmat_ref)
    yc = y - mu
    var = _group_mean(yc * yc, gmat_ref)
    yn = yc * lax.rsqrt(var + LN_EPS) * gn_g + gn_b
    return yn * jax.nn.sigmoid(yn)


def _store_token_tiles(ref, val, tok0=0):
    n = val.shape[0]
    for c in range(D_MODEL // LANES):
        ref[pl.ds(tok0 * SUBLANES + c, n, stride=SUBLANES), :] = val[:, c * LANES:(c + 1) * LANES]


def _load_token_tiles(ref, n, lead=(), tok0=0):
    parts = [ref[lead + (pl.ds(tok0 * SUBLANES + c, n, stride=SUBLANES), slice(None))]
             for c in range(D_MODEL // LANES)]
    return jnp.concatenate(parts, axis=1)


def _front(x, w):
    xn = _layer_norm(x, w["ln_in_g"][...], w["ln_in_b"][...])
    z = _dot(xn.astype(BF16), w["w_in"][...])
    a_val = z[:, 0:C_CONV]
    a_gate = z[:, C_CONV:2 * C_CONV]
    g_u = z[:, 2 * C_CONV:2 * C_CONV + C_GMLP]
    g_v = z[:, 2 * C_CONV + C_GMLP:]
    u = a_val * jax.nn.sigmoid(a_gate)
    ug = jax.nn.gelu(g_u)
    v = _layer_norm(jax.nn.gelu(g_v), w["vn_g"][...], w["vn_b"][...])
    return xn, u, ug, v


def _tail(xn, y_a, y_b, w, h_ref, idx_ref, gate_ref, row0=0):
    rows = slice(row0, row0 + xn.shape[0])
    mix = _dot(y_a.astype(BF16), w["w_out"][0:C_CONV, :]) + _dot(y_b.astype(BF16), w["w_out"][C_CONV:, :])
    h = _layer_norm(DEEPNORM_ALPHA * xn + mix, w["ln1_g"][...], w["ln1_b"][...])
    hb, h_lo = _split_bf16(h)
    _store_token_tiles(h_ref, h, row0)

    pair = _dot(hb, w["wr_pair"][...])
    logits = pair[:, :LANES] + pair[:, LANES:] + _dot(h_lo, w["wr_hi"][...]) + w["b_r"][...]
    lane = lax.broadcasted_iota(jnp.int32, logits.shape, 1)
    lane_f = lane.astype(F32)
    vals = jnp.where(lane < N_EXPERTS, logits, -jnp.inf)
    tops, ids = [], []
    for _ in range(TOP_K):
        m = jnp.max(vals, axis=-1, keepdims=True)
        i = jnp.min(jnp.where(vals == m, lane_f, float(LANES)), axis=-1, keepdims=True)
        vals = jnp.where(lane_f == i, -jnp.inf, vals)
        tops.append(m)
        ids.append(i)
    exps = [jnp.exp(m - tops[0]) for m in tops]
    denom = exps[0] + exps[1] + exps[2] + exps[3]
    idx_out = jnp.zeros(logits.shape, F32)
    gate_out = jnp.zeros(logits.shape, F32)
    for k in range(TOP_K):
        idx_out = jnp.where(lane == k, ids[k], idx_out)
        gate_out = jnp.where(lane == k, exps[k] / denom, gate_out)
    idx_ref[rows, :] = idx_out.astype(jnp.int32)
    gate_ref[rows, :] = gate_out


_WEIGHT_NAMES = ("ln_in_g", "ln_in_b", "w_in", "vn_g", "vn_b", "gmat", "gn_g", "gn_b", "w_out", "ln1_g", "ln1_b",
                 "wr_hi", "wr_pair", "b_r")


def _prompt_branch(j, x_ref, w, cwb_ref, ws_ref, bs_ref, outs, tail_ref, vch_ref, ubuf, yslab, last_j):
    t = T_PROMPT
    n = t // PROMPT_SPLIT
    n_slabs = C_CONV // LANES

    @pl.when(j == 0)
    def _():
        ubuf[:, 0:HIST, :] = jnp.zeros((n_slabs, HIST, LANES), F32)

    @pl.when(j > 0)
    def _():
        ubuf[:, 0:HIST, :] = ubuf[:, t:t + HIST, :]

    def front(h):
        xn, u, ug, v = _front(x_ref[h * n:(h + 1) * n, :], w)
        for s in range(n_slabs):
            ubuf[s, HIST + h * n:HIST + (h + 1) * n, :] = u[:, s * LANES:(s + 1) * LANES]
        return xn, u, ug, v

    def conv(h):
        rows = CONV_STRIDE * SUBLANES
        first = HIST - (CONV_W - 1)
        for s in range(n_slabs):
            for c in range(h * n // rows, (h + 1) * n // rows):
                accs = [None] * CONV_STRIDE
                for shift in range(CONV_STRIDE + CONV_W - 1):
                    win = ubuf[s, pl.ds(first + c * rows + shift, SUBLANES, stride=CONV_STRIDE), :]
                    for ph in range(CONV_STRIDE):
                        k = shift - ph
                        if 0 <= k < CONV_W:
                            term = cwb_ref[k, s] * win
                            accs[ph] = term if accs[ph] is None else accs[ph] + term
                for ph in range(CONV_STRIDE):
                    yslab[s, pl.ds(c * rows + ph, SUBLANES, stride=CONV_STRIDE), :] = accs[ph]
        y_conv = jnp.concatenate([yslab[s, h * n:(h + 1) * n, :] for s in range(n_slabs)], axis=1)
        return _group_norm_silu(y_conv, w["gmat"], w["gn_g"][...], w["gn_b"][...])

    def spatial_gate(ug, v):
        lane = lax.broadcasted_iota(jnp.int32, (CHUNK, LANES), 1)
        mixed_chunks = []
        for c in range(n // CHUNK):
            vc = v[c * CHUNK:(c + 1) * CHUNK, :]
            parts = []
            for q in range(N_HEADS // 2):
                vp = vc[:, q * LANES:(q + 1) * LANES]
                rhs = jnp.concatenate([jnp.where(lane < HEAD_DIM, vp, 0.0), jnp.where(lane >= HEAD_DIM, vp, 0.0)],
                                      axis=0).astype(BF16)
                parts.append(_dot(ws_ref[q], rhs))
            mixed_chunks.append(jnp.concatenate(parts, axis=1) + bs_ref[...])
        return ug * jnp.concatenate(mixed_chunks, axis=0)

    def tail(h, f, y_a, y_b):
        _tail(f[0], y_a, y_b, w, *outs, row0=h * n)

    fronts = [front(0)]
    y_as = {}
    for h in range(PROMPT_SPLIT):
        if h + 1 < PROMPT_SPLIT:
            fronts.append(front(h + 1))
        y_as[h] = conv(h)
        if h > 0:
            tail(h - 1, fronts[h - 1], y_as.pop(h - 1), spatial_gate(fronts[h - 1][2], fronts[h - 1][3]))
    last = PROMPT_SPLIT - 1
    tail(last, fronts[last], y_as.pop(last), spatial_gate(fronts[last][2], fronts[last][3]))

    @pl.when(j == last_j)
    def _():
        tail_ref[...] = fronts[last][1][n - HIST:, :]
        vch_ref[...] = fronts[last][3][n - CHUNK:, :]


def _sample_branch(x_ref, cpad_ref, w, cw_ref, convu_ref, gw_ref, b8_ref, outs, u_out_ref, v_out_ref,
                   uscr, vscr, yscr, mscr):
    xn, u, ug, v = _front(x_ref[...], w)
    u_out_ref[...] = u
    v_out_ref[...] = v
    uscr[...] = u
    vscr[...] = v
    first = HIST - (CONV_W - 1)
    rows = SEQ_CHUNK * SUBLANES

    def step(c, carry):
        r0 = pl.multiple_of(c * rows, rows)
        s0 = c * SEQ_CHUNK
        u3 = uscr[pl.ds(r0, rows), :].reshape(SEQ_CHUNK, SUBLANES, C_CONV)
        v3 = vscr[pl.ds(r0, rows), :].reshape(SEQ_CHUNK, SUBLANES, C_GMLP)
        acc = jnp.zeros((SEQ_CHUNK, SUBLANES, C_CONV), F32)
        for k in range(CONV_W):
            acc = acc + cw_ref[pl.ds(k, 1), :][None] * cpad_ref[pl.ds(s0, SEQ_CHUNK), pl.ds(first + k, SUBLANES), :]
        mix = jnp.zeros((SEQ_CHUNK, SUBLANES, C_GMLP), F32) + b8_ref[...][None]
        for s in range(SUBLANES):
            acc = acc + u3[:, s:s + 1, :] * convu_ref[s][None]
            mix = mix + v3[:, s:s + 1, :] * gw_ref[s][None]
        yscr[pl.ds(r0, rows), :] = acc.reshape(rows, C_CONV)
        mscr[pl.ds(r0, rows), :] = mix.reshape(rows, C_GMLP)
        return carry

    lax.fori_loop(0, SEQS_PER_TILE // SEQ_CHUNK, step, 0)
    y_a = _group_norm_silu(yscr[...], w["gmat"], w["gn_g"][...], w["gn_b"][...])
    y_b = ug * mscr[...]
    _tail(xn, y_a, y_b, w, *outs)


def _mixer_body(*refs, n_prompt_tiles, tiles_per_seq):
    n_w = len(_WEIGHT_NAMES)
    xp_ref, xs_ref, cpad_ref = refs[:3]
    w = dict(zip(_WEIGHT_NAMES, refs[3:3 + n_w]))
    cw_ref, cwb_ref, ws_ref, bs_ref, convu_ref, gw_ref, b8_ref = refs[3 + n_w:10 + n_w]
    outs = refs[10 + n_w:13 + n_w]
    tail_ref, vch_ref, u_out_ref, v_out_ref = refs[13 + n_w:17 + n_w]
    ubuf, yslab, yscr, uscr, vscr, mscr = refs[17 + n_w:]
    step = pl.program_id(0)

    @pl.when(step < n_prompt_tiles)
    def _():
        _prompt_branch(lax.rem(step, tiles_per_seq), xp_ref, w, cwb_ref, ws_ref, bs_ref, outs, tail_ref,
                       vch_ref, ubuf, yslab, tiles_per_seq - 1)

    @pl.when(step >= n_prompt_tiles)
    def _():
        _sample_branch(xs_ref, cpad_ref, w, cw_ref, convu_ref, gw_ref, b8_ref, outs, u_out_ref, v_out_ref,
                       uscr, vscr, yscr, mscr)


def _rank_body(idx_ref, tri_ref, before_ref, counts_ref, carry):
    i = pl.program_id(0)

    @pl.when(i == 0)
    def _():
        carry[...] = jnp.zeros(carry.shape, F32)

    idx = idx_ref[...]
    lane = lax.broadcasted_iota(jnp.int32, idx.shape, 1)
    multi = jnp.zeros(idx.shape, F32)
    for k in range(TOP_K):
        multi = multi + (lane == idx[:, k:k + 1]).astype(F32)
    before_ref[...] = (_dot(tri_ref[...], multi.astype(BF16)) + carry[...]).astype(jnp.int32)
    carry[...] = carry[...] + jnp.sum(multi, axis=0, keepdims=True)
    counts_ref[...] = carry[...].astype(jnp.int32)


def _pad_bits():
    b = TM // 2
    while b >= 1:
        yield b
        b //= 2


def _token_rows(t, n=1):
    return pl.ds(pl.multiple_of(t * SUBLANES, SUBLANES), n * SUBLANES)


def _dispatch_body(pad_off_ref, pad_n_ref, n_used_ref, dest_ref, h_ref, pp_ref, ps_ref, wg_ref, wp_ref, xs_ref, base_ref,
                   zbuf, sem, zsem, *, n_blocks, n_prompt_tiles):
    i = pl.program_id(0)

    def issue(r, carry):
        src = h_ref.at[_token_rows(r), :]
        for k in range(TOP_K):
            d = dest_ref[0, r * TOP_K + k]
            pltpu.make_async_copy(src, xs_ref.at[_token_rows(d), :], sem).start(priority=k % 2)
        return carry

    lax.fori_loop(0, T_DISPATCH, issue, 0, unroll=8)

    @pl.when(i == 0)
    def _():
        zbuf[...] = jnp.zeros(zbuf.shape, F32)

        def zero_copy(off, b):
            return pltpu.make_async_copy(zbuf.at[pl.ds(0, b * SUBLANES), :], xs_ref.at[_token_rows(off, b), :], zsem)

        def start_or_wait(cond, cp, wait):
            @pl.when(cond)
            def _():
                if wait:
                    cp.wait()
                else:
                    cp.start()

        for wait in (False, True):
            for e in range(N_EXPERTS):
                n = pad_n_ref[e]
                for b in _pad_bits():
                    start_or_wait((n & b) != 0, zero_copy(pad_off_ref[e] + (n & ~(2 * b - 1)), b), wait)

        half = TM // 2

        def tail_block(wait):
            def go(blk, carry):
                for s in range(TM // half):
                    cp = zero_copy(blk * TM + s * half, half)
                    cp.wait() if wait else cp.start()
                return carry
            return go

        lax.fori_loop(n_used_ref[0], n_blocks, tail_block(False), 0)
        lax.fori_loop(n_used_ref[0], n_blocks, tail_block(True), 0)

    rows = TM_CHAIN
    is_prompt = i < n_prompt_tiles
    for c in range(T_DISPATCH // rows):
        sl = slice(c * rows, (c + 1) * rows)
        h = _load_token_tiles(h_ref, rows, tok0=c * rows)
        p = jnp.where(is_prompt, pp_ref[sl, :], ps_ref[sl, :])
        ple = _dot(p.astype(BF16), wp_ref[...]) * jax.nn.sigmoid(_dot(h.astype(BF16), wg_ref[...]))
        base_ref[sl, :] = DEEPNORM_ALPHA * h + ple

    for _ in range(TOP_K):
        pltpu.make_async_copy(h_ref, xs_ref.at[_token_rows(0, T_DISPATCH), :], sem).wait()


def _expert_body(blk_e_ref, first_ref, slot_ref, next_ref, rows_ref, n_used_ref, x_ref, bu_ref, bd_ref, wu_hbm, wd_hbm,
                 y_ref, wu_f32, wd_f32, wu_bf, wd_bf, sems):
    i = pl.program_id(0)

    def weight_copies(e, s):
        return (pltpu.make_async_copy(wu_hbm.at[e], wu_f32.at[s], sems.at[s]),
                pltpu.make_async_copy(wd_hbm.at[e], wd_f32.at[s], sems.at[s]))

    @pl.when(i < n_used_ref[0])
    def _():
        s = slot_ref[i]

        @pl.when(first_ref[i] == 1)
        def _():
            @pl.when(i == 0)
            def _():
                for cp in weight_copies(blk_e_ref[i], s):
                    cp.start()

            for cp in weight_copies(blk_e_ref[i], s):
                cp.wait()
            nxt = next_ref[i]

            @pl.when(nxt >= 0)
            def _():
                for cp in weight_copies(nxt, 1 - s):
                    cp.start()

            chunk = D_MODEL // SUBLANES

            def cast_rows(c, carry):
                r = pl.multiple_of(c * chunk, chunk)
                wu_bf[pl.ds(r, chunk), :] = wu_f32[s, pl.ds(r, chunk), :].astype(BF16)
                wd_bf[pl.ds(r, chunk), :] = wd_f32[s, pl.ds(r, chunk), :].astype(BF16)
                return carry

            lax.fori_loop(0, SUBLANES, cast_rows, 0)

        def chain(c):
            x = _load_token_tiles(x_ref, TM_CHAIN, tok0=c * TM_CHAIN)
            hcat = _dot(x.astype(BF16), wu_bf[...]) + bu_ref[...]
            h_glu = jnp.minimum(hcat[:, :D_FF], SWIGLU_LIMIT)
            h_lin = jnp.clip(hcat[:, D_FF:], -SWIGLU_LIMIT, SWIGLU_LIMIT)
            act = h_glu * jax.nn.sigmoid(SWIGLU_ALPHA * h_glu) * (h_lin + 1.0)
            _store_token_tiles(y_ref, _dot(act.astype(BF16), wd_bf[...]) + bd_ref[...], tok0=c * TM_CHAIN)

        n_chains = TM // TM_CHAIN
        live = (rows_ref[i] + TM_CHAIN - 1) // TM_CHAIN
        for m in range(1, n_chains + 1):
            @pl.when(live == m)
            def _():
                for c in range(m):
                    chain(c)
                if m < n_chains:
                    y_ref[m * TM_CHAIN * SUBLANES:, :] = jnp.zeros(((n_chains - m) * TM_CHAIN * SUBLANES, LANES), F32)

    @pl.when(pl.program_id(0) >= n_used_ref[0])
    def _():
        y_ref[...] = jnp.zeros(y_ref.shape, F32)


def _combine_body(dest_ref, dest_next_ref, y_ref, base_ref, gate_ref, g_ref, b_ref, outp_ref, outs_ref, ybuf, sems, *,
                  n_prompt_tiles, n_tiles):
    i = pl.program_id(0)

    def issue_tile(d_ref, tile, slot):
        first = lax.rem(tile, T_DISPATCH // T_ROWS) * (T_ROWS * TOP_K)

        def issue(r, carry):
            for k in range(TOP_K):
                d = d_ref[0, first + r * TOP_K + k]
                pltpu.make_async_copy(y_ref.at[_token_rows(d), :], ybuf.at[slot * TOP_K + k, _token_rows(r), :],
                                      sems.at[slot]).start(priority=k % 2)
            return carry

        lax.fori_loop(0, T_ROWS, issue, 0, unroll=8)

    slot = lax.rem(i, 2)

    @pl.when(i == 0)
    def _():
        issue_tile(dest_ref, i, 0)

    @pl.when(i + 1 < n_tiles)
    def _():
        issue_tile(dest_next_ref, i + 1, 1 - slot)

    for k in range(TOP_K):
        pltpu.make_async_copy(y_ref.at[_token_rows(0, T_ROWS), :], ybuf.at[slot * TOP_K + k], sems.at[slot]).wait()

    gate = gate_ref[...]
    acc = base_ref[...]
    for k in range(TOP_K):
        acc = acc + gate[:, k:k + 1] * _load_token_tiles(ybuf, T_ROWS, lead=(slot * TOP_K + k,))
    out = _layer_norm(acc, g_ref[...], b_ref[...])

    @pl.when(i < n_prompt_tiles)
    def _():
        outp_ref[...] = out

    @pl.when(i >= n_prompt_tiles)
    def _():
        outs_ref[...] = out


def _full(shape):
    return pl.BlockSpec(shape, lambda *_: (0,) * len(shape), pipeline_mode=pl.Buffered(1))


def kernel(x_prompt, x_sample, cache_conv, p_prompt, p_sample, ln_in_g, ln_in_b, w_in, conv_w, gn_g, gn_b, vn_g, vn_b,
           w_spatial, b_spatial, w_out, ln1_g, ln1_b, w_router, b_router, w_up, b_up, w_down, b_down, w_ple,
           w_ple_gate, ln2_g, ln2_b):
    batch, seq, _ = x_prompt.shape
    dec_batch, dec_seq, _ = x_sample.shape
    assert w_in.shape[0] == DEPTH and dec_seq == SUBLANES and seq % T_PROMPT == 0
    n_prompt = batch * seq
    n_sample = dec_batch * dec_seq
    n_tok = n_prompt + n_sample
    t_s = SEQS_PER_TILE * dec_seq
    assert n_prompt % T_ROWS == 0 and n_sample % T_ROWS == 0 and n_tok % T_RANK == 0 and n_sample % t_s == 0
    assert n_tok % T_DISPATCH == 0

    row = lambda a: a.reshape(1, -1).astype(F32)
    gidx = jnp.arange(MXU_DIM) // (C_CONV // N_CONV_GROUPS)
    gmat = jnp.where(gidx[:, None] == gidx[None, :], 1.0 / (C_CONV // N_CONV_GROUPS), 0.0).astype(BF16)
    wr_pad = jnp.pad(w_router[0].astype(F32), ((0, 0), (0, LANES - N_EXPERTS)))
    wr_hi = wr_pad.astype(BF16)
    wr_lo = (wr_pad - wr_hi.astype(F32)).astype(BF16)
    weights = dict(
        ln_in_g=row(ln_in_g), ln_in_b=row(ln_in_b), w_in=w_in[0].astype(BF16), vn_g=row(vn_g[0]), vn_b=row(vn_b[0]),
        gmat=gmat, gn_g=row(gn_g[0]), gn_b=row(gn_b[0]), w_out=w_out[0].astype(BF16), ln1_g=row(ln1_g[0]),
        ln1_b=row(ln1_b[0]), wr_hi=wr_hi, wr_pair=jnp.concatenate([wr_hi, wr_lo], axis=1),
        b_r=jnp.pad(row(b_router[0]), ((0, 0), (0, LANES - N_EXPERTS))))
    w_list = [weights[n] for n in _WEIGHT_NAMES]
    w_specs = [_full(a.shape) for a in w_list]
    cw = jnp.pad(conv_w[0].astype(F32), ((0, HIST - CONV_W), (0, 0)))
    causal = jnp.tril(jnp.ones((CHUNK, CHUNK), bool))
    ws_m = jnp.where(causal[None], w_spatial[0], 0.0)
    ws_cat = jnp.concatenate([ws_m[0::2], ws_m[1::2]], axis=2).astype(BF16)
    bs_full = jnp.repeat(b_spatial[0].T.astype(F32), HEAD_DIM, axis=1)
    s_i = jnp.arange(SUBLANES)[:, None]
    t_i = jnp.arange(SUBLANES)[None, :]
    tap = jnp.clip(CONV_W - 1 - t_i + s_i, 0, CONV_W - 1)
    convu = jnp.where((s_i <= t_i)[:, :, None], conv_w[0].astype(F32)[tap], 0.0)
    gw8 = jnp.transpose(ws_m[:, :SUBLANES, :SUBLANES], (2, 1, 0))
    gw8 = jnp.repeat(gw8.astype(F32), HEAD_DIM, axis=2)
    b8 = bs_full[:SUBLANES]
    cpad = jnp.pad(cache_conv[0].astype(F32), ((0, 0), (HIST - (CONV_W - 1), SUBLANES), (0, 0)))

    cparams = lambda sem: pltpu.CompilerParams(dimension_semantics=sem, vmem_limit_bytes=VMEM_LIMIT)

    nj = seq // T_PROMPT
    npt = n_prompt // T_PROMPT
    assert t_s == T_PROMPT
    pstep = lambda i: jnp.minimum(i, npt - 1)
    sstep = lambda i: jnp.maximum(i - npt, 0)
    tok_block = lambda width: pl.BlockSpec((T_PROMPT, width), lambda i: (i, 0))
    cwb = jnp.broadcast_to(conv_w[0].astype(F32).reshape(CONV_W, C_CONV // LANES, 1, LANES),
                           (CONV_W, C_CONV // LANES, SUBLANES, LANES))
    tables = [cw, cwb, ws_cat, bs_full, convu, gw8, b8]
    h_all, idx_all, gate_all, u_tail, v_chunk, u_s, v_s = pl.pallas_call(
        functools.partial(_mixer_body, n_prompt_tiles=npt, tiles_per_seq=nj),
        grid=(npt + n_sample // t_s,),
        in_specs=[pl.BlockSpec((None, T_PROMPT, D_MODEL), lambda i: (pstep(i) // nj, pstep(i) % nj, 0)),
                  pl.BlockSpec((t_s, D_MODEL), lambda i: (sstep(i), 0)),
                  pl.BlockSpec((SEQS_PER_TILE,) + cpad.shape[1:], lambda i: (sstep(i), 0, 0),
                               pipeline_mode=pl.Buffered(1))]
                 + w_specs + [_full(a.shape) for a in tables],
        out_specs=[pl.BlockSpec((T_PROMPT * SUBLANES, LANES), lambda i: (i, 0)),
                   tok_block(LANES), tok_block(LANES),
                   pl.BlockSpec((None, HIST, C_CONV), lambda i: (pstep(i) // nj, 0, 0)),
                   pl.BlockSpec((None, CHUNK, C_GMLP), lambda i: (pstep(i) // nj, 0, 0)),
                   pl.BlockSpec((t_s, C_CONV), lambda i: (sstep(i), 0)),
                   pl.BlockSpec((t_s, C_GMLP), lambda i: (sstep(i), 0))],
        out_shape=(jax.ShapeDtypeStruct((n_tok * SUBLANES, LANES), F32),
                   jax.ShapeDtypeStruct((n_tok, LANES), jnp.int32), jax.ShapeDtypeStruct((n_tok, LANES), F32),
                   jax.ShapeDtypeStruct((batch, HIST, C_CONV), F32), jax.ShapeDtypeStruct((batch, CHUNK, C_GMLP), F32),
                   jax.ShapeDtypeStruct((n_sample, C_CONV), F32), jax.ShapeDtypeStruct((n_sample, C_GMLP), F32)),
        scratch_shapes=[pltpu.VMEM((C_CONV // LANES, T_PROMPT + HIST, LANES), F32),
                        pltpu.VMEM((C_CONV // LANES, T_PROMPT, LANES), F32)]
                       + [pltpu.VMEM((T_PROMPT, C_CONV), F32)] * 4,
        compiler_params=pltpu.CompilerParams(dimension_semantics=("arbitrary",), vmem_limit_bytes=BIG_VMEM_LIMIT),
        name="mixer",
    )(x_prompt, x_sample.reshape(n_sample, D_MODEL), cpad, *w_list, *tables)

    tri = (jnp.arange(T_RANK)[:, None] > jnp.arange(T_RANK)[None, :]).astype(BF16)
    before_all, counts = pl.pallas_call(
        _rank_body,
        grid=(n_tok // T_RANK,),
        in_specs=[pl.BlockSpec((T_RANK, LANES), lambda i: (i, 0)), _full(tri.shape)],
        out_specs=[pl.BlockSpec((T_RANK, LANES), lambda i: (i, 0)), _full((1, LANES))],
        out_shape=(jax.ShapeDtypeStruct((n_tok, LANES), jnp.int32), jax.ShapeDtypeStruct((1, LANES), jnp.int32)),
        scratch_shapes=[pltpu.VMEM((1, LANES), F32)],
        compiler_params=cparams(("arbitrary",)),
        name="rank",
    )(idx_all, tri)

    n_assign = n_tok * TOP_K
    n_blocks = n_assign // TM + N_EXPERTS
    cap = n_blocks * TM
    cnt = counts[0, :N_EXPERTS]
    padded = (cnt + TM - 1) // TM * TM
    p_end = jnp.cumsum(padded)
    p_start = p_end - padded
    e_ids = jnp.arange(N_EXPERTS, dtype=jnp.int32)
    lookup = lambda table, ids: jnp.sum(jnp.where(ids[..., None] == e_ids, table, 0), axis=-1)
    slot_table = p_start[None, :] + before_all[:, :N_EXPERTS]
    dest = jnp.sum(jnp.where(idx_all[:, :TOP_K, None] == e_ids, slot_table[:, None, :], 0), axis=-1).astype(jnp.int32)
    dest_tiles = dest.reshape(n_tok // T_DISPATCH, 1, T_DISPATCH * TOP_K)
    blk_row = jnp.arange(n_blocks, dtype=jnp.int32) * TM
    blk_e = jnp.minimum(jnp.sum(p_end[None, :] <= blk_row[:, None], axis=1), N_EXPERTS - 1).astype(jnp.int32)
    n_used = (p_end[-1:] // TM).astype(jnp.int32)
    pad_off = (p_start + cnt).astype(jnp.int32)
    pad_n = (padded - cnt).astype(jnp.int32)
    used = cnt > 0
    slot_e = (jnp.cumsum(used.astype(jnp.int32)) - 1) & 1
    later_used = jnp.where(used[None, :] & (e_ids[None, :] > e_ids[:, None]), e_ids[None, :], N_EXPERTS)
    next_e = jnp.min(later_used, axis=1)
    next_e = jnp.where(next_e < N_EXPERTS, next_e, -1).astype(jnp.int32)
    blk_first = ((blk_row == lookup(p_start, blk_e)) & (blk_row < p_end[-1])).astype(jnp.int32)
    blk_rows = jnp.clip(lookup(p_start + cnt, blk_e) - blk_row, 0, TM).astype(jnp.int32)
    blk_slot = lookup(slot_e, blk_e).astype(jnp.int32)
    blk_next = lookup(next_e, blk_e).astype(jnp.int32)


    assert seq % T_DISPATCH == 0 and n_sample % T_DISPATCH == 0
    npd = n_prompt // T_DISPATCH
    per_seq = seq // T_DISPATCH
    pd = lambda i: jnp.minimum(i, npd - 1)
    w_gate = w_ple_gate[0].astype(BF16)
    w_ple_b = w_ple[0].astype(BF16)
    x_sorted, base_all = pl.pallas_call(
        functools.partial(_dispatch_body, n_blocks=n_blocks, n_prompt_tiles=npd),
        grid_spec=pltpu.PrefetchScalarGridSpec(
            num_scalar_prefetch=3,
            grid=(n_tok // T_DISPATCH,),
            in_specs=[pl.BlockSpec((None, 1, T_DISPATCH * TOP_K), lambda i, *_: (i, 0, 0), memory_space=pltpu.SMEM),
                      pl.BlockSpec((T_DISPATCH * SUBLANES, LANES), lambda i, *_: (i, 0)),
                      pl.BlockSpec((None, T_DISPATCH, PLE_DIM), lambda i, *_: (pd(i) // per_seq, pd(i) % per_seq, 0)),
                      pl.BlockSpec((T_DISPATCH, PLE_DIM), lambda i, *_: (jnp.maximum(i - npd, 0), 0)),
                      _full(w_gate.shape), _full(w_ple_b.shape)],
            out_specs=[pl.BlockSpec(memory_space=pl.ANY),
                       pl.BlockSpec((T_DISPATCH, D_MODEL), lambda i, *_: (i, 0))],
            scratch_shapes=[pltpu.VMEM((TM // 2 * SUBLANES, LANES), F32), pltpu.SemaphoreType.DMA,
                            pltpu.SemaphoreType.DMA]),
        out_shape=(jax.ShapeDtypeStruct((cap * SUBLANES, LANES), F32), jax.ShapeDtypeStruct((n_tok, D_MODEL), F32)),
        compiler_params=cparams(("arbitrary",)),
        name="dispatch",
    )(pad_off, pad_n, n_used, dest_tiles, h_all, p_prompt[0], p_sample[0].reshape(n_sample, PLE_DIM), w_gate, w_ple_b)

    last = lambda i, nu: jnp.minimum(i, nu[0] - 1)
    y_sorted = pl.pallas_call(
        _expert_body,
        grid_spec=pltpu.PrefetchScalarGridSpec(
            num_scalar_prefetch=6,
            grid=(n_blocks,),
            in_specs=[pl.BlockSpec((TM * SUBLANES, LANES), lambda i, be, bf, bs, bn, br, nu: (last(i, nu), 0)),
                      pl.BlockSpec((None, 1, 2 * D_FF), lambda i, be, bf, bs, bn, br, nu: (be[last(i, nu)], 0, 0)),
                      pl.BlockSpec((None, 1, D_MODEL), lambda i, be, bf, bs, bn, br, nu: (be[last(i, nu)], 0, 0)),
                      pl.BlockSpec(memory_space=pl.ANY), pl.BlockSpec(memory_space=pl.ANY)],
            out_specs=pl.BlockSpec((TM * SUBLANES, LANES), lambda i, be, bf, bs, bn, br, nu: (i, 0)),
            scratch_shapes=[pltpu.VMEM((2, D_MODEL, 2 * D_FF), F32), pltpu.VMEM((2, D_FF, D_MODEL), F32),
                            pltpu.VMEM((D_MODEL, 2 * D_FF), BF16), pltpu.VMEM((D_FF, D_MODEL), BF16),
                            pltpu.SemaphoreType.DMA((2,))]),
        out_shape=jax.ShapeDtypeStruct((cap * SUBLANES, LANES), F32),
        compiler_params=pltpu.CompilerParams(dimension_semantics=("arbitrary",), vmem_limit_bytes=BIG_VMEM_LIMIT),
        name="experts",
    )(blk_e, blk_first, blk_slot, blk_next, blk_rows, n_used, x_sorted, b_up[0].astype(F32)[:, None, :],
      b_down[0].astype(F32)[:, None, :], w_up[0].astype(F32), w_down[0].astype(F32))

    npt = n_prompt // T_ROWS
    n_tiles = n_tok // T_ROWS
    per = T_DISPATCH // T_ROWS
    dest_spec = pl.BlockSpec((None, 1, T_DISPATCH * TOP_K), lambda i: (i // per, 0, 0), memory_space=pltpu.SMEM)
    dest_next_spec = pl.BlockSpec((None, 1, T_DISPATCH * TOP_K),
                                  lambda i: (jnp.minimum(i + 1, n_tiles - 1) // per, 0, 0), memory_space=pltpu.SMEM)
    out_p, out_s = pl.pallas_call(
        functools.partial(_combine_body, n_prompt_tiles=npt, n_tiles=n_tiles),
        grid=(n_tiles,),
        in_specs=[dest_spec, dest_next_spec, pl.BlockSpec(memory_space=pl.ANY),
                  pl.BlockSpec((T_ROWS, D_MODEL), lambda i: (i, 0)), pl.BlockSpec((T_ROWS, LANES), lambda i: (i, 0)),
                  _full((1, D_MODEL)), _full((1, D_MODEL))],
        out_specs=[pl.BlockSpec((T_ROWS, D_MODEL), lambda i: (jnp.minimum(i, npt - 1), 0)),
                   pl.BlockSpec((T_ROWS, D_MODEL), lambda i: (jnp.maximum(i - npt, 0), 0))],
        out_shape=(jax.ShapeDtypeStruct((n_prompt, D_MODEL), F32), jax.ShapeDtypeStruct((n_sample, D_MODEL), F32)),
        scratch_shapes=[pltpu.VMEM((2 * TOP_K, T_ROWS * SUBLANES, LANES), F32), pltpu.SemaphoreType.DMA((2,))],
        compiler_params=cparams(("arbitrary",)),
        name="combine",
    )(dest_tiles, dest_tiles, y_sorted, base_all, gate_all, row(ln2_g[0]), row(ln2_b[0]))

    y_prompt = out_p.reshape(batch, seq, D_MODEL)
    y_sample = out_s.reshape(dec_batch, dec_seq, D_MODEL)
    conv_state_prompt = u_tail[None, :, HIST - (CONV_W - 1):, :]
    u_s3 = u_s.reshape(dec_batch, dec_seq, C_CONV)
    conv_state_sample = jnp.concatenate([cache_conv[0][:, dec_seq:, :].astype(F32), u_s3], axis=1)[None]
    chunk_v_prompt = v_chunk[None]
    chunk_v_sample = v_s.reshape(1, dec_batch, dec_seq, C_GMLP)
    return (y_prompt, y_sample, conv_state_prompt, conv_state_sample, chunk_v_prompt, chunk_v_sample)
```

```python
import functools

import jax
import jax.numpy as jnp
from jax import lax
from jax.experimental import pallas as pl
from jax.experimental.pallas import tpu as pltpu

F32 = jnp.float32
BF16 = jnp.bfloat16

D_MODEL = 1024
C_CONV = 512
C_GMLP = 512
N_CONV_GROUPS = 8
CONV_W = 31
N_HEADS = 8
HEAD_DIM = C_GMLP // N_HEADS
CHUNK = 128
N_EXPERTS = 32
TOP_K = 4
D_FF = 1024
PLE_DIM = 256
SWIGLU_LIMIT = 7.0
SWIGLU_ALPHA = 1.702
LN_EPS = 1e-5
DEPTH = 1
DEEPNORM_ALPHA = (2.0 * DEPTH) ** 0.25

LANES = 128
SUBLANES = 8
MXU_DIM = 256
VMEM_LIMIT = 48 * 1024 * 1024
BIG_VMEM_LIMIT = 56 * 1024 * 1024

T_PROMPT = 512
PROMPT_SPLIT = 2
HIST = 32
CONV_STRIDE = 4
SEQS_PER_TILE = 64
SEQ_CHUNK = 4
T_RANK = 1024
T_DISPATCH = 1024
T_ROWS = 256
TM = 512
TM_CHAIN = 256


def _dot(a, b):
    return jnp.dot(a, b, preferred_element_type=F32)


def _layer_norm(x, g, b):
    mu = jnp.mean(x, axis=-1, keepdims=True)
    xc = x - mu
    var = jnp.mean(xc * xc, axis=-1, keepdims=True)
    return xc * lax.rsqrt(var + LN_EPS) * g + b


def _split_bf16(a):
    hi = a.astype(BF16)
    lo = (a - hi.astype(F32)).astype(BF16)
    return hi, lo


def _group_mean(a, gmat_ref):
    hi, lo = _split_bf16(a)
    g = gmat_ref[...]
    outs = []
    for s in range(C_CONV // MXU_DIM):
        sl = slice(MXU_DIM * s, MXU_DIM * (s + 1))
        outs.append(_dot(hi[:, sl], g) + _dot(lo[:, sl], g))
    return jnp.concatenate(outs, axis=1)


def _group_norm_silu(y, gmat_ref, gn_g, gn_b):
    mu = _group_mean(y, gmat_ref)
    yc = y - mu
    var = _group_mean(yc * yc, gmat_ref)
    yn = yc * lax.rsqrt(var + LN_EPS) * gn_g + gn_b
    return yn * jax.nn.sigmoid(yn)


def _store_token_tiles(ref, val, tok0=0):
    n = val.shape[0]
    for c in range(D_MODEL // LANES):
        ref[pl.ds(tok0 * SUBLANES + c, n, stride=SUBLANES), :] = val[:, c * LANES:(c + 1) * LANES]


def _load_token_tiles(ref, n, lead=(), tok0=0):
    parts = [ref[lead + (pl.ds(tok0 * SUBLANES + c, n, stride=SUBLANES), slice(None))]
             for c in range(D_MODEL // LANES)]
    return jnp.concatenate(parts, axis=1)


def _front(x, w):
    xn = _layer_norm(x, w["ln_in_g"][...], w["ln_in_b"][...])
    z = _dot(xn.astype(BF16), w["w_in"][...])
    a_val = z[:, 0:C_CONV]
    a_gate = z[:, C_CONV:2 * C_CONV]
    g_u = z[:, 2 * C_CONV:2 * C_CONV + C_GMLP]
    g_v = z[:, 2 * C_CONV + C_GMLP:]
    u = a_val * jax.nn.sigmoid(a_gate)
    ug = jax.nn.gelu(g_u)
    v = _layer_norm(jax.nn.gelu(g_v), w["vn_g"][...], w["vn_b"][...])
    return xn, u, ug, v


def _tail(xn, y_a, y_b, w, h_ref, idx_ref, gate_ref, row0=0):
    rows = slice(row0, row0 + xn.shape[0])
    mix = _dot(y_a.astype(BF16), w["w_out"][0:C_CONV, :]) + _dot(y_b.astype(BF16), w["w_out"][C_CONV:, :])
    h = _layer_norm(DEEPNORM_ALPHA * xn + mix, w["ln1_g"][...], w["ln1_b"][...])
    hb, h_lo = _split_bf16(h)
    _store_token_tiles(h_ref, h, row0)

    pair = _dot(hb, w["wr_pair"][...])
    logits = pair[:, :LANES] + pair[:, LANES:] + _dot(h_lo, w["wr_hi"][...]) + w["b_r"][...]
    lane = lax.broadcasted_iota(jnp.int32, logits.shape, 1)
    lane_f = lane.astype(F32)
    vals = jnp.where(lane < N_EXPERTS, logits, -jnp.inf)
    tops, ids = [], []
    for _ in range(TOP_K):
        m = jnp.max(vals, axis=-1, keepdims=True)
        i = jnp.min(jnp.where(vals == m, lane_f, float(LANES)), axis=-1, keepdims=True)
        vals = jnp.where(lane_f == i, -jnp.inf, vals)
        tops.append(m)
        ids.append(i)
    exps = [jnp.exp(m - tops[0]) for m in tops]
    denom = exps[0] + exps[1] + exps[2] + exps[3]
    idx_out = jnp.zeros(logits.shape, F32)
    gate_out = jnp.zeros(logits.shape, F32)
    for k in range(TOP_K):
        idx_out = jnp.where(lane == k, ids[k], idx_out)
        gate_out = jnp.where(lane == k, exps[k] / denom, gate_out)
    idx_ref[rows, :] = idx_out.astype(jnp.int32)
    gate_ref[rows, :] = gate_out


_WEIGHT_NAMES = ("ln_in_g", "ln_in_b", "w_in", "vn_g", "vn_b", "gmat", "gn_g", "gn_b", "w_out", "ln1_g", "ln1_b",
                 "wr_hi", "wr_pair", "b_r")


def _prompt_branch(j, x_ref, w, cwb_ref, ws_ref, bs_ref, outs, tail_ref, vch_ref, ubuf, yslab, last_j):
    t = T_PROMPT
    n = t // PROMPT_SPLIT
    n_slabs = C_CONV // LANES

    @pl.when(j == 0)
    def _():
        ubuf[:, 0:HIST, :] = jnp.zeros((n_slabs, HIST, LANES), F32)

    @pl.when(j > 0)
    def _():
        ubuf[:, 0:HIST, :] = ubuf[:, t:t + HIST, :]

    def front(h):
        xn, u, ug, v = _front(x_ref[h * n:(h + 1) * n, :], w)
        for s in range(n_slabs):
            ubuf[s, HIST + h * n:HIST + (h + 1) * n, :] = u[:, s * LANES:(s + 1) * LANES]
        return xn, u, ug, v

    def conv(h):
        rows = CONV_STRIDE * SUBLANES
        first = HIST - (CONV_W - 1)
        for s in range(n_slabs):
            for c in range(h * n // rows, (h + 1) * n // rows):
                accs = [None] * CONV_STRIDE
                for shift in range(CONV_STRIDE + CONV_W - 1):
                    win = ubuf[s, pl.ds(first + c * rows + shift, SUBLANES, stride=CONV_STRIDE), :]
                    for ph in range(CONV_STRIDE):
                        k = shift - ph
                        if 0 <= k < CONV_W:
                            term = cwb_ref[k, s] * win
                            accs[ph] = term if accs[ph] is None else accs[ph] + term
                for ph in range(CONV_STRIDE):
                    yslab[s, pl.ds(c * rows + ph, SUBLANES, stride=CONV_STRIDE), :] = accs[ph]
        y_conv = jnp.concatenate([yslab[s, h * n:(h + 1) * n, :] for s in range(n_slabs)], axis=1)
        return _group_norm_silu(y_conv, w["gmat"], w["gn_g"][...], w["gn_b"][...])

    def spatial_gate(ug, v):
        lane = lax.broadcasted_iota(jnp.int32, (CHUNK, LANES), 1)
        mixed_chunks = []
        for c in range(n // CHUNK):
            vc = v[c * CHUNK:(c + 1) * CHUNK, :]
            parts = []
            for q in range(N_HEADS // 2):
                vp = vc[:, q * LANES:(q + 1) * LANES]
                rhs = jnp.concatenate([jnp.where(lane < HEAD_DIM, vp, 0.0), jnp.where(lane >= HEAD_DIM, vp, 0.0)],
                                      axis=0).astype(BF16)
                parts.append(_dot(ws_ref[q], rhs))
            mixed_chunks.append(jnp.concatenate(parts, axis=1) + bs_ref[...])
        return ug * jnp.concatenate(mixed_chunks, axis=0)

    def tail(h, f, y_a, y_b):
        _tail(f[0], y_a, y_b, w, *outs, row0=h * n)

    fronts = [front(0)]
    y_as = {}
    for h in range(PROMPT_SPLIT):
        if h + 1 < PROMPT_SPLIT:
            fronts.append(front(h + 1))
        y_as[h] = conv(h)
        if h > 0:
            tail(h - 1, fronts[h - 1], y_as.pop(h - 1), spatial_gate(fronts[h - 1][2], fronts[h - 1][3]))
    last = PROMPT_SPLIT - 1
    tail(last, fronts[last], y_as.pop(last), spatial_gate(fronts[last][2], fronts[last][3]))

    @pl.when(j == last_j)
    def _():
        tail_ref[...] = fronts[last][1][n - HIST:, :]
        vch_ref[...] = fronts[last][3][n - CHUNK:, :]


def _sample_branch(x_ref, cpad_ref, w, cw_ref, convu_ref, gw_ref, b8_ref, outs, u_out_ref, v_out_ref,
                   uscr, vscr, yscr, mscr):
    xn, u, ug, v = _front(x_ref[...], w)
    u_out_ref[...] = u
    v_out_ref[...] = v
    uscr[...] = u
    vscr[...] = v
    first = HIST - (CONV_W - 1)
    rows = SEQ_CHUNK * SUBLANES

    def step(c, carry):
        r0 = pl.multiple_of(c * rows, rows)
        s0 = c * SEQ_CHUNK
        u3 = uscr[pl.ds(r0, rows), :].reshape(SEQ_CHUNK, SUBLANES, C_CONV)
        v3 = vscr[pl.ds(r0, rows), :].reshape(SEQ_CHUNK, SUBLANES, C_GMLP)
        acc = jnp.zeros((SEQ_CHUNK, SUBLANES, C_CONV), F32)
        for k in range(CONV_W):
            acc = acc + cw_ref[pl.ds(k, 1), :][None] * cpad_ref[pl.ds(s0, SEQ_CHUNK), pl.ds(first + k, SUBLANES), :]
        mix = jnp.zeros((SEQ_CHUNK, SUBLANES, C_GMLP), F32) + b8_ref[...][None]
        for s in range(SUBLANES):
            acc = acc + u3[:, s:s + 1, :] * convu_ref[s][None]
            mix = mix + v3[:, s:s + 1, :] * gw_ref[s][None]
        yscr[pl.ds(r0, rows), :] = acc.reshape(rows, C_CONV)
        mscr[pl.ds(r0, rows), :] = mix.reshape(rows, C_GMLP)
        return carry

    lax.fori_loop(0, SEQS_PER_TILE // SEQ_CHUNK, step, 0)
    y_a = _group_norm_silu(yscr[...], w["gmat"], w["gn_g"][...], w["gn_b"][...])
    y_b = ug * mscr[...]
    _tail(xn, y_a, y_b, w, *outs)


def _mixer_body(*refs, n_prompt_tiles, tiles_per_seq):
    n_w = len(_WEIGHT_NAMES)
    xp_ref, xs_ref, cpad_ref = refs[:3]
    w = dict(zip(_WEIGHT_NAMES, refs[3:3 + n_w]))
    cw_ref, cwb_ref, ws_ref, bs_ref, convu_ref, gw_ref, b8_ref = refs[3 + n_w:10 + n_w]
    outs = refs[10 + n_w:13 + n_w]
    tail_ref, vch_ref, u_out_ref, v_out_ref = refs[13 + n_w:17 + n_w]
    ubuf, yslab, yscr, uscr, vscr, mscr = refs[17 + n_w:]
    step = pl.program_id(0)

    @pl.when(step < n_prompt_tiles)
    def _():
        _prompt_branch(lax.rem(step, tiles_per_seq), xp_ref, w, cwb_ref, ws_ref, bs_ref, outs, tail_ref,
                       vch_ref, ubuf, yslab, tiles_per_seq - 1)

    @pl.when(step >= n_prompt_tiles)
    def _():
        _sample_branch(xs_ref, cpad_ref, w, cw_ref, convu_ref, gw_ref, b8_ref, outs, u_out_ref, v_out_ref,
                       uscr, vscr, yscr, mscr)


def _rank_body(idx_ref, tri_ref, before_ref, counts_ref, carry):
    i = pl.program_id(0)

    @pl.when(i == 0)
    def _():
        carry[...] = jnp.zeros(carry.shape, F32)

    idx = idx_ref[...]
    lane = lax.broadcasted_iota(jnp.int32, idx.shape, 1)
    multi = jnp.zeros(idx.shape, F32)
    for k in range(TOP_K):
        multi = multi + (lane == idx[:, k:k + 1]).astype(F32)
    before_ref[...] = (_dot(tri_ref[...], multi.astype(BF16)) + carry[...]).astype(jnp.int32)
    carry[...] = carry[...] + jnp.sum(multi, axis=0, keepdims=True)
    counts_ref[...] = carry[...].astype(jnp.int32)


def _pad_bits():
    b = TM // 2
    while b >= 1:
        yield b
        b //= 2


def _token_rows(t, n=1):
    return pl.ds(pl.multiple_of(t * SUBLANES, SUBLANES), n * SUBLANES)


def _dispatch_body(pad_off_ref, pad_n_ref, n_used_ref, dest_ref, h_ref, pp_ref, ps_ref, wg_ref, wp_ref, xs_ref, base_ref,
                   zbuf, sem, zsem, *, n_blocks, n_prompt_tiles):
    i = pl.program_id(0)

    def issue(r, carry):
        src = h_ref.at[_token_rows(r), :]
        for k in range(TOP_K):
            d = dest_ref[0, r * TOP_K + k]
            pltpu.make_async_copy(src, xs_ref.at[_token_rows(d), :], sem).start(priority=k % 2)
        return carry

    @pl.when(i == 0)
    def _():
        zbuf[...] = jnp.zeros(zbuf.shape, F32)

        def zero_copy(off, b):
            return pltpu.make_async_copy(zbuf.at[pl.ds(0, b * SUBLANES), :], xs_ref.at[_token_rows(off, b), :], zsem)

        def start_or_wait(cond, cp, wait):
            @pl.when(cond)
            def _():
                if wait:
                    cp.wait()
                else:
                    cp.start()

        for wait in (False, True):
            for e in range(N_EXPERTS):
                n = pad_n_ref[e]
                for b in _pad_bits():
                    start_or_wait((n & b) != 0, zero_copy(pad_off_ref[e] + (n & ~(2 * b - 1)), b), wait)

        half = TM // 2

        def tail_block(wait):
            def go(blk, carry):
                for s in range(TM // half):
                    cp = zero_copy(blk * TM + s * half, half)
                    cp.wait() if wait else cp.start()
                return carry
            return go

        lax.fori_loop(n_used_ref[0], n_blocks, tail_block(False), 0)
        lax.fori_loop(n_used_ref[0], n_blocks, tail_block(True), 0)

    rows = TM_CHAIN
    is_prompt = i < n_prompt_tiles
    for c in range(T_DISPATCH // rows):
        lax.fori_loop(c * rows, (c + 1) * rows, issue, 0, unroll=8)
        sl = slice(c * rows, (c + 1) * rows)
        h = _load_token_tiles(h_ref, rows, tok0=c * rows)
        p = jnp.where(is_prompt, pp_ref[sl, :], ps_ref[sl, :])
        ple = _dot(p.astype(BF16), wp_ref[...]) * jax.nn.sigmoid(_dot(h.astype(BF16), wg_ref[...]))
        base_ref[sl, :] = DEEPNORM_ALPHA * h + ple

    for _ in range(TOP_K):
        pltpu.make_async_copy(h_ref, xs_ref.at[_token_rows(0, T_DISPATCH), :], sem).wait()


def _expert_body(blk_e_ref, first_ref, slot_ref, next_ref, rows_ref, n_used_ref, x_ref, bu_ref, bd_ref, wu_hbm, wd_hbm,
                 y_ref, wu_f32, wd_f32, wu_bf, wd_bf, sems):
    i = pl.program_id(0)

    def weight_copies(e, s):
        return (pltpu.make_async_copy(wu_hbm.at[e], wu_f32.at[s], sems.at[s]),
                pltpu.make_async_copy(wd_hbm.at[e], wd_f32.at[s], sems.at[s]))

    @pl.when(i < n_used_ref[0])
    def _():
        s = slot_ref[i]

        @pl.when(first_ref[i] == 1)
        def _():
            @pl.when(i == 0)
            def _():
                for cp in weight_copies(blk_e_ref[i], s):
                    cp.start()

            for cp in weight_copies(blk_e_ref[i], s):
                cp.wait()
            nxt = next_ref[i]

            @pl.when(nxt >= 0)
            def _():
                for cp in weight_copies(nxt, 1 - s):
                    cp.start()

            chunk = D_MODEL // SUBLANES

            def cast_rows(c, carry):
                r = pl.multiple_of(c * chunk, chunk)
                wu_bf[pl.ds(r, chunk), :] = wu_f32[s, pl.ds(r, chunk), :].astype(BF16)
                wd_bf[pl.ds(r, chunk), :] = wd_f32[s, pl.ds(r, chunk), :].astype(BF16)
                return carry

            lax.fori_loop(0, SUBLANES, cast_rows, 0)

        def chain(c):
            x = _load_token_tiles(x_ref, TM_CHAIN, tok0=c * TM_CHAIN)
            hcat = _dot(x.astype(BF16), wu_bf[...]) + bu_ref[...]
            h_glu = jnp.minimum(hcat[:, :D_FF], SWIGLU_LIMIT)
            h_lin = jnp.clip(hcat[:, D_FF:], -SWIGLU_LIMIT, SWIGLU_LIMIT)
            act = h_glu * jax.nn.sigmoid(SWIGLU_ALPHA * h_glu) * (h_lin + 1.0)
            _store_token_tiles(y_ref, _dot(act.astype(BF16), wd_bf[...]) + bd_ref[...], tok0=c * TM_CHAIN)

        n_chains = TM // TM_CHAIN
        live = (rows_ref[i] + TM_CHAIN - 1) // TM_CHAIN
        for m in range(1, n_chains + 1):
            @pl.when(live == m)
            def _():
                for c in range(m):
                    chain(c)
                if m < n_chains:
                    y_ref[m * TM_CHAIN * SUBLANES:, :] = jnp.zeros(((n_chains - m) * TM_CHAIN * SUBLANES, LANES), F32)

    @pl.when(pl.program_id(0) >= n_used_ref[0])
    def _():
        y_ref[...] = jnp.zeros(y_ref.shape, F32)


def _combine_body(dest_ref, dest_next_ref, y_ref, base_ref, gate_ref, g_ref, b_ref, outp_ref, outs_ref, ybuf, sems, *,
                  n_prompt_tiles, n_tiles):
    i = pl.program_id(0)

    def issue_tile(d_ref, tile, slot):
        first = lax.rem(tile, T_DISPATCH // T_ROWS) * (T_ROWS * TOP_K)

        def issue(r, carry):
            for k in range(TOP_K):
                d = d_ref[0, first + r * TOP_K + k]
                pltpu.make_async_copy(y_ref.at[_token_rows(d), :], ybuf.at[slot * TOP_K + k, _token_rows(r), :],
                                      sems.at[slot]).start(priority=k % 2)
            return carry

        lax.fori_loop(0, T_ROWS, issue, 0, unroll=8)

    slot = lax.rem(i, 2)

    @pl.when(i == 0)
    def _():
        issue_tile(dest_ref, i, 0)

    @pl.when(i + 1 < n_tiles)
    def _():
        issue_tile(dest_next_ref, i + 1, 1 - slot)

    for k in range(TOP_K):
        pltpu.make_async_copy(y_ref.at[_token_rows(0, T_ROWS), :], ybuf.at[slot * TOP_K + k], sems.at[slot]).wait()

    gate = gate_ref[...]
    acc = base_ref[...]
    for k in range(TOP_K):
        acc = acc + gate[:, k:k + 1] * _load_token_tiles(ybuf, T_ROWS, lead=(slot * TOP_K + k,))
    out = _layer_norm(acc, g_ref[...], b_ref[...])

    @pl.when(i < n_prompt_tiles)
    def _():
        outp_ref[...] = out

    @pl.when(i >= n_prompt_tiles)
    def _():
        outs_ref[...] = out


def _full(shape):
    return pl.BlockSpec(shape, lambda *_: (0,) * len(shape), pipeline_mode=pl.Buffered(1))


def kernel(x_prompt, x_sample, cache_conv, p_prompt, p_sample, ln_in_g, ln_in_b, w_in, conv_w, gn_g, gn_b, vn_g, vn_b,
           w_spatial, b_spatial, w_out, ln1_g, ln1_b, w_router, b_router, w_up, b_up, w_down, b_down, w_ple,
           w_ple_gate, ln2_g, ln2_b):
    batch, seq, _ = x_prompt.shape
    dec_batch, dec_seq, _ = x_sample.shape
    assert w_in.shape[0] == DEPTH and dec_seq == SUBLANES and seq % T_PROMPT == 0
    n_prompt = batch * seq
    n_sample = dec_batch * dec_seq
    n_tok = n_prompt + n_sample
    t_s = SEQS_PER_TILE * dec_seq
    assert n_prompt % T_ROWS == 0 and n_sample % T_ROWS == 0 and n_tok % T_RANK == 0 and n_sample % t_s == 0
    assert n_tok % T_DISPATCH == 0

    row = lambda a: a.reshape(1, -1).astype(F32)
    gidx = jnp.arange(MXU_DIM) // (C_CONV // N_CONV_GROUPS)
    gmat = jnp.where(gidx[:, None] == gidx[None, :], 1.0 / (C_CONV // N_CONV_GROUPS), 0.0).astype(BF16)
    wr_pad = jnp.pad(w_router[0].astype(F32), ((0, 0), (0, LANES - N_EXPERTS)))
    wr_hi = wr_pad.astype(BF16)
    wr_lo = (wr_pad - wr_hi.astype(F32)).astype(BF16)
    weights = dict(
        ln_in_g=row(ln_in_g), ln_in_b=row(ln_in_b), w_in=w_in[0].astype(BF16), vn_g=row(vn_g[0]), vn_b=row(vn_b[0]),
        gmat=gmat, gn_g=row(gn_g[0]), gn_b=row(gn_b[0]), w_out=w_out[0].astype(BF16), ln1_g=row(ln1_g[0]),
        ln1_b=row(ln1_b[0]), wr_hi=wr_hi, wr_pair=jnp.concatenate([wr_hi, wr_lo], axis=1),
        b_r=jnp.pad(row(b_router[0]), ((0, 0), (0, LANES - N_EXPERTS))))
    w_list = [weights[n] for n in _WEIGHT_NAMES]
    w_specs = [_full(a.shape) for a in w_list]
    cw = jnp.pad(conv_w[0].astype(F32), ((0, HIST - CONV_W), (0, 0)))
    causal = jnp.tril(jnp.ones((CHUNK, CHUNK), bool))
    ws_m = jnp.where(causal[None], w_spatial[0], 0.0)
    ws_cat = jnp.concatenate([ws_m[0::2], ws_m[1::2]], axis=2).astype(BF16)
    bs_full = jnp.repeat(b_spatial[0].T.astype(F32), HEAD_DIM, axis=1)
    s_i = jnp.arange(SUBLANES)[:, None]
    t_i = jnp.arange(SUBLANES)[None, :]
    tap = jnp.clip(CONV_W - 1 - t_i + s_i, 0, CONV_W - 1)
    convu = jnp.where((s_i <= t_i)[:, :, None], conv_w[0].astype(F32)[tap], 0.0)
    gw8 = jnp.transpose(ws_m[:, :SUBLANES, :SUBLANES], (2, 1, 0))
    gw8 = jnp.repeat(gw8.astype(F32), HEAD_DIM, axis=2)
    b8 = bs_full[:SUBLANES]
    cpad = jnp.pad(cache_conv[0].astype(F32), ((0, 0), (HIST - (CONV_W - 1), SUBLANES), (0, 0)))

    cparams = lambda sem: pltpu.CompilerParams(dimension_semantics=sem, vmem_limit_bytes=VMEM_LIMIT)

    nj = seq // T_PROMPT
    npt = n_prompt // T_PROMPT
    assert t_s == T_PROMPT
    pstep = lambda i: jnp.minimum(i, npt - 1)
    sstep = lambda i: jnp.maximum(i - npt, 0)
    tok_block = lambda width: pl.BlockSpec((T_PROMPT, width), lambda i: (i, 0))
    cwb = jnp.broadcast_to(conv_w[0].astype(F32).reshape(CONV_W, C_CONV // LANES, 1, LANES),
                           (CONV_W, C_CONV // LANES, SUBLANES, LANES))
    tables = [cw, cwb, ws_cat, bs_full, convu, gw8, b8]
    h_all, idx_all, gate_all, u_tail, v_chunk, u_s, v_s = pl.pallas_call(
        functools.partial(_mixer_body, n_prompt_tiles=npt, tiles_per_seq=nj),
        grid=(npt + n_sample // t_s,),
        in_specs=[pl.BlockSpec((None, T_PROMPT, D_MODEL), lambda i: (pstep(i) // nj, pstep(i) % nj, 0)),
                  pl.BlockSpec((t_s, D_MODEL), lambda i: (sstep(i), 0)),
                  pl.BlockSpec((SEQS_PER_TILE,) + cpad.shape[1:], lambda i: (sstep(i), 0, 0),
                               pipeline_mode=pl.Buffered(1))]
                 + w_specs + [_full(a.shape) for a in tables],
        out_specs=[pl.BlockSpec((T_PROMPT * SUBLANES, LANES), lambda i: (i, 0)),
                   tok_block(LANES), tok_block(LANES),
                   pl.BlockSpec((None, HIST, C_CONV), lambda i: (pstep(i) // nj, 0, 0)),
                   pl.BlockSpec((None, CHUNK, C_GMLP), lambda i: (pstep(i) // nj, 0, 0)),
                   pl.BlockSpec((t_s, C_CONV), lambda i: (sstep(i), 0)),
                   pl.BlockSpec((t_s, C_GMLP), lambda i: (sstep(i), 0))],
        out_shape=(jax.ShapeDtypeStruct((n_tok * SUBLANES, LANES), F32),
                   jax.ShapeDtypeStruct((n_tok, LANES), jnp.int32), jax.ShapeDtypeStruct((n_tok, LANES), F32),
                   jax.ShapeDtypeStruct((batch, HIST, C_CONV), F32), jax.ShapeDtypeStruct((batch, CHUNK, C_GMLP), F32),
                   jax.ShapeDtypeStruct((n_sample, C_CONV), F32), jax.ShapeDtypeStruct((n_sample, C_GMLP), F32)),
        scratch_shapes=[pltpu.VMEM((C_CONV // LANES, T_PROMPT + HIST, LANES), F32),
                        pltpu.VMEM((C_CONV // LANES, T_PROMPT, LANES), F32)]
                       + [pltpu.VMEM((T_PROMPT, C_CONV), F32)] * 4,
        compiler_params=pltpu.CompilerParams(dimension_semantics=("arbitrary",), vmem_limit_bytes=BIG_VMEM_LIMIT),
        name="mixer",
    )(x_prompt, x_sample.reshape(n_sample, D_MODEL), cpad, *w_list, *tables)

    tri = (jnp.arange(T_RANK)[:, None] > jnp.arange(T_RANK)[None, :]).astype(BF16)
    before_all, counts = pl.pallas_call(
        _rank_body,
        grid=(n_tok // T_RANK,),
        in_specs=[pl.BlockSpec((T_RANK, LANES), lambda i: (i, 0)), _full(tri.shape)],
        out_specs=[pl.BlockSpec((T_RANK, LANES), lambda i: (i, 0)), _full((1, LANES))],
        out_shape=(jax.ShapeDtypeStruct((n_tok, LANES), jnp.int32), jax.ShapeDtypeStruct((1, LANES), jnp.int32)),
        scratch_shapes=[pltpu.VMEM((1, LANES), F32)],
        compiler_params=cparams(("arbitrary",)),
        name="rank",
    )(idx_all, tri)

    n_assign = n_tok * TOP_K
    n_blocks = n_assign // TM + N_EXPERTS
    cap = n_blocks * TM
    cnt = counts[0, :N_EXPERTS]
    padded = (cnt + TM - 1) // TM * TM
    p_end = jnp.cumsum(padded)
    p_start = p_end - padded
    e_ids = jnp.arange(N_EXPERTS, dtype=jnp.int32)
    lookup = lambda table, ids: jnp.sum(jnp.where(ids[..., None] == e_ids, table, 0), axis=-1)
    slot_table = p_start[None, :] + before_all[:, :N_EXPERTS]
    dest = jnp.sum(jnp.where(idx_all[:, :TOP_K, None] == e_ids, slot_table[:, None, :], 0), axis=-1).astype(jnp.int32)
    dest_tiles = dest.reshape(n_tok // T_DISPATCH, 1, T_DISPATCH * TOP_K)
    blk_row = jnp.arange(n_blocks, dtype=jnp.int32) * TM
    blk_e = jnp.minimum(jnp.sum(p_end[None, :] <= blk_row[:, None], axis=1), N_EXPERTS - 1).astype(jnp.int32)
    n_used = (p_end[-1:] // TM).astype(jnp.int32)
    pad_off = (p_start + cnt).astype(jnp.int32)
    pad_n = (padded - cnt).astype(jnp.int32)
    used = cnt > 0
    slot_e = (jnp.cumsum(used.astype(jnp.int32)) - 1) & 1
    later_used = jnp.where(used[None, :] & (e_ids[None, :] > e_ids[:, None]), e_ids[None, :], N_EXPERTS)
    next_e = jnp.min(later_used, axis=1)
    next_e = jnp.where(next_e < N_EXPERTS, next_e, -1).astype(jnp.int32)
    blk_first = ((blk_row == lookup(p_start, blk_e)) & (blk_row < p_end[-1])).astype(jnp.int32)
    blk_rows = jnp.clip(lookup(p_start + cnt, blk_e) - blk_row, 0, TM).astype(jnp.int32)
    blk_slot = lookup(slot_e, blk_e).astype(jnp.int32)
    blk_next = lookup(next_e, blk_e).astype(jnp.int32)


    assert seq % T_DISPATCH == 0 and n_sample % T_DISPATCH == 0
    npd = n_prompt // T_DISPATCH
    per_seq = seq // T_DISPATCH
    pd = lambda i: jnp.minimum(i, npd - 1)
    w_gate = w_ple_gate[0].astype(BF16)
    w_ple_b = w_ple[0].astype(BF16)
    x_sorted, base_all = pl.pallas_call(
        functools.partial(_dispatch_body, n_blocks=n_blocks, n_prompt_tiles=npd),
        grid_spec=pltpu.PrefetchScalarGridSpec(
            num_scalar_prefetch=3,
            grid=(n_tok // T_DISPATCH,),
            in_specs=[pl.BlockSpec((None, 1, T_DISPATCH * TOP_K), lambda i, *_: (i, 0, 0), memory_space=pltpu.SMEM),
                      pl.BlockSpec((T_DISPATCH * SUBLANES, LANES), lambda i, *_: (i, 0)),
                      pl.BlockSpec((None, T_DISPATCH, PLE_DIM), lambda i, *_: (pd(i) // per_seq, pd(i) % per_seq, 0)),
                      pl.BlockSpec((T_DISPATCH, PLE_DIM), lambda i, *_: (jnp.maximum(i - npd, 0), 0)),
                      _full(w_gate.shape), _full(w_ple_b.shape)],
            out_specs=[pl.BlockSpec(memory_space=pl.ANY),
                       pl.BlockSpec((T_DISPATCH, D_MODEL), lambda i, *_: (i, 0))],
            scratch_shapes=[pltpu.VMEM((TM // 2 * SUBLANES, LANES), F32), pltpu.SemaphoreType.DMA,
                            pltpu.SemaphoreType.DMA]),
        out_shape=(jax.ShapeDtypeStruct((cap * SUBLANES, LANES), F32), jax.ShapeDtypeStruct((n_tok, D_MODEL), F32)),
        compiler_params=cparams(("arbitrary",)),
        name="dispatch",
    )(pad_off, pad_n, n_used, dest_tiles, h_all, p_prompt[0], p_sample[0].reshape(n_sample, PLE_DIM), w_gate, w_ple_b)

    last = lambda i, nu: jnp.minimum(i, nu[0] - 1)
    y_sorted = pl.pallas_call(
        _expert_body,
        grid_spec=pltpu.PrefetchScalarGridSpec(
            num_scalar_prefetch=6,
            grid=(n_blocks,),
            in_specs=[pl.BlockSpec((TM * SUBLANES, LANES), lambda i, be, bf, bs, bn, br, nu: (last(i, nu), 0)),
                      pl.BlockSpec((None, 1, 2 * D_FF), lambda i, be, bf, bs, bn, br, nu: (be[last(i, nu)], 0, 0)),
                      pl.BlockSpec((None, 1, D_MODEL), lambda i, be, bf, bs, bn, br, nu: (be[last(i, nu)], 0, 0)),
                      pl.BlockSpec(memory_space=pl.ANY), pl.BlockSpec(memory_space=pl.ANY)],
            out_specs=pl.BlockSpec((TM * SUBLANES, LANES), lambda i, be, bf, bs, bn, br, nu: (i, 0)),
            scratch_shapes=[pltpu.VMEM((2, D_MODEL, 2 * D_FF), F32), pltpu.VMEM((2, D_FF, D_MODEL), F32),
                            pltpu.VMEM((D_MODEL, 2 * D_FF), BF16), pltpu.VMEM((D_FF, D_MODEL), BF16),
                            pltpu.SemaphoreType.DMA((2,))]),
        out_shape=jax.ShapeDtypeStruct((cap * SUBLANES, LANES), F32),
        compiler_params=pltpu.CompilerParams(dimension_semantics=("arbitrary",), vmem_limit_bytes=BIG_VMEM_LIMIT),
        name="experts",
    )(blk_e, blk_first, blk_slot, blk_next, blk_rows, n_used, x_sorted, b_up[0].astype(F32)[:, None, :],
      b_down[0].astype(F32)[:, None, :], w_up[0].astype(F32), w_down[0].astype(F32))

    npt = n_prompt // T_ROWS
    n_tiles = n_tok // T_ROWS
    per = T_DISPATCH // T_ROWS
    dest_spec = pl.BlockSpec((None, 1, T_DISPATCH * TOP_K), lambda i: (i // per, 0, 0), memory_space=pltpu.SMEM)
    dest_next_spec = pl.BlockSpec((None, 1, T_DISPATCH * TOP_K),
                                  lambda i: (jnp.minimum(i + 1, n_tiles - 1) // per, 0, 0), memory_space=pltpu.SMEM)
    out_p, out_s = pl.pallas_call(
        functools.partial(_combine_body, n_prompt_tiles=npt, n_tiles=n_tiles),
        grid=(n_tiles,),
        in_specs=[dest_spec, dest_next_spec, pl.BlockSpec(memory_space=pl.ANY),
                  pl.BlockSpec((T_ROWS, D_MODEL), lambda i: (i, 0)), pl.BlockSpec((T_ROWS, LANES), lambda i: (i, 0)),
                  _full((1, D_MODEL)), _full((1, D_MODEL))],
        out_specs=[pl.BlockSpec((T_ROWS, D_MODEL), lambda i: (jnp.minimum(i, npt - 1), 0)),
                   pl.BlockSpec((T_ROWS, D_MODEL), lambda i: (jnp.maximum(i - npt, 0), 0))],
        out_shape=(jax.ShapeDtypeStruct((n_prompt, D_MODEL), F32), jax.ShapeDtypeStruct((n_sample, D_MODEL), F32)),
        scratch_shapes=[pltpu.VMEM((2 * TOP_K, T_ROWS * SUBLANES, LANES), F32), pltpu.SemaphoreType.DMA((2,))],
        compiler_params=cparams(("arbitrary",)),
        name="combine",
    )(dest_tiles, dest_tiles, y_sorted, base_all, gate_all, row(ln2_g[0]), row(ln2_b[0]))

    y_prompt = out_p.reshape(batch, seq, D_MODEL)
    y_sample = out_s.reshape(dec_batch, dec_seq, D_MODEL)
    conv_state_prompt = u_tail[None, :, HIST - (CONV_W - 1):, :]
    u_s3 = u_s.reshape(dec_batch, dec_seq, C_CONV)
    conv_state_sample = jnp.concatenate([cache_conv[0][:, dec_seq:, :].astype(F32), u_s3], axis=1)[None]
    chunk_v_prompt = v_chunk[None]
    chunk_v_sample = v_s.reshape(1, dec_batch, dec_seq, C_GMLP)
    return (y_prompt, y_sample, conv_state_prompt, conv_state_sample, chunk_v_prompt, chunk_v_sample)
```

```python
import functools

import jax
import jax.numpy as jnp
from jax import lax
from jax.experimental import pallas as pl
from jax.experimental.pallas import tpu as pltpu

F32 = jnp.float32
BF16 = jnp.bfloat16

D_MODEL = 1024
C_CONV = 512
C_GMLP = 512
N_CONV_GROUPS = 8
CONV_W = 31
N_HEADS = 8
HEAD_DIM = C_GMLP // N_HEADS
CHUNK = 128
N_EXPERTS = 32
TOP_K = 4
D_FF = 1024
PLE_DIM = 256
SWIGLU_LIMIT = 7.0
SWIGLU_ALPHA = 1.702
LN_EPS = 1e-5
DEPTH = 1
DEEPNORM_ALPHA = (2.0 * DEPTH) ** 0.25

LANES = 128
SUBLANES = 8
MXU_DIM = 256
VMEM_LIMIT = 48 * 1024 * 1024
BIG_VMEM_LIMIT = 56 * 1024 * 1024

T_PROMPT = 512
PROMPT_SPLIT = 2
HIST = 32
CONV_STRIDE = 4
SEQS_PER_TILE = 64
SEQ_CHUNK = 4
T_RANK = 1024
T_DISPATCH = 1024
T_ROWS = 256
TM = 512
TM_CHAIN = 256


def _dot(a, b):
    return jnp.dot(a, b, preferred_element_type=F32)


def _layer_norm(x, g, b):
    mu = jnp.mean(x, axis=-1, keepdims=True)
    xc = x - mu
    var = jnp.mean(xc * xc, axis=-1, keepdims=True)
    return xc * lax.rsqrt(var + LN_EPS) * g + b


def _split_bf16(a):
    hi = a.astype(BF16)
    lo = (a - hi.astype(F32)).astype(BF16)
    return hi, lo


def _group_mean(a, gmat_ref):
    hi, lo = _split_bf16(a)
    g = gmat_ref[...]
    outs = []
    for s in range(C_CONV // MXU_DIM):
        sl = slice(MXU_DIM * s, MXU_DIM * (s + 1))
        outs.append(_dot(hi[:, sl], g) + _dot(lo[:, sl], g))
    return jnp.concatenate(outs, axis=1)


def _group_norm_silu(y, gmat_ref, gn_g, gn_b):
    mu = _group_mean(y, gmat_ref)
    yc = y - mu
    var = _group_mean(yc * yc, gmat_ref)
    yn = yc * lax.rsqrt(var + LN_EPS) * gn_g + gn_b
    return yn * jax.nn.sigmoid(yn)


def _store_token_tiles(ref, val, tok0=0):
    n = val.shape[0]
    for c in range(D_MODEL // LANES):
        ref[pl.ds(tok0 * SUBLANES + c, n, stride=SUBLANES), :] = val[:, c * LANES:(c + 1) * LANES]


def _load_token_tiles(ref, n, lead=(), tok0=0):
    parts = [ref[lead + (pl.ds(tok0 * SUBLANES + c, n, stride=SUBLANES), slice(None))]
             for c in range(D_MODEL // LANES)]
    return jnp.concatenate(parts, axis=1)


def _front(x, w):
    xn = _layer_norm(x, w["ln_in_g"][...], w["ln_in_b"][...])
    z = _dot(xn.astype(BF16), w["w_in"][...])
    a_val = z[:, 0:C_CONV]
    a_gate = z[:, C_CONV:2 * C_CONV]
    g_u = z[:, 2 * C_CONV:2 * C_CONV + C_GMLP]
    g_v = z[:, 2 * C_CONV + C_GMLP:]
    u = a_val * jax.nn.sigmoid(a_gate)
    ug = jax.nn.gelu(g_u)
    v = _layer_norm(jax.nn.gelu(g_v), w["vn_g"][...], w["vn_b"][...])
    return xn, u, ug, v


def _tail(xn, y_a, y_b, w, h_ref, idx_ref, gate_ref, row0=0):
    rows = slice(row0, row0 + xn.shape[0])
    mix = _dot(y_a.astype(BF16), w["w_out"][0:C_CONV, :]) + _dot(y_b.astype(BF16), w["w_out"][C_CONV:, :])
    h = _layer_norm(DEEPNORM_ALPHA * xn + mix, w["ln1_g"][...], w["ln1_b"][...])
    hb, h_lo = _split_bf16(h)
    _store_token_tiles(h_ref, h, row0)

    pair = _dot(hb, w["wr_pair"][...])
    logits = pair[:, :LANES] + pair[:, LANES:] + _dot(h_lo, w["wr_hi"][...]) + w["b_r"][...]
    lane = lax.broadcasted_iota(jnp.int32, logits.shape, 1)
    lane_f = lane.astype(F32)
    vals = jnp.where(lane < N_EXPERTS, logits, -jnp.inf)
    tops, ids = [], []
    for _ in range(TOP_K):
        m = jnp.max(vals, axis=-1, keepdims=True)
        i = jnp.min(jnp.where(vals == m, lane_f, float(LANES)), axis=-1, keepdims=True)
        vals = jnp.where(lane_f == i, -jnp.inf, vals)
        tops.append(m)
        ids.append(i)
    exps = [jnp.exp(m - tops[0]) for m in tops]
    denom = exps[0] + exps[1] + exps[2] + exps[3]
    idx_out = jnp.zeros(logits.shape, F32)
    gate_out = jnp.zeros(logits.shape, F32)
    for k in range(TOP_K):
        idx_out = jnp.where(lane == k, ids[k], idx_out)
        gate_out = jnp.where(lane == k, exps[k] / denom, gate_out)
    idx_ref[rows, :] = idx_out.astype(jnp.int32)
    gate_ref[rows, :] = gate_out


_WEIGHT_NAMES = ("ln_in_g", "ln_in_b", "w_in", "vn_g", "vn_b", "gmat", "gn_g", "gn_b", "w_out", "ln1_g", "ln1_b",
                 "wr_hi", "wr_pair", "b_r")


def _prompt_branch(j, x_ref, w, cwb_ref, ws_ref, bs_ref, outs, tail_ref, vch_ref, ubuf, yslab, last_j):
    t = T_PROMPT
    n = t // PROMPT_SPLIT
    n_slabs = C_CONV // LANES

    @pl.when(j == 0)
    def _():
        ubuf[:, 0:HIST, :] = jnp.zeros((n_slabs, HIST, LANES), F32)

    @pl.when(j > 0)
    def _():
        ubuf[:, 0:HIST, :] = ubuf[:, t:t + HIST, :]

    def front(h):
        xn, u, ug, v = _front(x_ref[h * n:(h + 1) * n, :], w)
        for s in range(n_slabs):
            ubuf[s, HIST + h * n:HIST + (h + 1) * n, :] = u[:, s * LANES:(s + 1) * LANES]
        return xn, u, ug, v

    def conv(h):
        rows = CONV_STRIDE * SUBLANES
        first = HIST - (CONV_W - 1)
        for s in range(n_slabs):
            for c in range(h * n // rows, (h + 1) * n // rows):
                accs = [None] * CONV_STRIDE
                for shift in range(CONV_STRIDE + CONV_W - 1):
                    win = ubuf[s, pl.ds(first + c * rows + shift, SUBLANES, stride=CONV_STRIDE), :]
                    for ph in range(CONV_STRIDE):
                        k = shift - ph
                        if 0 <= k < CONV_W:
                            term = cwb_ref[k, s] * win
                            accs[ph] = term if accs[ph] is None else accs[ph] + term
                for ph in range(CONV_STRIDE):
                    yslab[s, pl.ds(c * rows + ph, SUBLANES, stride=CONV_STRIDE), :] = accs[ph]
        y_conv = jnp.concatenate([yslab[s, h * n:(h + 1) * n, :] for s in range(n_slabs)], axis=1)
        return _group_norm_silu(y_conv, w["gmat"], w["gn_g"][...], w["gn_b"][...])

    def spatial_gate(ug, v):
        lane = lax.broadcasted_iota(jnp.int32, (CHUNK, LANES), 1)
        mixed_chunks = []
        for c in range(n // CHUNK):
            vc = v[c * CHUNK:(c + 1) * CHUNK, :]
            parts = []
            for q in range(N_HEADS // 2):
                vp = vc[:, q * LANES:(q + 1) * LANES]
                rhs = jnp.concatenate([jnp.where(lane < HEAD_DIM, vp, 0.0), jnp.where(lane >= HEAD_DIM, vp, 0.0)],
                                      axis=0).astype(BF16)
                parts.append(_dot(ws_ref[q], rhs))
            mixed_chunks.append(jnp.concatenate(parts, axis=1) + bs_ref[...])
        return ug * jnp.concatenate(mixed_chunks, axis=0)

    def tail(h, f, y_a, y_b):
        _tail(f[0], y_a, y_b, w, *outs, row0=h * n)

    fronts = [front(0)]
    y_as = {}
    for h in range(PROMPT_SPLIT):
        if h + 1 < PROMPT_SPLIT:
            fronts.append(front(h + 1))
        y_as[h] = conv(h)
        if h > 0:
            tail(h - 1, fronts[h - 1], y_as.pop(h - 1), spatial_gate(fronts[h - 1][2], fronts[h - 1][3]))
    last = PROMPT_SPLIT - 1
    tail(last, fronts[last], y_as.pop(last), spatial_gate(fronts[last][2], fronts[last][3]))

    @pl.when(j == last_j)
    def _():
        tail_ref[...] = fronts[last][1][n - HIST:, :]
        vch_ref[...] = fronts[last][3][n - CHUNK:, :]


def _sample_branch(x_ref, cpad_ref, w, cw_ref, convu_ref, gw_ref, b8_ref, outs, u_out_ref, v_out_ref,
                   uscr, vscr, yscr, mscr):
    xn, u, ug, v = _front(x_ref[...], w)
    u_out_ref[...] = u
    v_out_ref[...] = v
    uscr[...] = u
    vscr[...] = v
    first = HIST - (CONV_W - 1)
    rows = SEQ_CHUNK * SUBLANES

    def step(c, carry):
        r0 = pl.multiple_of(c * rows, rows)
        s0 = c * SEQ_CHUNK
        u3 = uscr[pl.ds(r0, rows), :].reshape(SEQ_CHUNK, SUBLANES, C_CONV)
        v3 = vscr[pl.ds(r0, rows), :].reshape(SEQ_CHUNK, SUBLANES, C_GMLP)
        acc = jnp.zeros((SEQ_CHUNK, SUBLANES, C_CONV), F32)
        for k in range(CONV_W):
            acc = acc + cw_ref[pl.ds(k, 1), :][None] * cpad_ref[pl.ds(s0, SEQ_CHUNK), pl.ds(first + k, SUBLANES), :]
        mix = jnp.zeros((SEQ_CHUNK, SUBLANES, C_GMLP), F32) + b8_ref[...][None]
        for s in range(SUBLANES):
            acc = acc + u3[:, s:s + 1, :] * convu_ref[s][None]
            mix = mix + v3[:, s:s + 1, :] * gw_ref[s][None]
        yscr[pl.ds(r0, rows), :] = acc.reshape(rows, C_CONV)
        mscr[pl.ds(r0, rows), :] = mix.reshape(rows, C_GMLP)
        return carry

    lax.fori_loop(0, SEQS_PER_TILE // SEQ_CHUNK, step, 0)
    y_a = _group_norm_silu(yscr[...], w["gmat"], w["gn_g"][...], w["gn_b"][...])
    y_b = ug * mscr[...]
    _tail(xn, y_a, y_b, w, *outs)


def _mixer_body(*refs, n_prompt_tiles, tiles_per_seq):
    n_w = len(_WEIGHT_NAMES)
    xp_ref, xs_ref, cpad_ref = refs[:3]
    w = dict(zip(_WEIGHT_NAMES, refs[3:3 + n_w]))
    cw_ref, cwb_ref, ws_ref, bs_ref, convu_ref, gw_ref, b8_ref = refs[3 + n_w:10 + n_w]
    outs = refs[10 + n_w:13 + n_w]
    tail_ref, vch_ref, u_out_ref, v_out_ref = refs[13 + n_w:17 + n_w]
    ubuf, yslab, yscr, uscr, vscr, mscr = refs[17 + n_w:]
    step = pl.program_id(0)

    @pl.when(step < n_prompt_tiles)
    def _():
        _prompt_branch(lax.rem(step, tiles_per_seq), xp_ref, w, cwb_ref, ws_ref, bs_ref, outs, tail_ref,
                       vch_ref, ubuf, yslab, tiles_per_seq - 1)

    @pl.when(step >= n_prompt_tiles)
    def _():
        _sample_branch(xs_ref, cpad_ref, w, cw_ref, convu_ref, gw_ref, b8_ref, outs, u_out_ref, v_out_ref,
                       uscr, vscr, yscr, mscr)


def _rank_body(idx_ref, tri_ref, before_ref, counts_ref, carry):
    i = pl.program_id(0)

    @pl.when(i == 0)
    def _():
        carry[...] = jnp.zeros(carry.shape, F32)

    idx = idx_ref[...]
    lane = lax.broadcasted_iota(jnp.int32, idx.shape, 1)
    multi = jnp.zeros(idx.shape, F32)
    for k in range(TOP_K):
        multi = multi + (lane == idx[:, k:k + 1]).astype(F32)
    before_ref[...] = (_dot(tri_ref[...], multi.astype(BF16)) + carry[...]).astype(jnp.int32)
    carry[...] = carry[...] + jnp.sum(multi, axis=0, keepdims=True)
    counts_ref[...] = carry[...].astype(jnp.int32)


def _pad_bits():
    b = TM // 2
    while b >= 1:
        yield b
        b //= 2


def _token_rows(t, n=1):
    return pl.ds(pl.multiple_of(t * SUBLANES, SUBLANES), n * SUBLANES)


def _dispatch_body(pad_off_ref, pad_n_ref, n_used_ref, dest_ref, h_ref, pp_ref, ps_ref, wg_ref, wp_ref, xs_ref, base_ref,
                   zbuf, sem, zsem, *, n_blocks, n_prompt_tiles):
    i = pl.program_id(0)

    def issue(r, carry):
        src = h_ref.at[_token_rows(r), :]
        for k in range(TOP_K):
            d = dest_ref[0, r * TOP_K + k]
            pltpu.make_async_copy(src, xs_ref.at[_token_rows(d), :], sem).start(priority=k % 2)
        return carry

    @pl.when(i == 0)
    def _():
        zbuf[...] = jnp.zeros(zbuf.shape, F32)

        def zero_copy(off, b):
            return pltpu.make_async_copy(zbuf.at[pl.ds(0, b * SUBLANES), :], xs_ref.at[_token_rows(off, b), :], zsem)

        def start_or_wait(cond, cp, wait):
            @pl.when(cond)
            def _():
                if wait:
                    cp.wait()
                else:
                    cp.start()

        for wait in (False, True):
            for e in range(N_EXPERTS):
                n = pad_n_ref[e]
                for b in _pad_bits():
                    start_or_wait((n & b) != 0, zero_copy(pad_off_ref[e] + (n & ~(2 * b - 1)), b), wait)

        half = TM // 2

        def tail_block(wait):
            def go(blk, carry):
                for s in range(TM // half):
                    cp = zero_copy(blk * TM + s * half, half)
                    cp.wait() if wait else cp.start()
                return carry
            return go

        lax.fori_loop(n_used_ref[0], n_blocks, tail_block(False), 0)
        lax.fori_loop(n_used_ref[0], n_blocks, tail_block(True), 0)

    rows = TM_CHAIN
    n_ranges = T_DISPATCH // rows
    is_prompt = i < n_prompt_tiles
    lax.fori_loop(0, rows, issue, 0, unroll=8)
    for c in range(n_ranges):
        if c + 1 < n_ranges:
            lax.fori_loop((c + 1) * rows, (c + 2) * rows, issue, 0, unroll=8)
        sl = slice(c * rows, (c + 1) * rows)
        h = _load_token_tiles(h_ref, rows, tok0=c * rows)
        p = jnp.where(is_prompt, pp_ref[sl, :], ps_ref[sl, :])
        ple = _dot(p.astype(BF16), wp_ref[...]) * jax.nn.sigmoid(_dot(h.astype(BF16), wg_ref[...]))
        base_ref[sl, :] = DEEPNORM_ALPHA * h + ple

    for _ in range(TOP_K):
        pltpu.make_async_copy(h_ref, xs_ref.at[_token_rows(0, T_DISPATCH), :], sem).wait()


def _expert_body(blk_e_ref, first_ref, slot_ref, next_ref, rows_ref, n_used_ref, x_ref, bu_ref, bd_ref, wu_hbm, wd_hbm,
                 y_ref, wu_f32, wd_f32, wu_bf, wd_bf, sems):
    i = pl.program_id(0)

    def weight_copies(e, s):
        return (pltpu.make_async_copy(wu_hbm.at[e], wu_f32.at[s], sems.at[s]),
                pltpu.make_async_copy(wd_hbm.at[e], wd_f32.at[s], sems.at[s]))

    @pl.when(i < n_used_ref[0])
    def _():
        s = slot_ref[i]

        @pl.when(first_ref[i] == 1)
        def _():
            @pl.when(i == 0)
            def _():
                for cp in weight_copies(blk_e_ref[i], s):
                    cp.start()

            for cp in weight_copies(blk_e_ref[i], s):
                cp.wait()
            nxt = next_ref[i]

            @pl.when(nxt >= 0)
            def _():
                for cp in weight_copies(nxt, 1 - s):
                    cp.start()

            chunk = D_MODEL // SUBLANES

            def cast_rows(c, carry):
                r = pl.multiple_of(c * chunk, chunk)
                wu_bf[pl.ds(r, chunk), :] = wu_f32[s, pl.ds(r, chunk), :].astype(BF16)
                wd_bf[pl.ds(r, chunk), :] = wd_f32[s, pl.ds(r, chunk), :].astype(BF16)
                return carry

            lax.fori_loop(0, SUBLANES, cast_rows, 0)

        def chain(c):
            x = _load_token_tiles(x_ref, TM_CHAIN, tok0=c * TM_CHAIN)
            hcat = _dot(x.astype(BF16), wu_bf[...]) + bu_ref[...]
            h_glu = jnp.minimum(hcat[:, :D_FF], SWIGLU_LIMIT)
            h_lin = jnp.clip(hcat[:, D_FF:], -SWIGLU_LIMIT, SWIGLU_LIMIT)
            act = h_glu * jax.nn.sigmoid(SWIGLU_ALPHA * h_glu) * (h_lin + 1.0)
            _store_token_tiles(y_ref, _dot(act.astype(BF16), wd_bf[...]) + bd_ref[...], tok0=c * TM_CHAIN)

        n_chains = TM // TM_CHAIN
        live = (rows_ref[i] + TM_CHAIN - 1) // TM_CHAIN
        for m in range(1, n_chains + 1):
            @pl.when(live == m)
            def _():
                for c in range(m):
                    chain(c)
                if m < n_chains:
                    y_ref[m * TM_CHAIN * SUBLANES:, :] = jnp.zeros(((n_chains - m) * TM_CHAIN * SUBLANES, LANES), F32)

    @pl.when(pl.program_id(0) >= n_used_ref[0])
    def _():
        y_ref[...] = jnp.zeros(y_ref.shape, F32)


def _combine_body(dest_ref, dest_next_ref, y_ref, base_ref, gate_ref, g_ref, b_ref, outp_ref, outs_ref, ybuf, sems, *,
                  n_prompt_tiles, n_tiles):
    i = pl.program_id(0)

    def issue_tile(d_ref, tile, slot):
        first = lax.rem(tile, T_DISPATCH // T_ROWS) * (T_ROWS * TOP_K)

        def issue(r, carry):
            for k in range(TOP_K):
                d = d_ref[0, first + r * TOP_K + k]
                pltpu.make_async_copy(y_ref.at[_token_rows(d), :], ybuf.at[slot * TOP_K + k, _token_rows(r), :],
                                      sems.at[slot]).start(priority=k % 2)
            return carry

        lax.fori_loop(0, T_ROWS, issue, 0, unroll=8)

    slot = lax.rem(i, 2)

    @pl.when(i == 0)
    def _():
        issue_tile(dest_ref, i, 0)

    @pl.when(i + 1 < n_tiles)
    def _():
        issue_tile(dest_next_ref, i + 1, 1 - slot)

    for k in range(TOP_K):
        pltpu.make_async_copy(y_ref.at[_token_rows(0, T_ROWS), :], ybuf.at[slot * TOP_K + k], sems.at[slot]).wait()

    gate = gate_ref[...]
    acc = base_ref[...]
    for k in range(TOP_K):
        acc = acc + gate[:, k:k + 1] * _load_token_tiles(ybuf, T_ROWS, lead=(slot * TOP_K + k,))
    out = _layer_norm(acc, g_ref[...], b_ref[...])

    @pl.when(i < n_prompt_tiles)
    def _():
        outp_ref[...] = out

    @pl.when(i >= n_prompt_tiles)
    def _():
        outs_ref[...] = out


def _full(shape):
    return pl.BlockSpec(shape, lambda *_: (0,) * len(shape), pipeline_mode=pl.Buffered(1))


def kernel(x_prompt, x_sample, cache_conv, p_prompt, p_sample, ln_in_g, ln_in_b, w_in, conv_w, gn_g, gn_b, vn_g, vn_b,
           w_spatial, b_spatial, w_out, ln1_g, ln1_b, w_router, b_router, w_up, b_up, w_down, b_down, w_ple,
           w_ple_gate, ln2_g, ln2_b):
    batch, seq, _ = x_prompt.shape
    dec_batch, dec_seq, _ = x_sample.shape
    assert w_in.shape[0] == DEPTH and dec_seq == SUBLANES and seq % T_PROMPT == 0
    n_prompt = batch * seq
    n_sample = dec_batch * dec_seq
    n_tok = n_prompt + n_sample
    t_s = SEQS_PER_TILE * dec_seq
    assert n_prompt % T_ROWS == 0 and n_sample % T_ROWS == 0 and n_tok % T_RANK == 0 and n_sample % t_s == 0
    assert n_tok % T_DISPATCH == 0

    row = lambda a: a.reshape(1, -1).astype(F32)
    gidx = jnp.arange(MXU_DIM) // (C_CONV // N_CONV_GROUPS)
    gmat = jnp.where(gidx[:, None] == gidx[None, :], 1.0 / (C_CONV // N_CONV_GROUPS), 0.0).astype(BF16)
    wr_pad = jnp.pad(w_router[0].astype(F32), ((0, 0), (0, LANES - N_EXPERTS)))
    wr_hi = wr_pad.astype(BF16)
    wr_lo = (wr_pad - wr_hi.astype(F32)).astype(BF16)
    weights = dict(
        ln_in_g=row(ln_in_g), ln_in_b=row(ln_in_b), w_in=w_in[0].astype(BF16), vn_g=row(vn_g[0]), vn_b=row(vn_b[0]),
        gmat=gmat, gn_g=row(gn_g[0]), gn_b=row(gn_b[0]), w_out=w_out[0].astype(BF16), ln1_g=row(ln1_g[0]),
        ln1_b=row(ln1_b[0]), wr_hi=wr_hi, wr_pair=jnp.concatenate([wr_hi, wr_lo], axis=1),
        b_r=jnp.pad(row(b_router[0]), ((0, 0), (0, LANES - N_EXPERTS))))
    w_list = [weights[n] for n in _WEIGHT_NAMES]
    w_specs = [_full(a.shape) for a in w_list]
    cw = jnp.pad(conv_w[0].astype(F32), ((0, HIST - CONV_W), (0, 0)))
    causal = jnp.tril(jnp.ones((CHUNK, CHUNK), bool))
    ws_m = jnp.where(causal[None], w_spatial[0], 0.0)
    ws_cat = jnp.concatenate([ws_m[0::2], ws_m[1::2]], axis=2).astype(BF16)
    bs_full = jnp.repeat(b_spatial[0].T.astype(F32), HEAD_DIM, axis=1)
    s_i = jnp.arange(SUBLANES)[:, None]
    t_i = jnp.arange(SUBLANES)[None, :]
    tap = jnp.clip(CONV_W - 1 - t_i + s_i, 0, CONV_W - 1)
    convu = jnp.where((s_i <= t_i)[:, :, None], conv_w[0].astype(F32)[tap], 0.0)
    gw8 = jnp.transpose(ws_m[:, :SUBLANES, :SUBLANES], (2, 1, 0))
    gw8 = jnp.repeat(gw8.astype(F32), HEAD_DIM, axis=2)
    b8 = bs_full[:SUBLANES]
    cpad = jnp.pad(cache_conv[0].astype(F32), ((0, 0), (HIST - (CONV_W - 1), SUBLANES), (0, 0)))

    cparams = lambda sem: pltpu.CompilerParams(dimension_semantics=sem, vmem_limit_bytes=VMEM_LIMIT)

    nj = seq // T_PROMPT
    npt = n_prompt // T_PROMPT
    assert t_s == T_PROMPT
    pstep = lambda i: jnp.minimum(i, npt - 1)
    sstep = lambda i: jnp.maximum(i - npt, 0)
    tok_block = lambda width: pl.BlockSpec((T_PROMPT, width), lambda i: (i, 0))
    cwb = jnp.broadcast_to(conv_w[0].astype(F32).reshape(CONV_W, C_CONV // LANES, 1, LANES),
                           (CONV_W, C_CONV // LANES, SUBLANES, LANES))
    tables = [cw, cwb, ws_cat, bs_full, convu, gw8, b8]
    h_all, idx_all, gate_all, u_tail, v_chunk, u_s, v_s = pl.pallas_call(
        functools.partial(_mixer_body, n_prompt_tiles=npt, tiles_per_seq=nj),
        grid=(npt + n_sample // t_s,),
        in_specs=[pl.BlockSpec((None, T_PROMPT, D_MODEL), lambda i: (pstep(i) // nj, pstep(i) % nj, 0)),
                  pl.BlockSpec((t_s, D_MODEL), lambda i: (sstep(i), 0)),
                  pl.BlockSpec((SEQS_PER_TILE,) + cpad.shape[1:], lambda i: (sstep(i), 0, 0),
                               pipeline_mode=pl.Buffered(1))]
                 + w_specs + [_full(a.shape) for a in tables],
        out_specs=[pl.BlockSpec((T_PROMPT * SUBLANES, LANES), lambda i: (i, 0)),
                   tok_block(LANES), tok_block(LANES),
                   pl.BlockSpec((None, HIST, C_CONV), lambda i: (pstep(i) // nj, 0, 0)),
                   pl.BlockSpec((None, CHUNK, C_GMLP), lambda i: (pstep(i) // nj, 0, 0)),
                   pl.BlockSpec((t_s, C_CONV), lambda i: (sstep(i), 0)),
                   pl.BlockSpec((t_s, C_GMLP), lambda i: (sstep(i), 0))],
        out_shape=(jax.ShapeDtypeStruct((n_tok * SUBLANES, LANES), F32),
                   jax.ShapeDtypeStruct((n_tok, LANES), jnp.int32), jax.ShapeDtypeStruct((n_tok, LANES), F32),
                   jax.ShapeDtypeStruct((batch, HIST, C_CONV), F32), jax.ShapeDtypeStruct((batch, CHUNK, C_GMLP), F32),
                   jax.ShapeDtypeStruct((n_sample, C_CONV), F32), jax.ShapeDtypeStruct((n_sample, C_GMLP), F32)),
        scratch_shapes=[pltpu.VMEM((C_CONV // LANES, T_PROMPT + HIST, LANES), F32),
                        pltpu.VMEM((C_CONV // LANES, T_PROMPT, LANES), F32)]
                       + [pltpu.VMEM((T_PROMPT, C_CONV), F32)] * 4,
        compiler_params=pltpu.CompilerParams(dimension_semantics=("arbitrary",), vmem_limit_bytes=BIG_VMEM_LIMIT),
        name="mixer",
    )(x_prompt, x_sample.reshape(n_sample, D_MODEL), cpad, *w_list, *tables)

    tri = (jnp.arange(T_RANK)[:, None] > jnp.arange(T_RANK)[None, :]).astype(BF16)
    before_all, counts = pl.pallas_call(
        _rank_body,
        grid=(n_tok // T_RANK,),
        in_specs=[pl.BlockSpec((T_RANK, LANES), lambda i: (i, 0)), _full(tri.shape)],
        out_specs=[pl.BlockSpec((T_RANK, LANES), lambda i: (i, 0)), _full((1, LANES))],
        out_shape=(jax.ShapeDtypeStruct((n_tok, LANES), jnp.int32), jax.ShapeDtypeStruct((1, LANES), jnp.int32)),
        scratch_shapes=[pltpu.VMEM((1, LANES), F32)],
        compiler_params=cparams(("arbitrary",)),
        name="rank",
    )(idx_all, tri)

    n_assign = n_tok * TOP_K
    n_blocks = n_assign // TM + N_EXPERTS
    cap = n_blocks * TM
    cnt = counts[0, :N_EXPERTS]
    padded = (cnt + TM - 1) // TM * TM
    p_end = jnp.cumsum(padded)
    p_start = p_end - padded
    e_ids = jnp.arange(N_EXPERTS, dtype=jnp.int32)
    lookup = lambda table, ids: jnp.sum(jnp.where(ids[..., None] == e_ids, table, 0), axis=-1)
    slot_table = p_start[None, :] + before_all[:, :N_EXPERTS]
    dest = jnp.sum(jnp.where(idx_all[:, :TOP_K, None] == e_ids, slot_table[:, None, :], 0), axis=-1).astype(jnp.int32)
    dest_tiles = dest.reshape(n_tok // T_DISPATCH, 1, T_DISPATCH * TOP_K)
    blk_row = jnp.arange(n_blocks, dtype=jnp.int32) * TM
    blk_e = jnp.minimum(jnp.sum(p_end[None, :] <= blk_row[:, None], axis=1), N_EXPERTS - 1).astype(jnp.int32)
    n_used = (p_end[-1:] // TM).astype(jnp.int32)
    pad_off = (p_start + cnt).astype(jnp.int32)
    pad_n = (padded - cnt).astype(jnp.int32)
    used = cnt > 0
    slot_e = (jnp.cumsum(used.astype(jnp.int32)) - 1) & 1
    later_used = jnp.where(used[None, :] & (e_ids[None, :] > e_ids[:, None]), e_ids[None, :], N_EXPERTS)
    next_e = jnp.min(later_used, axis=1)
    next_e = jnp.where(next_e < N_EXPERTS, next_e, -1).astype(jnp.int32)
    blk_first = ((blk_row == lookup(p_start, blk_e)) & (blk_row < p_end[-1])).astype(jnp.int32)
    blk_rows = jnp.clip(lookup(p_start + cnt, blk_e) - blk_row, 0, TM).astype(jnp.int32)
    blk_slot = lookup(slot_e, blk_e).astype(jnp.int32)
    blk_next = lookup(next_e, blk_e).astype(jnp.int32)


    assert seq % T_DISPATCH == 0 and n_sample % T_DISPATCH == 0
    npd = n_prompt // T_DISPATCH
    per_seq = seq // T_DISPATCH
    pd = lambda i: jnp.minimum(i, npd - 1)
    w_gate = w_ple_gate[0].astype(BF16)
    w_ple_b = w_ple[0].astype(BF16)
    x_sorted, base_all = pl.pallas_call(
        functools.partial(_dispatch_body, n_blocks=n_blocks, n_prompt_tiles=npd),
        grid_spec=pltpu.PrefetchScalarGridSpec(
            num_scalar_prefetch=3,
            grid=(n_tok // T_DISPATCH,),
            in_specs=[pl.BlockSpec((None, 1, T_DISPATCH * TOP_K), lambda i, *_: (i, 0, 0), memory_space=pltpu.SMEM),
                      pl.BlockSpec((T_DISPATCH * SUBLANES, LANES), lambda i, *_: (i, 0)),
                      pl.BlockSpec((None, T_DISPATCH, PLE_DIM), lambda i, *_: (pd(i) // per_seq, pd(i) % per_seq, 0)),
                      pl.BlockSpec((T_DISPATCH, PLE_DIM), lambda i, *_: (jnp.maximum(i - npd, 0), 0)),
                      _full(w_gate.shape), _full(w_ple_b.shape)],
            out_specs=[pl.BlockSpec(memory_space=pl.ANY),
                       pl.BlockSpec((T_DISPATCH, D_MODEL), lambda i, *_: (i, 0))],
            scratch_shapes=[pltpu.VMEM((TM // 2 * SUBLANES, LANES), F32), pltpu.SemaphoreType.DMA,
                            pltpu.SemaphoreType.DMA]),
        out_shape=(jax.ShapeDtypeStruct((cap * SUBLANES, LANES), F32), jax.ShapeDtypeStruct((n_tok, D_MODEL), F32)),
        compiler_params=cparams(("arbitrary",)),
        name="dispatch",
    )(pad_off, pad_n, n_used, dest_tiles, h_all, p_prompt[0], p_sample[0].reshape(n_sample, PLE_DIM), w_gate, w_ple_b)

    last = lambda i, nu: jnp.minimum(i, nu[0] - 1)
    y_sorted = pl.pallas_call(
        _expert_body,
        grid_spec=pltpu.PrefetchScalarGridSpec(
            num_scalar_prefetch=6,
            grid=(n_blocks,),
            in_specs=[pl.BlockSpec((TM * SUBLANES, LANES), lambda i, be, bf, bs, bn, br, nu: (last(i, nu), 0)),
                      pl.BlockSpec((None, 1, 2 * D_FF), lambda i, be, bf, bs, bn, br, nu: (be[last(i, nu)], 0, 0)),
                      pl.BlockSpec((None, 1, D_MODEL), lambda i, be, bf, bs, bn, br, nu: (be[last(i, nu)], 0, 0)),
                      pl.BlockSpec(memory_space=pl.ANY), pl.BlockSpec(memory_space=pl.ANY)],
            out_specs=pl.BlockSpec((TM * SUBLANES, LANES), lambda i, be, bf, bs, bn, br, nu: (i, 0)),
            scratch_shapes=[pltpu.VMEM((2, D_MODEL, 2 * D_FF), F32), pltpu.VMEM((2, D_FF, D_MODEL), F32),
                            pltpu.VMEM((D_MODEL, 2 * D_FF), BF16), pltpu.VMEM((D_FF, D_MODEL), BF16),
                            pltpu.SemaphoreType.DMA((2,))]),
        out_shape=jax.ShapeDtypeStruct((cap * SUBLANES, LANES), F32),
        compiler_params=pltpu.CompilerParams(dimension_semantics=("arbitrary",), vmem_limit_bytes=BIG_VMEM_LIMIT),
        name="experts",
    )(blk_e, blk_first, blk_slot, blk_next, blk_rows, n_used, x_sorted, b_up[0].astype(F32)[:, None, :],
      b_down[0].astype(F32)[:, None, :], w_up[0].astype(F32), w_down[0].astype(F32))

    npt = n_prompt // T_ROWS
    n_tiles = n_tok // T_ROWS
    per = T_DISPATCH // T_ROWS
    dest_spec = pl.BlockSpec((None, 1, T_DISPATCH * TOP_K), lambda i: (i // per, 0, 0), memory_space=pltpu.SMEM)
    dest_next_spec = pl.BlockSpec((None, 1, T_DISPATCH * TOP_K),
                                  lambda i: (jnp.minimum(i + 1, n_tiles - 1) // per, 0, 0), memory_space=pltpu.SMEM)
    out_p, out_s = pl.pallas_call(
        functools.partial(_combine_body, n_prompt_tiles=npt, n_tiles=n_tiles),
        grid=(n_tiles,),
        in_specs=[dest_spec, dest_next_spec, pl.BlockSpec(memory_space=pl.ANY),
                  pl.BlockSpec((T_ROWS, D_MODEL), lambda i: (i, 0)), pl.BlockSpec((T_ROWS, LANES), lambda i: (i, 0)),
                  _full((1, D_MODEL)), _full((1, D_MODEL))],
        out_specs=[pl.BlockSpec((T_ROWS, D_MODEL), lambda i: (jnp.minimum(i, npt - 1), 0)),
                   pl.BlockSpec((T_ROWS, D_MODEL), lambda i: (jnp.maximum(i - npt, 0), 0))],
        out_shape=(jax.ShapeDtypeStruct((n_prompt, D_MODEL), F32), jax.ShapeDtypeStruct((n_sample, D_MODEL), F32)),
        scratch_shapes=[pltpu.VMEM((2 * TOP_K, T_ROWS * SUBLANES, LANES), F32), pltpu.SemaphoreType.DMA((2,))],
        compiler_params=cparams(("arbitrary",)),
        name="combine",
    )(dest_tiles, dest_tiles, y_sorted, base_all, gate_all, row(ln2_g[0]), row(ln2_b[0]))

    y_prompt = out_p.reshape(batch, seq, D_MODEL)
    y_sample = out_s.reshape(dec_batch, dec_seq, D_MODEL)
    conv_state_prompt = u_tail[None, :, HIST - (CONV_W - 1):, :]
    u_s3 = u_s.reshape(dec_batch, dec_seq, C_CONV)
    conv_state_sample = jnp.concatenate([cache_conv[0][:, dec_seq:, :].astype(F32), u_s3], axis=1)[None]
    chunk_v_prompt = v_chunk[None]
    chunk_v_sample = v_s.reshape(1, dec_batch, dec_seq, C_GMLP)
    return (y_prompt, y_sample, conv_state_prompt, conv_state_sample, chunk_v_prompt, chunk_v_sample)
```

```python
import functools

import jax
import jax.numpy as jnp
from jax import lax
from jax.experimental import pallas as pl
from jax.experimental.pallas import tpu as pltpu

F32 = jnp.float32
BF16 = jnp.bfloat16

D_MODEL = 1024
C_CONV = 512
C_GMLP = 512
N_CONV_GROUPS = 8
CONV_W = 31
N_HEADS = 8
HEAD_DIM = C_GMLP // N_HEADS
CHUNK = 128
N_EXPERTS = 32
TOP_K = 4
D_FF = 1024
PLE_DIM = 256
SWIGLU_LIMIT = 7.0
SWIGLU_ALPHA = 1.702
LN_EPS = 1e-5
DEPTH = 1
DEEPNORM_ALPHA = (2.0 * DEPTH) ** 0.25

LANES = 128
SUBLANES = 8
MXU_DIM = 256
VMEM_LIMIT = 48 * 1024 * 1024
BIG_VMEM_LIMIT = 56 * 1024 * 1024

T_PROMPT = 512
PROMPT_SPLIT = 2
HIST = 32
CONV_STRIDE = 4
SEQS_PER_TILE = 64
SAMPLE_PITCH = 36
T_RANK = 1024
T_DISPATCH = 1024
T_ROWS = 256
TM = 512
TM_CHAIN = 256


def _dot(a, b):
    return jnp.dot(a, b, preferred_element_type=F32)


def _layer_norm(x, g, b):
    mu = jnp.mean(x, axis=-1, keepdims=True)
    xc = x - mu
    var = jnp.mean(xc * xc, axis=-1, keepdims=True)
    return xc * lax.rsqrt(var + LN_EPS) * g + b


def _split_bf16(a):
    hi = a.astype(BF16)
    lo = (a - hi.astype(F32)).astype(BF16)
    return hi, lo


def _group_mean(a, gmat_ref):
    hi, lo = _split_bf16(a)
    g = gmat_ref[...]
    outs = []
    for s in range(C_CONV // MXU_DIM):
        sl = slice(MXU_DIM * s, MXU_DIM * (s + 1))
        outs.append(_dot(hi[:, sl], g) + _dot(lo[:, sl], g))
    return jnp.concatenate(outs, axis=1)


def _group_norm_silu(y, gmat_ref, gn_g, gn_b):
    mu = _group_mean(y, gmat_ref)
    yc = y - mu
    var = _group_mean(yc * yc, gmat_ref)
    yn = yc * lax.rsqrt(var + LN_EPS) * gn_g + gn_b
    return yn * jax.nn.sigmoid(yn)


def _store_token_tiles(ref, val, tok0=0):
    n = val.shape[0]
    for c in range(D_MODEL // LANES):
        ref[pl.ds(tok0 * SUBLANES + c, n, stride=SUBLANES), :] = val[:, c * LANES:(c + 1) * LANES]


def _load_token_tiles(ref, n, lead=(), tok0=0):
    parts = [ref[lead + (pl.ds(tok0 * SUBLANES + c, n, stride=SUBLANES), slice(None))]
             for c in range(D_MODEL // LANES)]
    return jnp.concatenate(parts, axis=1)


def _front(x, w):
    xn = _layer_norm(x, w["ln_in_g"][...], w["ln_in_b"][...])
    z = _dot(xn.astype(BF16), w["w_in"][...])
    a_val = z[:, 0:C_CONV]
    a_gate = z[:, C_CONV:2 * C_CONV]
    g_u = z[:, 2 * C_CONV:2 * C_CONV + C_GMLP]
    g_v = z[:, 2 * C_CONV + C_GMLP:]
    u = a_val * jax.nn.sigmoid(a_gate)
    ug = jax.nn.gelu(g_u)
    v = _layer_norm(jax.nn.gelu(g_v), w["vn_g"][...], w["vn_b"][...])
    return xn, u, ug, v


def _tail(xn, y_a, y_b, w, h_ref, idx_ref, gate_ref, row0=0):
    rows = slice(row0, row0 + xn.shape[0])
    mix = _dot(y_a.astype(BF16), w["w_out"][0:C_CONV, :]) + _dot(y_b.astype(BF16), w["w_out"][C_CONV:, :])
    h = _layer_norm(DEEPNORM_ALPHA * xn + mix, w["ln1_g"][...], w["ln1_b"][...])
    hb, h_lo = _split_bf16(h)
    _store_token_tiles(h_ref, h, row0)

    pair = _dot(hb, w["wr_pair"][...])
    logits = pair[:, :LANES] + pair[:, LANES:] + _dot(h_lo, w["wr_hi"][...]) + w["b_r"][...]
    lane = lax.broadcasted_iota(jnp.int32, logits.shape, 1)
    lane_f = lane.astype(F32)
    vals = jnp.where(lane < N_EXPERTS, logits, -jnp.inf)
    tops, ids = [], []
    for _ in range(TOP_K):
        m = jnp.max(vals, axis=-1, keepdims=True)
        i = jnp.min(jnp.where(vals == m, lane_f, float(LANES)), axis=-1, keepdims=True)
        vals = jnp.where(lane_f == i, -jnp.inf, vals)
        tops.append(m)
        ids.append(i)
    exps = [jnp.exp(m - tops[0]) for m in tops]
    denom = exps[0] + exps[1] + exps[2] + exps[3]
    idx_out = jnp.zeros(logits.shape, F32)
    gate_out = jnp.zeros(logits.shape, F32)
    for k in range(TOP_K):
        idx_out = jnp.where(lane == k, ids[k], idx_out)
        gate_out = jnp.where(lane == k, exps[k] / denom, gate_out)
    idx_ref[rows, :] = idx_out.astype(jnp.int32)
    gate_ref[rows, :] = gate_out


_WEIGHT_NAMES = ("ln_in_g", "ln_in_b", "w_in", "vn_g", "vn_b", "gmat", "gn_g", "gn_b", "w_out", "ln1_g", "ln1_b",
                 "wr_hi", "wr_pair", "b_r")


def _prompt_branch(j, x_ref, w, cwb_ref, ws_ref, bs_ref, outs, tail_ref, vch_ref, ubuf, yslab, last_j):
    t = T_PROMPT
    n = t // PROMPT_SPLIT
    n_slabs = C_CONV // LANES

    @pl.when(j == 0)
    def _():
        ubuf[:, 0:HIST, :] = jnp.zeros((n_slabs, HIST, LANES), F32)

    @pl.when(j > 0)
    def _():
        ubuf[:, 0:HIST, :] = ubuf[:, t:t + HIST, :]

    def front(h):
        xn, u, ug, v = _front(x_ref[h * n:(h + 1) * n, :], w)
        for s in range(n_slabs):
            ubuf[s, HIST + h * n:HIST + (h + 1) * n, :] = u[:, s * LANES:(s + 1) * LANES]
        return xn, u, ug, v

    def conv(h):
        rows = CONV_STRIDE * SUBLANES
        first = HIST - (CONV_W - 1)
        for s in range(n_slabs):
            for c in range(h * n // rows, (h + 1) * n // rows):
                accs = [None] * CONV_STRIDE
                for shift in range(CONV_STRIDE + CONV_W - 1):
                    win = ubuf[s, pl.ds(first + c * rows + shift, SUBLANES, stride=CONV_STRIDE), :]
                    for ph in range(CONV_STRIDE):
                        k = shift - ph
                        if 0 <= k < CONV_W:
                            term = cwb_ref[k, s] * win
                            accs[ph] = term if accs[ph] is None else accs[ph] + term
                for ph in range(CONV_STRIDE):
                    yslab[s, pl.ds(c * rows + ph, SUBLANES, stride=CONV_STRIDE), :] = accs[ph]
        y_conv = jnp.concatenate([yslab[s, h * n:(h + 1) * n, :] for s in range(n_slabs)], axis=1)
        return _group_norm_silu(y_conv, w["gmat"], w["gn_g"][...], w["gn_b"][...])

    def spatial_gate(ug, v):
        lane = lax.broadcasted_iota(jnp.int32, (CHUNK, LANES), 1)
        mixed_chunks = []
        for c in range(n // CHUNK):
            vc = v[c * CHUNK:(c + 1) * CHUNK, :]
            parts = []
            for q in range(N_HEADS // 2):
                vp = vc[:, q * LANES:(q + 1) * LANES]
                rhs = jnp.concatenate([jnp.where(lane < HEAD_DIM, vp, 0.0), jnp.where(lane >= HEAD_DIM, vp, 0.0)],
                                      axis=0).astype(BF16)
                parts.append(_dot(ws_ref[q], rhs))
            mixed_chunks.append(jnp.concatenate(parts, axis=1) + bs_ref[...])
        return ug * jnp.concatenate(mixed_chunks, axis=0)

    def tail(h, f, y_a, y_b):
        _tail(f[0], y_a, y_b, w, *outs, row0=h * n)

    fronts = [front(0)]
    y_as = {}
    for h in range(PROMPT_SPLIT):
        if h + 1 < PROMPT_SPLIT:
            fronts.append(front(h + 1))
        y_as[h] = conv(h)
        if h > 0:
            tail(h - 1, fronts[h - 1], y_as.pop(h - 1), spatial_gate(fronts[h - 1][2], fronts[h - 1][3]))
    last = PROMPT_SPLIT - 1
    tail(last, fronts[last], y_as.pop(last), spatial_gate(fronts[last][2], fronts[last][3]))

    @pl.when(j == last_j)
    def _():
        tail_ref[...] = fronts[last][1][n - HIST:, :]
        vch_ref[...] = fronts[last][3][n - CHUNK:, :]


def _sample_branch(x_ref, cslab_ref, w, cwb_ref, convub_ref, gwb_ref, b8b_ref, outs, u_out_ref, v_out_ref,
                   uslab, vslab, yslab, mslab):
    xn, u, ug, v = _front(x_ref[...], w)
    u_out_ref[...] = u
    v_out_ref[...] = v
    n_slabs = C_CONV // LANES
    for s in range(n_slabs):
        uslab[s] = u[:, s * LANES:(s + 1) * LANES]
        vslab[s] = v[:, s * LANES:(s + 1) * LANES]
    first = HIST - (CONV_W - 1)
    n_cache = CONV_W - 1

    def group(g, carry):
        row0 = g * (SUBLANES * SUBLANES)
        for s in range(n_slabs):
            cache = [cslab_ref[s, pl.ds(g * (SUBLANES * SAMPLE_PITCH) + first + j, SUBLANES, stride=SAMPLE_PITCH), :]
                     for j in range(n_cache)]
            u_pos = [uslab[s, pl.ds(row0 + q, SUBLANES, stride=SUBLANES), :] for q in range(SUBLANES)]
            v_pos = [vslab[s, pl.ds(row0 + q, SUBLANES, stride=SUBLANES), :] for q in range(SUBLANES)]
            for t in range(SUBLANES):
                acc = None
                for k in range(n_cache - t):
                    term = cwb_ref[k, s] * cache[t + k]
                    acc = term if acc is None else acc + term
                mix = b8b_ref[t, s]
                for q in range(t + 1):
                    acc = acc + convub_ref[q, t, s] * u_pos[q]
                    mix = mix + gwb_ref[q, t, s] * v_pos[q]
                yslab[s, pl.ds(row0 + t, SUBLANES, stride=SUBLANES), :] = acc
                mslab[s, pl.ds(row0 + t, SUBLANES, stride=SUBLANES), :] = mix
        return carry

    lax.fori_loop(0, SEQS_PER_TILE // SUBLANES, group, 0)
    y_conv = jnp.concatenate([yslab[s] for s in range(n_slabs)], axis=1)
    mixed = jnp.concatenate([mslab[s] for s in range(n_slabs)], axis=1)
    y_a = _group_norm_silu(y_conv, w["gmat"], w["gn_g"][...], w["gn_b"][...])
    _tail(xn, y_a, ug * mixed, w, *outs)


def _mixer_body(*refs, n_prompt_tiles, tiles_per_seq):
    n_w = len(_WEIGHT_NAMES)
    xp_ref, xs_ref, cslab_ref = refs[:3]
    w = dict(zip(_WEIGHT_NAMES, refs[3:3 + n_w]))
    cwb_ref, ws_ref, bs_ref, convub_ref, gwb_ref, b8b_ref = refs[3 + n_w:9 + n_w]
    outs = refs[9 + n_w:12 + n_w]
    tail_ref, vch_ref, u_out_ref, v_out_ref = refs[12 + n_w:16 + n_w]
    ubuf, yslab, uslab, vslab, mslab = refs[16 + n_w:]
    step = pl.program_id(0)

    @pl.when(step < n_prompt_tiles)
    def _():
        _prompt_branch(lax.rem(step, tiles_per_seq), xp_ref, w, cwb_ref, ws_ref, bs_ref, outs, tail_ref,
                       vch_ref, ubuf, yslab, tiles_per_seq - 1)

    @pl.when(step >= n_prompt_tiles)
    def _():
        _sample_branch(xs_ref, cslab_ref, w, cwb_ref, convub_ref, gwb_ref, b8b_ref, outs, u_out_ref, v_out_ref,
                       uslab, vslab, yslab, mslab)


def _rank_body(idx_ref, tri_ref, before_ref, counts_ref, carry):
    i = pl.program_id(0)

    @pl.when(i == 0)
    def _():
        carry[...] = jnp.zeros(carry.shape, F32)

    idx = idx_ref[...]
    lane = lax.broadcasted_iota(jnp.int32, idx.shape, 1)
    multi = jnp.zeros(idx.shape, F32)
    for k in range(TOP_K):
        multi = multi + (lane == idx[:, k:k + 1]).astype(F32)
    before_ref[...] = (_dot(tri_ref[...], multi.astype(BF16)) + carry[...]).astype(jnp.int32)
    carry[...] = carry[...] + jnp.sum(multi, axis=0, keepdims=True)
    counts_ref[...] = carry[...].astype(jnp.int32)


def _pad_bits():
    b = TM // 2
    while b >= 1:
        yield b
        b //= 2


def _token_rows(t, n=1):
    return pl.ds(pl.multiple_of(t * SUBLANES, SUBLANES), n * SUBLANES)


def _dispatch_body(pad_off_ref, pad_n_ref, n_used_ref, dest_ref, h_ref, pp_ref, ps_ref, wg_ref, wp_ref, xs_ref, base_ref,
                   zbuf, sem, zsem, *, n_blocks, n_prompt_tiles):
    i = pl.program_id(0)

    def issue(r, carry):
        src = h_ref.at[_token_rows(r), :]
        for k in range(TOP_K):
            d = dest_ref[0, r * TOP_K + k]
            pltpu.make_async_copy(src, xs_ref.at[_token_rows(d), :], sem).start(priority=k % 2)
        return carry

    @pl.when(i == 0)
    def _():
        zbuf[...] = jnp.zeros(zbuf.shape, F32)

        def zero_copy(off, b):
            return pltpu.make_async_copy(zbuf.at[pl.ds(0, b * SUBLANES), :], xs_ref.at[_token_rows(off, b), :], zsem)

        def start_or_wait(cond, cp, wait):
            @pl.when(cond)
            def _():
                if wait:
                    cp.wait()
                else:
                    cp.start()

        for wait in (False, True):
            for e in range(N_EXPERTS):
                n = pad_n_ref[e]
                for b in _pad_bits():
                    start_or_wait((n & b) != 0, zero_copy(pad_off_ref[e] + (n & ~(2 * b - 1)), b), wait)

        half = TM // 2

        def tail_block(wait):
            def go(blk, carry):
                for s in range(TM // half):
                    cp = zero_copy(blk * TM + s * half, half)
                    cp.wait() if wait else cp.start()
                return carry
            return go

        lax.fori_loop(n_used_ref[0], n_blocks, tail_block(False), 0)
        lax.fori_loop(n_used_ref[0], n_blocks, tail_block(True), 0)

    rows = TM_CHAIN
    is_prompt = i < n_prompt_tiles
    for c in range(T_DISPATCH // rows):
        lax.fori_loop(c * rows, (c + 1) * rows, issue, 0, unroll=8)
        sl = slice(c * rows, (c + 1) * rows)
        h = _load_token_tiles(h_ref, rows, tok0=c * rows)
        p = jnp.where(is_prompt, pp_ref[sl, :], ps_ref[sl, :])
        ple = _dot(p.astype(BF16), wp_ref[...]) * jax.nn.sigmoid(_dot(h.astype(BF16), wg_ref[...]))
        base_ref[sl, :] = DEEPNORM_ALPHA * h + ple

    for _ in range(TOP_K):
        pltpu.make_async_copy(h_ref, xs_ref.at[_token_rows(0, T_DISPATCH), :], sem).wait()


def _expert_body(blk_e_ref, first_ref, slot_ref, next_ref, rows_ref, n_used_ref, x_ref, bu_ref, bd_ref, wu_hbm, wd_hbm,
                 y_ref, wu_f32, wd_f32, wu_bf, wd_bf, sems):
    i = pl.program_id(0)

    def weight_copies(e, s):
        return (pltpu.make_async_copy(wu_hbm.at[e], wu_f32.at[s], sems.at[s]),
                pltpu.make_async_copy(wd_hbm.at[e], wd_f32.at[s], sems.at[s]))

    @pl.when(i < n_used_ref[0])
    def _():
        s = slot_ref[i]

        @pl.when(first_ref[i] == 1)
        def _():
            @pl.when(i == 0)
            def _():
                for cp in weight_copies(blk_e_ref[i], s):
                    cp.start()

            for cp in weight_copies(blk_e_ref[i], s):
                cp.wait()
            nxt = next_ref[i]

            @pl.when(nxt >= 0)
            def _():
                for cp in weight_copies(nxt, 1 - s):
                    cp.start()

            chunk = D_MODEL // SUBLANES

            def cast_rows(c, carry):
                r = pl.multiple_of(c * chunk, chunk)
                wu_bf[pl.ds(r, chunk), :] = wu_f32[s, pl.ds(r, chunk), :].astype(BF16)
                wd_bf[pl.ds(r, chunk), :] = wd_f32[s, pl.ds(r, chunk), :].astype(BF16)
                return carry

            lax.fori_loop(0, SUBLANES, cast_rows, 0)

        def chain(c):
            x = _load_token_tiles(x_ref, TM_CHAIN, tok0=c * TM_CHAIN)
            hcat = _dot(x.astype(BF16), wu_bf[...]) + bu_ref[...]
            h_glu = jnp.minimum(hcat[:, :D_FF], SWIGLU_LIMIT)
            h_lin = jnp.clip(hcat[:, D_FF:], -SWIGLU_LIMIT, SWIGLU_LIMIT)
            act = h_glu * jax.nn.sigmoid(SWIGLU_ALPHA * h_glu) * (h_lin + 1.0)
            _store_token_tiles(y_ref, _dot(act.astype(BF16), wd_bf[...]) + bd_ref[...], tok0=c * TM_CHAIN)

        n_chains = TM // TM_CHAIN
        live = (rows_ref[i] + TM_CHAIN - 1) // TM_CHAIN
        for m in range(1, n_chains + 1):
            @pl.when(live == m)
            def _():
                for c in range(m):
                    chain(c)
                if m < n_chains:
                    y_ref[m * TM_CHAIN * SUBLANES:, :] = jnp.zeros(((n_chains - m) * TM_CHAIN * SUBLANES, LANES), F32)

    @pl.when(pl.program_id(0) >= n_used_ref[0])
    def _():
        y_ref[...] = jnp.zeros(y_ref.shape, F32)


def _combine_body(dest_ref, dest_next_ref, y_ref, base_ref, gate_ref, g_ref, b_ref, outp_ref, outs_ref, ybuf, sems, *,
                  n_prompt_tiles, n_tiles):
    i = pl.program_id(0)

    def issue_tile(d_ref, tile, slot):
        first = lax.rem(tile, T_DISPATCH // T_ROWS) * (T_ROWS * TOP_K)

        def issue(r, carry):
            for k in range(TOP_K):
                d = d_ref[0, first + r * TOP_K + k]
                pltpu.make_async_copy(y_ref.at[_token_rows(d), :], ybuf.at[slot * TOP_K + k, _token_rows(r), :],
                                      sems.at[slot]).start(priority=k % 2)
            return carry

        lax.fori_loop(0, T_ROWS, issue, 0, unroll=8)

    slot = lax.rem(i, 2)

    @pl.when(i == 0)
    def _():
        issue_tile(dest_ref, i, 0)

    @pl.when(i + 1 < n_tiles)
    def _():
        issue_tile(dest_next_ref, i + 1, 1 - slot)

    for k in range(TOP_K):
        pltpu.make_async_copy(y_ref.at[_token_rows(0, T_ROWS), :], ybuf.at[slot * TOP_K + k], sems.at[slot]).wait()

    gate = gate_ref[...]
    acc = base_ref[...]
    for k in range(TOP_K):
        acc = acc + gate[:, k:k + 1] * _load_token_tiles(ybuf, T_ROWS, lead=(slot * TOP_K + k,))
    out = _layer_norm(acc, g_ref[...], b_ref[...])

    @pl.when(i < n_prompt_tiles)
    def _():
        outp_ref[...] = out

    @pl.when(i >= n_prompt_tiles)
    def _():
        outs_ref[...] = out


def _full(shape):
    return pl.BlockSpec(shape, lambda *_: (0,) * len(shape), pipeline_mode=pl.Buffered(1))


def kernel(x_prompt, x_sample, cache_conv, p_prompt, p_sample, ln_in_g, ln_in_b, w_in, conv_w, gn_g, gn_b, vn_g, vn_b,
           w_spatial, b_spatial, w_out, ln1_g, ln1_b, w_router, b_router, w_up, b_up, w_down, b_down, w_ple,
           w_ple_gate, ln2_g, ln2_b):
    batch, seq, _ = x_prompt.shape
    dec_batch, dec_seq, _ = x_sample.shape
    assert w_in.shape[0] == DEPTH and dec_seq == SUBLANES and seq % T_PROMPT == 0
    n_prompt = batch * seq
    n_sample = dec_batch * dec_seq
    n_tok = n_prompt + n_sample
    t_s = SEQS_PER_TILE * dec_seq
    assert n_prompt % T_ROWS == 0 and n_sample % T_ROWS == 0 and n_tok % T_RANK == 0 and n_sample % t_s == 0
    assert n_tok % T_DISPATCH == 0

    row = lambda a: a.reshape(1, -1).astype(F32)
    gidx = jnp.arange(MXU_DIM) // (C_CONV // N_CONV_GROUPS)
    gmat = jnp.where(gidx[:, None] == gidx[None, :], 1.0 / (C_CONV // N_CONV_GROUPS), 0.0).astype(BF16)
    wr_pad = jnp.pad(w_router[0].astype(F32), ((0, 0), (0, LANES - N_EXPERTS)))
    wr_hi = wr_pad.astype(BF16)
    wr_lo = (wr_pad - wr_hi.astype(F32)).astype(BF16)
    weights = dict(
        ln_in_g=row(ln_in_g), ln_in_b=row(ln_in_b), w_in=w_in[0].astype(BF16), vn_g=row(vn_g[0]), vn_b=row(vn_b[0]),
        gmat=gmat, gn_g=row(gn_g[0]), gn_b=row(gn_b[0]), w_out=w_out[0].astype(BF16), ln1_g=row(ln1_g[0]),
        ln1_b=row(ln1_b[0]), wr_hi=wr_hi, wr_pair=jnp.concatenate([wr_hi, wr_lo], axis=1),
        b_r=jnp.pad(row(b_router[0]), ((0, 0), (0, LANES - N_EXPERTS))))
    w_list = [weights[n] for n in _WEIGHT_NAMES]
    w_specs = [_full(a.shape) for a in w_list]
    causal = jnp.tril(jnp.ones((CHUNK, CHUNK), bool))
    ws_m = jnp.where(causal[None], w_spatial[0], 0.0)
    ws_cat = jnp.concatenate([ws_m[0::2], ws_m[1::2]], axis=2).astype(BF16)
    bs_full = jnp.repeat(b_spatial[0].T.astype(F32), HEAD_DIM, axis=1)
    s_i = jnp.arange(SUBLANES)[:, None]
    t_i = jnp.arange(SUBLANES)[None, :]
    tap = jnp.clip(CONV_W - 1 - t_i + s_i, 0, CONV_W - 1)
    convu = jnp.where((s_i <= t_i)[:, :, None], conv_w[0].astype(F32)[tap], 0.0)
    gw8 = jnp.transpose(ws_m[:, :SUBLANES, :SUBLANES], (2, 1, 0))
    gw8 = jnp.repeat(gw8.astype(F32), HEAD_DIM, axis=2)
    b8 = bs_full[:SUBLANES]
    n_slabs = C_CONV // LANES
    slab_rows = lambda a: jnp.broadcast_to(a.reshape(a.shape[:-1] + (n_slabs, 1, LANES)),
                                           a.shape[:-1] + (n_slabs, SUBLANES, LANES))
    convub, gwb, b8b = slab_rows(convu), slab_rows(gw8), slab_rows(b8)
    first = HIST - (CONV_W - 1)
    cslab = jnp.pad(cache_conv[0].astype(F32), ((0, 0), (first, SAMPLE_PITCH - first - (CONV_W - 1)), (0, 0)))
    cslab = cslab.reshape(dec_batch * SAMPLE_PITCH, n_slabs, LANES).transpose(1, 0, 2)

    cparams = lambda sem: pltpu.CompilerParams(dimension_semantics=sem, vmem_limit_bytes=VMEM_LIMIT)

    nj = seq // T_PROMPT
    npt = n_prompt // T_PROMPT
    assert t_s == T_PROMPT
    pstep = lambda i: jnp.minimum(i, npt - 1)
    sstep = lambda i: jnp.maximum(i - npt, 0)
    tok_block = lambda width: pl.BlockSpec((T_PROMPT, width), lambda i: (i, 0))
    cwb = slab_rows(conv_w[0].astype(F32))
    tables = [cwb, ws_cat, bs_full, convub, gwb, b8b]
    h_all, idx_all, gate_all, u_tail, v_chunk, u_s, v_s = pl.pallas_call(
        functools.partial(_mixer_body, n_prompt_tiles=npt, tiles_per_seq=nj),
        grid=(npt + n_sample // t_s,),
        in_specs=[pl.BlockSpec((None, T_PROMPT, D_MODEL), lambda i: (pstep(i) // nj, pstep(i) % nj, 0)),
                  pl.BlockSpec((t_s, D_MODEL), lambda i: (sstep(i), 0)),
                  pl.BlockSpec((n_slabs, SEQS_PER_TILE * SAMPLE_PITCH, LANES), lambda i: (0, sstep(i), 0),
                               pipeline_mode=pl.Buffered(1))]
                 + w_specs + [_full(a.shape) for a in tables],
        out_specs=[pl.BlockSpec((T_PROMPT * SUBLANES, LANES), lambda i: (i, 0)),
                   tok_block(LANES), tok_block(LANES),
                   pl.BlockSpec((None, HIST, C_CONV), lambda i: (pstep(i) // nj, 0, 0)),
                   pl.BlockSpec((None, CHUNK, C_GMLP), lambda i: (pstep(i) // nj, 0, 0)),
                   pl.BlockSpec((t_s, C_CONV), lambda i: (sstep(i), 0)),
                   pl.BlockSpec((t_s, C_GMLP), lambda i: (sstep(i), 0))],
        out_shape=(jax.ShapeDtypeStruct((n_tok * SUBLANES, LANES), F32),
                   jax.ShapeDtypeStruct((n_tok, LANES), jnp.int32), jax.ShapeDtypeStruct((n_tok, LANES), F32),
                   jax.ShapeDtypeStruct((batch, HIST, C_CONV), F32), jax.ShapeDtypeStruct((batch, CHUNK, C_GMLP), F32),
                   jax.ShapeDtypeStruct((n_sample, C_CONV), F32), jax.ShapeDtypeStruct((n_sample, C_GMLP), F32)),
        scratch_shapes=[pltpu.VMEM((n_slabs, T_PROMPT + HIST, LANES), F32)]
                       + [pltpu.VMEM((n_slabs, T_PROMPT, LANES), F32)] * 4,
        compiler_params=pltpu.CompilerParams(dimension_semantics=("arbitrary",), vmem_limit_bytes=BIG_VMEM_LIMIT),
        name="mixer",
    )(x_prompt, x_sample.reshape(n_sample, D_MODEL), cslab, *w_list, *tables)

    tri = (jnp.arange(T_RANK)[:, None] > jnp.arange(T_RANK)[None, :]).astype(BF16)
    before_all, counts = pl.pallas_call(
        _rank_body,
        grid=(n_tok // T_RANK,),
        in_specs=[pl.BlockSpec((T_RANK, LANES), lambda i: (i, 0)), _full(tri.shape)],
        out_specs=[pl.BlockSpec((T_RANK, LANES), lambda i: (i, 0)), _full((1, LANES))],
        out_shape=(jax.ShapeDtypeStruct((n_tok, LANES), jnp.int32), jax.ShapeDtypeStruct((1, LANES), jnp.int32)),
        scratch_shapes=[pltpu.VMEM((1, LANES), F32)],
        compiler_params=cparams(("arbitrary",)),
        name="rank",
    )(idx_all, tri)

    n_assign = n_tok * TOP_K
    n_blocks = n_assign // TM + N_EXPERTS
    cap = n_blocks * TM
    cnt = counts[0, :N_EXPERTS]
    padded = (cnt + TM - 1) // TM * TM
    p_end = jnp.cumsum(padded)
    p_start = p_end - padded
    e_ids = jnp.arange(N_EXPERTS, dtype=jnp.int32)
    lookup = lambda table, ids: jnp.sum(jnp.where(ids[..., None] == e_ids, table, 0), axis=-1)
    slot_table = p_start[None, :] + before_all[:, :N_EXPERTS]
    dest = jnp.sum(jnp.where(idx_all[:, :TOP_K, None] == e_ids, slot_table[:, None, :], 0), axis=-1).astype(jnp.int32)
    dest_tiles = dest.reshape(n_tok // T_DISPATCH, 1, T_DISPATCH * TOP_K)
    blk_row = jnp.arange(n_blocks, dtype=jnp.int32) * TM
    blk_e = jnp.minimum(jnp.sum(p_end[None, :] <= blk_row[:, None], axis=1), N_EXPERTS - 1).astype(jnp.int32)
    n_used = (p_end[-1:] // TM).astype(jnp.int32)
    pad_off = (p_start + cnt).astype(jnp.int32)
    pad_n = (padded - cnt).astype(jnp.int32)
    used = cnt > 0
    slot_e = (jnp.cumsum(used.astype(jnp.int32)) - 1) & 1
    later_used = jnp.where(used[None, :] & (e_ids[None, :] > e_ids[:, None]), e_ids[None, :], N_EXPERTS)
    next_e = jnp.min(later_used, axis=1)
    next_e = jnp.where(next_e < N_EXPERTS, next_e, -1).astype(jnp.int32)
    blk_first = ((blk_row == lookup(p_start, blk_e)) & (blk_row < p_end[-1])).astype(jnp.int32)
    blk_rows = jnp.clip(lookup(p_start + cnt, blk_e) - blk_row, 0, TM).astype(jnp.int32)
    blk_slot = lookup(slot_e, blk_e).astype(jnp.int32)
    blk_next = lookup(next_e, blk_e).astype(jnp.int32)


    assert seq % T_DISPATCH == 0 and n_sample % T_DISPATCH == 0
    npd = n_prompt // T_DISPATCH
    per_seq = seq // T_DISPATCH
    pd = lambda i: jnp.minimum(i, npd - 1)
    w_gate = w_ple_gate[0].astype(BF16)
    w_ple_b = w_ple[0].astype(BF16)
    x_sorted, base_all = pl.pallas_call(
        functools.partial(_dispatch_body, n_blocks=n_blocks, n_prompt_tiles=npd),
        grid_spec=pltpu.PrefetchScalarGridSpec(
            num_scalar_prefetch=3,
            grid=(n_tok // T_DISPATCH,),
            in_specs=[pl.BlockSpec((None, 1, T_DISPATCH * TOP_K), lambda i, *_: (i, 0, 0), memory_space=pltpu.SMEM),
                      pl.BlockSpec((T_DISPATCH * SUBLANES, LANES), lambda i, *_: (i, 0)),
                      pl.BlockSpec((None, T_DISPATCH, PLE_DIM), lambda i, *_: (pd(i) // per_seq, pd(i) % per_seq, 0)),
                      pl.BlockSpec((T_DISPATCH, PLE_DIM), lambda i, *_: (jnp.maximum(i - npd, 0), 0)),
                      _full(w_gate.shape), _full(w_ple_b.shape)],
            out_specs=[pl.BlockSpec(memory_space=pl.ANY),
                       pl.BlockSpec((T_DISPATCH, D_MODEL), lambda i, *_: (i, 0))],
            scratch_shapes=[pltpu.VMEM((TM // 2 * SUBLANES, LANES), F32), pltpu.SemaphoreType.DMA,
                            pltpu.SemaphoreType.DMA]),
        out_shape=(jax.ShapeDtypeStruct((cap * SUBLANES, LANES), F32), jax.ShapeDtypeStruct((n_tok, D_MODEL), F32)),
        compiler_params=cparams(("arbitrary",)),
        name="dispatch",
    )(pad_off, pad_n, n_used, dest_tiles, h_all, p_prompt[0], p_sample[0].reshape(n_sample, PLE_DIM), w_gate, w_ple_b)

    last = lambda i, nu: jnp.minimum(i, nu[0] - 1)
    y_sorted = pl.pallas_call(
        _expert_body,
        grid_spec=pltpu.PrefetchScalarGridSpec(
            num_scalar_prefetch=6,
            grid=(n_blocks,),
            in_specs=[pl.BlockSpec((TM * SUBLANES, LANES), lambda i, be, bf, bs, bn, br, nu: (last(i, nu), 0)),
                      pl.BlockSpec((None, 1, 2 * D_FF), lambda i, be, bf, bs, bn, br, nu: (be[last(i, nu)], 0, 0)),
                      pl.BlockSpec((None, 1, D_MODEL), lambda i, be, bf, bs, bn, br, nu: (be[last(i, nu)], 0, 0)),
                      pl.BlockSpec(memory_space=pl.ANY), pl.BlockSpec(memory_space=pl.ANY)],
            out_specs=pl.BlockSpec((TM * SUBLANES, LANES), lambda i, be, bf, bs, bn, br, nu: (i, 0)),
            scratch_shapes=[pltpu.VMEM((2, D_MODEL, 2 * D_FF), F32), pltpu.VMEM((2, D_FF, D_MODEL), F32),
                            pltpu.VMEM((D_MODEL, 2 * D_FF), BF16), pltpu.VMEM((D_FF, D_MODEL), BF16),
                            pltpu.SemaphoreType.DMA((2,))]),
        out_shape=jax.ShapeDtypeStruct((cap * SUBLANES, LANES), F32),
        compiler_params=pltpu.CompilerParams(dimension_semantics=("arbitrary",), vmem_limit_bytes=BIG_VMEM_LIMIT),
        name="experts",
    )(blk_e, blk_first, blk_slot, blk_next, blk_rows, n_used, x_sorted, b_up[0].astype(F32)[:, None, :],
      b_down[0].astype(F32)[:, None, :], w_up[0].astype(F32), w_down[0].astype(F32))

    npt = n_prompt // T_ROWS
    n_tiles = n_tok // T_ROWS
    per = T_DISPATCH // T_ROWS
    dest_spec = pl.BlockSpec((None, 1, T_DISPATCH * TOP_K), lambda i: (i // per, 0, 0), memory_space=pltpu.SMEM)
    dest_next_spec = pl.BlockSpec((None, 1, T_DISPATCH * TOP_K),
                                  lambda i: (jnp.minimum(i + 1, n_tiles - 1) // per, 0, 0), memory_space=pltpu.SMEM)
    out_p, out_s = pl.pallas_call(
        functools.partial(_combine_body, n_prompt_tiles=npt, n_tiles=n_tiles),
        grid=(n_tiles,),
        in_specs=[dest_spec, dest_next_spec, pl.BlockSpec(memory_space=pl.ANY),
                  pl.BlockSpec((T_ROWS, D_MODEL), lambda i: (i, 0)), pl.BlockSpec((T_ROWS, LANES), lambda i: (i, 0)),
                  _full((1, D_MODEL)), _full((1, D_MODEL))],
        out_specs=[pl.BlockSpec((T_ROWS, D_MODEL), lambda i: (jnp.minimum(i, npt - 1), 0)),
                   pl.BlockSpec((T_ROWS, D_MODEL), lambda i: (jnp.maximum(i - npt, 0), 0))],
        out_shape=(jax.ShapeDtypeStruct((n_prompt, D_MODEL), F32), jax.ShapeDtypeStruct((n_sample, D_MODEL), F32)),
        scratch_shapes=[pltpu.VMEM((2 * TOP_K, T_ROWS * SUBLANES, LANES), F32), pltpu.SemaphoreType.DMA((2,))],
        compiler_params=cparams(("arbitrary",)),
        name="combine",
    )(dest_tiles, dest_tiles, y_sorted, base_all, gate_all, row(ln2_g[0]), row(ln2_b[0]))

    y_prompt = out_p.reshape(batch, seq, D_MODEL)
    y_sample = out_s.reshape(dec_batch, dec_seq, D_MODEL)
    conv_state_prompt = u_tail[None, :, HIST - (CONV_W - 1):, :]
    u_s3 = u_s.reshape(dec_batch, dec_seq, C_CONV)
    conv_state_sample = jnp.concatenate([cache_conv[0][:, dec_seq:, :].astype(F32), u_s3], axis=1)[None]
    chunk_v_prompt = v_chunk[None]
    chunk_v_sample = v_s.reshape(1, dec_batch, dec_seq, C_GMLP)
    return (y_prompt, y_sample, conv_state_prompt, conv_state_sample, chunk_v_prompt, chunk_v_sample)
```

```python
import functools

import jax
import jax.numpy as jnp
from jax import lax
from jax.experimental import pallas as pl
from jax.experimental.pallas import tpu as pltpu

F32 = jnp.float32
BF16 = jnp.bfloat16

D_MODEL = 1024
C_CONV = 512
C_GMLP = 512
N_CONV_GROUPS = 8
CONV_W = 31
N_HEADS = 8
HEAD_DIM = C_GMLP // N_HEADS
CHUNK = 128
N_EXPERTS = 32
TOP_K = 4
D_FF = 1024
PLE_DIM = 256
SWIGLU_LIMIT = 7.0
SWIGLU_ALPHA = 1.702
LN_EPS = 1e-5
DEPTH = 1
DEEPNORM_ALPHA = (2.0 * DEPTH) ** 0.25

LANES = 128
SUBLANES = 8
MXU_DIM = 256
VMEM_LIMIT = 48 * 1024 * 1024
BIG_VMEM_LIMIT = 56 * 1024 * 1024

T_PROMPT = 512
PROMPT_SPLIT = 2
HIST = 32
CONV_STRIDE = 4
SEQS_PER_TILE = 64
SAMPLE_PITCH = 33
T_RANK = 1024
T_DISPATCH = 1024
T_ROWS = 256
TM = 512
TM_CHAIN = 256


def _dot(a, b):
    return jnp.dot(a, b, preferred_element_type=F32)


def _layer_norm(x, g, b):
    mu = jnp.mean(x, axis=-1, keepdims=True)
    xc = x - mu
    var = jnp.mean(xc * xc, axis=-1, keepdims=True)
    return xc * lax.rsqrt(var + LN_EPS) * g + b


def _split_bf16(a):
    hi = a.astype(BF16)
    lo = (a - hi.astype(F32)).astype(BF16)
    return hi, lo


def _group_mean(a, gmat_ref):
    hi, lo = _split_bf16(a)
    g = gmat_ref[...]
    outs = []
    for s in range(C_CONV // MXU_DIM):
        sl = slice(MXU_DIM * s, MXU_DIM * (s + 1))
        outs.append(_dot(hi[:, sl], g) + _dot(lo[:, sl], g))
    return jnp.concatenate(outs, axis=1)


def _group_norm_silu(y, gmat_ref, gn_g, gn_b):
    mu = _group_mean(y, gmat_ref)
    yc = y - mu
    var = _group_mean(yc * yc, gmat_ref)
    yn = yc * lax.rsqrt(var + LN_EPS) * gn_g + gn_b
    return yn * jax.nn.sigmoid(yn)


def _store_token_tiles(ref, val, tok0=0):
    n = val.shape[0]
    for c in range(D_MODEL // LANES):
        ref[pl.ds(tok0 * SUBLANES + c, n, stride=SUBLANES), :] = val[:, c * LANES:(c + 1) * LANES]


def _load_token_tiles(ref, n, lead=(), tok0=0):
    parts = [ref[lead + (pl.ds(tok0 * SUBLANES + c, n, stride=SUBLANES), slice(None))]
             for c in range(D_MODEL // LANES)]
    return jnp.concatenate(parts, axis=1)


def _front(x, w):
    xn = _layer_norm(x, w["ln_in_g"][...], w["ln_in_b"][...])
    z = _dot(xn.astype(BF16), w["w_in"][...])
    a_val = z[:, 0:C_CONV]
    a_gate = z[:, C_CONV:2 * C_CONV]
    g_u = z[:, 2 * C_CONV:2 * C_CONV + C_GMLP]
    g_v = z[:, 2 * C_CONV + C_GMLP:]
    u = a_val * jax.nn.sigmoid(a_gate)
    ug = jax.nn.gelu(g_u)
    v = _layer_norm(jax.nn.gelu(g_v), w["vn_g"][...], w["vn_b"][...])
    return xn, u, ug, v


def _tail(xn, y_a, y_b, w, h_ref, idx_ref, gate_ref, row0=0):
    rows = slice(row0, row0 + xn.shape[0])
    mix = _dot(y_a.astype(BF16), w["w_out"][0:C_CONV, :]) + _dot(y_b.astype(BF16), w["w_out"][C_CONV:, :])
    h = _layer_norm(DEEPNORM_ALPHA * xn + mix, w["ln1_g"][...], w["ln1_b"][...])
    hb, h_lo = _split_bf16(h)
    _store_token_tiles(h_ref, h, row0)

    pair = _dot(hb, w["wr_pair"][...])
    logits = pair[:, :LANES] + pair[:, LANES:] + _dot(h_lo, w["wr_hi"][...]) + w["b_r"][...]
    lane = lax.broadcasted_iota(jnp.int32, logits.shape, 1)
    lane_f = lane.astype(F32)
    vals = jnp.where(lane < N_EXPERTS, logits, -jnp.inf)
    tops, ids = [], []
    for _ in range(TOP_K):
        m = jnp.max(vals, axis=-1, keepdims=True)
        i = jnp.min(jnp.where(vals == m, lane_f, float(LANES)), axis=-1, keepdims=True)
        vals = jnp.where(lane_f == i, -jnp.inf, vals)
        tops.append(m)
        ids.append(i)
    exps = [jnp.exp(m - tops[0]) for m in tops]
    denom = exps[0] + exps[1] + exps[2] + exps[3]
    idx_out = jnp.zeros(logits.shape, F32)
    gate_out = jnp.zeros(logits.shape, F32)
    for k in range(TOP_K):
        idx_out = jnp.where(lane == k, ids[k], idx_out)
        gate_out = jnp.where(lane == k, exps[k] / denom, gate_out)
    idx_ref[rows, :] = idx_out.astype(jnp.int32)
    gate_ref[rows, :] = gate_out


_WEIGHT_NAMES = ("ln_in_g", "ln_in_b", "w_in", "vn_g", "vn_b", "gmat", "gn_g", "gn_b", "w_out", "ln1_g", "ln1_b",
                 "wr_hi", "wr_pair", "b_r")


def _prompt_branch(j, x_ref, w, cwb_ref, ws_ref, bs_ref, outs, tail_ref, vch_ref, ubuf, yslab, last_j):
    t = T_PROMPT
    n = t // PROMPT_SPLIT
    n_slabs = C_CONV // LANES

    @pl.when(j == 0)
    def _():
        ubuf[:, 0:HIST, :] = jnp.zeros((n_slabs, HIST, LANES), F32)

    @pl.when(j > 0)
    def _():
        ubuf[:, 0:HIST, :] = ubuf[:, t:t + HIST, :]

    def front(h):
        xn, u, ug, v = _front(x_ref[h * n:(h + 1) * n, :], w)
        for s in range(n_slabs):
            ubuf[s, HIST + h * n:HIST + (h + 1) * n, :] = u[:, s * LANES:(s + 1) * LANES]
        return xn, u, ug, v

    def conv(h):
        rows = CONV_STRIDE * SUBLANES
        first = HIST - (CONV_W - 1)
        for s in range(n_slabs):
            for c in range(h * n // rows, (h + 1) * n // rows):
                accs = [None] * CONV_STRIDE
                for shift in range(CONV_STRIDE + CONV_W - 1):
                    win = ubuf[s, pl.ds(first + c * rows + shift, SUBLANES, stride=CONV_STRIDE), :]
                    for ph in range(CONV_STRIDE):
                        k = shift - ph
                        if 0 <= k < CONV_W:
                            term = cwb_ref[k, s] * win
                            accs[ph] = term if accs[ph] is None else accs[ph] + term
                for ph in range(CONV_STRIDE):
                    yslab[s, pl.ds(c * rows + ph, SUBLANES, stride=CONV_STRIDE), :] = accs[ph]
        y_conv = jnp.concatenate([yslab[s, h * n:(h + 1) * n, :] for s in range(n_slabs)], axis=1)
        return _group_norm_silu(y_conv, w["gmat"], w["gn_g"][...], w["gn_b"][...])

    def spatial_gate(ug, v):
        lane = lax.broadcasted_iota(jnp.int32, (CHUNK, LANES), 1)
        mixed_chunks = []
        for c in range(n // CHUNK):
            vc = v[c * CHUNK:(c + 1) * CHUNK, :]
            parts = []
            for q in range(N_HEADS // 2):
                vp = vc[:, q * LANES:(q + 1) * LANES]
                rhs = jnp.concatenate([jnp.where(lane < HEAD_DIM, vp, 0.0), jnp.where(lane >= HEAD_DIM, vp, 0.0)],
                                      axis=0).astype(BF16)
                parts.append(_dot(ws_ref[q], rhs))
            mixed_chunks.append(jnp.concatenate(parts, axis=1) + bs_ref[...])
        return ug * jnp.concatenate(mixed_chunks, axis=0)

    def tail(h, f, y_a, y_b):
        _tail(f[0], y_a, y_b, w, *outs, row0=h * n)

    fronts = [front(0)]
    y_as = {}
    for h in range(PROMPT_SPLIT):
        if h + 1 < PROMPT_SPLIT:
            fronts.append(front(h + 1))
        y_as[h] = conv(h)
        if h > 0:
            tail(h - 1, fronts[h - 1], y_as.pop(h - 1), spatial_gate(fronts[h - 1][2], fronts[h - 1][3]))
    last = PROMPT_SPLIT - 1
    tail(last, fronts[last], y_as.pop(last), spatial_gate(fronts[last][2], fronts[last][3]))

    @pl.when(j == last_j)
    def _():
        tail_ref[...] = fronts[last][1][n - HIST:, :]
        vch_ref[...] = fronts[last][3][n - CHUNK:, :]


def _sample_branch(x_ref, cslab_ref, w, cwb_ref, convub_ref, gwb_ref, b8b_ref, outs, u_out_ref, v_out_ref,
                   uslab, vslab, yslab, mslab):
    xn, u, ug, v = _front(x_ref[...], w)
    u_out_ref[...] = u
    v_out_ref[...] = v
    n_slabs = C_CONV // LANES
    for s in range(n_slabs):
        uslab[s] = u[:, s * LANES:(s + 1) * LANES]
        vslab[s] = v[:, s * LANES:(s + 1) * LANES]
    first = HIST - (CONV_W - 1)
    n_cache = CONV_W - 1

    def group(g, carry):
        row0 = g * (SUBLANES * SUBLANES)
        for s in range(n_slabs):
            cache = [cslab_ref[pl.ds((g * (SUBLANES * SAMPLE_PITCH) + first + j) * n_slabs + s, SUBLANES,
                                     stride=SAMPLE_PITCH * n_slabs), :] for j in range(n_cache)]
            u_pos = [uslab[s, pl.ds(row0 + q, SUBLANES, stride=SUBLANES), :] for q in range(SUBLANES)]
            v_pos = [vslab[s, pl.ds(row0 + q, SUBLANES, stride=SUBLANES), :] for q in range(SUBLANES)]
            for t in range(SUBLANES):
                acc = None
                for k in range(n_cache - t):
                    term = cwb_ref[k, s] * cache[t + k]
                    acc = term if acc is None else acc + term
                mix = b8b_ref[t, s]
                for q in range(t + 1):
                    acc = acc + convub_ref[q, t, s] * u_pos[q]
                    mix = mix + gwb_ref[q, t, s] * v_pos[q]
                yslab[s, pl.ds(row0 + t, SUBLANES, stride=SUBLANES), :] = acc
                mslab[s, pl.ds(row0 + t, SUBLANES, stride=SUBLANES), :] = mix
        return carry

    lax.fori_loop(0, SEQS_PER_TILE // SUBLANES, group, 0)
    y_conv = jnp.concatenate([yslab[s] for s in range(n_slabs)], axis=1)
    mixed = jnp.concatenate([mslab[s] for s in range(n_slabs)], axis=1)
    y_a = _group_norm_silu(y_conv, w["gmat"], w["gn_g"][...], w["gn_b"][...])
    _tail(xn, y_a, ug * mixed, w, *outs)


def _mixer_body(*refs, n_prompt_tiles, tiles_per_seq):
    n_w = len(_WEIGHT_NAMES)
    xp_ref, xs_ref, cslab_ref = refs[:3]
    w = dict(zip(_WEIGHT_NAMES, refs[3:3 + n_w]))
    cwb_ref, ws_ref, bs_ref, convub_ref, gwb_ref, b8b_ref = refs[3 + n_w:9 + n_w]
    outs = refs[9 + n_w:12 + n_w]
    tail_ref, vch_ref, u_out_ref, v_out_ref = refs[12 + n_w:16 + n_w]
    ubuf, yslab, uslab, vslab, mslab = refs[16 + n_w:]
    step = pl.program_id(0)

    @pl.when(step < n_prompt_tiles)
    def _():
        _prompt_branch(lax.rem(step, tiles_per_seq), xp_ref, w, cwb_ref, ws_ref, bs_ref, outs, tail_ref,
                       vch_ref, ubuf, yslab, tiles_per_seq - 1)

    @pl.when(step >= n_prompt_tiles)
    def _():
        _sample_branch(xs_ref, cslab_ref, w, cwb_ref, convub_ref, gwb_ref, b8b_ref, outs, u_out_ref, v_out_ref,
                       uslab, vslab, yslab, mslab)


def _rank_body(idx_ref, tri_ref, before_ref, counts_ref, carry):
    i = pl.program_id(0)

    @pl.when(i == 0)
    def _():
        carry[...] = jnp.zeros(carry.shape, F32)

    idx = idx_ref[...]
    lane = lax.broadcasted_iota(jnp.int32, idx.shape, 1)
    multi = jnp.zeros(idx.shape, F32)
    for k in range(TOP_K):
        multi = multi + (lane == idx[:, k:k + 1]).astype(F32)
    before_ref[...] = (_dot(tri_ref[...], multi.astype(BF16)) + carry[...]).astype(jnp.int32)
    carry[...] = carry[...] + jnp.sum(multi, axis=0, keepdims=True)
    counts_ref[...] = carry[...].astype(jnp.int32)


def _pad_bits():
    b = TM // 2
    while b >= 1:
        yield b
        b //= 2


def _token_rows(t, n=1):
    return pl.ds(pl.multiple_of(t * SUBLANES, SUBLANES), n * SUBLANES)


def _dispatch_body(pad_off_ref, pad_n_ref, n_used_ref, dest_ref, h_ref, pp_ref, ps_ref, wg_ref, wp_ref, xs_ref, base_ref,
                   zbuf, sem, zsem, *, n_blocks, n_prompt_tiles):
    i = pl.program_id(0)

    def issue(r, carry):
        src = h_ref.at[_token_rows(r), :]
        for k in range(TOP_K):
            d = dest_ref[0, r * TOP_K + k]
            pltpu.make_async_copy(src, xs_ref.at[_token_rows(d), :], sem).start(priority=k % 2)
        return carry

    @pl.when(i == 0)
    def _():
        zbuf[...] = jnp.zeros(zbuf.shape, F32)

        def zero_copy(off, b):
            return pltpu.make_async_copy(zbuf.at[pl.ds(0, b * SUBLANES), :], xs_ref.at[_token_rows(off, b), :], zsem)

        def start_or_wait(cond, cp, wait):
            @pl.when(cond)
            def _():
                if wait:
                    cp.wait()
                else:
                    cp.start()

        for wait in (False, True):
            for e in range(N_EXPERTS):
                n = pad_n_ref[e]
                for b in _pad_bits():
                    start_or_wait((n & b) != 0, zero_copy(pad_off_ref[e] + (n & ~(2 * b - 1)), b), wait)

        half = TM // 2

        def tail_block(wait):
            def go(blk, carry):
                for s in range(TM // half):
                    cp = zero_copy(blk * TM + s * half, half)
                    cp.wait() if wait else cp.start()
                return carry
            return go

        lax.fori_loop(n_used_ref[0], n_blocks, tail_block(False), 0)
        lax.fori_loop(n_used_ref[0], n_blocks, tail_block(True), 0)

    rows = TM_CHAIN
    is_prompt = i < n_prompt_tiles
    for c in range(T_DISPATCH // rows):
        lax.fori_loop(c * rows, (c + 1) * rows, issue, 0, unroll=8)
        sl = slice(c * rows, (c + 1) * rows)
        h = _load_token_tiles(h_ref, rows, tok0=c * rows)
        p = jnp.where(is_prompt, pp_ref[sl, :], ps_ref[sl, :])
        ple = _dot(p.astype(BF16), wp_ref[...]) * jax.nn.sigmoid(_dot(h.astype(BF16), wg_ref[...]))
        base_ref[sl, :] = DEEPNORM_ALPHA * h + ple

    for _ in range(TOP_K):
        pltpu.make_async_copy(h_ref, xs_ref.at[_token_rows(0, T_DISPATCH), :], sem).wait()


def _expert_body(blk_e_ref, first_ref, slot_ref, next_ref, rows_ref, n_used_ref, x_ref, bu_ref, bd_ref, wu_hbm, wd_hbm,
                 y_ref, wu_f32, wd_f32, wu_bf, wd_bf, sems):
    i = pl.program_id(0)

    def weight_copies(e, s):
        return (pltpu.make_async_copy(wu_hbm.at[e], wu_f32.at[s], sems.at[s]),
                pltpu.make_async_copy(wd_hbm.at[e], wd_f32.at[s], sems.at[s]))

    @pl.when(i < n_used_ref[0])
    def _():
        s = slot_ref[i]

        @pl.when(first_ref[i] == 1)
        def _():
            @pl.when(i == 0)
            def _():
                for cp in weight_copies(blk_e_ref[i], s):
                    cp.start()

            for cp in weight_copies(blk_e_ref[i], s):
                cp.wait()
            nxt = next_ref[i]

            @pl.when(nxt >= 0)
            def _():
                for cp in weight_copies(nxt, 1 - s):
                    cp.start()

            chunk = D_MODEL // SUBLANES

            def cast_rows(c, carry):
                r = pl.multiple_of(c * chunk, chunk)
                wu_bf[pl.ds(r, chunk), :] = wu_f32[s, pl.ds(r, chunk), :].astype(BF16)
                wd_bf[pl.ds(r, chunk), :] = wd_f32[s, pl.ds(r, chunk), :].astype(BF16)
                return carry

            lax.fori_loop(0, SUBLANES, cast_rows, 0)

        def chain(c):
            x = _load_token_tiles(x_ref, TM_CHAIN, tok0=c * TM_CHAIN)
            hcat = _dot(x.astype(BF16), wu_bf[...]) + bu_ref[...]
            h_glu = jnp.minimum(hcat[:, :D_FF], SWIGLU_LIMIT)
            h_lin = jnp.clip(hcat[:, D_FF:], -SWIGLU_LIMIT, SWIGLU_LIMIT)
            act = h_glu * jax.nn.sigmoid(SWIGLU_ALPHA * h_glu) * (h_lin + 1.0)
            _store_token_tiles(y_ref, _dot(act.astype(BF16), wd_bf[...]) + bd_ref[...], tok0=c * TM_CHAIN)

        n_chains = TM // TM_CHAIN
        live = (rows_ref[i] + TM_CHAIN - 1) // TM_CHAIN
        for m in range(1, n_chains + 1):
            @pl.when(live == m)
            def _():
                for c in range(m):
                    chain(c)
                if m < n_chains:
                    y_ref[m * TM_CHAIN * SUBLANES:, :] = jnp.zeros(((n_chains - m) * TM_CHAIN * SUBLANES, LANES), F32)

    @pl.when(pl.program_id(0) >= n_used_ref[0])
    def _():
        y_ref[...] = jnp.zeros(y_ref.shape, F32)


def _combine_body(dest_ref, dest_next_ref, y_ref, base_ref, gate_ref, g_ref, b_ref, outp_ref, outs_ref, ybuf, sems, *,
                  n_prompt_tiles, n_tiles):
    i = pl.program_id(0)

    def issue_tile(d_ref, tile, slot):
        first = lax.rem(tile, T_DISPATCH // T_ROWS) * (T_ROWS * TOP_K)

        def issue(r, carry):
            for k in range(TOP_K):
                d = d_ref[0, first + r * TOP_K + k]
                pltpu.make_async_copy(y_ref.at[_token_rows(d), :], ybuf.at[slot * TOP_K + k, _token_rows(r), :],
                                      sems.at[slot]).start(priority=k % 2)
            return carry

        lax.fori_loop(0, T_ROWS, issue, 0, unroll=8)

    slot = lax.rem(i, 2)

    @pl.when(i == 0)
    def _():
        issue_tile(dest_ref, i, 0)

    @pl.when(i + 1 < n_tiles)
    def _():
        issue_tile(dest_next_ref, i + 1, 1 - slot)

    for k in range(TOP_K):
        pltpu.make_async_copy(y_ref.at[_token_rows(0, T_ROWS), :], ybuf.at[slot * TOP_K + k], sems.at[slot]).wait()

    gate = gate_ref[...]
    acc = base_ref[...]
    for k in range(TOP_K):
        acc = acc + gate[:, k:k + 1] * _load_token_tiles(ybuf, T_ROWS, lead=(slot * TOP_K + k,))
    out = _layer_norm(acc, g_ref[...], b_ref[...])

    @pl.when(i < n_prompt_tiles)
    def _():
        outp_ref[...] = out

    @pl.when(i >= n_prompt_tiles)
    def _():
        outs_ref[...] = out


def _full(shape):
    return pl.BlockSpec(shape, lambda *_: (0,) * len(shape), pipeline_mode=pl.Buffered(1))


def kernel(x_prompt, x_sample, cache_conv, p_prompt, p_sample, ln_in_g, ln_in_b, w_in, conv_w, gn_g, gn_b, vn_g, vn_b,
           w_spatial, b_spatial, w_out, ln1_g, ln1_b, w_router, b_router, w_up, b_up, w_down, b_down, w_ple,
           w_ple_gate, ln2_g, ln2_b):
    batch, seq, _ = x_prompt.shape
    dec_batch, dec_seq, _ = x_sample.shape
    assert w_in.shape[0] == DEPTH and dec_seq == SUBLANES and seq % T_PROMPT == 0
    n_prompt = batch * seq
    n_sample = dec_batch * dec_seq
    n_tok = n_prompt + n_sample
    t_s = SEQS_PER_TILE * dec_seq
    assert n_prompt % T_ROWS == 0 and n_sample % T_ROWS == 0 and n_tok % T_RANK == 0 and n_sample % t_s == 0
    assert n_tok % T_DISPATCH == 0

    row = lambda a: a.reshape(1, -1).astype(F32)
    gidx = jnp.arange(MXU_DIM) // (C_CONV // N_CONV_GROUPS)
    gmat = jnp.where(gidx[:, None] == gidx[None, :], 1.0 / (C_CONV // N_CONV_GROUPS), 0.0).astype(BF16)
    wr_pad = jnp.pad(w_router[0].astype(F32), ((0, 0), (0, LANES - N_EXPERTS)))
    wr_hi = wr_pad.astype(BF16)
    wr_lo = (wr_pad - wr_hi.astype(F32)).astype(BF16)
    weights = dict(
        ln_in_g=row(ln_in_g), ln_in_b=row(ln_in_b), w_in=w_in[0].astype(BF16), vn_g=row(vn_g[0]), vn_b=row(vn_b[0]),
        gmat=gmat, gn_g=row(gn_g[0]), gn_b=row(gn_b[0]), w_out=w_out[0].astype(BF16), ln1_g=row(ln1_g[0]),
        ln1_b=row(ln1_b[0]), wr_hi=wr_hi, wr_pair=jnp.concatenate([wr_hi, wr_lo], axis=1),
        b_r=jnp.pad(row(b_router[0]), ((0, 0), (0, LANES - N_EXPERTS))))
    w_list = [weights[n] for n in _WEIGHT_NAMES]
    w_specs = [_full(a.shape) for a in w_list]
    causal = jnp.tril(jnp.ones((CHUNK, CHUNK), bool))
    ws_m = jnp.where(causal[None], w_spatial[0], 0.0)
    ws_cat = jnp.concatenate([ws_m[0::2], ws_m[1::2]], axis=2).astype(BF16)
    bs_full = jnp.repeat(b_spatial[0].T.astype(F32), HEAD_DIM, axis=1)
    s_i = jnp.arange(SUBLANES)[:, None]
    t_i = jnp.arange(SUBLANES)[None, :]
    tap = jnp.clip(CONV_W - 1 - t_i + s_i, 0, CONV_W - 1)
    convu = jnp.where((s_i <= t_i)[:, :, None], conv_w[0].astype(F32)[tap], 0.0)
    gw8 = jnp.transpose(ws_m[:, :SUBLANES, :SUBLANES], (2, 1, 0))
    gw8 = jnp.repeat(gw8.astype(F32), HEAD_DIM, axis=2)
    b8 = bs_full[:SUBLANES]
    n_slabs = C_CONV // LANES
    slab_rows = lambda a: jnp.broadcast_to(a.reshape(a.shape[:-1] + (n_slabs, 1, LANES)),
                                           a.shape[:-1] + (n_slabs, SUBLANES, LANES))
    convub, gwb, b8b = slab_rows(convu), slab_rows(gw8), slab_rows(b8)
    first = HIST - (CONV_W - 1)
    cslab = jnp.pad(cache_conv[0].astype(F32), ((0, 0), (first, SAMPLE_PITCH - first - (CONV_W - 1)), (0, 0)))
    cslab = cslab.reshape(dec_batch * SAMPLE_PITCH * n_slabs, LANES)

    cparams = lambda sem: pltpu.CompilerParams(dimension_semantics=sem, vmem_limit_bytes=VMEM_LIMIT)

    nj = seq // T_PROMPT
    npt = n_prompt // T_PROMPT
    assert t_s == T_PROMPT
    pstep = lambda i: jnp.minimum(i, npt - 1)
    sstep = lambda i: jnp.maximum(i - npt, 0)
    tok_block = lambda width: pl.BlockSpec((T_PROMPT, width), lambda i: (i, 0))
    cwb = slab_rows(conv_w[0].astype(F32))
    tables = [cwb, ws_cat, bs_full, convub, gwb, b8b]
    h_all, idx_all, gate_all, u_tail, v_chunk, u_s, v_s = pl.pallas_call(
        functools.partial(_mixer_body, n_prompt_tiles=npt, tiles_per_seq=nj),
        grid=(npt + n_sample // t_s,),
        in_specs=[pl.BlockSpec((None, T_PROMPT, D_MODEL), lambda i: (pstep(i) // nj, pstep(i) % nj, 0)),
                  pl.BlockSpec((t_s, D_MODEL), lambda i: (sstep(i), 0)),
                  pl.BlockSpec((SEQS_PER_TILE * SAMPLE_PITCH * n_slabs, LANES), lambda i: (sstep(i), 0),
                               pipeline_mode=pl.Buffered(1))]
                 + w_specs + [_full(a.shape) for a in tables],
        out_specs=[pl.BlockSpec((T_PROMPT * SUBLANES, LANES), lambda i: (i, 0)),
                   tok_block(LANES), tok_block(LANES),
                   pl.BlockSpec((None, HIST, C_CONV), lambda i: (pstep(i) // nj, 0, 0)),
                   pl.BlockSpec((None, CHUNK, C_GMLP), lambda i: (pstep(i) // nj, 0, 0)),
                   pl.BlockSpec((t_s, C_CONV), lambda i: (sstep(i), 0)),
                   pl.BlockSpec((t_s, C_GMLP), lambda i: (sstep(i), 0))],
        out_shape=(jax.ShapeDtypeStruct((n_tok * SUBLANES, LANES), F32),
                   jax.ShapeDtypeStruct((n_tok, LANES), jnp.int32), jax.ShapeDtypeStruct((n_tok, LANES), F32),
                   jax.ShapeDtypeStruct((batch, HIST, C_CONV), F32), jax.ShapeDtypeStruct((batch, CHUNK, C_GMLP), F32),
                   jax.ShapeDtypeStruct((n_sample, C_CONV), F32), jax.ShapeDtypeStruct((n_sample, C_GMLP), F32)),
        scratch_shapes=[pltpu.VMEM((n_slabs, T_PROMPT + HIST, LANES), F32)]
                       + [pltpu.VMEM((n_slabs, T_PROMPT, LANES), F32)] * 4,
        compiler_params=pltpu.CompilerParams(dimension_semantics=("arbitrary",), vmem_limit_bytes=BIG_VMEM_LIMIT),
        name="mixer",
    )(x_prompt, x_sample.reshape(n_sample, D_MODEL), cslab, *w_list, *tables)

    tri = (jnp.arange(T_RANK)[:, None] > jnp.arange(T_RANK)[None, :]).astype(BF16)
    before_all, counts = pl.pallas_call(
        _rank_body,
        grid=(n_tok // T_RANK,),
        in_specs=[pl.BlockSpec((T_RANK, LANES), lambda i: (i, 0)), _full(tri.shape)],
        out_specs=[pl.BlockSpec((T_RANK, LANES), lambda i: (i, 0)), _full((1, LANES))],
        out_shape=(jax.ShapeDtypeStruct((n_tok, LANES), jnp.int32), jax.ShapeDtypeStruct((1, LANES), jnp.int32)),
        scratch_shapes=[pltpu.VMEM((1, LANES), F32)],
        compiler_params=cparams(("arbitrary",)),
        name="rank",
    )(idx_all, tri)

    n_assign = n_tok * TOP_K
    n_blocks = n_assign // TM + N_EXPERTS
    cap = n_blocks * TM
    cnt = counts[0, :N_EXPERTS]
    padded = (cnt + TM - 1) // TM * TM
    p_end = jnp.cumsum(padded)
    p_start = p_end - padded
    e_ids = jnp.arange(N_EXPERTS, dtype=jnp.int32)
    lookup = lambda table, ids: jnp.sum(jnp.where(ids[..., None] == e_ids, table, 0), axis=-1)
    slot_table = p_start[None, :] + before_all[:, :N_EXPERTS]
    dest = jnp.sum(jnp.where(idx_all[:, :TOP_K, None] == e_ids, slot_table[:, None, :], 0), axis=-1).astype(jnp.int32)
    dest_tiles = dest.reshape(n_tok // T_DISPATCH, 1, T_DISPATCH * TOP_K)
    blk_row = jnp.arange(n_blocks, dtype=jnp.int32) * TM
    blk_e = jnp.minimum(jnp.sum(p_end[None, :] <= blk_row[:, None], axis=1), N_EXPERTS - 1).astype(jnp.int32)
    n_used = (p_end[-1:] // TM).astype(jnp.int32)
    pad_off = (p_start + cnt).astype(jnp.int32)
    pad_n = (padded - cnt).astype(jnp.int32)
    used = cnt > 0
    slot_e = (jnp.cumsum(used.astype(jnp.int32)) - 1) & 1
    later_used = jnp.where(used[None, :] & (e_ids[None, :] > e_ids[:, None]), e_ids[None, :], N_EXPERTS)
    next_e = jnp.min(later_used, axis=1)
    next_e = jnp.where(next_e < N_EXPERTS, next_e, -1).astype(jnp.int32)
    blk_first = ((blk_row == lookup(p_start, blk_e)) & (blk_row < p_end[-1])).astype(jnp.int32)
    blk_rows = jnp.clip(lookup(p_start + cnt, blk_e) - blk_row, 0, TM).astype(jnp.int32)
    blk_slot = lookup(slot_e, blk_e).astype(jnp.int32)
    blk_next = lookup(next_e, blk_e).astype(jnp.int32)


    assert seq % T_DISPATCH == 0 and n_sample % T_DISPATCH == 0
    npd = n_prompt // T_DISPATCH
    per_seq = seq // T_DISPATCH
    pd = lambda i: jnp.minimum(i, npd - 1)
    w_gate = w_ple_gate[0].astype(BF16)
    w_ple_b = w_ple[0].astype(BF16)
    x_sorted, base_all = pl.pallas_call(
        functools.partial(_dispatch_body, n_blocks=n_blocks, n_prompt_tiles=npd),
        grid_spec=pltpu.PrefetchScalarGridSpec(
            num_scalar_prefetch=3,
            grid=(n_tok // T_DISPATCH,),
            in_specs=[pl.BlockSpec((None, 1, T_DISPATCH * TOP_K), lambda i, *_: (i, 0, 0), memory_space=pltpu.SMEM),
                      pl.BlockSpec((T_DISPATCH * SUBLANES, LANES), lambda i, *_: (i, 0)),
                      pl.BlockSpec((None, T_DISPATCH, PLE_DIM), lambda i, *_: (pd(i) // per_seq, pd(i) % per_seq, 0)),
                      pl.BlockSpec((T_DISPATCH, PLE_DIM), lambda i, *_: (jnp.maximum(i - npd, 0), 0)),
                      _full(w_gate.shape), _full(w_ple_b.shape)],
            out_specs=[pl.BlockSpec(memory_space=pl.ANY),
                       pl.BlockSpec((T_DISPATCH, D_MODEL), lambda i, *_: (i, 0))],
            scratch_shapes=[pltpu.VMEM((TM // 2 * SUBLANES, LANES), F32), pltpu.SemaphoreType.DMA,
                            pltpu.SemaphoreType.DMA]),
        out_shape=(jax.ShapeDtypeStruct((cap * SUBLANES, LANES), F32), jax.ShapeDtypeStruct((n_tok, D_MODEL), F32)),
        compiler_params=cparams(("arbitrary",)),
        name="dispatch",
    )(pad_off, pad_n, n_used, dest_tiles, h_all, p_prompt[0], p_sample[0].reshape(n_sample, PLE_DIM), w_gate, w_ple_b)

    last = lambda i, nu: jnp.minimum(i, nu[0] - 1)
    y_sorted = pl.pallas_call(
        _expert_body,
        grid_spec=pltpu.PrefetchScalarGridSpec(
            num_scalar_prefetch=6,
            grid=(n_blocks,),
            in_specs=[pl.BlockSpec((TM * SUBLANES, LANES), lambda i, be, bf, bs, bn, br, nu: (last(i, nu), 0)),
                      pl.BlockSpec((None, 1, 2 * D_FF), lambda i, be, bf, bs, bn, br, nu: (be[last(i, nu)], 0, 0)),
                      pl.BlockSpec((None, 1, D_MODEL), lambda i, be, bf, bs, bn, br, nu: (be[last(i, nu)], 0, 0)),
                      pl.BlockSpec(memory_space=pl.ANY), pl.BlockSpec(memory_space=pl.ANY)],
            out_specs=pl.BlockSpec((TM * SUBLANES, LANES), lambda i, be, bf, bs, bn, br, nu: (i, 0)),
            scratch_shapes=[pltpu.VMEM((2, D_MODEL, 2 * D_FF), F32), pltpu.VMEM((2, D_FF, D_MODEL), F32),
                            pltpu.VMEM((D_MODEL, 2 * D_FF), BF16), pltpu.VMEM((D_FF, D_MODEL), BF16),
                            pltpu.SemaphoreType.DMA((2,))]),
        out_shape=jax.ShapeDtypeStruct((cap * SUBLANES, LANES), F32),
        compiler_params=pltpu.CompilerParams(dimension_semantics=("arbitrary",), vmem_limit_bytes=BIG_VMEM_LIMIT),
        name="experts",
    )(blk_e, blk_first, blk_slot, blk_next, blk_rows, n_used, x_sorted, b_up[0].astype(F32)[:, None, :],
      b_down[0].astype(F32)[:, None, :], w_up[0].astype(F32), w_down[0].astype(F32))

    npt = n_prompt // T_ROWS
    n_tiles = n_tok // T_ROWS
    per = T_DISPATCH // T_ROWS
    dest_spec = pl.BlockSpec((None, 1, T_DISPATCH * TOP_K), lambda i: (i // per, 0, 0), memory_space=pltpu.SMEM)
    dest_next_spec = pl.BlockSpec((None, 1, T_DISPATCH * TOP_K),
                                  lambda i: (jnp.minimum(i + 1, n_tiles - 1) // per, 0, 0), memory_space=pltpu.SMEM)
    out_p, out_s = pl.pallas_call(
        functools.partial(_combine_body, n_prompt_tiles=npt, n_tiles=n_tiles),
        grid=(n_tiles,),
        in_specs=[dest_spec, dest_next_spec, pl.BlockSpec(memory_space=pl.ANY),
                  pl.BlockSpec((T_ROWS, D_MODEL), lambda i: (i, 0)), pl.BlockSpec((T_ROWS, LANES), lambda i: (i, 0)),
                  _full((1, D_MODEL)), _full((1, D_MODEL))],
        out_specs=[pl.BlockSpec((T_ROWS, D_MODEL), lambda i: (jnp.minimum(i, npt - 1), 0)),
                   pl.BlockSpec((T_ROWS, D_MODEL), lambda i: (jnp.maximum(i - npt, 0), 0))],
        out_shape=(jax.ShapeDtypeStruct((n_prompt, D_MODEL), F32), jax.ShapeDtypeStruct((n_sample, D_MODEL), F32)),
        scratch_shapes=[pltpu.VMEM((2 * TOP_K, T_ROWS * SUBLANES, LANES), F32), pltpu.SemaphoreType.DMA((2,))],
        compiler_params=cparams(("arbitrary",)),
        name="combine",
    )(dest_tiles, dest_tiles, y_sorted, base_all, gate_all, row(ln2_g[0]), row(ln2_b[0]))

    y_prompt = out_p.reshape(batch, seq, D_MODEL)
    y_sample = out_s.reshape(dec_batch, dec_seq, D_MODEL)
    conv_state_prompt = u_tail[None, :, HIST - (CONV_W - 1):, :]
    u_s3 = u_s.reshape(dec_batch, dec_seq, C_CONV)
    conv_state_sample = jnp.concatenate([cache_conv[0][:, dec_seq:, :].astype(F32), u_s3], axis=1)[None]
    chunk_v_prompt = v_chunk[None]
    chunk_v_sample = v_s.reshape(1, dec_batch, dec_seq, C_GMLP)
    return (y_prompt, y_sample, conv_state_prompt, conv_state_sample, chunk_v_prompt, chunk_v_sample)
```

```python
import functools

import jax
import jax.numpy as jnp
from jax import lax
from jax.experimental import pallas as pl
from jax.experimental.pallas import tpu as pltpu

F32 = jnp.float32
BF16 = jnp.bfloat16

D_MODEL = 1024
C_CONV = 512
C_GMLP = 512
N_CONV_GROUPS = 8
CONV_W = 31
N_HEADS = 8
HEAD_DIM = C_GMLP // N_HEADS
CHUNK = 128
N_EXPERTS = 32
TOP_K = 4
D_FF = 1024
PLE_DIM = 256
SWIGLU_LIMIT = 7.0
SWIGLU_ALPHA = 1.702
LN_EPS = 1e-5
DEPTH = 1
DEEPNORM_ALPHA = (2.0 * DEPTH) ** 0.25

LANES = 128
SUBLANES = 8
MXU_DIM = 256
VMEM_LIMIT = 48 * 1024 * 1024
BIG_VMEM_LIMIT = 56 * 1024 * 1024

T_PROMPT = 512
PROMPT_SPLIT = 2
HIST = 32
CONV_STRIDE = 4
SEQS_PER_TILE = 64
SAMPLE_PITCH = CONV_W - 1
T_RANK = 1024
T_DISPATCH = 1024
T_ROWS = 256
TM = 512
TM_CHAIN = 256


def _dot(a, b):
    return jnp.dot(a, b, preferred_element_type=F32)


def _layer_norm(x, g, b):
    mu = jnp.mean(x, axis=-1, keepdims=True)
    xc = x - mu
    var = jnp.mean(xc * xc, axis=-1, keepdims=True)
    return xc * lax.rsqrt(var + LN_EPS) * g + b


def _split_bf16(a):
    hi = a.astype(BF16)
    lo = (a - hi.astype(F32)).astype(BF16)
    return hi, lo


def _group_mean(a, gmat_ref):
    hi, lo = _split_bf16(a)
    g = gmat_ref[...]
    outs = []
    for s in range(C_CONV // MXU_DIM):
        sl = slice(MXU_DIM * s, MXU_DIM * (s + 1))
        outs.append(_dot(hi[:, sl], g) + _dot(lo[:, sl], g))
    return jnp.concatenate(outs, axis=1)


def _group_norm_silu(y, gmat_ref, gn_g, gn_b):
    mu = _group_mean(y, gmat_ref)
    yc = y - mu
    var = _group_mean(yc * yc, gmat_ref)
    yn = yc * lax.rsqrt(var + LN_EPS) * gn_g + gn_b
    return yn * jax.nn.sigmoid(yn)


def _store_token_tiles(ref, val, tok0=0):
    n = val.shape[0]
    for c in range(D_MODEL // LANES):
        ref[pl.ds(tok0 * SUBLANES + c, n, stride=SUBLANES), :] = val[:, c * LANES:(c + 1) * LANES]


def _load_token_tiles(ref, n, lead=(), tok0=0):
    parts = [ref[lead + (pl.ds(tok0 * SUBLANES + c, n, stride=SUBLANES), slice(None))]
             for c in range(D_MODEL // LANES)]
    return jnp.concatenate(parts, axis=1)


def _front(x, w):
    xn = _layer_norm(x, w["ln_in_g"][...], w["ln_in_b"][...])
    z = _dot(xn.astype(BF16), w["w_in"][...])
    a_val = z[:, 0:C_CONV]
    a_gate = z[:, C_CONV:2 * C_CONV]
    g_u = z[:, 2 * C_CONV:2 * C_CONV + C_GMLP]
    g_v = z[:, 2 * C_CONV + C_GMLP:]
    u = a_val * jax.nn.sigmoid(a_gate)
    ug = jax.nn.gelu(g_u)
    v = _layer_norm(jax.nn.gelu(g_v), w["vn_g"][...], w["vn_b"][...])
    return xn, u, ug, v


def _tail(xn, y_a, y_b, w, h_ref, idx_ref, gate_ref, row0=0):
    rows = slice(row0, row0 + xn.shape[0])
    mix = _dot(y_a.astype(BF16), w["w_out"][0:C_CONV, :]) + _dot(y_b.astype(BF16), w["w_out"][C_CONV:, :])
    h = _layer_norm(DEEPNORM_ALPHA * xn + mix, w["ln1_g"][...], w["ln1_b"][...])
    hb, h_lo = _split_bf16(h)
    _store_token_tiles(h_ref, h, row0)

    pair = _dot(hb, w["wr_pair"][...])
    logits = pair[:, :LANES] + pair[:, LANES:] + _dot(h_lo, w["wr_hi"][...]) + w["b_r"][...]
    lane = lax.broadcasted_iota(jnp.int32, logits.shape, 1)
    lane_f = lane.astype(F32)
    vals = jnp.where(lane < N_EXPERTS, logits, -jnp.inf)
    tops, ids = [], []
    for _ in range(TOP_K):
        m = jnp.max(vals, axis=-1, keepdims=True)
        i = jnp.min(jnp.where(vals == m, lane_f, float(LANES)), axis=-1, keepdims=True)
        vals = jnp.where(lane_f == i, -jnp.inf, vals)
        tops.append(m)
        ids.append(i)
    exps = [jnp.exp(m - tops[0]) for m in tops]
    denom = exps[0] + exps[1] + exps[2] + exps[3]
    idx_out = jnp.zeros(logits.shape, F32)
    gate_out = jnp.zeros(logits.shape, F32)
    for k in range(TOP_K):
        idx_out = jnp.where(lane == k, ids[k], idx_out)
        gate_out = jnp.where(lane == k, exps[k] / denom, gate_out)
    idx_ref[rows, :] = idx_out.astype(jnp.int32)
    gate_ref[rows, :] = gate_out


_WEIGHT_NAMES = ("ln_in_g", "ln_in_b", "w_in", "vn_g", "vn_b", "gmat", "gn_g", "gn_b", "w_out", "ln1_g", "ln1_b",
                 "wr_hi", "wr_pair", "b_r")


def _prompt_branch(j, x_ref, w, cwb_ref, ws_ref, bs_ref, outs, tail_ref, vch_ref, ubuf, yslab, last_j):
    t = T_PROMPT
    n = t // PROMPT_SPLIT
    n_slabs = C_CONV // LANES

    @pl.when(j == 0)
    def _():
        ubuf[:, 0:HIST, :] = jnp.zeros((n_slabs, HIST, LANES), F32)

    @pl.when(j > 0)
    def _():
        ubuf[:, 0:HIST, :] = ubuf[:, t:t + HIST, :]

    def front(h):
        xn, u, ug, v = _front(x_ref[h * n:(h + 1) * n, :], w)
        for s in range(n_slabs):
            ubuf[s, HIST + h * n:HIST + (h + 1) * n, :] = u[:, s * LANES:(s + 1) * LANES]
        return xn, u, ug, v

    def conv(h):
        rows = CONV_STRIDE * SUBLANES
        first = HIST - (CONV_W - 1)
        for s in range(n_slabs):
            for c in range(h * n // rows, (h + 1) * n // rows):
                accs = [None] * CONV_STRIDE
                for shift in range(CONV_STRIDE + CONV_W - 1):
                    win = ubuf[s, pl.ds(first + c * rows + shift, SUBLANES, stride=CONV_STRIDE), :]
                    for ph in range(CONV_STRIDE):
                        k = shift - ph
                        if 0 <= k < CONV_W:
                            term = cwb_ref[k, s] * win
                            accs[ph] = term if accs[ph] is None else accs[ph] + term
                for ph in range(CONV_STRIDE):
                    yslab[s, pl.ds(c * rows + ph, SUBLANES, stride=CONV_STRIDE), :] = accs[ph]
        y_conv = jnp.concatenate([yslab[s, h * n:(h + 1) * n, :] for s in range(n_slabs)], axis=1)
        return _group_norm_silu(y_conv, w["gmat"], w["gn_g"][...], w["gn_b"][...])

    def spatial_gate(ug, v):
        lane = lax.broadcasted_iota(jnp.int32, (CHUNK, LANES), 1)
        mixed_chunks = []
        for c in range(n // CHUNK):
            vc = v[c * CHUNK:(c + 1) * CHUNK, :]
            parts = []
            for q in range(N_HEADS // 2):
                vp = vc[:, q * LANES:(q + 1) * LANES]
                rhs = jnp.concatenate([jnp.where(lane < HEAD_DIM, vp, 0.0), jnp.where(lane >= HEAD_DIM, vp, 0.0)],
                                      axis=0).astype(BF16)
                parts.append(_dot(ws_ref[q], rhs))
            mixed_chunks.append(jnp.concatenate(parts, axis=1) + bs_ref[...])
        return ug * jnp.concatenate(mixed_chunks, axis=0)

    def tail(h, f, y_a, y_b):
        _tail(f[0], y_a, y_b, w, *outs, row0=h * n)

    fronts = [front(0)]
    y_as = {}
    for h in range(PROMPT_SPLIT):
        if h + 1 < PROMPT_SPLIT:
            fronts.append(front(h + 1))
        y_as[h] = conv(h)
        if h > 0:
            tail(h - 1, fronts[h - 1], y_as.pop(h - 1), spatial_gate(fronts[h - 1][2], fronts[h - 1][3]))
    last = PROMPT_SPLIT - 1
    tail(last, fronts[last], y_as.pop(last), spatial_gate(fronts[last][2], fronts[last][3]))

    @pl.when(j == last_j)
    def _():
        tail_ref[...] = fronts[last][1][n - HIST:, :]
        vch_ref[...] = fronts[last][3][n - CHUNK:, :]


def _sample_branch(x_ref, cslab_ref, w, cwb_ref, convub_ref, gwb_ref, b8b_ref, outs, u_out_ref, v_out_ref,
                   uslab, vslab, yslab, mslab):
    xn, u, ug, v = _front(x_ref[...], w)
    u_out_ref[...] = u
    v_out_ref[...] = v
    n_slabs = C_CONV // LANES
    for s in range(n_slabs):
        uslab[s] = u[:, s * LANES:(s + 1) * LANES]
        vslab[s] = v[:, s * LANES:(s + 1) * LANES]
    n_cache = CONV_W - 1

    def group(g, carry):
        row0 = g * (SUBLANES * SUBLANES)
        for s in range(n_slabs):
            cache = [cslab_ref[pl.ds((g * (SUBLANES * SAMPLE_PITCH) + j) * n_slabs + s, SUBLANES,
                                     stride=SAMPLE_PITCH * n_slabs), :] for j in range(n_cache)]
            u_pos = [uslab[s, pl.ds(row0 + q, SUBLANES, stride=SUBLANES), :] for q in range(SUBLANES)]
            v_pos = [vslab[s, pl.ds(row0 + q, SUBLANES, stride=SUBLANES), :] for q in range(SUBLANES)]
            for t in range(SUBLANES):
                acc = None
                for k in range(n_cache - t):
                    term = cwb_ref[k, s] * cache[t + k]
                    acc = term if acc is None else acc + term
                mix = b8b_ref[t, s]
                for q in range(t + 1):
                    acc = acc + convub_ref[q, t, s] * u_pos[q]
                    mix = mix + gwb_ref[q, t, s] * v_pos[q]
                yslab[s, pl.ds(row0 + t, SUBLANES, stride=SUBLANES), :] = acc
                mslab[s, pl.ds(row0 + t, SUBLANES, stride=SUBLANES), :] = mix
        return carry

    lax.fori_loop(0, SEQS_PER_TILE // SUBLANES, group, 0)
    y_conv = jnp.concatenate([yslab[s] for s in range(n_slabs)], axis=1)
    mixed = jnp.concatenate([mslab[s] for s in range(n_slabs)], axis=1)
    y_a = _group_norm_silu(y_conv, w["gmat"], w["gn_g"][...], w["gn_b"][...])
    _tail(xn, y_a, ug * mixed, w, *outs)


def _mixer_body(*refs, n_prompt_tiles, tiles_per_seq):
    n_w = len(_WEIGHT_NAMES)
    xp_ref, xs_ref, cslab_ref = refs[:3]
    w = dict(zip(_WEIGHT_NAMES, refs[3:3 + n_w]))
    cwb_ref, ws_ref, bs_ref, convub_ref, gwb_ref, b8b_ref = refs[3 + n_w:9 + n_w]
    outs = refs[9 + n_w:12 + n_w]
    tail_ref, vch_ref, u_out_ref, v_out_ref = refs[12 + n_w:16 + n_w]
    ubuf, yslab, uslab, vslab, mslab = refs[16 + n_w:]
    step = pl.program_id(0)

    @pl.when(step < n_prompt_tiles)
    def _():
        _prompt_branch(lax.rem(step, tiles_per_seq), xp_ref, w, cwb_ref, ws_ref, bs_ref, outs, tail_ref,
                       vch_ref, ubuf, yslab, tiles_per_seq - 1)

    @pl.when(step >= n_prompt_tiles)
    def _():
        _sample_branch(xs_ref, cslab_ref, w, cwb_ref, convub_ref, gwb_ref, b8b_ref, outs, u_out_ref, v_out_ref,
                       uslab, vslab, yslab, mslab)


def _rank_body(idx_ref, tri_ref, before_ref, counts_ref, carry):
    i = pl.program_id(0)

    @pl.when(i == 0)
    def _():
        carry[...] = jnp.zeros(carry.shape, F32)

    idx = idx_ref[...]
    lane = lax.broadcasted_iota(jnp.int32, idx.shape, 1)
    multi = jnp.zeros(idx.shape, F32)
    for k in range(TOP_K):
        multi = multi + (lane == idx[:, k:k + 1]).astype(F32)
    before_ref[...] = (_dot(tri_ref[...], multi.astype(BF16)) + carry[...]).astype(jnp.int32)
    carry[...] = carry[...] + jnp.sum(multi, axis=0, keepdims=True)
    counts_ref[...] = carry[...].astype(jnp.int32)


def _pad_bits():
    b = TM // 2
    while b >= 1:
        yield b
        b //= 2


def _token_rows(t, n=1):
    return pl.ds(pl.multiple_of(t * SUBLANES, SUBLANES), n * SUBLANES)


def _dispatch_body(pad_off_ref, pad_n_ref, n_used_ref, dest_ref, h_ref, pp_ref, ps_ref, wg_ref, wp_ref, xs_ref, base_ref,
                   zbuf, sem, zsem, *, n_blocks, n_prompt_tiles):
    i = pl.program_id(0)

    def issue(r, carry):
        src = h_ref.at[_token_rows(r), :]
        for k in range(TOP_K):
            d = dest_ref[0, r * TOP_K + k]
            pltpu.make_async_copy(src, xs_ref.at[_token_rows(d), :], sem).start(priority=k % 2)
        return carry

    @pl.when(i == 0)
    def _():
        zbuf[...] = jnp.zeros(zbuf.shape, F32)

        def zero_copy(off, b):
            return pltpu.make_async_copy(zbuf.at[pl.ds(0, b * SUBLANES), :], xs_ref.at[_token_rows(off, b), :], zsem)

        def start_or_wait(cond, cp, wait):
            @pl.when(cond)
            def _():
                if wait:
                    cp.wait()
                else:
                    cp.start()

        for wait in (False, True):
            for e in range(N_EXPERTS):
                n = pad_n_ref[e]
                for b in _pad_bits():
                    start_or_wait((n & b) != 0, zero_copy(pad_off_ref[e] + (n & ~(2 * b - 1)), b), wait)

        half = TM // 2

        def tail_block(wait):
            def go(blk, carry):
                for s in range(TM // half):
                    cp = zero_copy(blk * TM + s * half, half)
                    cp.wait() if wait else cp.start()
                return carry
            return go

        lax.fori_loop(n_used_ref[0], n_blocks, tail_block(False), 0)
        lax.fori_loop(n_used_ref[0], n_blocks, tail_block(True), 0)

    rows = TM_CHAIN
    is_prompt = i < n_prompt_tiles
    for c in range(T_DISPATCH // rows):
        lax.fori_loop(c * rows, (c + 1) * rows, issue, 0, unroll=8)
        sl = slice(c * rows, (c + 1) * rows)
        h = _load_token_tiles(h_ref, rows, tok0=c * rows)
        p = jnp.where(is_prompt, pp_ref[sl, :], ps_ref[sl, :])
        ple = _dot(p.astype(BF16), wp_ref[...]) * jax.nn.sigmoid(_dot(h.astype(BF16), wg_ref[...]))
        base_ref[sl, :] = DEEPNORM_ALPHA * h + ple

    for _ in range(TOP_K):
        pltpu.make_async_copy(h_ref, xs_ref.at[_token_rows(0, T_DISPATCH), :], sem).wait()


def _expert_body(blk_e_ref, first_ref, slot_ref, next_ref, rows_ref, n_used_ref, x_ref, bu_ref, bd_ref, wu_hbm, wd_hbm,
                 y_ref, wu_f32, wd_f32, wu_bf, wd_bf, sems):
    i = pl.program_id(0)

    def weight_copies(e, s):
        return (pltpu.make_async_copy(wu_hbm.at[e], wu_f32.at[s], sems.at[s]),
                pltpu.make_async_copy(wd_hbm.at[e], wd_f32.at[s], sems.at[s]))

    @pl.when(i < n_used_ref[0])
    def _():
        s = slot_ref[i]

        @pl.when(first_ref[i] == 1)
        def _():
            @pl.when(i == 0)
            def _():
                for cp in weight_copies(blk_e_ref[i], s):
                    cp.start()

            for cp in weight_copies(blk_e_ref[i], s):
                cp.wait()
            nxt = next_ref[i]

            @pl.when(nxt >= 0)
            def _():
                for cp in weight_copies(nxt, 1 - s):
                    cp.start()

            chunk = D_MODEL // SUBLANES

            def cast_rows(c, carry):
                r = pl.multiple_of(c * chunk, chunk)
                wu_bf[pl.ds(r, chunk), :] = wu_f32[s, pl.ds(r, chunk), :].astype(BF16)
                wd_bf[pl.ds(r, chunk), :] = wd_f32[s, pl.ds(r, chunk), :].astype(BF16)
                return carry

            lax.fori_loop(0, SUBLANES, cast_rows, 0)

        def chain(c):
            x = _load_token_tiles(x_ref, TM_CHAIN, tok0=c * TM_CHAIN)
            hcat = _dot(x.astype(BF16), wu_bf[...]) + bu_ref[...]
            h_glu = jnp.minimum(hcat[:, :D_FF], SWIGLU_LIMIT)
            h_lin = jnp.clip(hcat[:, D_FF:], -SWIGLU_LIMIT, SWIGLU_LIMIT)
            act = h_glu * jax.nn.sigmoid(SWIGLU_ALPHA * h_glu) * (h_lin + 1.0)
            _store_token_tiles(y_ref, _dot(act.astype(BF16), wd_bf[...]) + bd_ref[...], tok0=c * TM_CHAIN)

        n_chains = TM // TM_CHAIN
        live = (rows_ref[i] + TM_CHAIN - 1) // TM_CHAIN
        for m in range(1, n_chains + 1):
            @pl.when(live == m)
            def _():
                for c in range(m):
                    chain(c)
                if m < n_chains:
                    y_ref[m * TM_CHAIN * SUBLANES:, :] = jnp.zeros(((n_chains - m) * TM_CHAIN * SUBLANES, LANES), F32)

    @pl.when(pl.program_id(0) >= n_used_ref[0])
    def _():
        y_ref[...] = jnp.zeros(y_ref.shape, F32)


def _combine_body(dest_ref, dest_next_ref, y_ref, base_ref, gate_ref, g_ref, b_ref, outp_ref, outs_ref, ybuf, sems, *,
                  n_prompt_tiles, n_tiles):
    i = pl.program_id(0)

    def issue_tile(d_ref, tile, slot):
        first = lax.rem(tile, T_DISPATCH // T_ROWS) * (T_ROWS * TOP_K)

        def issue(r, carry):
            for k in range(TOP_K):
                d = d_ref[0, first + r * TOP_K + k]
                pltpu.make_async_copy(y_ref.at[_token_rows(d), :], ybuf.at[slot * TOP_K + k, _token_rows(r), :],
                                      sems.at[slot]).start(priority=k % 2)
            return carry

        lax.fori_loop(0, T_ROWS, issue, 0, unroll=8)

    slot = lax.rem(i, 2)

    @pl.when(i == 0)
    def _():
        issue_tile(dest_ref, i, 0)

    @pl.when(i + 1 < n_tiles)
    def _():
        issue_tile(dest_next_ref, i + 1, 1 - slot)

    for k in range(TOP_K):
        pltpu.make_async_copy(y_ref.at[_token_rows(0, T_ROWS), :], ybuf.at[slot * TOP_K + k], sems.at[slot]).wait()

    gate = gate_ref[...]
    acc = base_ref[...]
    for k in range(TOP_K):
        acc = acc + gate[:, k:k + 1] * _load_token_tiles(ybuf, T_ROWS, lead=(slot * TOP_K + k,))
    out = _layer_norm(acc, g_ref[...], b_ref[...])

    @pl.when(i < n_prompt_tiles)
    def _():
        outp_ref[...] = out

    @pl.when(i >= n_prompt_tiles)
    def _():
        outs_ref[...] = out


def _full(shape):
    return pl.BlockSpec(shape, lambda *_: (0,) * len(shape), pipeline_mode=pl.Buffered(1))


def kernel(x_prompt, x_sample, cache_conv, p_prompt, p_sample, ln_in_g, ln_in_b, w_in, conv_w, gn_g, gn_b, vn_g, vn_b,
           w_spatial, b_spatial, w_out, ln1_g, ln1_b, w_router, b_router, w_up, b_up, w_down, b_down, w_ple,
           w_ple_gate, ln2_g, ln2_b):
    batch, seq, _ = x_prompt.shape
    dec_batch, dec_seq, _ = x_sample.shape
    assert w_in.shape[0] == DEPTH and dec_seq == SUBLANES and seq % T_PROMPT == 0
    n_prompt = batch * seq
    n_sample = dec_batch * dec_seq
    n_tok = n_prompt + n_sample
    t_s = SEQS_PER_TILE * dec_seq
    assert n_prompt % T_ROWS == 0 and n_sample % T_ROWS == 0 and n_tok % T_RANK == 0 and n_sample % t_s == 0
    assert n_tok % T_DISPATCH == 0

    row = lambda a: a.reshape(1, -1).astype(F32)
    gidx = jnp.arange(MXU_DIM) // (C_CONV // N_CONV_GROUPS)
    gmat = jnp.where(gidx[:, None] == gidx[None, :], 1.0 / (C_CONV // N_CONV_GROUPS), 0.0).astype(BF16)
    wr_pad = jnp.pad(w_router[0].astype(F32), ((0, 0), (0, LANES - N_EXPERTS)))
    wr_hi = wr_pad.astype(BF16)
    wr_lo = (wr_pad - wr_hi.astype(F32)).astype(BF16)
    weights = dict(
        ln_in_g=row(ln_in_g), ln_in_b=row(ln_in_b), w_in=w_in[0].astype(BF16), vn_g=row(vn_g[0]), vn_b=row(vn_b[0]),
        gmat=gmat, gn_g=row(gn_g[0]), gn_b=row(gn_b[0]), w_out=w_out[0].astype(BF16), ln1_g=row(ln1_g[0]),
        ln1_b=row(ln1_b[0]), wr_hi=wr_hi, wr_pair=jnp.concatenate([wr_hi, wr_lo], axis=1),
        b_r=jnp.pad(row(b_router[0]), ((0, 0), (0, LANES - N_EXPERTS))))
    w_list = [weights[n] for n in _WEIGHT_NAMES]
    w_specs = [_full(a.shape) for a in w_list]
    causal = jnp.tril(jnp.ones((CHUNK, CHUNK), bool))
    ws_m = jnp.where(causal[None], w_spatial[0], 0.0)
    ws_cat = jnp.concatenate([ws_m[0::2], ws_m[1::2]], axis=2).astype(BF16)
    bs_full = jnp.repeat(b_spatial[0].T.astype(F32), HEAD_DIM, axis=1)
    s_i = jnp.arange(SUBLANES)[:, None]
    t_i = jnp.arange(SUBLANES)[None, :]
    tap = jnp.clip(CONV_W - 1 - t_i + s_i, 0, CONV_W - 1)
    convu = jnp.where((s_i <= t_i)[:, :, None], conv_w[0].astype(F32)[tap], 0.0)
    gw8 = jnp.transpose(ws_m[:, :SUBLANES, :SUBLANES], (2, 1, 0))
    gw8 = jnp.repeat(gw8.astype(F32), HEAD_DIM, axis=2)
    b8 = bs_full[:SUBLANES]
    n_slabs = C_CONV // LANES
    slab_rows = lambda a: jnp.broadcast_to(a.reshape(a.shape[:-1] + (n_slabs, 1, LANES)),
                                           a.shape[:-1] + (n_slabs, SUBLANES, LANES))
    convub, gwb, b8b = slab_rows(convu), slab_rows(gw8), slab_rows(b8)
    cslab = cache_conv[0].astype(F32).reshape(dec_batch * SAMPLE_PITCH * n_slabs, LANES)

    cparams = lambda sem: pltpu.CompilerParams(dimension_semantics=sem, vmem_limit_bytes=VMEM_LIMIT)

    nj = seq // T_PROMPT
    npt = n_prompt // T_PROMPT
    assert t_s == T_PROMPT
    pstep = lambda i: jnp.minimum(i, npt - 1)
    sstep = lambda i: jnp.maximum(i - npt, 0)
    tok_block = lambda width: pl.BlockSpec((T_PROMPT, width), lambda i: (i, 0))
    cwb = slab_rows(conv_w[0].astype(F32))
    tables = [cwb, ws_cat, bs_full, convub, gwb, b8b]
    h_all, idx_all, gate_all, u_tail, v_chunk, u_s, v_s = pl.pallas_call(
        functools.partial(_mixer_body, n_prompt_tiles=npt, tiles_per_seq=nj),
        grid=(npt + n_sample // t_s,),
        in_specs=[pl.BlockSpec((None, T_PROMPT, D_MODEL), lambda i: (pstep(i) // nj, pstep(i) % nj, 0)),
                  pl.BlockSpec((t_s, D_MODEL), lambda i: (sstep(i), 0)),
                  pl.BlockSpec((SEQS_PER_TILE * SAMPLE_PITCH * n_slabs, LANES), lambda i: (sstep(i), 0),
                               pipeline_mode=pl.Buffered(1))]
                 + w_specs + [_full(a.shape) for a in tables],
        out_specs=[pl.BlockSpec((T_PROMPT * SUBLANES, LANES), lambda i: (i, 0)),
                   tok_block(LANES), tok_block(LANES),
                   pl.BlockSpec((None, HIST, C_CONV), lambda i: (pstep(i) // nj, 0, 0)),
                   pl.BlockSpec((None, CHUNK, C_GMLP), lambda i: (pstep(i) // nj, 0, 0)),
                   pl.BlockSpec((t_s, C_CONV), lambda i: (sstep(i), 0)),
                   pl.BlockSpec((t_s, C_GMLP), lambda i: (sstep(i), 0))],
        out_shape=(jax.ShapeDtypeStruct((n_tok * SUBLANES, LANES), F32),
                   jax.ShapeDtypeStruct((n_tok, LANES), jnp.int32), jax.ShapeDtypeStruct((n_tok, LANES), F32),
                   jax.ShapeDtypeStruct((batch, HIST, C_CONV), F32), jax.ShapeDtypeStruct((batch, CHUNK, C_GMLP), F32),
                   jax.ShapeDtypeStruct((n_sample, C_CONV), F32), jax.ShapeDtypeStruct((n_sample, C_GMLP), F32)),
        scratch_shapes=[pltpu.VMEM((n_slabs, T_PROMPT + HIST, LANES), F32)]
                       + [pltpu.VMEM((n_slabs, T_PROMPT, LANES), F32)] * 4,
        compiler_params=pltpu.CompilerParams(dimension_semantics=("arbitrary",), vmem_limit_bytes=BIG_VMEM_LIMIT),
        name="mixer",
    )(x_prompt, x_sample.reshape(n_sample, D_MODEL), cslab, *w_list, *tables)

    tri = (jnp.arange(T_RANK)[:, None] > jnp.arange(T_RANK)[None, :]).astype(BF16)
    before_all, counts = pl.pallas_call(
        _rank_body,
        grid=(n_tok // T_RANK,),
        in_specs=[pl.BlockSpec((T_RANK, LANES), lambda i: (i, 0)), _full(tri.shape)],
        out_specs=[pl.BlockSpec((T_RANK, LANES), lambda i: (i, 0)), _full((1, LANES))],
        out_shape=(jax.ShapeDtypeStruct((n_tok, LANES), jnp.int32), jax.ShapeDtypeStruct((1, LANES), jnp.int32)),
        scratch_shapes=[pltpu.VMEM((1, LANES), F32)],
        compiler_params=cparams(("arbitrary",)),
        name="rank",
    )(idx_all, tri)

    n_assign = n_tok * TOP_K
    n_blocks = n_assign // TM + N_EXPERTS
    cap = n_blocks * TM
    cnt = counts[0, :N_EXPERTS]
    padded = (cnt + TM - 1) // TM * TM
    p_end = jnp.cumsum(padded)
    p_start = p_end - padded
    e_ids = jnp.arange(N_EXPERTS, dtype=jnp.int32)
    lookup = lambda table, ids: jnp.sum(jnp.where(ids[..., None] == e_ids, table, 0), axis=-1)
    slot_table = p_start[None, :] + before_all[:, :N_EXPERTS]
    dest = jnp.sum(jnp.where(idx_all[:, :TOP_K, None] == e_ids, slot_table[:, None, :], 0), axis=-1).astype(jnp.int32)
    dest_tiles = dest.reshape(n_tok // T_DISPATCH, 1, T_DISPATCH * TOP_K)
    blk_row = jnp.arange(n_blocks, dtype=jnp.int32) * TM
    blk_e = jnp.minimum(jnp.sum(p_end[None, :] <= blk_row[:, None], axis=1), N_EXPERTS - 1).astype(jnp.int32)
    n_used = (p_end[-1:] // TM).astype(jnp.int32)
    pad_off = (p_start + cnt).astype(jnp.int32)
    pad_n = (padded - cnt).astype(jnp.int32)
    used = cnt > 0
    slot_e = (jnp.cumsum(used.astype(jnp.int32)) - 1) & 1
    later_used = jnp.where(used[None, :] & (e_ids[None, :] > e_ids[:, None]), e_ids[None, :], N_EXPERTS)
    next_e = jnp.min(later_used, axis=1)
    next_e = jnp.where(next_e < N_EXPERTS, next_e, -1).astype(jnp.int32)
    blk_first = ((blk_row == lookup(p_start, blk_e)) & (blk_row < p_end[-1])).astype(jnp.int32)
    blk_rows = jnp.clip(lookup(p_start + cnt, blk_e) - blk_row, 0, TM).astype(jnp.int32)
    blk_slot = lookup(slot_e, blk_e).astype(jnp.int32)
    blk_next = lookup(next_e, blk_e).astype(jnp.int32)


    assert seq % T_DISPATCH == 0 and n_sample % T_DISPATCH == 0
    npd = n_prompt // T_DISPATCH
    per_seq = seq // T_DISPATCH
    pd = lambda i: jnp.minimum(i, npd - 1)
    w_gate = w_ple_gate[0].astype(BF16)
    w_ple_b = w_ple[0].astype(BF16)
    x_sorted, base_all = pl.pallas_call(
        functools.partial(_dispatch_body, n_blocks=n_blocks, n_prompt_tiles=npd),
        grid_spec=pltpu.PrefetchScalarGridSpec(
            num_scalar_prefetch=3,
            grid=(n_tok // T_DISPATCH,),
            in_specs=[pl.BlockSpec((None, 1, T_DISPATCH * TOP_K), lambda i, *_: (i, 0, 0), memory_space=pltpu.SMEM),
                      pl.BlockSpec((T_DISPATCH * SUBLANES, LANES), lambda i, *_: (i, 0)),
                      pl.BlockSpec((None, T_DISPATCH, PLE_DIM), lambda i, *_: (pd(i) // per_seq, pd(i) % per_seq, 0)),
                      pl.BlockSpec((T_DISPATCH, PLE_DIM), lambda i, *_: (jnp.maximum(i - npd, 0), 0)),
                      _full(w_gate.shape), _full(w_ple_b.shape)],
            out_specs=[pl.BlockSpec(memory_space=pl.ANY),
                       pl.BlockSpec((T_DISPATCH, D_MODEL), lambda i, *_: (i, 0))],
            scratch_shapes=[pltpu.VMEM((TM // 2 * SUBLANES, LANES), F32), pltpu.SemaphoreType.DMA,
                            pltpu.SemaphoreType.DMA]),
        out_shape=(jax.ShapeDtypeStruct((cap * SUBLANES, LANES), F32), jax.ShapeDtypeStruct((n_tok, D_MODEL), F32)),
        compiler_params=cparams(("arbitrary",)),
        name="dispatch",
    )(pad_off, pad_n, n_used, dest_tiles, h_all, p_prompt[0], p_sample[0].reshape(n_sample, PLE_DIM), w_gate, w_ple_b)

    last = lambda i, nu: jnp.minimum(i, nu[0] - 1)
    y_sorted = pl.pallas_call(
        _expert_body,
        grid_spec=pltpu.PrefetchScalarGridSpec(
            num_scalar_prefetch=6,
            grid=(n_blocks,),
            in_specs=[pl.BlockSpec((TM * SUBLANES, LANES), lambda i, be, bf, bs, bn, br, nu: (last(i, nu), 0)),
                      pl.BlockSpec((None, 1, 2 * D_FF), lambda i, be, bf, bs, bn, br, nu: (be[last(i, nu)], 0, 0)),
                      pl.BlockSpec((None, 1, D_MODEL), lambda i, be, bf, bs, bn, br, nu: (be[last(i, nu)], 0, 0)),
                      pl.BlockSpec(memory_space=pl.ANY), pl.BlockSpec(memory_space=pl.ANY)],
            out_specs=pl.BlockSpec((TM * SUBLANES, LANES), lambda i, be, bf, bs, bn, br, nu: (i, 0)),
            scratch_shapes=[pltpu.VMEM((2, D_MODEL, 2 * D_FF), F32), pltpu.VMEM((2, D_FF, D_MODEL), F32),
                            pltpu.VMEM((D_MODEL, 2 * D_FF), BF16), pltpu.VMEM((D_FF, D_MODEL), BF16),
                            pltpu.SemaphoreType.DMA((2,))]),
        out_shape=jax.ShapeDtypeStruct((cap * SUBLANES, LANES), F32),
        compiler_params=pltpu.CompilerParams(dimension_semantics=("arbitrary",), vmem_limit_bytes=BIG_VMEM_LIMIT),
        name="experts",
    )(blk_e, blk_first, blk_slot, blk_next, blk_rows, n_used, x_sorted, b_up[0].astype(F32)[:, None, :],
      b_down[0].astype(F32)[:, None, :], w_up[0].astype(F32), w_down[0].astype(F32))

    npt = n_prompt // T_ROWS
    n_tiles = n_tok // T_ROWS
    per = T_DISPATCH // T_ROWS
    dest_spec = pl.BlockSpec((None, 1, T_DISPATCH * TOP_K), lambda i: (i // per, 0, 0), memory_space=pltpu.SMEM)
    dest_next_spec = pl.BlockSpec((None, 1, T_DISPATCH * TOP_K),
                                  lambda i: (jnp.minimum(i + 1, n_tiles - 1) // per, 0, 0), memory_space=pltpu.SMEM)
    out_p, out_s = pl.pallas_call(
        functools.partial(_combine_body, n_prompt_tiles=npt, n_tiles=n_tiles),
        grid=(n_tiles,),
        in_specs=[dest_spec, dest_next_spec, pl.BlockSpec(memory_space=pl.ANY),
                  pl.BlockSpec((T_ROWS, D_MODEL), lambda i: (i, 0)), pl.BlockSpec((T_ROWS, LANES), lambda i: (i, 0)),
                  _full((1, D_MODEL)), _full((1, D_MODEL))],
        out_specs=[pl.BlockSpec((T_ROWS, D_MODEL), lambda i: (jnp.minimum(i, npt - 1), 0)),
                   pl.BlockSpec((T_ROWS, D_MODEL), lambda i: (jnp.maximum(i - npt, 0), 0))],
        out_shape=(jax.ShapeDtypeStruct((n_prompt, D_MODEL), F32), jax.ShapeDtypeStruct((n_sample, D_MODEL), F32)),
        scratch_shapes=[pltpu.VMEM((2 * TOP_K, T_ROWS * SUBLANES, LANES), F32), pltpu.SemaphoreType.DMA((2,))],
        compiler_params=cparams(("arbitrary",)),
        name="combine",
    )(dest_tiles, dest_tiles, y_sorted, base_all, gate_all, row(ln2_g[0]), row(ln2_b[0]))

    y_prompt = out_p.reshape(batch, seq, D_MODEL)
    y_sample = out_s.reshape(dec_batch, dec_seq, D_MODEL)
    conv_state_prompt = u_tail[None, :, HIST - (CONV_W - 1):, :]
    u_s3 = u_s.reshape(dec_batch, dec_seq, C_CONV)
    conv_state_sample = jnp.concatenate([cache_conv[0][:, dec_seq:, :].astype(F32), u_s3], axis=1)[None]
    chunk_v_prompt = v_chunk[None]
    chunk_v_sample = v_s.reshape(1, dec_batch, dec_seq, C_GMLP)
    return (y_prompt, y_sample, conv_state_prompt, conv_state_sample, chunk_v_prompt, chunk_v_sample)
```

```python
import functools

import jax
import jax.numpy as jnp
from jax import lax
from jax.experimental import pallas as pl
from jax.experimental.pallas import tpu as pltpu

F32 = jnp.float32
BF16 = jnp.bfloat16

D_MODEL = 1024
C_CONV = 512
C_GMLP = 512
N_CONV_GROUPS = 8
CONV_W = 31
N_HEADS = 8
HEAD_DIM = C_GMLP // N_HEADS
CHUNK = 128
N_EXPERTS = 32
TOP_K = 4
D_FF = 1024
PLE_DIM = 256
SWIGLU_LIMIT = 7.0
SWIGLU_ALPHA = 1.702
LN_EPS = 1e-5
DEPTH = 1
DEEPNORM_ALPHA = (2.0 * DEPTH) ** 0.25

LANES = 128
SUBLANES = 8
MXU_DIM = 256
VMEM_LIMIT = 48 * 1024 * 1024
BIG_VMEM_LIMIT = 56 * 1024 * 1024

T_PROMPT = 512
PROMPT_SPLIT = 2
HIST = 32
CONV_STRIDE = 4
SEQS_PER_TILE = 64
SAMPLE_PITCH = CONV_W - 1
T_RANK = 1024
T_DISPATCH = 1024
T_ROWS = 256
TM = 512
TM_CHAIN = 256


def _dot(a, b):
    return jnp.dot(a, b, preferred_element_type=F32)


def _layer_norm(x, g, b):
    mu = jnp.mean(x, axis=-1, keepdims=True)
    xc = x - mu
    var = jnp.mean(xc * xc, axis=-1, keepdims=True)
    return xc * lax.rsqrt(var + LN_EPS) * g + b


def _split_bf16(a):
    hi = a.astype(BF16)
    lo = (a - hi.astype(F32)).astype(BF16)
    return hi, lo


def _group_mean(a, gmat_ref):
    hi, lo = _split_bf16(a)
    g = gmat_ref[...]
    outs = []
    for s in range(C_CONV // MXU_DIM):
        sl = slice(MXU_DIM * s, MXU_DIM * (s + 1))
        outs.append(_dot(hi[:, sl], g) + _dot(lo[:, sl], g))
    return jnp.concatenate(outs, axis=1)


def _group_norm_silu(y, gmat_ref, gn_g, gn_b):
    mu = _group_mean(y, gmat_ref)
    yc = y - mu
    var = _group_mean(yc * yc, gmat_ref)
    yn = yc * lax.rsqrt(var + LN_EPS) * gn_g + gn_b
    return yn * jax.nn.sigmoid(yn)


def _store_token_tiles(ref, val, tok0=0):
    n = val.shape[0]
    for c in range(D_MODEL // LANES):
        ref[pl.ds(tok0 * SUBLANES + c, n, stride=SUBLANES), :] = val[:, c * LANES:(c + 1) * LANES]


def _load_token_tiles(ref, n, lead=(), tok0=0):
    parts = [ref[lead + (pl.ds(tok0 * SUBLANES + c, n, stride=SUBLANES), slice(None))]
             for c in range(D_MODEL // LANES)]
    return jnp.concatenate(parts, axis=1)


def _front(x, w):
    xn = _layer_norm(x, w["ln_in_g"][...], w["ln_in_b"][...])
    z = _dot(xn.astype(BF16), w["w_in"][...])
    a_val = z[:, 0:C_CONV]
    a_gate = z[:, C_CONV:2 * C_CONV]
    g_u = z[:, 2 * C_CONV:2 * C_CONV + C_GMLP]
    g_v = z[:, 2 * C_CONV + C_GMLP:]
    u = a_val * jax.nn.sigmoid(a_gate)
    ug = jax.nn.gelu(g_u)
    v = _layer_norm(jax.nn.gelu(g_v), w["vn_g"][...], w["vn_b"][...])
    return xn, u, ug, v


def _tail(xn, y_a, y_b, w, h_ref, idx_ref, gate_ref, row0=0):
    rows = slice(row0, row0 + xn.shape[0])
    mix = _dot(y_a.astype(BF16), w["w_out"][0:C_CONV, :]) + _dot(y_b.astype(BF16), w["w_out"][C_CONV:, :])
    h = _layer_norm(DEEPNORM_ALPHA * xn + mix, w["ln1_g"][...], w["ln1_b"][...])
    hb, h_lo = _split_bf16(h)
    _store_token_tiles(h_ref, h, row0)

    pair = _dot(hb, w["wr_pair"][...])
    logits = pair[:, :LANES] + pair[:, LANES:] + _dot(h_lo, w["wr_hi"][...]) + w["b_r"][...]
    lane = lax.broadcasted_iota(jnp.int32, logits.shape, 1)
    lane_f = lane.astype(F32)
    vals = jnp.where(lane < N_EXPERTS, logits, -jnp.inf)
    tops, ids = [], []
    for _ in range(TOP_K):
        m = jnp.max(vals, axis=-1, keepdims=True)
        i = jnp.min(jnp.where(vals == m, lane_f, float(LANES)), axis=-1, keepdims=True)
        vals = jnp.where(lane_f == i, -jnp.inf, vals)
        tops.append(m)
        ids.append(i)
    exps = [jnp.exp(m - tops[0]) for m in tops]
    denom = exps[0] + exps[1] + exps[2] + exps[3]
    idx_out = jnp.zeros(logits.shape, F32)
    gate_out = jnp.zeros(logits.shape, F32)
    for k in range(TOP_K):
        idx_out = jnp.where(lane == k, ids[k], idx_out)
        gate_out = jnp.where(lane == k, exps[k] / denom, gate_out)
    idx_ref[rows, :] = idx_out.astype(jnp.int32)
    gate_ref[rows, :] = gate_out


_WEIGHT_NAMES = ("ln_in_g", "ln_in_b", "w_in", "vn_g", "vn_b", "gmat", "gn_g", "gn_b", "w_out", "ln1_g", "ln1_b",
                 "wr_hi", "wr_pair", "b_r")


def _prompt_branch(j, x_ref, w, cwb_ref, ws_ref, bs_ref, outs, tail_ref, vch_ref, ubuf, yslab, last_j):
    t = T_PROMPT
    n = t // PROMPT_SPLIT
    n_slabs = C_CONV // LANES

    @pl.when(j == 0)
    def _():
        ubuf[:, 0:HIST, :] = jnp.zeros((n_slabs, HIST, LANES), F32)

    @pl.when(j > 0)
    def _():
        ubuf[:, 0:HIST, :] = ubuf[:, t:t + HIST, :]

    def front(h):
        xn, u, ug, v = _front(x_ref[h * n:(h + 1) * n, :], w)
        for s in range(n_slabs):
            ubuf[s, HIST + h * n:HIST + (h + 1) * n, :] = u[:, s * LANES:(s + 1) * LANES]
        return xn, u, ug, v

    def conv(h):
        rows = CONV_STRIDE * SUBLANES
        first = HIST - (CONV_W - 1)
        for s in range(n_slabs):
            for c in range(h * n // rows, (h + 1) * n // rows):
                accs = [None] * CONV_STRIDE
                for shift in range(CONV_STRIDE + CONV_W - 1):
                    win = ubuf[s, pl.ds(first + c * rows + shift, SUBLANES, stride=CONV_STRIDE), :]
                    for ph in range(CONV_STRIDE):
                        k = shift - ph
                        if 0 <= k < CONV_W:
                            term = cwb_ref[k, s] * win
                            accs[ph] = term if accs[ph] is None else accs[ph] + term
                for ph in range(CONV_STRIDE):
                    yslab[s, pl.ds(c * rows + ph, SUBLANES, stride=CONV_STRIDE), :] = accs[ph]
        y_conv = jnp.concatenate([yslab[s, h * n:(h + 1) * n, :] for s in range(n_slabs)], axis=1)
        return _group_norm_silu(y_conv, w["gmat"], w["gn_g"][...], w["gn_b"][...])

    def spatial_gate(ug, v):
        lane = lax.broadcasted_iota(jnp.int32, (CHUNK, LANES), 1)
        mixed_chunks = []
        for c in range(n // CHUNK):
            vc = v[c * CHUNK:(c + 1) * CHUNK, :]
            parts = []
            for q in range(N_HEADS // 2):
                vp = vc[:, q * LANES:(q + 1) * LANES]
                rhs = jnp.concatenate([jnp.where(lane < HEAD_DIM, vp, 0.0), jnp.where(lane >= HEAD_DIM, vp, 0.0)],
                                      axis=0).astype(BF16)
                parts.append(_dot(ws_ref[q], rhs))
            mixed_chunks.append(jnp.concatenate(parts, axis=1) + bs_ref[...])
        return ug * jnp.concatenate(mixed_chunks, axis=0)

    def tail(h, f, y_a, y_b):
        _tail(f[0], y_a, y_b, w, *outs, row0=h * n)

    fronts = [front(0)]
    y_as = {}
    for h in range(PROMPT_SPLIT):
        if h + 1 < PROMPT_SPLIT:
            fronts.append(front(h + 1))
        y_as[h] = conv(h)
        if h > 0:
            tail(h - 1, fronts[h - 1], y_as.pop(h - 1), spatial_gate(fronts[h - 1][2], fronts[h - 1][3]))
    last = PROMPT_SPLIT - 1
    tail(last, fronts[last], y_as.pop(last), spatial_gate(fronts[last][2], fronts[last][3]))

    @pl.when(j == last_j)
    def _():
        tail_ref[...] = fronts[last][1][n - HIST:, :]
        vch_ref[...] = fronts[last][3][n - CHUNK:, :]


def _sample_branch(x_ref, cslab_ref, w, cwb_ref, convub_ref, gwb_ref, b8b_ref, outs, u_out_ref, v_out_ref,
                   uslab, vslab, yslab, mslab):
    xn, u, ug, v = _front(x_ref[...], w)
    u_out_ref[...] = u
    v_out_ref[...] = v
    n_slabs = C_CONV // LANES
    for s in range(n_slabs):
        uslab[s] = u[:, s * LANES:(s + 1) * LANES]
        vslab[s] = v[:, s * LANES:(s + 1) * LANES]
    n_cache = CONV_W - 1

    def group(g, carry):
        row0 = g * (SUBLANES * SUBLANES)
        for s in range(n_slabs):
            cache = [cslab_ref[pl.ds((g * (SUBLANES * SAMPLE_PITCH) + j) * n_slabs + s, SUBLANES,
                                     stride=SAMPLE_PITCH * n_slabs), :] for j in range(n_cache)]
            u_pos = [uslab[s, pl.ds(row0 + q, SUBLANES, stride=SUBLANES), :] for q in range(SUBLANES)]
            v_pos = [vslab[s, pl.ds(row0 + q, SUBLANES, stride=SUBLANES), :] for q in range(SUBLANES)]
            for t in range(SUBLANES):
                acc = None
                for k in range(n_cache - t):
                    term = cwb_ref[k, s] * cache[t + k]
                    acc = term if acc is None else acc + term
                mix = b8b_ref[t, s]
                for q in range(t + 1):
                    acc = acc + convub_ref[q, t, s] * u_pos[q]
                    mix = mix + gwb_ref[q, t, s] * v_pos[q]
                yslab[s, pl.ds(row0 + t, SUBLANES, stride=SUBLANES), :] = acc
                mslab[s, pl.ds(row0 + t, SUBLANES, stride=SUBLANES), :] = mix
        return carry

    lax.fori_loop(0, SEQS_PER_TILE // SUBLANES, group, 0)
    y_conv = jnp.concatenate([yslab[s] for s in range(n_slabs)], axis=1)
    mixed = jnp.concatenate([mslab[s] for s in range(n_slabs)], axis=1)
    y_a = _group_norm_silu(y_conv, w["gmat"], w["gn_g"][...], w["gn_b"][...])
    _tail(xn, y_a, ug * mixed, w, *outs)


def _mixer_body(*refs, n_prompt_tiles, tiles_per_seq):
    n_w = len(_WEIGHT_NAMES)
    xp_ref, xs_ref, cslab_ref = refs[:3]
    w = dict(zip(_WEIGHT_NAMES, refs[3:3 + n_w]))
    cwb_ref, ws_ref, bs_ref, convub_ref, gwb_ref, b8b_ref = refs[3 + n_w:9 + n_w]
    outs = refs[9 + n_w:12 + n_w]
    tail_ref, vch_ref, u_out_ref, v_out_ref = refs[12 + n_w:16 + n_w]
    ubuf, yslab, uslab, vslab, mslab = refs[16 + n_w:]
    step = pl.program_id(0)

    @pl.when(step < n_prompt_tiles)
    def _():
        _prompt_branch(lax.rem(step, tiles_per_seq), xp_ref, w, cwb_ref, ws_ref, bs_ref, outs, tail_ref,
                       vch_ref, ubuf, yslab, tiles_per_seq - 1)

    @pl.when(step >= n_prompt_tiles)
    def _():
        _sample_branch(xs_ref, cslab_ref, w, cwb_ref, convub_ref, gwb_ref, b8b_ref, outs, u_out_ref, v_out_ref,
                       uslab, vslab, yslab, mslab)


def _rank_body(idx_ref, tri_ref, before_ref, counts_ref, carry):
    i = pl.program_id(0)

    @pl.when(i == 0)
    def _():
        carry[...] = jnp.zeros(carry.shape, F32)

    idx = idx_ref[...]
    lane = lax.broadcasted_iota(jnp.int32, idx.shape, 1)
    multi = jnp.zeros(idx.shape, F32)
    for k in range(TOP_K):
        multi = multi + (lane == idx[:, k:k + 1]).astype(F32)
    before_ref[...] = (_dot(tri_ref[...], multi.astype(BF16)) + carry[...]).astype(jnp.int32)
    carry[...] = carry[...] + jnp.sum(multi, axis=0, keepdims=True)
    counts_ref[...] = carry[...].astype(jnp.int32)


def _pad_bits():
    b = TM // 2
    while b >= 1:
        yield b
        b //= 2


def _token_rows(t, n=1):
    return pl.ds(pl.multiple_of(t * SUBLANES, SUBLANES), n * SUBLANES)


def _dispatch_body(pad_off_ref, pad_n_ref, n_used_ref, dest_ref, h_ref, pp_ref, ps_ref, wg_ref, wp_ref, xs_ref, base_ref,
                   zbuf, sem, zsem, *, n_blocks, n_prompt_tiles):
    i = pl.program_id(0)

    def issue(r, carry):
        src = h_ref.at[_token_rows(r), :]
        for k in range(TOP_K):
            d = dest_ref[0, r * TOP_K + k]
            pltpu.make_async_copy(src, xs_ref.at[_token_rows(d), :], sem).start(priority=k % 2)
        return carry

    @pl.when(i == 0)
    def _():
        zbuf[...] = jnp.zeros(zbuf.shape, F32)

        def zero_copy(off, b):
            return pltpu.make_async_copy(zbuf.at[pl.ds(0, b * SUBLANES), :], xs_ref.at[_token_rows(off, b), :], zsem)

        def start_or_wait(cond, cp, wait):
            @pl.when(cond)
            def _():
                if wait:
                    cp.wait()
                else:
                    cp.start()

        for wait in (False, True):
            for e in range(N_EXPERTS):
                n = pad_n_ref[e]
                for b in _pad_bits():
                    start_or_wait((n & b) != 0, zero_copy(pad_off_ref[e] + (n & ~(2 * b - 1)), b), wait)

        half = TM // 2

        def tail_block(wait):
            def go(blk, carry):
                for s in range(TM // half):
                    cp = zero_copy(blk * TM + s * half, half)
                    cp.wait() if wait else cp.start()
                return carry
            return go

        lax.fori_loop(n_used_ref[0], n_blocks, tail_block(False), 0)
        lax.fori_loop(n_used_ref[0], n_blocks, tail_block(True), 0)

    rows = TM_CHAIN
    is_prompt = i < n_prompt_tiles
    for c in range(T_DISPATCH // rows):
        lax.fori_loop(c * rows, (c + 1) * rows, issue, 0, unroll=8)
        sl = slice(c * rows, (c + 1) * rows)
        h = _load_token_tiles(h_ref, rows, tok0=c * rows)
        p = jnp.where(is_prompt, pp_ref[sl, :], ps_ref[sl, :])
        ple = _dot(p.astype(BF16), wp_ref[...]) * jax.nn.sigmoid(_dot(h.astype(BF16), wg_ref[...]))
        base_ref[sl, :] = DEEPNORM_ALPHA * h + ple

    for _ in range(TOP_K):
        pltpu.make_async_copy(h_ref, xs_ref.at[_token_rows(0, T_DISPATCH), :], sem).wait()


def _expert_body(blk_e_ref, first_ref, slot_ref, next_ref, rows_ref, n_used_ref, x_ref, bu_ref, bd_ref, wu_hbm, wd_hbm,
                 y_ref, wu_f32, wd_f32, wu_bf, wd_bf, sems):
    i = pl.program_id(0)

    def weight_copies(e, s):
        return (pltpu.make_async_copy(wu_hbm.at[e], wu_f32.at[s], sems.at[s]),
                pltpu.make_async_copy(wd_hbm.at[e], wd_f32.at[s], sems.at[s]))

    @pl.when(i < n_used_ref[0])
    def _():
        s = slot_ref[i]

        @pl.when(first_ref[i] == 1)
        def _():
            @pl.when(i == 0)
            def _():
                for cp in weight_copies(blk_e_ref[i], s):
                    cp.start()

            for cp in weight_copies(blk_e_ref[i], s):
                cp.wait()
            nxt = next_ref[i]

            @pl.when(nxt >= 0)
            def _():
                for cp in weight_copies(nxt, 1 - s):
                    cp.start()

            chunk = D_MODEL // SUBLANES

            def cast_rows(c, carry):
                r = pl.multiple_of(c * chunk, chunk)
                wu_bf[pl.ds(r, chunk), :] = wu_f32[s, pl.ds(r, chunk), :].astype(BF16)
                wd_bf[pl.ds(r, chunk), :] = wd_f32[s, pl.ds(r, chunk), :].astype(BF16)
                return carry

            lax.fori_loop(0, SUBLANES, cast_rows, 0)

        def chain(c):
            x = _load_token_tiles(x_ref, TM_CHAIN, tok0=c * TM_CHAIN)
            hcat = _dot(x.astype(BF16), wu_bf[...]) + bu_ref[...]
            h_glu = jnp.minimum(hcat[:, :D_FF], SWIGLU_LIMIT)
            h_lin = jnp.clip(hcat[:, D_FF:], -SWIGLU_LIMIT, SWIGLU_LIMIT)
            act = h_glu * jax.nn.sigmoid(SWIGLU_ALPHA * h_glu) * (h_lin + 1.0)
            _store_token_tiles(y_ref, _dot(act.astype(BF16), wd_bf[...]) + bd_ref[...], tok0=c * TM_CHAIN)

        n_chains = TM // TM_CHAIN
        live = (rows_ref[i] + TM_CHAIN - 1) // TM_CHAIN
        for m in range(1, n_chains + 1):
            @pl.when(live == m)
            def _():
                for c in range(m):
                    chain(c)
                if m < n_chains:
                    y_ref[m * TM_CHAIN * SUBLANES:, :] = jnp.zeros(((n_chains - m) * TM_CHAIN * SUBLANES, LANES), F32)

    @pl.when(pl.program_id(0) >= n_used_ref[0])
    def _():
        y_ref[...] = jnp.zeros(y_ref.shape, F32)


def _combine_body(dest_ref, dest_next_ref, y_ref, base_ref, gate_ref, g_ref, b_ref, outp_ref, outs_ref, ybuf, sems, *,
                  n_prompt_tiles, n_tiles):
    i = pl.program_id(0)

    def issue_tile(d_ref, tile, slot):
        first = lax.rem(tile, T_DISPATCH // T_ROWS) * (T_ROWS * TOP_K)

        def issue(r, carry):
            for k in range(TOP_K):
                d = d_ref[0, first + r * TOP_K + k]
                pltpu.make_async_copy(y_ref.at[_token_rows(d), :], ybuf.at[slot * TOP_K + k, _token_rows(r), :],
                                      sems.at[slot]).start(priority=k % 2)
            return carry

        lax.fori_loop(0, T_ROWS, issue, 0, unroll=8)

    slot = lax.rem(i, 2)

    @pl.when(i == 0)
    def _():
        issue_tile(dest_ref, i, 0)

    @pl.when(i + 1 < n_tiles)
    def _():
        issue_tile(dest_next_ref, i + 1, 1 - slot)

    for k in range(TOP_K):
        pltpu.make_async_copy(y_ref.at[_token_rows(0, T_ROWS), :], ybuf.at[slot * TOP_K + k], sems.at[slot]).wait()

    gate = gate_ref[...]
    acc = base_ref[...]
    for k in range(TOP_K):
        acc = acc + gate[:, k:k + 1] * _load_token_tiles(ybuf, T_ROWS, lead=(slot * TOP_K + k,))
    out = _layer_norm(acc, g_ref[...], b_ref[...])

    @pl.when(i < n_prompt_tiles)
    def _():
        outp_ref[...] = out

    @pl.when(i >= n_prompt_tiles)
    def _():
        outs_ref[...] = out


def _full(shape):
    return pl.BlockSpec(shape, lambda *_: (0,) * len(shape), pipeline_mode=pl.Buffered(1))


def kernel(x_prompt, x_sample, cache_conv, p_prompt, p_sample, ln_in_g, ln_in_b, w_in, conv_w, gn_g, gn_b, vn_g, vn_b,
           w_spatial, b_spatial, w_out, ln1_g, ln1_b, w_router, b_router, w_up, b_up, w_down, b_down, w_ple,
           w_ple_gate, ln2_g, ln2_b):
    batch, seq, _ = x_prompt.shape
    dec_batch, dec_seq, _ = x_sample.shape
    assert w_in.shape[0] == DEPTH and dec_seq == SUBLANES and seq % T_PROMPT == 0
    n_prompt = batch * seq
    n_sample = dec_batch * dec_seq
    n_tok = n_prompt + n_sample
    t_s = SEQS_PER_TILE * dec_seq
    assert n_prompt % T_ROWS == 0 and n_sample % T_ROWS == 0 and n_tok % T_RANK == 0 and n_sample % t_s == 0
    assert n_tok % T_DISPATCH == 0

    row = lambda a: a.reshape(1, -1).astype(F32)
    gidx = jnp.arange(MXU_DIM) // (C_CONV // N_CONV_GROUPS)
    gmat = jnp.where(gidx[:, None] == gidx[None, :], 1.0 / (C_CONV // N_CONV_GROUPS), 0.0).astype(BF16)
    wr_pad = jnp.pad(w_router[0].astype(F32), ((0, 0), (0, LANES - N_EXPERTS)))
    wr_hi = wr_pad.astype(BF16)
    wr_lo = (wr_pad - wr_hi.astype(F32)).astype(BF16)
    weights = dict(
        ln_in_g=row(ln_in_g), ln_in_b=row(ln_in_b), w_in=w_in[0].astype(BF16), vn_g=row(vn_g[0]), vn_b=row(vn_b[0]),
        gmat=gmat, gn_g=row(gn_g[0]), gn_b=row(gn_b[0]), w_out=w_out[0].astype(BF16), ln1_g=row(ln1_g[0]),
        ln1_b=row(ln1_b[0]), wr_hi=wr_hi, wr_pair=jnp.concatenate([wr_hi, wr_lo], axis=1),
        b_r=jnp.pad(row(b_router[0]), ((0, 0), (0, LANES - N_EXPERTS))))
    w_list = [weights[n] for n in _WEIGHT_NAMES]
    w_specs = [_full(a.shape) for a in w_list]
    causal = jnp.tril(jnp.ones((CHUNK, CHUNK), bool))
    ws_m = jnp.where(causal[None], w_spatial[0], 0.0)
    ws_cat = jnp.concatenate([ws_m[0::2], ws_m[1::2]], axis=2).astype(BF16)
    bs_full = jnp.repeat(b_spatial[0].T.astype(F32), HEAD_DIM, axis=1)
    s_i = jnp.arange(SUBLANES)[:, None]
    t_i = jnp.arange(SUBLANES)[None, :]
    tap = jnp.clip(CONV_W - 1 - t_i + s_i, 0, CONV_W - 1)
    convu = jnp.where((s_i <= t_i)[:, :, None], conv_w[0].astype(F32)[tap], 0.0)
    gw8 = jnp.transpose(ws_m[:, :SUBLANES, :SUBLANES], (2, 1, 0))
    gw8 = jnp.repeat(gw8.astype(F32), HEAD_DIM, axis=2)
    b8 = bs_full[:SUBLANES]
    n_slabs = C_CONV // LANES
    slab_rows = lambda a: jnp.broadcast_to(a.reshape(a.shape[:-1] + (n_slabs, 1, LANES)),
                                           a.shape[:-1] + (n_slabs, SUBLANES, LANES))
    convub, gwb, b8b = slab_rows(convu), slab_rows(gw8), slab_rows(b8)
    cslab = cache_conv.astype(F32).reshape(dec_batch * SAMPLE_PITCH * n_slabs, LANES)

    cparams = lambda sem: pltpu.CompilerParams(dimension_semantics=sem, vmem_limit_bytes=VMEM_LIMIT)

    nj = seq // T_PROMPT
    npt = n_prompt // T_PROMPT
    assert t_s == T_PROMPT
    pstep = lambda i: jnp.minimum(i, npt - 1)
    sstep = lambda i: jnp.maximum(i - npt, 0)
    tok_block = lambda width: pl.BlockSpec((T_PROMPT, width), lambda i: (i, 0))
    cwb = slab_rows(conv_w[0].astype(F32))
    tables = [cwb, ws_cat, bs_full, convub, gwb, b8b]
    h_all, idx_all, gate_all, u_tail, v_chunk, u_s, v_s = pl.pallas_call(
        functools.partial(_mixer_body, n_prompt_tiles=npt, tiles_per_seq=nj),
        grid=(npt + n_sample // t_s,),
        in_specs=[pl.BlockSpec((None, T_PROMPT, D_MODEL), lambda i: (pstep(i) // nj, pstep(i) % nj, 0)),
                  pl.BlockSpec((t_s, D_MODEL), lambda i: (sstep(i), 0)),
                  pl.BlockSpec((SEQS_PER_TILE * SAMPLE_PITCH * n_slabs, LANES), lambda i: (sstep(i), 0),
                               pipeline_mode=pl.Buffered(1))]
                 + w_specs + [_full(a.shape) for a in tables],
        out_specs=[pl.BlockSpec((T_PROMPT * SUBLANES, LANES), lambda i: (i, 0)),
                   tok_block(LANES), tok_block(LANES),
                   pl.BlockSpec((None, HIST, C_CONV), lambda i: (pstep(i) // nj, 0, 0)),
                   pl.BlockSpec((None, CHUNK, C_GMLP), lambda i: (pstep(i) // nj, 0, 0)),
                   pl.BlockSpec((t_s, C_CONV), lambda i: (sstep(i), 0)),
                   pl.BlockSpec((t_s, C_GMLP), lambda i: (sstep(i), 0))],
        out_shape=(jax.ShapeDtypeStruct((n_tok * SUBLANES, LANES), F32),
                   jax.ShapeDtypeStruct((n_tok, LANES), jnp.int32), jax.ShapeDtypeStruct((n_tok, LANES), F32),
                   jax.ShapeDtypeStruct((batch, HIST, C_CONV), F32), jax.ShapeDtypeStruct((batch, CHUNK, C_GMLP), F32),
                   jax.ShapeDtypeStruct((n_sample, C_CONV), F32), jax.ShapeDtypeStruct((n_sample, C_GMLP), F32)),
        scratch_shapes=[pltpu.VMEM((n_slabs, T_PROMPT + HIST, LANES), F32)]
                       + [pltpu.VMEM((n_slabs, T_PROMPT, LANES), F32)] * 4,
        compiler_params=pltpu.CompilerParams(dimension_semantics=("arbitrary",), vmem_limit_bytes=BIG_VMEM_LIMIT),
        name="mixer",
    )(x_prompt, x_sample.reshape(n_sample, D_MODEL), cslab, *w_list, *tables)

    tri = (jnp.arange(T_RANK)[:, None] > jnp.arange(T_RANK)[None, :]).astype(BF16)
    before_all, counts = pl.pallas_call(
        _rank_body,
        grid=(n_tok // T_RANK,),
        in_specs=[pl.BlockSpec((T_RANK, LANES), lambda i: (i, 0)), _full(tri.shape)],
        out_specs=[pl.BlockSpec((T_RANK, LANES), lambda i: (i, 0)), _full((1, LANES))],
        out_shape=(jax.ShapeDtypeStruct((n_tok, LANES), jnp.int32), jax.ShapeDtypeStruct((1, LANES), jnp.int32)),
        scratch_shapes=[pltpu.VMEM((1, LANES), F32)],
        compiler_params=cparams(("arbitrary",)),
        name="rank",
    )(idx_all, tri)

    n_assign = n_tok * TOP_K
    n_blocks = n_assign // TM + N_EXPERTS
    cap = n_blocks * TM
    cnt = counts[0, :N_EXPERTS]
    padded = (cnt + TM - 1) // TM * TM
    p_end = jnp.cumsum(padded)
    p_start = p_end - padded
    e_ids = jnp.arange(N_EXPERTS, dtype=jnp.int32)
    lookup = lambda table, ids: jnp.sum(jnp.where(ids[..., None] == e_ids, table, 0), axis=-1)
    slot_table = p_start[None, :] + before_all[:, :N_EXPERTS]
    dest = jnp.sum(jnp.where(idx_all[:, :TOP_K, None] == e_ids, slot_table[:, None, :], 0), axis=-1).astype(jnp.int32)
    dest_tiles = dest.reshape(n_tok // T_DISPATCH, 1, T_DISPATCH * TOP_K)
    blk_row = jnp.arange(n_blocks, dtype=jnp.int32) * TM
    blk_e = jnp.minimum(jnp.sum(p_end[None, :] <= blk_row[:, None], axis=1), N_EXPERTS - 1).astype(jnp.int32)
    n_used = (p_end[-1:] // TM).astype(jnp.int32)
    pad_off = (p_start + cnt).astype(jnp.int32)
    pad_n = (padded - cnt).astype(jnp.int32)
    used = cnt > 0
    slot_e = (jnp.cumsum(used.astype(jnp.int32)) - 1) & 1
    later_used = jnp.where(used[None, :] & (e_ids[None, :] > e_ids[:, None]), e_ids[None, :], N_EXPERTS)
    next_e = jnp.min(later_used, axis=1)
    next_e = jnp.where(next_e < N_EXPERTS, next_e, -1).astype(jnp.int32)
    blk_first = ((blk_row == lookup(p_start, blk_e)) & (blk_row < p_end[-1])).astype(jnp.int32)
    blk_rows = jnp.clip(lookup(p_start + cnt, blk_e) - blk_row, 0, TM).astype(jnp.int32)
    blk_slot = lookup(slot_e, blk_e).astype(jnp.int32)
    blk_next = lookup(next_e, blk_e).astype(jnp.int32)


    assert seq % T_DISPATCH == 0 and n_sample % T_DISPATCH == 0
    npd = n_prompt // T_DISPATCH
    per_seq = seq // T_DISPATCH
    pd = lambda i: jnp.minimum(i, npd - 1)
    w_gate = w_ple_gate[0].astype(BF16)
    w_ple_b = w_ple[0].astype(BF16)
    x_sorted, base_all = pl.pallas_call(
        functools.partial(_dispatch_body, n_blocks=n_blocks, n_prompt_tiles=npd),
        grid_spec=pltpu.PrefetchScalarGridSpec(
            num_scalar_prefetch=3,
            grid=(n_tok // T_DISPATCH,),
            in_specs=[pl.BlockSpec((None, 1, T_DISPATCH * TOP_K), lambda i, *_: (i, 0, 0), memory_space=pltpu.SMEM),
                      pl.BlockSpec((T_DISPATCH * SUBLANES, LANES), lambda i, *_: (i, 0)),
                      pl.BlockSpec((None, T_DISPATCH, PLE_DIM), lambda i, *_: (pd(i) // per_seq, pd(i) % per_seq, 0)),
                      pl.BlockSpec((T_DISPATCH, PLE_DIM), lambda i, *_: (jnp.maximum(i - npd, 0), 0)),
                      _full(w_gate.shape), _full(w_ple_b.shape)],
            out_specs=[pl.BlockSpec(memory_space=pl.ANY),
                       pl.BlockSpec((T_DISPATCH, D_MODEL), lambda i, *_: (i, 0))],
            scratch_shapes=[pltpu.VMEM((TM // 2 * SUBLANES, LANES), F32), pltpu.SemaphoreType.DMA,
                            pltpu.SemaphoreType.DMA]),
        out_shape=(jax.ShapeDtypeStruct((cap * SUBLANES, LANES), F32), jax.ShapeDtypeStruct((n_tok, D_MODEL), F32)),
        compiler_params=cparams(("arbitrary",)),
        name="dispatch",
    )(pad_off, pad_n, n_used, dest_tiles, h_all, p_prompt[0], p_sample[0].reshape(n_sample, PLE_DIM), w_gate, w_ple_b)

    last = lambda i, nu: jnp.minimum(i, nu[0] - 1)
    y_sorted = pl.pallas_call(
        _expert_body,
        grid_spec=pltpu.PrefetchScalarGridSpec(
            num_scalar_prefetch=6,
            grid=(n_blocks,),
            in_specs=[pl.BlockSpec((TM * SUBLANES, LANES), lambda i, be, bf, bs, bn, br, nu: (last(i, nu), 0)),
                      pl.BlockSpec((None, 1, 2 * D_FF), lambda i, be, bf, bs, bn, br, nu: (be[last(i, nu)], 0, 0)),
                      pl.BlockSpec((None, 1, D_MODEL), lambda i, be, bf, bs, bn, br, nu: (be[last(i, nu)], 0, 0)),
                      pl.BlockSpec(memory_space=pl.ANY), pl.BlockSpec(memory_space=pl.ANY)],
            out_specs=pl.BlockSpec((TM * SUBLANES, LANES), lambda i, be, bf, bs, bn, br, nu: (i, 0)),
            scratch_shapes=[pltpu.VMEM((2, D_MODEL, 2 * D_FF), F32), pltpu.VMEM((2, D_FF, D_MODEL), F32),
                            pltpu.VMEM((D_MODEL, 2 * D_FF), BF16), pltpu.VMEM((D_FF, D_MODEL), BF16),
                            pltpu.SemaphoreType.DMA((2,))]),
        out_shape=jax.ShapeDtypeStruct((cap * SUBLANES, LANES), F32),
        compiler_params=pltpu.CompilerParams(dimension_semantics=("arbitrary",), vmem_limit_bytes=BIG_VMEM_LIMIT),
        name="experts",
    )(blk_e, blk_first, blk_slot, blk_next, blk_rows, n_used, x_sorted, b_up[0].astype(F32)[:, None, :],
      b_down[0].astype(F32)[:, None, :], w_up[0].astype(F32), w_down[0].astype(F32))

    npt = n_prompt // T_ROWS
    n_tiles = n_tok // T_ROWS
    per = T_DISPATCH // T_ROWS
    dest_spec = pl.BlockSpec((None, 1, T_DISPATCH * TOP_K), lambda i: (i // per, 0, 0), memory_space=pltpu.SMEM)
    dest_next_spec = pl.BlockSpec((None, 1, T_DISPATCH * TOP_K),
                                  lambda i: (jnp.minimum(i + 1, n_tiles - 1) // per, 0, 0), memory_space=pltpu.SMEM)
    out_p, out_s = pl.pallas_call(
        functools.partial(_combine_body, n_prompt_tiles=npt, n_tiles=n_tiles),
        grid=(n_tiles,),
        in_specs=[dest_spec, dest_next_spec, pl.BlockSpec(memory_space=pl.ANY),
                  pl.BlockSpec((T_ROWS, D_MODEL), lambda i: (i, 0)), pl.BlockSpec((T_ROWS, LANES), lambda i: (i, 0)),
                  _full((1, D_MODEL)), _full((1, D_MODEL))],
        out_specs=[pl.BlockSpec((T_ROWS, D_MODEL), lambda i: (jnp.minimum(i, npt - 1), 0)),
                   pl.BlockSpec((T_ROWS, D_MODEL), lambda i: (jnp.maximum(i - npt, 0), 0))],
        out_shape=(jax.ShapeDtypeStruct((n_prompt, D_MODEL), F32), jax.ShapeDtypeStruct((n_sample, D_MODEL), F32)),
        scratch_shapes=[pltpu.VMEM((2 * TOP_K, T_ROWS * SUBLANES, LANES), F32), pltpu.SemaphoreType.DMA((2,))],
        compiler_params=cparams(("arbitrary",)),
        name="combine",
    )(dest_tiles, dest_tiles, y_sorted, base_all, gate_all, row(ln2_g[0]), row(ln2_b[0]))

    y_prompt = out_p.reshape(batch, seq, D_MODEL)
    y_sample = out_s.reshape(dec_batch, dec_seq, D_MODEL)
    conv_state_prompt = u_tail[None, :, HIST - (CONV_W - 1):, :]
    u_s4 = u_s.reshape(1, dec_batch, dec_seq, C_CONV)
    conv_state_sample = jnp.concatenate([cache_conv[:, :, dec_seq:, :].astype(F32), u_s4], axis=2)
    chunk_v_prompt = v_chunk[None]
    chunk_v_sample = v_s.reshape(1, dec_batch, dec_seq, C_GMLP)
    return (y_prompt, y_sample, conv_state_prompt, conv_state_sample, chunk_v_prompt, chunk_v_sample)
```

```python
import functools

import jax
import jax.numpy as jnp
from jax import lax
from jax.experimental import pallas as pl
from jax.experimental.pallas import tpu as pltpu

F32 = jnp.float32
BF16 = jnp.bfloat16

D_MODEL = 1024
C_CONV = 512
C_GMLP = 512
N_CONV_GROUPS = 8
CONV_W = 31
N_HEADS = 8
HEAD_DIM = C_GMLP // N_HEADS
CHUNK = 128
N_EXPERTS = 32
TOP_K = 4
D_FF = 1024
PLE_DIM = 256
SWIGLU_LIMIT = 7.0
SWIGLU_ALPHA = 1.702
LN_EPS = 1e-5
DEPTH = 1
DEEPNORM_ALPHA = (2.0 * DEPTH) ** 0.25

LANES = 128
SUBLANES = 8
MXU_DIM = 256
VMEM_LIMIT = 48 * 1024 * 1024
BIG_VMEM_LIMIT = 56 * 1024 * 1024

T_PROMPT = 512
PROMPT_SPLIT = 2
HIST = 32
CONV_STRIDE = 4
SEQS_PER_TILE = 64
SAMPLE_PITCH = CONV_W - 1
T_RANK = 1024
T_DISPATCH = 1024
T_ROWS = 256
TM = 512
TM_CHAIN = 256


def _dot(a, b):
    return jnp.dot(a, b, preferred_element_type=F32)


def _layer_norm(x, g, b):
    mu = jnp.mean(x, axis=-1, keepdims=True)
    xc = x - mu
    var = jnp.mean(xc * xc, axis=-1, keepdims=True)
    return xc * lax.rsqrt(var + LN_EPS) * g + b


def _split_bf16(a):
    hi = a.astype(BF16)
    lo = (a - hi.astype(F32)).astype(BF16)
    return hi, lo


def _group_mean(a, gmat_ref):
    hi, lo = _split_bf16(a)
    g = gmat_ref[...]
    outs = []
    for s in range(C_CONV // MXU_DIM):
        sl = slice(MXU_DIM * s, MXU_DIM * (s + 1))
        outs.append(_dot(hi[:, sl], g) + _dot(lo[:, sl], g))
    return jnp.concatenate(outs, axis=1)


def _group_norm_silu(y, gmat_ref, gn_g, gn_b):
    mu = _group_mean(y, gmat_ref)
    yc = y - mu
    var = _group_mean(yc * yc, gmat_ref)
    yn = yc * lax.rsqrt(var + LN_EPS) * gn_g + gn_b
    return yn * jax.nn.sigmoid(yn)


def _store_token_tiles(ref, val, tok0=0):
    n = val.shape[0]
    for c in range(D_MODEL // LANES):
        ref[pl.ds(tok0 * SUBLANES + c, n, stride=SUBLANES), :] = val[:, c * LANES:(c + 1) * LANES]


def _load_token_tiles(ref, n, lead=(), tok0=0):
    parts = [ref[lead + (pl.ds(tok0 * SUBLANES + c, n, stride=SUBLANES), slice(None))]
             for c in range(D_MODEL // LANES)]
    return jnp.concatenate(parts, axis=1)


def _front(x, w):
    xn = _layer_norm(x, w["ln_in_g"][...], w["ln_in_b"][...])
    z = _dot(xn.astype(BF16), w["w_in"][...])
    a_val = z[:, 0:C_CONV]
    a_gate = z[:, C_CONV:2 * C_CONV]
    g_u = z[:, 2 * C_CONV:2 * C_CONV + C_GMLP]
    g_v = z[:, 2 * C_CONV + C_GMLP:]
    u = a_val * jax.nn.sigmoid(a_gate)
    ug = jax.nn.gelu(g_u)
    v = _layer_norm(jax.nn.gelu(g_v), w["vn_g"][...], w["vn_b"][...])
    return xn, u, ug, v


def _tail(xn, y_a, y_b, w, h_ref, idx_ref, gate_ref, row0=0):
    rows = slice(row0, row0 + xn.shape[0])
    mix = _dot(y_a.astype(BF16), w["w_out"][0:C_CONV, :]) + _dot(y_b.astype(BF16), w["w_out"][C_CONV:, :])
    h = _layer_norm(DEEPNORM_ALPHA * xn + mix, w["ln1_g"][...], w["ln1_b"][...])
    hb, h_lo = _split_bf16(h)
    _store_token_tiles(h_ref, h, row0)

    pair = _dot(hb, w["wr_pair"][...])
    logits = pair[:, :LANES] + pair[:, LANES:] + _dot(h_lo, w["wr_hi"][...]) + w["b_r"][...]
    lane = lax.broadcasted_iota(jnp.int32, logits.shape, 1)
    lane_f = lane.astype(F32)
    vals = jnp.where(lane < N_EXPERTS, logits, -jnp.inf)
    tops, ids = [], []
    for _ in range(TOP_K):
        m = jnp.max(vals, axis=-1, keepdims=True)
        i = jnp.min(jnp.where(vals == m, lane_f, float(LANES)), axis=-1, keepdims=True)
        vals = jnp.where(lane_f == i, -jnp.inf, vals)
        tops.append(m)
        ids.append(i)
    exps = [jnp.exp(m - tops[0]) for m in tops]
    denom = exps[0] + exps[1] + exps[2] + exps[3]
    idx_out = jnp.zeros(logits.shape, F32)
    gate_out = jnp.zeros(logits.shape, F32)
    for k in range(TOP_K):
        idx_out = jnp.where(lane == k, ids[k], idx_out)
        gate_out = jnp.where(lane == k, exps[k] / denom, gate_out)
    idx_ref[rows, :] = idx_out.astype(jnp.int32)
    gate_ref[rows, :] = gate_out


_WEIGHT_NAMES = ("ln_in_g", "ln_in_b", "w_in", "vn_g", "vn_b", "gmat", "gn_g", "gn_b", "w_out", "ln1_g", "ln1_b",
                 "wr_hi", "wr_pair", "b_r")


def _prompt_branch(j, x_ref, w, cwb_ref, ws_ref, bs_ref, outs, tail_ref, vch_ref, ubuf, yslab, last_j):
    t = T_PROMPT
    n = t // PROMPT_SPLIT
    n_slabs = C_CONV // LANES

    @pl.when(j == 0)
    def _():
        ubuf[:, 0:HIST, :] = jnp.zeros((n_slabs, HIST, LANES), F32)

    @pl.when(j > 0)
    def _():
        ubuf[:, 0:HIST, :] = ubuf[:, t:t + HIST, :]

    def front(h):
        xn, u, ug, v = _front(x_ref[h * n:(h + 1) * n, :], w)
        for s in range(n_slabs):
            ubuf[s, HIST + h * n:HIST + (h + 1) * n, :] = u[:, s * LANES:(s + 1) * LANES]
        return xn, u, ug, v

    def conv(h):
        rows = CONV_STRIDE * SUBLANES
        first = HIST - (CONV_W - 1)
        for s in range(n_slabs):
            for c in range(h * n // rows, (h + 1) * n // rows):
                accs = [None] * CONV_STRIDE
                for shift in range(CONV_STRIDE + CONV_W - 1):
                    win = ubuf[s, pl.ds(first + c * rows + shift, SUBLANES, stride=CONV_STRIDE), :]
                    for ph in range(CONV_STRIDE):
                        k = shift - ph
                        if 0 <= k < CONV_W:
                            term = cwb_ref[k, s] * win
                            accs[ph] = term if accs[ph] is None else accs[ph] + term
                for ph in range(CONV_STRIDE):
                    yslab[s, pl.ds(c * rows + ph, SUBLANES, stride=CONV_STRIDE), :] = accs[ph]
        y_conv = jnp.concatenate([yslab[s, h * n:(h + 1) * n, :] for s in range(n_slabs)], axis=1)
        return _group_norm_silu(y_conv, w["gmat"], w["gn_g"][...], w["gn_b"][...])

    def spatial_gate(ug, v):
        lane = lax.broadcasted_iota(jnp.int32, (CHUNK, LANES), 1)
        mixed_chunks = []
        for c in range(n // CHUNK):
            vc = v[c * CHUNK:(c + 1) * CHUNK, :]
            parts = []
            for q in range(N_HEADS // 2):
                vp = vc[:, q * LANES:(q + 1) * LANES]
                rhs = jnp.concatenate([jnp.where(lane < HEAD_DIM, vp, 0.0), jnp.where(lane >= HEAD_DIM, vp, 0.0)],
                                      axis=0).astype(BF16)
                parts.append(_dot(ws_ref[q], rhs))
            mixed_chunks.append(jnp.concatenate(parts, axis=1) + bs_ref[...])
        return ug * jnp.concatenate(mixed_chunks, axis=0)

    def tail(h, f, y_a, y_b):
        _tail(f[0], y_a, y_b, w, *outs, row0=h * n)

    fronts = [front(0)]
    y_as = {}
    for h in range(PROMPT_SPLIT):
        if h + 1 < PROMPT_SPLIT:
            fronts.append(front(h + 1))
        y_as[h] = conv(h)
        if h > 0:
            tail(h - 1, fronts[h - 1], y_as.pop(h - 1), spatial_gate(fronts[h - 1][2], fronts[h - 1][3]))
    last = PROMPT_SPLIT - 1
    tail(last, fronts[last], y_as.pop(last), spatial_gate(fronts[last][2], fronts[last][3]))

    @pl.when(j == last_j)
    def _():
        tail_ref[...] = fronts[last][1][n - HIST:, :]
        vch_ref[...] = fronts[last][3][n - CHUNK:, :]


def _sample_branch(x_ref, cslab_ref, w, cwb_ref, convub_ref, gwb_ref, b8b_ref, outs, u_out_ref, v_out_ref,
                   uslab, vslab, yslab, mslab):
    xn, u, ug, v = _front(x_ref[...], w)
    u_out_ref[...] = u
    v_out_ref[...] = v
    n_slabs = C_CONV // LANES
    for s in range(n_slabs):
        uslab[s] = u[:, s * LANES:(s + 1) * LANES]
        vslab[s] = v[:, s * LANES:(s + 1) * LANES]
    n_cache = CONV_W - 1

    def group(g, carry):
        row0 = g * (SUBLANES * SUBLANES)
        for s in range(n_slabs):
            cache = [cslab_ref[pl.ds(g * SUBLANES, SUBLANES), j, pl.ds(s * LANES, LANES)] for j in range(n_cache)]
            u_pos = [uslab[s, pl.ds(row0 + q, SUBLANES, stride=SUBLANES), :] for q in range(SUBLANES)]
            v_pos = [vslab[s, pl.ds(row0 + q, SUBLANES, stride=SUBLANES), :] for q in range(SUBLANES)]
            for t in range(SUBLANES):
                acc = None
                for k in range(n_cache - t):
                    term = cwb_ref[k, s] * cache[t + k]
                    acc = term if acc is None else acc + term
                mix = b8b_ref[t, s]
                for q in range(t + 1):
                    acc = acc + convub_ref[q, t, s] * u_pos[q]
                    mix = mix + gwb_ref[q, t, s] * v_pos[q]
                yslab[s, pl.ds(row0 + t, SUBLANES, stride=SUBLANES), :] = acc
                mslab[s, pl.ds(row0 + t, SUBLANES, stride=SUBLANES), :] = mix
        return carry

    lax.fori_loop(0, SEQS_PER_TILE // SUBLANES, group, 0)
    y_conv = jnp.concatenate([yslab[s] for s in range(n_slabs)], axis=1)
    mixed = jnp.concatenate([mslab[s] for s in range(n_slabs)], axis=1)
    y_a = _group_norm_silu(y_conv, w["gmat"], w["gn_g"][...], w["gn_b"][...])
    _tail(xn, y_a, ug * mixed, w, *outs)


def _mixer_body(*refs, n_prompt_tiles, tiles_per_seq):
    n_w = len(_WEIGHT_NAMES)
    xp_ref, xs_ref, cslab_ref = refs[:3]
    w = dict(zip(_WEIGHT_NAMES, refs[3:3 + n_w]))
    cwb_ref, ws_ref, bs_ref, convub_ref, gwb_ref, b8b_ref = refs[3 + n_w:9 + n_w]
    outs = refs[9 + n_w:12 + n_w]
    tail_ref, vch_ref, u_out_ref, v_out_ref = refs[12 + n_w:16 + n_w]
    ubuf, yslab, uslab, vslab, mslab = refs[16 + n_w:]
    step = pl.program_id(0)

    @pl.when(step < n_prompt_tiles)
    def _():
        _prompt_branch(lax.rem(step, tiles_per_seq), xp_ref, w, cwb_ref, ws_ref, bs_ref, outs, tail_ref,
                       vch_ref, ubuf, yslab, tiles_per_seq - 1)

    @pl.when(step >= n_prompt_tiles)
    def _():
        _sample_branch(xs_ref, cslab_ref, w, cwb_ref, convub_ref, gwb_ref, b8b_ref, outs, u_out_ref, v_out_ref,
                       uslab, vslab, yslab, mslab)


def _rank_body(idx_ref, tri_ref, before_ref, counts_ref, carry):
    i = pl.program_id(0)

    @pl.when(i == 0)
    def _():
        carry[...] = jnp.zeros(carry.shape, F32)

    idx = idx_ref[...]
    lane = lax.broadcasted_iota(jnp.int32, idx.shape, 1)
    multi = jnp.zeros(idx.shape, F32)
    for k in range(TOP_K):
        multi = multi + (lane == idx[:, k:k + 1]).astype(F32)
    before_ref[...] = (_dot(tri_ref[...], multi.astype(BF16)) + carry[...]).astype(jnp.int32)
    carry[...] = carry[...] + jnp.sum(multi, axis=0, keepdims=True)
    counts_ref[...] = carry[...].astype(jnp.int32)


def _pad_bits():
    b = TM // 2
    while b >= 1:
        yield b
        b //= 2


def _token_rows(t, n=1):
    return pl.ds(pl.multiple_of(t * SUBLANES, SUBLANES), n * SUBLANES)


def _dispatch_body(pad_off_ref, pad_n_ref, n_used_ref, dest_ref, h_ref, pp_ref, ps_ref, wg_ref, wp_ref, xs_ref, base_ref,
                   zbuf, sem, zsem, *, n_blocks, n_prompt_tiles):
    i = pl.program_id(0)

    def issue(r, carry):
        src = h_ref.at[_token_rows(r), :]
        for k in range(TOP_K):
            d = dest_ref[0, r * TOP_K + k]
            pltpu.make_async_copy(src, xs_ref.at[_token_rows(d), :], sem).start(priority=k % 2)
        return carry

    @pl.when(i == 0)
    def _():
        zbuf[...] = jnp.zeros(zbuf.shape, F32)

        def zero_copy(off, b):
            return pltpu.make_async_copy(zbuf.at[pl.ds(0, b * SUBLANES), :], xs_ref.at[_token_rows(off, b), :], zsem)

        def start_or_wait(cond, cp, wait):
            @pl.when(cond)
            def _():
                if wait:
                    cp.wait()
                else:
                    cp.start()

        for wait in (False, True):
            for e in range(N_EXPERTS):
                n = pad_n_ref[e]
                for b in _pad_bits():
                    start_or_wait((n & b) != 0, zero_copy(pad_off_ref[e] + (n & ~(2 * b - 1)), b), wait)

        half = TM // 2

        def tail_block(wait):
            def go(blk, carry):
                for s in range(TM // half):
                    cp = zero_copy(blk * TM + s * half, half)
                    cp.wait() if wait else cp.start()
                return carry
            return go

        lax.fori_loop(n_used_ref[0], n_blocks, tail_block(False), 0)
        lax.fori_loop(n_used_ref[0], n_blocks, tail_block(True), 0)

    rows = TM_CHAIN
    is_prompt = i < n_prompt_tiles
    for c in range(T_DISPATCH // rows):
        lax.fori_loop(c * rows, (c + 1) * rows, issue, 0, unroll=8)
        sl = slice(c * rows, (c + 1) * rows)
        h = _load_token_tiles(h_ref, rows, tok0=c * rows)
        p = jnp.where(is_prompt, pp_ref[sl, :], ps_ref[sl, :])
        ple = _dot(p.astype(BF16), wp_ref[...]) * jax.nn.sigmoid(_dot(h.astype(BF16), wg_ref[...]))
        base_ref[sl, :] = DEEPNORM_ALPHA * h + ple

    for _ in range(TOP_K):
        pltpu.make_async_copy(h_ref, xs_ref.at[_token_rows(0, T_DISPATCH), :], sem).wait()


def _expert_body(blk_e_ref, first_ref, slot_ref, next_ref, rows_ref, n_used_ref, x_ref, bu_ref, bd_ref, wu_hbm, wd_hbm,
                 y_ref, wu_f32, wd_f32, wu_bf, wd_bf, sems):
    i = pl.program_id(0)

    def weight_copies(e, s):
        return (pltpu.make_async_copy(wu_hbm.at[e], wu_f32.at[s], sems.at[s]),
                pltpu.make_async_copy(wd_hbm.at[e], wd_f32.at[s], sems.at[s]))

    @pl.when(i < n_used_ref[0])
    def _():
        s = slot_ref[i]

        @pl.when(first_ref[i] == 1)
        def _():
            @pl.when(i == 0)
            def _():
                for cp in weight_copies(blk_e_ref[i], s):
                    cp.start()

            for cp in weight_copies(blk_e_ref[i], s):
                cp.wait()
            nxt = next_ref[i]

            @pl.when(nxt >= 0)
            def _():
                for cp in weight_copies(nxt, 1 - s):
                    cp.start()

            chunk = D_MODEL // SUBLANES

            def cast_rows(c, carry):
                r = pl.multiple_of(c * chunk, chunk)
                wu_bf[pl.ds(r, chunk), :] = wu_f32[s, pl.ds(r, chunk), :].astype(BF16)
                wd_bf[pl.ds(r, chunk), :] = wd_f32[s, pl.ds(r, chunk), :].astype(BF16)
                return carry

            lax.fori_loop(0, SUBLANES, cast_rows, 0)

        def chain(c):
            x = _load_token_tiles(x_ref, TM_CHAIN, tok0=c * TM_CHAIN)
            hcat = _dot(x.astype(BF16), wu_bf[...]) + bu_ref[...]
            h_glu = jnp.minimum(hcat[:, :D_FF], SWIGLU_LIMIT)
            h_lin = jnp.clip(hcat[:, D_FF:], -SWIGLU_LIMIT, SWIGLU_LIMIT)
            act = h_glu * jax.nn.sigmoid(SWIGLU_ALPHA * h_glu) * (h_lin + 1.0)
            _store_token_tiles(y_ref, _dot(act.astype(BF16), wd_bf[...]) + bd_ref[...], tok0=c * TM_CHAIN)

        n_chains = TM // TM_CHAIN
        live = (rows_ref[i] + TM_CHAIN - 1) // TM_CHAIN
        for m in range(1, n_chains + 1):
            @pl.when(live == m)
            def _():
                for c in range(m):
                    chain(c)
                if m < n_chains:
                    y_ref[m * TM_CHAIN * SUBLANES:, :] = jnp.zeros(((n_chains - m) * TM_CHAIN * SUBLANES, LANES), F32)

    @pl.when(pl.program_id(0) >= n_used_ref[0])
    def _():
        y_ref[...] = jnp.zeros(y_ref.shape, F32)


def _combine_body(dest_ref, dest_next_ref, y_ref, base_ref, gate_ref, g_ref, b_ref, outp_ref, outs_ref, ybuf, sems, *,
                  n_prompt_tiles, n_tiles):
    i = pl.program_id(0)

    def issue_tile(d_ref, tile, slot):
        first = lax.rem(tile, T_DISPATCH // T_ROWS) * (T_ROWS * TOP_K)

        def issue(r, carry):
            for k in range(TOP_K):
                d = d_ref[0, first + r * TOP_K + k]
                pltpu.make_async_copy(y_ref.at[_token_rows(d), :], ybuf.at[slot * TOP_K + k, _token_rows(r), :],
                                      sems.at[slot]).start(priority=k % 2)
            return carry

        lax.fori_loop(0, T_ROWS, issue, 0, unroll=8)

    slot = lax.rem(i, 2)

    @pl.when(i == 0)
    def _():
        issue_tile(dest_ref, i, 0)

    @pl.when(i + 1 < n_tiles)
    def _():
        issue_tile(dest_next_ref, i + 1, 1 - slot)

    for k in range(TOP_K):
        pltpu.make_async_copy(y_ref.at[_token_rows(0, T_ROWS), :], ybuf.at[slot * TOP_K + k], sems.at[slot]).wait()

    gate = gate_ref[...]
    acc = base_ref[...]
    for k in range(TOP_K):
        acc = acc + gate[:, k:k + 1] * _load_token_tiles(ybuf, T_ROWS, lead=(slot * TOP_K + k,))
    out = _layer_norm(acc, g_ref[...], b_ref[...])

    @pl.when(i < n_prompt_tiles)
    def _():
        outp_ref[...] = out

    @pl.when(i >= n_prompt_tiles)
    def _():
        outs_ref[...] = out


def _full(shape):
    return pl.BlockSpec(shape, lambda *_: (0,) * len(shape), pipeline_mode=pl.Buffered(1))


def kernel(x_prompt, x_sample, cache_conv, p_prompt, p_sample, ln_in_g, ln_in_b, w_in, conv_w, gn_g, gn_b, vn_g, vn_b,
           w_spatial, b_spatial, w_out, ln1_g, ln1_b, w_router, b_router, w_up, b_up, w_down, b_down, w_ple,
           w_ple_gate, ln2_g, ln2_b):
    batch, seq, _ = x_prompt.shape
    dec_batch, dec_seq, _ = x_sample.shape
    assert w_in.shape[0] == DEPTH and dec_seq == SUBLANES and seq % T_PROMPT == 0
    n_prompt = batch * seq
    n_sample = dec_batch * dec_seq
    n_tok = n_prompt + n_sample
    t_s = SEQS_PER_TILE * dec_seq
    assert n_prompt % T_ROWS == 0 and n_sample % T_ROWS == 0 and n_tok % T_RANK == 0 and n_sample % t_s == 0
    assert n_tok % T_DISPATCH == 0

    row = lambda a: a.reshape(1, -1).astype(F32)
    gidx = jnp.arange(MXU_DIM) // (C_CONV // N_CONV_GROUPS)
    gmat = jnp.where(gidx[:, None] == gidx[None, :], 1.0 / (C_CONV // N_CONV_GROUPS), 0.0).astype(BF16)
    wr_pad = jnp.pad(w_router[0].astype(F32), ((0, 0), (0, LANES - N_EXPERTS)))
    wr_hi = wr_pad.astype(BF16)
    wr_lo = (wr_pad - wr_hi.astype(F32)).astype(BF16)
    weights = dict(
        ln_in_g=row(ln_in_g), ln_in_b=row(ln_in_b), w_in=w_in[0].astype(BF16), vn_g=row(vn_g[0]), vn_b=row(vn_b[0]),
        gmat=gmat, gn_g=row(gn_g[0]), gn_b=row(gn_b[0]), w_out=w_out[0].astype(BF16), ln1_g=row(ln1_g[0]),
        ln1_b=row(ln1_b[0]), wr_hi=wr_hi, wr_pair=jnp.concatenate([wr_hi, wr_lo], axis=1),
        b_r=jnp.pad(row(b_router[0]), ((0, 0), (0, LANES - N_EXPERTS))))
    w_list = [weights[n] for n in _WEIGHT_NAMES]
    w_specs = [_full(a.shape) for a in w_list]
    causal = jnp.tril(jnp.ones((CHUNK, CHUNK), bool))
    ws_m = jnp.where(causal[None], w_spatial[0], 0.0)
    ws_cat = jnp.concatenate([ws_m[0::2], ws_m[1::2]], axis=2).astype(BF16)
    bs_full = jnp.repeat(b_spatial[0].T.astype(F32), HEAD_DIM, axis=1)
    s_i = jnp.arange(SUBLANES)[:, None]
    t_i = jnp.arange(SUBLANES)[None, :]
    tap = jnp.clip(CONV_W - 1 - t_i + s_i, 0, CONV_W - 1)
    convu = jnp.where((s_i <= t_i)[:, :, None], conv_w[0].astype(F32)[tap], 0.0)
    gw8 = jnp.transpose(ws_m[:, :SUBLANES, :SUBLANES], (2, 1, 0))
    gw8 = jnp.repeat(gw8.astype(F32), HEAD_DIM, axis=2)
    b8 = bs_full[:SUBLANES]
    n_slabs = C_CONV // LANES
    slab_rows = lambda a: jnp.broadcast_to(a.reshape(a.shape[:-1] + (n_slabs, 1, LANES)),
                                           a.shape[:-1] + (n_slabs, SUBLANES, LANES))
    convub, gwb, b8b = slab_rows(convu), slab_rows(gw8), slab_rows(b8)
    cslab = cache_conv.astype(F32)

    cparams = lambda sem: pltpu.CompilerParams(dimension_semantics=sem, vmem_limit_bytes=VMEM_LIMIT)

    nj = seq // T_PROMPT
    npt = n_prompt // T_PROMPT
    assert t_s == T_PROMPT
    pstep = lambda i: jnp.minimum(i, npt - 1)
    sstep = lambda i: jnp.maximum(i - npt, 0)
    tok_block = lambda width: pl.BlockSpec((T_PROMPT, width), lambda i: (i, 0))
    cwb = slab_rows(conv_w[0].astype(F32))
    tables = [cwb, ws_cat, bs_full, convub, gwb, b8b]
    h_all, idx_all, gate_all, u_tail, v_chunk, u_s, v_s = pl.pallas_call(
        functools.partial(_mixer_body, n_prompt_tiles=npt, tiles_per_seq=nj),
        grid=(npt + n_sample // t_s,),
        in_specs=[pl.BlockSpec((None, T_PROMPT, D_MODEL), lambda i: (pstep(i) // nj, pstep(i) % nj, 0)),
                  pl.BlockSpec((t_s, D_MODEL), lambda i: (sstep(i), 0)),
                  pl.BlockSpec((None, SEQS_PER_TILE, CONV_W - 1, C_CONV), lambda i: (0, sstep(i), 0, 0),
                               pipeline_mode=pl.Buffered(1))]
                 + w_specs + [_full(a.shape) for a in tables],
        out_specs=[pl.BlockSpec((T_PROMPT * SUBLANES, LANES), lambda i: (i, 0)),
                   tok_block(LANES), tok_block(LANES),
                   pl.BlockSpec((None, HIST, C_CONV), lambda i: (pstep(i) // nj, 0, 0)),
                   pl.BlockSpec((None, CHUNK, C_GMLP), lambda i: (pstep(i) // nj, 0, 0)),
                   pl.BlockSpec((t_s, C_CONV), lambda i: (sstep(i), 0)),
                   pl.BlockSpec((t_s, C_GMLP), lambda i: (sstep(i), 0))],
        out_shape=(jax.ShapeDtypeStruct((n_tok * SUBLANES, LANES), F32),
                   jax.ShapeDtypeStruct((n_tok, LANES), jnp.int32), jax.ShapeDtypeStruct((n_tok, LANES), F32),
                   jax.ShapeDtypeStruct((batch, HIST, C_CONV), F32), jax.ShapeDtypeStruct((batch, CHUNK, C_GMLP), F32),
                   jax.ShapeDtypeStruct((n_sample, C_CONV), F32), jax.ShapeDtypeStruct((n_sample, C_GMLP), F32)),
        scratch_shapes=[pltpu.VMEM((n_slabs, T_PROMPT + HIST, LANES), F32)]
                       + [pltpu.VMEM((n_slabs, T_PROMPT, LANES), F32)] * 4,
        compiler_params=pltpu.CompilerParams(dimension_semantics=("arbitrary",), vmem_limit_bytes=BIG_VMEM_LIMIT),
        name="mixer",
    )(x_prompt, x_sample.reshape(n_sample, D_MODEL), cslab, *w_list, *tables)

    tri = (jnp.arange(T_RANK)[:, None] > jnp.arange(T_RANK)[None, :]).astype(BF16)
    before_all, counts = pl.pallas_call(
        _rank_body,
        grid=(n_tok // T_RANK,),
        in_specs=[pl.BlockSpec((T_RANK, LANES), lambda i: (i, 0)), _full(tri.shape)],
        out_specs=[pl.BlockSpec((T_RANK, LANES), lambda i: (i, 0)), _full((1, LANES))],
        out_shape=(jax.ShapeDtypeStruct((n_tok, LANES), jnp.int32), jax.ShapeDtypeStruct((1, LANES), jnp.int32)),
        scratch_shapes=[pltpu.VMEM((1, LANES), F32)],
        compiler_params=cparams(("arbitrary",)),
        name="rank",
    )(idx_all, tri)

    n_assign = n_tok * TOP_K
    n_blocks = n_assign // TM + N_EXPERTS
    cap = n_blocks * TM
    cnt = counts[0, :N_EXPERTS]
    padded = (cnt + TM - 1) // TM * TM
    p_end = jnp.cumsum(padded)
    p_start = p_end - padded
    e_ids = jnp.arange(N_EXPERTS, dtype=jnp.int32)
    lookup = lambda table, ids: jnp.sum(jnp.where(ids[..., None] == e_ids, table, 0), axis=-1)
    slot_table = p_start[None, :] + before_all[:, :N_EXPERTS]
    dest = jnp.sum(jnp.where(idx_all[:, :TOP_K, None] == e_ids, slot_table[:, None, :], 0), axis=-1).astype(jnp.int32)
    dest_tiles = dest.reshape(n_tok // T_DISPATCH, 1, T_DISPATCH * TOP_K)
    blk_row = jnp.arange(n_blocks, dtype=jnp.int32) * TM
    blk_e = jnp.minimum(jnp.sum(p_end[None, :] <= blk_row[:, None], axis=1), N_EXPERTS - 1).astype(jnp.int32)
    n_used = (p_end[-1:] // TM).astype(jnp.int32)
    pad_off = (p_start + cnt).astype(jnp.int32)
    pad_n = (padded - cnt).astype(jnp.int32)
    used = cnt > 0
    slot_e = (jnp.cumsum(used.astype(jnp.int32)) - 1) & 1
    later_used = jnp.where(used[None, :] & (e_ids[None, :] > e_ids[:, None]), e_ids[None, :], N_EXPERTS)
    next_e = jnp.min(later_used, axis=1)
    next_e = jnp.where(next_e < N_EXPERTS, next_e, -1).astype(jnp.int32)
    blk_first = ((blk_row == lookup(p_start, blk_e)) & (blk_row < p_end[-1])).astype(jnp.int32)
    blk_rows = jnp.clip(lookup(p_start + cnt, blk_e) - blk_row, 0, TM).astype(jnp.int32)
    blk_slot = lookup(slot_e, blk_e).astype(jnp.int32)
    blk_next = lookup(next_e, blk_e).astype(jnp.int32)


    assert seq % T_DISPATCH == 0 and n_sample % T_DISPATCH == 0
    npd = n_prompt // T_DISPATCH
    per_seq = seq // T_DISPATCH
    pd = lambda i: jnp.minimum(i, npd - 1)
    w_gate = w_ple_gate[0].astype(BF16)
    w_ple_b = w_ple[0].astype(BF16)
    x_sorted, base_all = pl.pallas_call(
        functools.partial(_dispatch_body, n_blocks=n_blocks, n_prompt_tiles=npd),
        grid_spec=pltpu.PrefetchScalarGridSpec(
            num_scalar_prefetch=3,
            grid=(n_tok // T_DISPATCH,),
            in_specs=[pl.BlockSpec((None, 1, T_DISPATCH * TOP_K), lambda i, *_: (i, 0, 0), memory_space=pltpu.SMEM),
                      pl.BlockSpec((T_DISPATCH * SUBLANES, LANES), lambda i, *_: (i, 0)),
                      pl.BlockSpec((None, T_DISPATCH, PLE_DIM), lambda i, *_: (pd(i) // per_seq, pd(i) % per_seq, 0)),
                      pl.BlockSpec((T_DISPATCH, PLE_DIM), lambda i, *_: (jnp.maximum(i - npd, 0), 0)),
                      _full(w_gate.shape), _full(w_ple_b.shape)],
            out_specs=[pl.BlockSpec(memory_space=pl.ANY),
                       pl.BlockSpec((T_DISPATCH, D_MODEL), lambda i, *_: (i, 0))],
            scratch_shapes=[pltpu.VMEM((TM // 2 * SUBLANES, LANES), F32), pltpu.SemaphoreType.DMA,
                            pltpu.SemaphoreType.DMA]),
        out_shape=(jax.ShapeDtypeStruct((cap * SUBLANES, LANES), F32), jax.ShapeDtypeStruct((n_tok, D_MODEL), F32)),
        compiler_params=cparams(("arbitrary",)),
        name="dispatch",
    )(pad_off, pad_n, n_used, dest_tiles, h_all, p_prompt[0], p_sample[0].reshape(n_sample, PLE_DIM), w_gate, w_ple_b)

    last = lambda i, nu: jnp.minimum(i, nu[0] - 1)
    y_sorted = pl.pallas_call(
        _expert_body,
        grid_spec=pltpu.PrefetchScalarGridSpec(
            num_scalar_prefetch=6,
            grid=(n_blocks,),
            in_specs=[pl.BlockSpec((TM * SUBLANES, LANES), lambda i, be, bf, bs, bn, br, nu: (last(i, nu), 0)),
                      pl.BlockSpec((None, 1, 2 * D_FF), lambda i, be, bf, bs, bn, br, nu: (be[last(i, nu)], 0, 0)),
                      pl.BlockSpec((None, 1, D_MODEL), lambda i, be, bf, bs, bn, br, nu: (be[last(i, nu)], 0, 0)),
                      pl.BlockSpec(memory_space=pl.ANY), pl.BlockSpec(memory_space=pl.ANY)],
            out_specs=pl.BlockSpec((TM * SUBLANES, LANES), lambda i, be, bf, bs, bn, br, nu: (i, 0)),
            scratch_shapes=[pltpu.VMEM((2, D_MODEL, 2 * D_FF), F32), pltpu.VMEM((2, D_FF, D_MODEL), F32),
                            pltpu.VMEM((D_MODEL, 2 * D_FF), BF16), pltpu.VMEM((D_FF, D_MODEL), BF16),
                            pltpu.SemaphoreType.DMA((2,))]),
        out_shape=jax.ShapeDtypeStruct((cap * SUBLANES, LANES), F32),
        compiler_params=pltpu.CompilerParams(dimension_semantics=("arbitrary",), vmem_limit_bytes=BIG_VMEM_LIMIT),
        name="experts",
    )(blk_e, blk_first, blk_slot, blk_next, blk_rows, n_used, x_sorted, b_up[0].astype(F32)[:, None, :],
      b_down[0].astype(F32)[:, None, :], w_up[0].astype(F32), w_down[0].astype(F32))

    npt = n_prompt // T_ROWS
    n_tiles = n_tok // T_ROWS
    per = T_DISPATCH // T_ROWS
    dest_spec = pl.BlockSpec((None, 1, T_DISPATCH * TOP_K), lambda i: (i // per, 0, 0), memory_space=pltpu.SMEM)
    dest_next_spec = pl.BlockSpec((None, 1, T_DISPATCH * TOP_K),
                                  lambda i: (jnp.minimum(i + 1, n_tiles - 1) // per, 0, 0), memory_space=pltpu.SMEM)
    out_p, out_s = pl.pallas_call(
        functools.partial(_combine_body, n_prompt_tiles=npt, n_tiles=n_tiles),
        grid=(n_tiles,),
        in_specs=[dest_spec, dest_next_spec, pl.BlockSpec(memory_space=pl.ANY),
                  pl.BlockSpec((T_ROWS, D_MODEL), lambda i: (i, 0)), pl.BlockSpec((T_ROWS, LANES), lambda i: (i, 0)),
                  _full((1, D_MODEL)), _full((1, D_MODEL))],
        out_specs=[pl.BlockSpec((T_ROWS, D_MODEL), lambda i: (jnp.minimum(i, npt - 1), 0)),
                   pl.BlockSpec((T_ROWS, D_MODEL), lambda i: (jnp.maximum(i - npt, 0), 0))],
        out_shape=(jax.ShapeDtypeStruct((n_prompt, D_MODEL), F32), jax.ShapeDtypeStruct((n_sample, D_MODEL), F32)),
        scratch_shapes=[pltpu.VMEM((2 * TOP_K, T_ROWS * SUBLANES, LANES), F32), pltpu.SemaphoreType.DMA((2,))],
        compiler_params=cparams(("arbitrary",)),
        name="combine",
    )(dest_tiles, dest_tiles, y_sorted, base_all, gate_all, row(ln2_g[0]), row(ln2_b[0]))

    y_prompt = out_p.reshape(batch, seq, D_MODEL)
    y_sample = out_s.reshape(dec_batch, dec_seq, D_MODEL)
    conv_state_prompt = u_tail[None, :, HIST - (CONV_W - 1):, :]
    u_s4 = u_s.reshape(1, dec_batch, dec_seq, C_CONV)
    conv_state_sample = jnp.concatenate([cache_conv[:, :, dec_seq:, :].astype(F32), u_s4], axis=2)
    chunk_v_prompt = v_chunk[None]
    chunk_v_sample = v_s.reshape(1, dec_batch, dec_seq, C_GMLP)
    return (y_prompt, y_sample, conv_state_prompt, conv_state_sample, chunk_v_prompt, chunk_v_sample)
```

```python
import functools

import jax
import jax.numpy as jnp
from jax import lax
from jax.experimental import pallas as pl
from jax.experimental.pallas import tpu as pltpu

F32 = jnp.float32
BF16 = jnp.bfloat16

D_MODEL = 1024
C_CONV = 512
C_GMLP = 512
N_CONV_GROUPS = 8
CONV_W = 31
N_HEADS = 8
HEAD_DIM = C_GMLP // N_HEADS
CHUNK = 128
N_EXPERTS = 32
TOP_K = 4
D_FF = 1024
PLE_DIM = 256
SWIGLU_LIMIT = 7.0
SWIGLU_ALPHA = 1.702
LN_EPS = 1e-5
DEPTH = 1
DEEPNORM_ALPHA = (2.0 * DEPTH) ** 0.25

LANES = 128
SUBLANES = 8
MXU_DIM = 256
VMEM_LIMIT = 48 * 1024 * 1024
BIG_VMEM_LIMIT = 56 * 1024 * 1024

T_PROMPT = 512
PROMPT_SPLIT = 2
HIST = 32
CONV_STRIDE = 4
SEQS_PER_TILE = 64
SAMPLE_PITCH = CONV_W - 1
T_RANK = 1024
T_DISPATCH = 1024
T_ROWS = 256
TM = 512
TM_CHAIN = 256


def _dot(a, b):
    return jnp.dot(a, b, preferred_element_type=F32)


def _layer_norm(x, g, b):
    mu = jnp.mean(x, axis=-1, keepdims=True)
    xc = x - mu
    var = jnp.mean(xc * xc, axis=-1, keepdims=True)
    return xc * lax.rsqrt(var + LN_EPS) * g + b


def _split_bf16(a):
    hi = a.astype(BF16)
    lo = (a - hi.astype(F32)).astype(BF16)
    return hi, lo


def _group_mean(a, gmat_ref):
    hi, lo = _split_bf16(a)
    g = gmat_ref[...]
    outs = []
    for s in range(C_CONV // MXU_DIM):
        sl = slice(MXU_DIM * s, MXU_DIM * (s + 1))
        outs.append(_dot(hi[:, sl], g) + _dot(lo[:, sl], g))
    return jnp.concatenate(outs, axis=1)


def _group_norm_silu(y, gmat_ref, gn_g, gn_b):
    mu = _group_mean(y, gmat_ref)
    yc = y - mu
    var = _group_mean(yc * yc, gmat_ref)
    yn = yc * lax.rsqrt(var + LN_EPS) * gn_g + gn_b
    return yn * jax.nn.sigmoid(yn)


def _store_token_tiles(ref, val, tok0=0):
    n = val.shape[0]
    for c in range(D_MODEL // LANES):
        ref[pl.ds(tok0 * SUBLANES + c, n, stride=SUBLANES), :] = val[:, c * LANES:(c + 1) * LANES]


def _load_token_tiles(ref, n, lead=(), tok0=0):
    parts = [ref[lead + (pl.ds(tok0 * SUBLANES + c, n, stride=SUBLANES), slice(None))]
             for c in range(D_MODEL // LANES)]
    return jnp.concatenate(parts, axis=1)


def _front(x, w):
    xn = _layer_norm(x, w["ln_in_g"][...], w["ln_in_b"][...])
    z = _dot(xn.astype(BF16), w["w_in"][...])
    a_val = z[:, 0:C_CONV]
    a_gate = z[:, C_CONV:2 * C_CONV]
    g_u = z[:, 2 * C_CONV:2 * C_CONV + C_GMLP]
    g_v = z[:, 2 * C_CONV + C_GMLP:]
    u = a_val * jax.nn.sigmoid(a_gate)
    ug = jax.nn.gelu(g_u)
    v = _layer_norm(jax.nn.gelu(g_v), w["vn_g"][...], w["vn_b"][...])
    return xn, u, ug, v


def _tail(xn, y_a, y_b, w, h_ref, idx_ref, gate_ref, row0=0):
    rows = slice(row0, row0 + xn.shape[0])
    mix = _dot(y_a.astype(BF16), w["w_out"][0:C_CONV, :]) + _dot(y_b.astype(BF16), w["w_out"][C_CONV:, :])
    h = _layer_norm(DEEPNORM_ALPHA * xn + mix, w["ln1_g"][...], w["ln1_b"][...])
    hb, h_lo = _split_bf16(h)
    _store_token_tiles(h_ref, h, row0)

    pair = _dot(hb, w["wr_pair"][...])
    logits = pair[:, :LANES] + pair[:, LANES:] + _dot(h_lo, w["wr_hi"][...]) + w["b_r"][...]
    lane = lax.broadcasted_iota(jnp.int32, logits.shape, 1)
    lane_f = lane.astype(F32)
    vals = jnp.where(lane < N_EXPERTS, logits, -jnp.inf)
    tops, ids = [], []
    for _ in range(TOP_K):
        m = jnp.max(vals, axis=-1, keepdims=True)
        i = jnp.min(jnp.where(vals == m, lane_f, float(LANES)), axis=-1, keepdims=True)
        vals = jnp.where(lane_f == i, -jnp.inf, vals)
        tops.append(m)
        ids.append(i)
    exps = [jnp.exp(m - tops[0]) for m in tops]
    denom = exps[0] + exps[1] + exps[2] + exps[3]
    idx_out = jnp.zeros(logits.shape, F32)
    gate_out = jnp.zeros(logits.shape, F32)
    for k in range(TOP_K):
        idx_out = jnp.where(lane == k, ids[k], idx_out)
        gate_out = jnp.where(lane == k, exps[k] / denom, gate_out)
    idx_ref[rows, :] = idx_out.astype(jnp.int32)
    gate_ref[rows, :] = gate_out


_WEIGHT_NAMES = ("ln_in_g", "ln_in_b", "w_in", "vn_g", "vn_b", "gmat", "gn_g", "gn_b", "w_out", "ln1_g", "ln1_b",
                 "wr_hi", "wr_pair", "b_r")


def _prompt_branch(j, x_ref, w, cwb_ref, ws_ref, bs_ref, outs, tail_ref, vch_ref, ubuf, yslab, last_j):
    t = T_PROMPT
    n = t // PROMPT_SPLIT
    n_slabs = C_CONV // LANES

    @pl.when(j == 0)
    def _():
        ubuf[:, 0:HIST, :] = jnp.zeros((n_slabs, HIST, LANES), F32)

    @pl.when(j > 0)
    def _():
        ubuf[:, 0:HIST, :] = ubuf[:, t:t + HIST, :]

    def front(h):
        xn, u, ug, v = _front(x_ref[h * n:(h + 1) * n, :], w)
        for s in range(n_slabs):
            ubuf[s, HIST + h * n:HIST + (h + 1) * n, :] = u[:, s * LANES:(s + 1) * LANES]
        return xn, u, ug, v

    def conv(h):
        rows = CONV_STRIDE * SUBLANES
        first = HIST - (CONV_W - 1)
        for s in range(n_slabs):
            for c in range(h * n // rows, (h + 1) * n // rows):
                accs = [None] * CONV_STRIDE
                for shift in range(CONV_STRIDE + CONV_W - 1):
                    win = ubuf[s, pl.ds(first + c * rows + shift, SUBLANES, stride=CONV_STRIDE), :]
                    for ph in range(CONV_STRIDE):
                        k = shift - ph
                        if 0 <= k < CONV_W:
                            term = cwb_ref[k, s] * win
                            accs[ph] = term if accs[ph] is None else accs[ph] + term
                for ph in range(CONV_STRIDE):
                    yslab[s, pl.ds(c * rows + ph, SUBLANES, stride=CONV_STRIDE), :] = accs[ph]
        y_conv = jnp.concatenate([yslab[s, h * n:(h + 1) * n, :] for s in range(n_slabs)], axis=1)
        return _group_norm_silu(y_conv, w["gmat"], w["gn_g"][...], w["gn_b"][...])

    def spatial_gate(ug, v):
        lane = lax.broadcasted_iota(jnp.int32, (CHUNK, LANES), 1)
        mixed_chunks = []
        for c in range(n // CHUNK):
            vc = v[c * CHUNK:(c + 1) * CHUNK, :]
            parts = []
            for q in range(N_HEADS // 2):
                vp = vc[:, q * LANES:(q + 1) * LANES]
                rhs = jnp.concatenate([jnp.where(lane < HEAD_DIM, vp, 0.0), jnp.where(lane >= HEAD_DIM, vp, 0.0)],
                                      axis=0).astype(BF16)
                parts.append(_dot(ws_ref[q], rhs))
            mixed_chunks.append(jnp.concatenate(parts, axis=1) + bs_ref[...])
        return ug * jnp.concatenate(mixed_chunks, axis=0)

    def tail(h, f, y_a, y_b):
        _tail(f[0], y_a, y_b, w, *outs, row0=h * n)

    fronts = [front(0)]
    y_as = {}
    for h in range(PROMPT_SPLIT):
        if h + 1 < PROMPT_SPLIT:
            fronts.append(front(h + 1))
        y_as[h] = conv(h)
        if h > 0:
            tail(h - 1, fronts[h - 1], y_as.pop(h - 1), spatial_gate(fronts[h - 1][2], fronts[h - 1][3]))
    last = PROMPT_SPLIT - 1
    tail(last, fronts[last], y_as.pop(last), spatial_gate(fronts[last][2], fronts[last][3]))

    @pl.when(j == last_j)
    def _():
        tail_ref[...] = fronts[last][1][n - HIST:, :]
        vch_ref[...] = fronts[last][3][n - CHUNK:, :]


def _sample_branch(x_ref, cslab_ref, w, cwb_ref, convub_ref, gwb_ref, b8b_ref, outs, u_out_ref, v_out_ref,
                   uslab, vslab, yslab, mslab):
    xn, u, ug, v = _front(x_ref[...], w)
    n_keep = CONV_W - 1 - SUBLANES
    whole = n_keep // SUBLANES * SUBLANES
    rolled = pltpu.roll(u.reshape(SEQS_PER_TILE, SUBLANES, C_CONV), n_keep - whole, axis=1)
    u_out_ref[:, 0:whole, :] = cslab_ref[:, SUBLANES:SUBLANES + whole, :]
    u_out_ref[:, whole:whole + SUBLANES, :] = jnp.concatenate(
        [cslab_ref[:, SUBLANES + whole:, :], rolled[:, n_keep - whole:, :]], axis=1)
    u_out_ref[:, whole + SUBLANES:, :] = rolled[:, 0:n_keep - whole, :]
    v_out_ref[...] = v
    n_slabs = C_CONV // LANES
    for s in range(n_slabs):
        uslab[s] = u[:, s * LANES:(s + 1) * LANES]
        vslab[s] = v[:, s * LANES:(s + 1) * LANES]
    n_cache = CONV_W - 1

    def group(g, carry):
        row0 = g * (SUBLANES * SUBLANES)
        for s in range(n_slabs):
            cache = [cslab_ref[pl.ds(g * SUBLANES, SUBLANES), j, pl.ds(s * LANES, LANES)] for j in range(n_cache)]
            u_pos = [uslab[s, pl.ds(row0 + q, SUBLANES, stride=SUBLANES), :] for q in range(SUBLANES)]
            v_pos = [vslab[s, pl.ds(row0 + q, SUBLANES, stride=SUBLANES), :] for q in range(SUBLANES)]
            for t in range(SUBLANES):
                acc = None
                for k in range(n_cache - t):
                    term = cwb_ref[k, s] * cache[t + k]
                    acc = term if acc is None else acc + term
                mix = b8b_ref[t, s]
                for q in range(t + 1):
                    acc = acc + convub_ref[q, t, s] * u_pos[q]
                    mix = mix + gwb_ref[q, t, s] * v_pos[q]
                yslab[s, pl.ds(row0 + t, SUBLANES, stride=SUBLANES), :] = acc
                mslab[s, pl.ds(row0 + t, SUBLANES, stride=SUBLANES), :] = mix
        return carry

    lax.fori_loop(0, SEQS_PER_TILE // SUBLANES, group, 0)
    y_conv = jnp.concatenate([yslab[s] for s in range(n_slabs)], axis=1)
    mixed = jnp.concatenate([mslab[s] for s in range(n_slabs)], axis=1)
    y_a = _group_norm_silu(y_conv, w["gmat"], w["gn_g"][...], w["gn_b"][...])
    _tail(xn, y_a, ug * mixed, w, *outs)


def _mixer_body(*refs, n_prompt_tiles, tiles_per_seq):
    n_w = len(_WEIGHT_NAMES)
    xp_ref, xs_ref, cslab_ref = refs[:3]
    w = dict(zip(_WEIGHT_NAMES, refs[3:3 + n_w]))
    cwb_ref, ws_ref, bs_ref, convub_ref, gwb_ref, b8b_ref = refs[3 + n_w:9 + n_w]
    outs = refs[9 + n_w:12 + n_w]
    tail_ref, vch_ref, u_out_ref, v_out_ref = refs[12 + n_w:16 + n_w]
    ubuf, yslab, uslab, vslab, mslab = refs[16 + n_w:]
    step = pl.program_id(0)

    @pl.when(step < n_prompt_tiles)
    def _():
        _prompt_branch(lax.rem(step, tiles_per_seq), xp_ref, w, cwb_ref, ws_ref, bs_ref, outs, tail_ref,
                       vch_ref, ubuf, yslab, tiles_per_seq - 1)

    @pl.when(step >= n_prompt_tiles)
    def _():
        _sample_branch(xs_ref, cslab_ref, w, cwb_ref, convub_ref, gwb_ref, b8b_ref, outs, u_out_ref, v_out_ref,
                       uslab, vslab, yslab, mslab)


def _rank_body(idx_ref, tri_ref, before_ref, counts_ref, carry):
    i = pl.program_id(0)

    @pl.when(i == 0)
    def _():
        carry[...] = jnp.zeros(carry.shape, F32)

    idx = idx_ref[...]
    lane = lax.broadcasted_iota(jnp.int32, idx.shape, 1)
    multi = jnp.zeros(idx.shape, F32)
    for k in range(TOP_K):
        multi = multi + (lane == idx[:, k:k + 1]).astype(F32)
    before_ref[...] = (_dot(tri_ref[...], multi.astype(BF16)) + carry[...]).astype(jnp.int32)
    carry[...] = carry[...] + jnp.sum(multi, axis=0, keepdims=True)
    counts_ref[...] = carry[...].astype(jnp.int32)


def _pad_bits():
    b = TM // 2
    while b >= 1:
        yield b
        b //= 2


def _token_rows(t, n=1):
    return pl.ds(pl.multiple_of(t * SUBLANES, SUBLANES), n * SUBLANES)


def _dispatch_body(pad_off_ref, pad_n_ref, n_used_ref, dest_ref, h_ref, pp_ref, ps_ref, wg_ref, wp_ref, xs_ref, base_ref,
                   zbuf, sem, zsem, *, n_blocks, n_prompt_tiles):
    i = pl.program_id(0)

    def issue(r, carry):
        src = h_ref.at[_token_rows(r), :]
        for k in range(TOP_K):
            d = dest_ref[0, r * TOP_K + k]
            pltpu.make_async_copy(src, xs_ref.at[_token_rows(d), :], sem).start(priority=k % 2)
        return carry

    @pl.when(i == 0)
    def _():
        zbuf[...] = jnp.zeros(zbuf.shape, F32)

        def zero_copy(off, b):
            return pltpu.make_async_copy(zbuf.at[pl.ds(0, b * SUBLANES), :], xs_ref.at[_token_rows(off, b), :], zsem)

        def start_or_wait(cond, cp, wait):
            @pl.when(cond)
            def _():
                if wait:
                    cp.wait()
                else:
                    cp.start()

        for wait in (False, True):
            for e in range(N_EXPERTS):
                n = pad_n_ref[e]
                for b in _pad_bits():
                    start_or_wait((n & b) != 0, zero_copy(pad_off_ref[e] + (n & ~(2 * b - 1)), b), wait)

        half = TM // 2

        def tail_block(wait):
            def go(blk, carry):
                for s in range(TM // half):
                    cp = zero_copy(blk * TM + s * half, half)
                    cp.wait() if wait else cp.start()
                return carry
            return go

        lax.fori_loop(n_used_ref[0], n_blocks, tail_block(False), 0)
        lax.fori_loop(n_used_ref[0], n_blocks, tail_block(True), 0)

    rows = TM_CHAIN
    is_prompt = i < n_prompt_tiles
    for c in range(T_DISPATCH // rows):
        lax.fori_loop(c * rows, (c + 1) * rows, issue, 0, unroll=8)
        sl = slice(c * rows, (c + 1) * rows)
        h = _load_token_tiles(h_ref, rows, tok0=c * rows)
        p = jnp.where(is_prompt, pp_ref[sl, :], ps_ref[sl, :])
        ple = _dot(p.astype(BF16), wp_ref[...]) * jax.nn.sigmoid(_dot(h.astype(BF16), wg_ref[...]))
        base_ref[sl, :] = DEEPNORM_ALPHA * h + ple

    for _ in range(TOP_K):
        pltpu.make_async_copy(h_ref, xs_ref.at[_token_rows(0, T_DISPATCH), :], sem).wait()


def _expert_body(blk_e_ref, first_ref, slot_ref, next_ref, rows_ref, n_used_ref, x_ref, bu_ref, bd_ref, wu_hbm, wd_hbm,
                 y_ref, wu_f32, wd_f32, wu_bf, wd_bf, sems):
    i = pl.program_id(0)

    def weight_copies(e, s):
        return (pltpu.make_async_copy(wu_hbm.at[e], wu_f32.at[s], sems.at[s]),
                pltpu.make_async_copy(wd_hbm.at[e], wd_f32.at[s], sems.at[s]))

    @pl.when(i < n_used_ref[0])
    def _():
        s = slot_ref[i]

        @pl.when(first_ref[i] == 1)
        def _():
            @pl.when(i == 0)
            def _():
                for cp in weight_copies(blk_e_ref[i], s):
                    cp.start()

            for cp in weight_copies(blk_e_ref[i], s):
                cp.wait()
            nxt = next_ref[i]

            @pl.when(nxt >= 0)
            def _():
                for cp in weight_copies(nxt, 1 - s):
                    cp.start()

            chunk = D_MODEL // SUBLANES

            def cast_rows(c, carry):
                r = pl.multiple_of(c * chunk, chunk)
                wu_bf[pl.ds(r, chunk), :] = wu_f32[s, pl.ds(r, chunk), :].astype(BF16)
                wd_bf[pl.ds(r, chunk), :] = wd_f32[s, pl.ds(r, chunk), :].astype(BF16)
                return carry

            lax.fori_loop(0, SUBLANES, cast_rows, 0)

        def chain(c):
            x = _load_token_tiles(x_ref, TM_CHAIN, tok0=c * TM_CHAIN)
            hcat = _dot(x.astype(BF16), wu_bf[...]) + bu_ref[...]
            h_glu = jnp.minimum(hcat[:, :D_FF], SWIGLU_LIMIT)
            h_lin = jnp.clip(hcat[:, D_FF:], -SWIGLU_LIMIT, SWIGLU_LIMIT)
            act = h_glu * jax.nn.sigmoid(SWIGLU_ALPHA * h_glu) * (h_lin + 1.0)
            _store_token_tiles(y_ref, _dot(act.astype(BF16), wd_bf[...]) + bd_ref[...], tok0=c * TM_CHAIN)

        n_chains = TM // TM_CHAIN
        live = (rows_ref[i] + TM_CHAIN - 1) // TM_CHAIN
        for m in range(1, n_chains + 1):
            @pl.when(live == m)
            def _():
                for c in range(m):
                    chain(c)
                if m < n_chains:
                    y_ref[m * TM_CHAIN * SUBLANES:, :] = jnp.zeros(((n_chains - m) * TM_CHAIN * SUBLANES, LANES), F32)

    @pl.when(pl.program_id(0) >= n_used_ref[0])
    def _():
        y_ref[...] = jnp.zeros(y_ref.shape, F32)


def _combine_body(dest_ref, dest_next_ref, y_ref, base_ref, gate_ref, g_ref, b_ref, outp_ref, outs_ref, ybuf, sems, *,
                  n_prompt_tiles, n_tiles):
    i = pl.program_id(0)

    def issue_tile(d_ref, tile, slot):
        first = lax.rem(tile, T_DISPATCH // T_ROWS) * (T_ROWS * TOP_K)

        def issue(r, carry):
            for k in range(TOP_K):
                d = d_ref[0, first + r * TOP_K + k]
                pltpu.make_async_copy(y_ref.at[_token_rows(d), :], ybuf.at[slot * TOP_K + k, _token_rows(r), :],
                                      sems.at[slot]).start(priority=k % 2)
            return carry

        lax.fori_loop(0, T_ROWS, issue, 0, unroll=8)

    slot = lax.rem(i, 2)

    @pl.when(i == 0)
    def _():
        issue_tile(dest_ref, i, 0)

    @pl.when(i + 1 < n_tiles)
    def _():
        issue_tile(dest_next_ref, i + 1, 1 - slot)

    for k in range(TOP_K):
        pltpu.make_async_copy(y_ref.at[_token_rows(0, T_ROWS), :], ybuf.at[slot * TOP_K + k], sems.at[slot]).wait()

    gate = gate_ref[...]
    acc = base_ref[...]
    for k in range(TOP_K):
        acc = acc + gate[:, k:k + 1] * _load_token_tiles(ybuf, T_ROWS, lead=(slot * TOP_K + k,))
    out = _layer_norm(acc, g_ref[...], b_ref[...])

    @pl.when(i < n_prompt_tiles)
    def _():
        outp_ref[...] = out

    @pl.when(i >= n_prompt_tiles)
    def _():
        outs_ref[...] = out


def _full(shape):
    return pl.BlockSpec(shape, lambda *_: (0,) * len(shape), pipeline_mode=pl.Buffered(1))


def kernel(x_prompt, x_sample, cache_conv, p_prompt, p_sample, ln_in_g, ln_in_b, w_in, conv_w, gn_g, gn_b, vn_g, vn_b,
           w_spatial, b_spatial, w_out, ln1_g, ln1_b, w_router, b_router, w_up, b_up, w_down, b_down, w_ple,
           w_ple_gate, ln2_g, ln2_b):
    batch, seq, _ = x_prompt.shape
    dec_batch, dec_seq, _ = x_sample.shape
    assert w_in.shape[0] == DEPTH and dec_seq == SUBLANES and seq % T_PROMPT == 0
    n_prompt = batch * seq
    n_sample = dec_batch * dec_seq
    n_tok = n_prompt + n_sample
    t_s = SEQS_PER_TILE * dec_seq
    assert n_prompt % T_ROWS == 0 and n_sample % T_ROWS == 0 and n_tok % T_RANK == 0 and n_sample % t_s == 0
    assert n_tok % T_DISPATCH == 0

    row = lambda a: a.reshape(1, -1).astype(F32)
    gidx = jnp.arange(MXU_DIM) // (C_CONV // N_CONV_GROUPS)
    gmat = jnp.where(gidx[:, None] == gidx[None, :], 1.0 / (C_CONV // N_CONV_GROUPS), 0.0).astype(BF16)
    wr_pad = jnp.pad(w_router[0].astype(F32), ((0, 0), (0, LANES - N_EXPERTS)))
    wr_hi = wr_pad.astype(BF16)
    wr_lo = (wr_pad - wr_hi.astype(F32)).astype(BF16)
    weights = dict(
        ln_in_g=row(ln_in_g), ln_in_b=row(ln_in_b), w_in=w_in[0].astype(BF16), vn_g=row(vn_g[0]), vn_b=row(vn_b[0]),
        gmat=gmat, gn_g=row(gn_g[0]), gn_b=row(gn_b[0]), w_out=w_out[0].astype(BF16), ln1_g=row(ln1_g[0]),
        ln1_b=row(ln1_b[0]), wr_hi=wr_hi, wr_pair=jnp.concatenate([wr_hi, wr_lo], axis=1),
        b_r=jnp.pad(row(b_router[0]), ((0, 0), (0, LANES - N_EXPERTS))))
    w_list = [weights[n] for n in _WEIGHT_NAMES]
    w_specs = [_full(a.shape) for a in w_list]
    causal = jnp.tril(jnp.ones((CHUNK, CHUNK), bool))
    ws_m = jnp.where(causal[None], w_spatial[0], 0.0)
    ws_cat = jnp.concatenate([ws_m[0::2], ws_m[1::2]], axis=2).astype(BF16)
    bs_full = jnp.repeat(b_spatial[0].T.astype(F32), HEAD_DIM, axis=1)
    s_i = jnp.arange(SUBLANES)[:, None]
    t_i = jnp.arange(SUBLANES)[None, :]
    tap = jnp.clip(CONV_W - 1 - t_i + s_i, 0, CONV_W - 1)
    convu = jnp.where((s_i <= t_i)[:, :, None], conv_w[0].astype(F32)[tap], 0.0)
    gw8 = jnp.transpose(ws_m[:, :SUBLANES, :SUBLANES], (2, 1, 0))
    gw8 = jnp.repeat(gw8.astype(F32), HEAD_DIM, axis=2)
    b8 = bs_full[:SUBLANES]
    n_slabs = C_CONV // LANES
    slab_rows = lambda a: jnp.broadcast_to(a.reshape(a.shape[:-1] + (n_slabs, 1, LANES)),
                                           a.shape[:-1] + (n_slabs, SUBLANES, LANES))
    convub, gwb, b8b = slab_rows(convu), slab_rows(gw8), slab_rows(b8)
    cslab = cache_conv.astype(F32)

    cparams = lambda sem: pltpu.CompilerParams(dimension_semantics=sem, vmem_limit_bytes=VMEM_LIMIT)

    nj = seq // T_PROMPT
    npt = n_prompt // T_PROMPT
    assert t_s == T_PROMPT
    pstep = lambda i: jnp.minimum(i, npt - 1)
    sstep = lambda i: jnp.maximum(i - npt, 0)
    tok_block = lambda width: pl.BlockSpec((T_PROMPT, width), lambda i: (i, 0))
    cwb = slab_rows(conv_w[0].astype(F32))
    tables = [cwb, ws_cat, bs_full, convub, gwb, b8b]
    h_all, idx_all, gate_all, u_tail, v_chunk, conv_state_sample, v_s = pl.pallas_call(
        functools.partial(_mixer_body, n_prompt_tiles=npt, tiles_per_seq=nj),
        grid=(npt + n_sample // t_s,),
        in_specs=[pl.BlockSpec((None, T_PROMPT, D_MODEL), lambda i: (pstep(i) // nj, pstep(i) % nj, 0)),
                  pl.BlockSpec((t_s, D_MODEL), lambda i: (sstep(i), 0)),
                  pl.BlockSpec((None, SEQS_PER_TILE, CONV_W - 1, C_CONV), lambda i: (0, sstep(i), 0, 0),
                               pipeline_mode=pl.Buffered(1))]
                 + w_specs + [_full(a.shape) for a in tables],
        out_specs=[pl.BlockSpec((T_PROMPT * SUBLANES, LANES), lambda i: (i, 0)),
                   tok_block(LANES), tok_block(LANES),
                   pl.BlockSpec((None, HIST, C_CONV), lambda i: (pstep(i) // nj, 0, 0)),
                   pl.BlockSpec((None, CHUNK, C_GMLP), lambda i: (pstep(i) // nj, 0, 0)),
                   pl.BlockSpec((None, SEQS_PER_TILE, CONV_W - 1, C_CONV), lambda i: (0, sstep(i), 0, 0)),
                   pl.BlockSpec((t_s, C_GMLP), lambda i: (sstep(i), 0))],
        out_shape=(jax.ShapeDtypeStruct((n_tok * SUBLANES, LANES), F32),
                   jax.ShapeDtypeStruct((n_tok, LANES), jnp.int32), jax.ShapeDtypeStruct((n_tok, LANES), F32),
                   jax.ShapeDtypeStruct((batch, HIST, C_CONV), F32), jax.ShapeDtypeStruct((batch, CHUNK, C_GMLP), F32),
                   jax.ShapeDtypeStruct((1, dec_batch, CONV_W - 1, C_CONV), F32),
                   jax.ShapeDtypeStruct((n_sample, C_GMLP), F32)),
        scratch_shapes=[pltpu.VMEM((n_slabs, T_PROMPT + HIST, LANES), F32)]
                       + [pltpu.VMEM((n_slabs, T_PROMPT, LANES), F32)] * 4,
        compiler_params=pltpu.CompilerParams(dimension_semantics=("arbitrary",), vmem_limit_bytes=BIG_VMEM_LIMIT),
        name="mixer",
    )(x_prompt, x_sample.reshape(n_sample, D_MODEL), cslab, *w_list, *tables)

    tri = (jnp.arange(T_RANK)[:, None] > jnp.arange(T_RANK)[None, :]).astype(BF16)
    before_all, counts = pl.pallas_call(
        _rank_body,
        grid=(n_tok // T_RANK,),
        in_specs=[pl.BlockSpec((T_RANK, LANES), lambda i: (i, 0)), _full(tri.shape)],
        out_specs=[pl.BlockSpec((T_RANK, LANES), lambda i: (i, 0)), _full((1, LANES))],
        out_shape=(jax.ShapeDtypeStruct((n_tok, LANES), jnp.int32), jax.ShapeDtypeStruct((1, LANES), jnp.int32)),
        scratch_shapes=[pltpu.VMEM((1, LANES), F32)],
        compiler_params=cparams(("arbitrary",)),
        name="rank",
    )(idx_all, tri)

    n_assign = n_tok * TOP_K
    n_blocks = n_assign // TM + N_EXPERTS
    cap = n_blocks * TM
    cnt = counts[0, :N_EXPERTS]
    padded = (cnt + TM - 1) // TM * TM
    p_end = jnp.cumsum(padded)
    p_start = p_end - padded
    e_ids = jnp.arange(N_EXPERTS, dtype=jnp.int32)
    lookup = lambda table, ids: jnp.sum(jnp.where(ids[..., None] == e_ids, table, 0), axis=-1)
    slot_table = p_start[None, :] + before_all[:, :N_EXPERTS]
    dest = jnp.sum(jnp.where(idx_all[:, :TOP_K, None] == e_ids, slot_table[:, None, :], 0), axis=-1).astype(jnp.int32)
    dest_tiles = dest.reshape(n_tok // T_DISPATCH, 1, T_DISPATCH * TOP_K)
    blk_row = jnp.arange(n_blocks, dtype=jnp.int32) * TM
    blk_e = jnp.minimum(jnp.sum(p_end[None, :] <= blk_row[:, None], axis=1), N_EXPERTS - 1).astype(jnp.int32)
    n_used = (p_end[-1:] // TM).astype(jnp.int32)
    pad_off = (p_start + cnt).astype(jnp.int32)
    pad_n = (padded - cnt).astype(jnp.int32)
    used = cnt > 0
    slot_e = (jnp.cumsum(used.astype(jnp.int32)) - 1) & 1
    later_used = jnp.where(used[None, :] & (e_ids[None, :] > e_ids[:, None]), e_ids[None, :], N_EXPERTS)
    next_e = jnp.min(later_used, axis=1)
    next_e = jnp.where(next_e < N_EXPERTS, next_e, -1).astype(jnp.int32)
    blk_first = ((blk_row == lookup(p_start, blk_e)) & (blk_row < p_end[-1])).astype(jnp.int32)
    blk_rows = jnp.clip(lookup(p_start + cnt, blk_e) - blk_row, 0, TM).astype(jnp.int32)
    blk_slot = lookup(slot_e, blk_e).astype(jnp.int32)
    blk_next = lookup(next_e, blk_e).astype(jnp.int32)


    assert seq % T_DISPATCH == 0 and n_sample % T_DISPATCH == 0
    npd = n_prompt // T_DISPATCH
    per_seq = seq // T_DISPATCH
    pd = lambda i: jnp.minimum(i, npd - 1)
    w_gate = w_ple_gate[0].astype(BF16)
    w_ple_b = w_ple[0].astype(BF16)
    x_sorted, base_all = pl.pallas_call(
        functools.partial(_dispatch_body, n_blocks=n_blocks, n_prompt_tiles=npd),
        grid_spec=pltpu.PrefetchScalarGridSpec(
            num_scalar_prefetch=3,
            grid=(n_tok // T_DISPATCH,),
            in_specs=[pl.BlockSpec((None, 1, T_DISPATCH * TOP_K), lambda i, *_: (i, 0, 0), memory_space=pltpu.SMEM),
                      pl.BlockSpec((T_DISPATCH * SUBLANES, LANES), lambda i, *_: (i, 0)),
                      pl.BlockSpec((None, T_DISPATCH, PLE_DIM), lambda i, *_: (pd(i) // per_seq, pd(i) % per_seq, 0)),
                      pl.BlockSpec((T_DISPATCH, PLE_DIM), lambda i, *_: (jnp.maximum(i - npd, 0), 0)),
                      _full(w_gate.shape), _full(w_ple_b.shape)],
            out_specs=[pl.BlockSpec(memory_space=pl.ANY),
                       pl.BlockSpec((T_DISPATCH, D_MODEL), lambda i, *_: (i, 0))],
            scratch_shapes=[pltpu.VMEM((TM // 2 * SUBLANES, LANES), F32), pltpu.SemaphoreType.DMA,
                            pltpu.SemaphoreType.DMA]),
        out_shape=(jax.ShapeDtypeStruct((cap * SUBLANES, LANES), F32), jax.ShapeDtypeStruct((n_tok, D_MODEL), F32)),
        compiler_params=cparams(("arbitrary",)),
        name="dispatch",
    )(pad_off, pad_n, n_used, dest_tiles, h_all, p_prompt[0], p_sample[0].reshape(n_sample, PLE_DIM), w_gate, w_ple_b)

    last = lambda i, nu: jnp.minimum(i, nu[0] - 1)
    y_sorted = pl.pallas_call(
        _expert_body,
        grid_spec=pltpu.PrefetchScalarGridSpec(
            num_scalar_prefetch=6,
            grid=(n_blocks,),
            in_specs=[pl.BlockSpec((TM * SUBLANES, LANES), lambda i, be, bf, bs, bn, br, nu: (last(i, nu), 0)),
                      pl.BlockSpec((None, 1, 2 * D_FF), lambda i, be, bf, bs, bn, br, nu: (be[last(i, nu)], 0, 0)),
                      pl.BlockSpec((None, 1, D_MODEL), lambda i, be, bf, bs, bn, br, nu: (be[last(i, nu)], 0, 0)),
                      pl.BlockSpec(memory_space=pl.ANY), pl.BlockSpec(memory_space=pl.ANY)],
            out_specs=pl.BlockSpec((TM * SUBLANES, LANES), lambda i, be, bf, bs, bn, br, nu: (i, 0)),
            scratch_shapes=[pltpu.VMEM((2, D_MODEL, 2 * D_FF), F32), pltpu.VMEM((2, D_FF, D_MODEL), F32),
                            pltpu.VMEM((D_MODEL, 2 * D_FF), BF16), pltpu.VMEM((D_FF, D_MODEL), BF16),
                            pltpu.SemaphoreType.DMA((2,))]),
        out_shape=jax.ShapeDtypeStruct((cap * SUBLANES, LANES), F32),
        compiler_params=pltpu.CompilerParams(dimension_semantics=("arbitrary",), vmem_limit_bytes=BIG_VMEM_LIMIT),
        name="experts",
    )(blk_e, blk_first, blk_slot, blk_next, blk_rows, n_used, x_sorted, b_up[0].astype(F32)[:, None, :],
      b_down[0].astype(F32)[:, None, :], w_up[0].astype(F32), w_down[0].astype(F32))

    npt = n_prompt // T_ROWS
    n_tiles = n_tok // T_ROWS
    per = T_DISPATCH // T_ROWS
    dest_spec = pl.BlockSpec((None, 1, T_DISPATCH * TOP_K), lambda i: (i // per, 0, 0), memory_space=pltpu.SMEM)
    dest_next_spec = pl.BlockSpec((None, 1, T_DISPATCH * TOP_K),
                                  lambda i: (jnp.minimum(i + 1, n_tiles - 1) // per, 0, 0), memory_space=pltpu.SMEM)
    out_p, out_s = pl.pallas_call(
        functools.partial(_combine_body, n_prompt_tiles=npt, n_tiles=n_tiles),
        grid=(n_tiles,),
        in_specs=[dest_spec, dest_next_spec, pl.BlockSpec(memory_space=pl.ANY),
                  pl.BlockSpec((T_ROWS, D_MODEL), lambda i: (i, 0)), pl.BlockSpec((T_ROWS, LANES), lambda i: (i, 0)),
                  _full((1, D_MODEL)), _full((1, D_MODEL))],
        out_specs=[pl.BlockSpec((T_ROWS, D_MODEL), lambda i: (jnp.minimum(i, npt - 1), 0)),
                   pl.BlockSpec((T_ROWS, D_MODEL), lambda i: (jnp.maximum(i - npt, 0), 0))],
        out_shape=(jax.ShapeDtypeStruct((n_prompt, D_MODEL), F32), jax.ShapeDtypeStruct((n_sample, D_MODEL), F32)),
        scratch_shapes=[pltpu.VMEM((2 * TOP_K, T_ROWS * SUBLANES, LANES), F32), pltpu.SemaphoreType.DMA((2,))],
        compiler_params=cparams(("arbitrary",)),
        name="combine",
    )(dest_tiles, dest_tiles, y_sorted, base_all, gate_all, row(ln2_g[0]), row(ln2_b[0]))

    y_prompt = out_p.reshape(batch, seq, D_MODEL)
    y_sample = out_s.reshape(dec_batch, dec_seq, D_MODEL)
    conv_state_prompt = u_tail[None, :, HIST - (CONV_W - 1):, :]
    chunk_v_prompt = v_chunk[None]
    chunk_v_sample = v_s.reshape(1, dec_batch, dec_seq, C_GMLP)
    return (y_prompt, y_sample, conv_state_prompt, conv_state_sample, chunk_v_prompt, chunk_v_sample)
```

```python
import functools

import jax
import jax.numpy as jnp
from jax import lax
from jax.experimental import pallas as pl
from jax.experimental.pallas import tpu as pltpu

F32 = jnp.float32
BF16 = jnp.bfloat16

D_MODEL = 1024
C_CONV = 512
C_GMLP = 512
N_CONV_GROUPS = 8
CONV_W = 31
N_HEADS = 8
HEAD_DIM = C_GMLP // N_HEADS
CHUNK = 128
N_EXPERTS = 32
TOP_K = 4
D_FF = 1024
PLE_DIM = 256
SWIGLU_LIMIT = 7.0
SWIGLU_ALPHA = 1.702
LN_EPS = 1e-5
DEPTH = 1
DEEPNORM_ALPHA = (2.0 * DEPTH) ** 0.25

LANES = 128
SUBLANES = 8
MXU_DIM = 256
VMEM_LIMIT = 48 * 1024 * 1024
BIG_VMEM_LIMIT = 56 * 1024 * 1024

T_PROMPT = 512
PROMPT_SPLIT = 2
HIST = 32
CONV_STRIDE = 4
SEQS_PER_TILE = 64
SAMPLE_PITCH = CONV_W - 1
T_RANK = 1024
T_DISPATCH = 1024
T_ROWS = 256
COMBINE_RING = 3
TM = 512
TM_CHAIN = 256


def _dot(a, b):
    return jnp.dot(a, b, preferred_element_type=F32)


def _layer_norm(x, g, b):
    mu = jnp.mean(x, axis=-1, keepdims=True)
    xc = x - mu
    var = jnp.mean(xc * xc, axis=-1, keepdims=True)
    return xc * lax.rsqrt(var + LN_EPS) * g + b


def _split_bf16(a):
    hi = a.astype(BF16)
    lo = (a - hi.astype(F32)).astype(BF16)
    return hi, lo


def _group_mean(a, gmat_ref):
    hi, lo = _split_bf16(a)
    g = gmat_ref[...]
    outs = []
    for s in range(C_CONV // MXU_DIM):
        sl = slice(MXU_DIM * s, MXU_DIM * (s + 1))
        outs.append(_dot(hi[:, sl], g) + _dot(lo[:, sl], g))
    return jnp.concatenate(outs, axis=1)


def _group_norm_silu(y, gmat_ref, gn_g, gn_b):
    mu = _group_mean(y, gmat_ref)
    yc = y - mu
    var = _group_mean(yc * yc, gmat_ref)
    yn = yc * lax.rsqrt(var + LN_EPS) * gn_g + gn_b
    return yn * jax.nn.sigmoid(yn)


def _store_token_tiles(ref, val, tok0=0):
    n = val.shape[0]
    for c in range(D_MODEL // LANES):
        ref[pl.ds(tok0 * SUBLANES + c, n, stride=SUBLANES), :] = val[:, c * LANES:(c + 1) * LANES]


def _load_token_tiles(ref, n, lead=(), tok0=0):
    parts = [ref[lead + (pl.ds(tok0 * SUBLANES + c, n, stride=SUBLANES), slice(None))]
             for c in range(D_MODEL // LANES)]
    return jnp.concatenate(parts, axis=1)


def _front(x, w):
    xn = _layer_norm(x, w["ln_in_g"][...], w["ln_in_b"][...])
    z = _dot(xn.astype(BF16), w["w_in"][...])
    a_val = z[:, 0:C_CONV]
    a_gate = z[:, C_CONV:2 * C_CONV]
    g_u = z[:, 2 * C_CONV:2 * C_CONV + C_GMLP]
    g_v = z[:, 2 * C_CONV + C_GMLP:]
    u = a_val * jax.nn.sigmoid(a_gate)
    ug = jax.nn.gelu(g_u)
    v = _layer_norm(jax.nn.gelu(g_v), w["vn_g"][...], w["vn_b"][...])
    return xn, u, ug, v


def _tail(xn, y_a, y_b, w, h_ref, idx_ref, gate_ref, row0=0):
    rows = slice(row0, row0 + xn.shape[0])
    mix = _dot(y_a.astype(BF16), w["w_out"][0:C_CONV, :]) + _dot(y_b.astype(BF16), w["w_out"][C_CONV:, :])
    h = _layer_norm(DEEPNORM_ALPHA * xn + mix, w["ln1_g"][...], w["ln1_b"][...])
    hb, h_lo = _split_bf16(h)
    _store_token_tiles(h_ref, h, row0)

    pair = _dot(hb, w["wr_pair"][...])
    logits = pair[:, :LANES] + pair[:, LANES:] + _dot(h_lo, w["wr_hi"][...]) + w["b_r"][...]
    lane = lax.broadcasted_iota(jnp.int32, logits.shape, 1)
    lane_f = lane.astype(F32)
    vals = jnp.where(lane < N_EXPERTS, logits, -jnp.inf)
    tops, ids = [], []
    for _ in range(TOP_K):
        m = jnp.max(vals, axis=-1, keepdims=True)
        i = jnp.min(jnp.where(vals == m, lane_f, float(LANES)), axis=-1, keepdims=True)
        vals = jnp.where(lane_f == i, -jnp.inf, vals)
        tops.append(m)
        ids.append(i)
    exps = [jnp.exp(m - tops[0]) for m in tops]
    denom = exps[0] + exps[1] + exps[2] + exps[3]
    idx_out = jnp.zeros(logits.shape, F32)
    gate_out = jnp.zeros(logits.shape, F32)
    for k in range(TOP_K):
        idx_out = jnp.where(lane == k, ids[k], idx_out)
        gate_out = jnp.where(lane == k, exps[k] / denom, gate_out)
    idx_ref[rows, :] = idx_out.astype(jnp.int32)
    gate_ref[rows, :] = gate_out


_WEIGHT_NAMES = ("ln_in_g", "ln_in_b", "w_in", "vn_g", "vn_b", "gmat", "gn_g", "gn_b", "w_out", "ln1_g", "ln1_b",
                 "wr_hi", "wr_pair", "b_r")


def _prompt_branch(j, x_ref, w, cwb_ref, ws_ref, bs_ref, outs, tail_ref, vch_ref, ubuf, yslab, last_j):
    t = T_PROMPT
    n = t // PROMPT_SPLIT
    n_slabs = C_CONV // LANES

    @pl.when(j == 0)
    def _():
        ubuf[:, 0:HIST, :] = jnp.zeros((n_slabs, HIST, LANES), F32)

    @pl.when(j > 0)
    def _():
        ubuf[:, 0:HIST, :] = ubuf[:, t:t + HIST, :]

    def front(h):
        xn, u, ug, v = _front(x_ref[h * n:(h + 1) * n, :], w)
        for s in range(n_slabs):
            ubuf[s, HIST + h * n:HIST + (h + 1) * n, :] = u[:, s * LANES:(s + 1) * LANES]
        return xn, u, ug, v

    def conv(h):
        rows = CONV_STRIDE * SUBLANES
        first = HIST - (CONV_W - 1)
        for s in range(n_slabs):
            for c in range(h * n // rows, (h + 1) * n // rows):
                accs = [None] * CONV_STRIDE
                for shift in range(CONV_STRIDE + CONV_W - 1):
                    win = ubuf[s, pl.ds(first + c * rows + shift, SUBLANES, stride=CONV_STRIDE), :]
                    for ph in range(CONV_STRIDE):
                        k = shift - ph
                        if 0 <= k < CONV_W:
                            term = cwb_ref[k, s] * win
                            accs[ph] = term if accs[ph] is None else accs[ph] + term
                for ph in range(CONV_STRIDE):
                    yslab[s, pl.ds(c * rows + ph, SUBLANES, stride=CONV_STRIDE), :] = accs[ph]
        y_conv = jnp.concatenate([yslab[s, h * n:(h + 1) * n, :] for s in range(n_slabs)], axis=1)
        return _group_norm_silu(y_conv, w["gmat"], w["gn_g"][...], w["gn_b"][...])

    def spatial_gate(ug, v):
        lane = lax.broadcasted_iota(jnp.int32, (CHUNK, LANES), 1)
        mixed_chunks = []
        for c in range(n // CHUNK):
            vc = v[c * CHUNK:(c + 1) * CHUNK, :]
            parts = []
            for q in range(N_HEADS // 2):
                vp = vc[:, q * LANES:(q + 1) * LANES]
                rhs = jnp.concatenate([jnp.where(lane < HEAD_DIM, vp, 0.0), jnp.where(lane >= HEAD_DIM, vp, 0.0)],
                                      axis=0).astype(BF16)
                parts.append(_dot(ws_ref[q], rhs))
            mixed_chunks.append(jnp.concatenate(parts, axis=1) + bs_ref[...])
        return ug * jnp.concatenate(mixed_chunks, axis=0)

    def tail(h, f, y_a, y_b):
        _tail(f[0], y_a, y_b, w, *outs, row0=h * n)

    fronts = [front(0)]
    y_as = {}
    for h in range(PROMPT_SPLIT):
        if h + 1 < PROMPT_SPLIT:
            fronts.append(front(h + 1))
        y_as[h] = conv(h)
        if h > 0:
            tail(h - 1, fronts[h - 1], y_as.pop(h - 1), spatial_gate(fronts[h - 1][2], fronts[h - 1][3]))
    last = PROMPT_SPLIT - 1
    tail(last, fronts[last], y_as.pop(last), spatial_gate(fronts[last][2], fronts[last][3]))

    @pl.when(j == last_j)
    def _():
        tail_ref[...] = fronts[last][1][n - HIST:, :]
        vch_ref[...] = fronts[last][3][n - CHUNK:, :]


def _sample_branch(x_ref, cslab_ref, w, cwb_ref, convub_ref, gwb_ref, b8b_ref, outs, u_out_ref, v_out_ref,
                   uslab, vslab, yslab, mslab):
    xn, u, ug, v = _front(x_ref[...], w)
    u_out_ref[...] = u
    v_out_ref[...] = v
    n_slabs = C_CONV // LANES
    for s in range(n_slabs):
        uslab[s] = u[:, s * LANES:(s + 1) * LANES]
        vslab[s] = v[:, s * LANES:(s + 1) * LANES]
    n_cache = CONV_W - 1

    def group(g, carry):
        row0 = g * (SUBLANES * SUBLANES)
        for s in range(n_slabs):
            cache = [cslab_ref[pl.ds(g * SUBLANES, SUBLANES), j, pl.ds(s * LANES, LANES)] for j in range(n_cache)]
            u_pos = [uslab[s, pl.ds(row0 + q, SUBLANES, stride=SUBLANES), :] for q in range(SUBLANES)]
            v_pos = [vslab[s, pl.ds(row0 + q, SUBLANES, stride=SUBLANES), :] for q in range(SUBLANES)]
            for t in range(SUBLANES):
                acc = None
                for k in range(n_cache - t):
                    term = cwb_ref[k, s] * cache[t + k]
                    acc = term if acc is None else acc + term
                mix = b8b_ref[t, s]
                for q in range(t + 1):
                    acc = acc + convub_ref[q, t, s] * u_pos[q]
                    mix = mix + gwb_ref[q, t, s] * v_pos[q]
                yslab[s, pl.ds(row0 + t, SUBLANES, stride=SUBLANES), :] = acc
                mslab[s, pl.ds(row0 + t, SUBLANES, stride=SUBLANES), :] = mix
        return carry

    lax.fori_loop(0, SEQS_PER_TILE // SUBLANES, group, 0)
    y_conv = jnp.concatenate([yslab[s] for s in range(n_slabs)], axis=1)
    mixed = jnp.concatenate([mslab[s] for s in range(n_slabs)], axis=1)
    y_a = _group_norm_silu(y_conv, w["gmat"], w["gn_g"][...], w["gn_b"][...])
    _tail(xn, y_a, ug * mixed, w, *outs)


def _mixer_body(*refs, n_prompt_tiles, tiles_per_seq):
    n_w = len(_WEIGHT_NAMES)
    xp_ref, xs_ref, cslab_ref = refs[:3]
    w = dict(zip(_WEIGHT_NAMES, refs[3:3 + n_w]))
    cwb_ref, ws_ref, bs_ref, convub_ref, gwb_ref, b8b_ref = refs[3 + n_w:9 + n_w]
    outs = refs[9 + n_w:12 + n_w]
    tail_ref, vch_ref, u_out_ref, v_out_ref = refs[12 + n_w:16 + n_w]
    ubuf, yslab, uslab, vslab, mslab = refs[16 + n_w:]
    step = pl.program_id(0)

    @pl.when(step < n_prompt_tiles)
    def _():
        _prompt_branch(lax.rem(step, tiles_per_seq), xp_ref, w, cwb_ref, ws_ref, bs_ref, outs, tail_ref,
                       vch_ref, ubuf, yslab, tiles_per_seq - 1)

    @pl.when(step >= n_prompt_tiles)
    def _():
        _sample_branch(xs_ref, cslab_ref, w, cwb_ref, convub_ref, gwb_ref, b8b_ref, outs, u_out_ref, v_out_ref,
                       uslab, vslab, yslab, mslab)


def _rank_body(idx_ref, tri_ref, before_ref, counts_ref, carry):
    i = pl.program_id(0)

    @pl.when(i == 0)
    def _():
        carry[...] = jnp.zeros(carry.shape, F32)

    idx = idx_ref[...]
    lane = lax.broadcasted_iota(jnp.int32, idx.shape, 1)
    multi = jnp.zeros(idx.shape, F32)
    for k in range(TOP_K):
        multi = multi + (lane == idx[:, k:k + 1]).astype(F32)
    before_ref[...] = (_dot(tri_ref[...], multi.astype(BF16)) + carry[...]).astype(jnp.int32)
    carry[...] = carry[...] + jnp.sum(multi, axis=0, keepdims=True)
    counts_ref[...] = carry[...].astype(jnp.int32)


def _pad_bits():
    b = TM // 2
    while b >= 1:
        yield b
        b //= 2


def _token_rows(t, n=1):
    return pl.ds(pl.multiple_of(t * SUBLANES, SUBLANES), n * SUBLANES)


def _dispatch_body(pad_off_ref, pad_n_ref, n_used_ref, dest_ref, h_ref, pp_ref, ps_ref, wg_ref, wp_ref, xs_ref, base_ref,
                   zbuf, sem, zsem, *, n_blocks, n_prompt_tiles):
    i = pl.program_id(0)

    def issue(r, carry):
        src = h_ref.at[_token_rows(r), :]
        for k in range(TOP_K):
            d = dest_ref[0, r * TOP_K + k]
            pltpu.make_async_copy(src, xs_ref.at[_token_rows(d), :], sem).start(priority=k % 2)
        return carry

    @pl.when(i == 0)
    def _():
        zbuf[...] = jnp.zeros(zbuf.shape, F32)

        def zero_copy(off, b):
            return pltpu.make_async_copy(zbuf.at[pl.ds(0, b * SUBLANES), :], xs_ref.at[_token_rows(off, b), :], zsem)

        def start_or_wait(cond, cp, wait):
            @pl.when(cond)
            def _():
                if wait:
                    cp.wait()
                else:
                    cp.start()

        for wait in (False, True):
            for e in range(N_EXPERTS):
                n = pad_n_ref[e]
                for b in _pad_bits():
                    start_or_wait((n & b) != 0, zero_copy(pad_off_ref[e] + (n & ~(2 * b - 1)), b), wait)

        half = TM // 2

        def tail_block(wait):
            def go(blk, carry):
                for s in range(TM // half):
                    cp = zero_copy(blk * TM + s * half, half)
                    cp.wait() if wait else cp.start()
                return carry
            return go

        lax.fori_loop(n_used_ref[0], n_blocks, tail_block(False), 0)
        lax.fori_loop(n_used_ref[0], n_blocks, tail_block(True), 0)

    rows = TM_CHAIN
    is_prompt = i < n_prompt_tiles
    for c in range(T_DISPATCH // rows):
        lax.fori_loop(c * rows, (c + 1) * rows, issue, 0, unroll=8)
        sl = slice(c * rows, (c + 1) * rows)
        h = _load_token_tiles(h_ref, rows, tok0=c * rows)
        p = jnp.where(is_prompt, pp_ref[sl, :], ps_ref[sl, :])
        ple = _dot(p.astype(BF16), wp_ref[...]) * jax.nn.sigmoid(_dot(h.astype(BF16), wg_ref[...]))
        base_ref[sl, :] = DEEPNORM_ALPHA * h + ple

    for _ in range(TOP_K):
        pltpu.make_async_copy(h_ref, xs_ref.at[_token_rows(0, T_DISPATCH), :], sem).wait()


def _expert_body(blk_e_ref, first_ref, slot_ref, next_ref, rows_ref, n_used_ref, x_ref, bu_ref, bd_ref, wu_hbm, wd_hbm,
                 y_ref, wu_f32, wd_f32, wu_bf, wd_bf, sems):
    i = pl.program_id(0)

    def weight_copies(e, s):
        return (pltpu.make_async_copy(wu_hbm.at[e], wu_f32.at[s], sems.at[s]),
                pltpu.make_async_copy(wd_hbm.at[e], wd_f32.at[s], sems.at[s]))

    @pl.when(i < n_used_ref[0])
    def _():
        s = slot_ref[i]

        @pl.when(first_ref[i] == 1)
        def _():
            @pl.when(i == 0)
            def _():
                for cp in weight_copies(blk_e_ref[i], s):
                    cp.start()

            for cp in weight_copies(blk_e_ref[i], s):
                cp.wait()
            nxt = next_ref[i]

            @pl.when(nxt >= 0)
            def _():
                for cp in weight_copies(nxt, 1 - s):
                    cp.start()

            chunk = D_MODEL // SUBLANES

            def cast_rows(c, carry):
                r = pl.multiple_of(c * chunk, chunk)
                wu_bf[pl.ds(r, chunk), :] = wu_f32[s, pl.ds(r, chunk), :].astype(BF16)
                wd_bf[pl.ds(r, chunk), :] = wd_f32[s, pl.ds(r, chunk), :].astype(BF16)
                return carry

            lax.fori_loop(0, SUBLANES, cast_rows, 0)

        def chain(c):
            x = _load_token_tiles(x_ref, TM_CHAIN, tok0=c * TM_CHAIN)
            hcat = _dot(x.astype(BF16), wu_bf[...]) + bu_ref[...]
            h_glu = jnp.minimum(hcat[:, :D_FF], SWIGLU_LIMIT)
            h_lin = jnp.clip(hcat[:, D_FF:], -SWIGLU_LIMIT, SWIGLU_LIMIT)
            act = h_glu * jax.nn.sigmoid(SWIGLU_ALPHA * h_glu) * (h_lin + 1.0)
            _store_token_tiles(y_ref, _dot(act.astype(BF16), wd_bf[...]) + bd_ref[...], tok0=c * TM_CHAIN)

        n_chains = TM // TM_CHAIN
        live = (rows_ref[i] + TM_CHAIN - 1) // TM_CHAIN
        for m in range(1, n_chains + 1):
            @pl.when(live == m)
            def _():
                for c in range(m):
                    chain(c)
                if m < n_chains:
                    y_ref[m * TM_CHAIN * SUBLANES:, :] = jnp.zeros(((n_chains - m) * TM_CHAIN * SUBLANES, LANES), F32)

    @pl.when(pl.program_id(0) >= n_used_ref[0])
    def _():
        y_ref[...] = jnp.zeros(y_ref.shape, F32)


def _combine_body(dest_ref, dest_next_ref, y_ref, base_ref, gate_ref, g_ref, b_ref, outp_ref, outs_ref, ybuf, sems, *,
                  n_prompt_tiles, n_tiles):
    i = pl.program_id(0)

    def issue_tile(d_ref, tile, slot):
        first = lax.rem(tile, T_DISPATCH // T_ROWS) * (T_ROWS * TOP_K)

        def issue(r, carry):
            for k in range(TOP_K):
                d = d_ref[0, first + r * TOP_K + k]
                pltpu.make_async_copy(y_ref.at[_token_rows(d), :], ybuf.at[slot * TOP_K + k, _token_rows(r), :],
                                      sems.at[slot]).start(priority=k % 2)
            return carry

        lax.fori_loop(0, T_ROWS, issue, 0, unroll=8)

    ahead = COMBINE_RING - 1
    slot = lax.rem(i, COMBINE_RING)

    @pl.when(i == 0)
    def _():
        for t in range(ahead):
            issue_tile(dest_ref, t, t)

    @pl.when(i + ahead < n_tiles)
    def _():
        issue_tile(dest_next_ref, i + ahead, lax.rem(i + ahead, COMBINE_RING))

    for k in range(TOP_K):
        pltpu.make_async_copy(y_ref.at[_token_rows(0, T_ROWS), :], ybuf.at[slot * TOP_K + k], sems.at[slot]).wait()

    gate = gate_ref[...]
    acc = base_ref[...]
    for k in range(TOP_K):
        acc = acc + gate[:, k:k + 1] * _load_token_tiles(ybuf, T_ROWS, lead=(slot * TOP_K + k,))
    out = _layer_norm(acc, g_ref[...], b_ref[...])

    @pl.when(i < n_prompt_tiles)
    def _():
        outp_ref[...] = out

    @pl.when(i >= n_prompt_tiles)
    def _():
        outs_ref[...] = out


def _full(shape):
    return pl.BlockSpec(shape, lambda *_: (0,) * len(shape), pipeline_mode=pl.Buffered(1))


def kernel(x_prompt, x_sample, cache_conv, p_prompt, p_sample, ln_in_g, ln_in_b, w_in, conv_w, gn_g, gn_b, vn_g, vn_b,
           w_spatial, b_spatial, w_out, ln1_g, ln1_b, w_router, b_router, w_up, b_up, w_down, b_down, w_ple,
           w_ple_gate, ln2_g, ln2_b):
    batch, seq, _ = x_prompt.shape
    dec_batch, dec_seq, _ = x_sample.shape
    assert w_in.shape[0] == DEPTH and dec_seq == SUBLANES and seq % T_PROMPT == 0
    n_prompt = batch * seq
    n_sample = dec_batch * dec_seq
    n_tok = n_prompt + n_sample
    t_s = SEQS_PER_TILE * dec_seq
    assert n_prompt % T_ROWS == 0 and n_sample % T_ROWS == 0 and n_tok % T_RANK == 0 and n_sample % t_s == 0
    assert n_tok % T_DISPATCH == 0

    row = lambda a: a.reshape(1, -1).astype(F32)
    gidx = jnp.arange(MXU_DIM) // (C_CONV // N_CONV_GROUPS)
    gmat = jnp.where(gidx[:, None] == gidx[None, :], 1.0 / (C_CONV // N_CONV_GROUPS), 0.0).astype(BF16)
    wr_pad = jnp.pad(w_router[0].astype(F32), ((0, 0), (0, LANES - N_EXPERTS)))
    wr_hi = wr_pad.astype(BF16)
    wr_lo = (wr_pad - wr_hi.astype(F32)).astype(BF16)
    weights = dict(
        ln_in_g=row(ln_in_g), ln_in_b=row(ln_in_b), w_in=w_in[0].astype(BF16), vn_g=row(vn_g[0]), vn_b=row(vn_b[0]),
        gmat=gmat, gn_g=row(gn_g[0]), gn_b=row(gn_b[0]), w_out=w_out[0].astype(BF16), ln1_g=row(ln1_g[0]),
        ln1_b=row(ln1_b[0]), wr_hi=wr_hi, wr_pair=jnp.concatenate([wr_hi, wr_lo], axis=1),
        b_r=jnp.pad(row(b_router[0]), ((0, 0), (0, LANES - N_EXPERTS))))
    w_list = [weights[n] for n in _WEIGHT_NAMES]
    w_specs = [_full(a.shape) for a in w_list]
    causal = jnp.tril(jnp.ones((CHUNK, CHUNK), bool))
    ws_m = jnp.where(causal[None], w_spatial[0], 0.0)
    ws_cat = jnp.concatenate([ws_m[0::2], ws_m[1::2]], axis=2).astype(BF16)
    bs_full = jnp.repeat(b_spatial[0].T.astype(F32), HEAD_DIM, axis=1)
    s_i = jnp.arange(SUBLANES)[:, None]
    t_i = jnp.arange(SUBLANES)[None, :]
    tap = jnp.clip(CONV_W - 1 - t_i + s_i, 0, CONV_W - 1)
    convu = jnp.where((s_i <= t_i)[:, :, None], conv_w[0].astype(F32)[tap], 0.0)
    gw8 = jnp.transpose(ws_m[:, :SUBLANES, :SUBLANES], (2, 1, 0))
    gw8 = jnp.repeat(gw8.astype(F32), HEAD_DIM, axis=2)
    b8 = bs_full[:SUBLANES]
    n_slabs = C_CONV // LANES
    slab_rows = lambda a: jnp.broadcast_to(a.reshape(a.shape[:-1] + (n_slabs, 1, LANES)),
                                           a.shape[:-1] + (n_slabs, SUBLANES, LANES))
    convub, gwb, b8b = slab_rows(convu), slab_rows(gw8), slab_rows(b8)
    cslab = cache_conv.astype(F32)

    cparams = lambda sem: pltpu.CompilerParams(dimension_semantics=sem, vmem_limit_bytes=VMEM_LIMIT)

    nj = seq // T_PROMPT
    npt = n_prompt // T_PROMPT
    assert t_s == T_PROMPT
    pstep = lambda i: jnp.minimum(i, npt - 1)
    sstep = lambda i: jnp.maximum(i - npt, 0)
    tok_block = lambda width: pl.BlockSpec((T_PROMPT, width), lambda i: (i, 0))
    cwb = slab_rows(conv_w[0].astype(F32))
    tables = [cwb, ws_cat, bs_full, convub, gwb, b8b]
    h_all, idx_all, gate_all, u_tail, v_chunk, u_s, v_s = pl.pallas_call(
        functools.partial(_mixer_body, n_prompt_tiles=npt, tiles_per_seq=nj),
        grid=(npt + n_sample // t_s,),
        in_specs=[pl.BlockSpec((None, T_PROMPT, D_MODEL), lambda i: (pstep(i) // nj, pstep(i) % nj, 0)),
                  pl.BlockSpec((t_s, D_MODEL), lambda i: (sstep(i), 0)),
                  pl.BlockSpec((None, SEQS_PER_TILE, CONV_W - 1, C_CONV), lambda i: (0, sstep(i), 0, 0),
                               pipeline_mode=pl.Buffered(1))]
                 + w_specs + [_full(a.shape) for a in tables],
        out_specs=[pl.BlockSpec((T_PROMPT * SUBLANES, LANES), lambda i: (i, 0)),
                   tok_block(LANES), tok_block(LANES),
                   pl.BlockSpec((None, HIST, C_CONV), lambda i: (pstep(i) // nj, 0, 0)),
                   pl.BlockSpec((None, CHUNK, C_GMLP), lambda i: (pstep(i) // nj, 0, 0)),
                   pl.BlockSpec((t_s, C_CONV), lambda i: (sstep(i), 0)),
                   pl.BlockSpec((t_s, C_GMLP), lambda i: (sstep(i), 0))],
        out_shape=(jax.ShapeDtypeStruct((n_tok * SUBLANES, LANES), F32),
                   jax.ShapeDtypeStruct((n_tok, LANES), jnp.int32), jax.ShapeDtypeStruct((n_tok, LANES), F32),
                   jax.ShapeDtypeStruct((batch, HIST, C_CONV), F32), jax.ShapeDtypeStruct((batch, CHUNK, C_GMLP), F32),
                   jax.ShapeDtypeStruct((n_sample, C_CONV), F32), jax.ShapeDtypeStruct((n_sample, C_GMLP), F32)),
        scratch_shapes=[pltpu.VMEM((n_slabs, T_PROMPT + HIST, LANES), F32)]
                       + [pltpu.VMEM((n_slabs, T_PROMPT, LANES), F32)] * 4,
        compiler_params=pltpu.CompilerParams(dimension_semantics=("arbitrary",), vmem_limit_bytes=BIG_VMEM_LIMIT),
        name="mixer",
    )(x_prompt, x_sample.reshape(n_sample, D_MODEL), cslab, *w_list, *tables)

    tri = (jnp.arange(T_RANK)[:, None] > jnp.arange(T_RANK)[None, :]).astype(BF16)
    before_all, counts = pl.pallas_call(
        _rank_body,
        grid=(n_tok // T_RANK,),
        in_specs=[pl.BlockSpec((T_RANK, LANES), lambda i: (i, 0)), _full(tri.shape)],
        out_specs=[pl.BlockSpec((T_RANK, LANES), lambda i: (i, 0)), _full((1, LANES))],
        out_shape=(jax.ShapeDtypeStruct((n_tok, LANES), jnp.int32), jax.ShapeDtypeStruct((1, LANES), jnp.int32)),
        scratch_shapes=[pltpu.VMEM((1, LANES), F32)],
        compiler_params=cparams(("arbitrary",)),
        name="rank",
    )(idx_all, tri)

    n_assign = n_tok * TOP_K
    n_blocks = n_assign // TM + N_EXPERTS
    cap = n_blocks * TM
    cnt = counts[0, :N_EXPERTS]
    padded = (cnt + TM - 1) // TM * TM
    p_end = jnp.cumsum(padded)
    p_start = p_end - padded
    e_ids = jnp.arange(N_EXPERTS, dtype=jnp.int32)
    lookup = lambda table, ids: jnp.sum(jnp.where(ids[..., None] == e_ids, table, 0), axis=-1)
    slot_table = p_start[None, :] + before_all[:, :N_EXPERTS]
    dest = jnp.sum(jnp.where(idx_all[:, :TOP_K, None] == e_ids, slot_table[:, None, :], 0), axis=-1).astype(jnp.int32)
    dest_tiles = dest.reshape(n_tok // T_DISPATCH, 1, T_DISPATCH * TOP_K)
    blk_row = jnp.arange(n_blocks, dtype=jnp.int32) * TM
    blk_e = jnp.minimum(jnp.sum(p_end[None, :] <= blk_row[:, None], axis=1), N_EXPERTS - 1).astype(jnp.int32)
    n_used = (p_end[-1:] // TM).astype(jnp.int32)
    pad_off = (p_start + cnt).astype(jnp.int32)
    pad_n = (padded - cnt).astype(jnp.int32)
    used = cnt > 0
    slot_e = (jnp.cumsum(used.astype(jnp.int32)) - 1) & 1
    later_used = jnp.where(used[None, :] & (e_ids[None, :] > e_ids[:, None]), e_ids[None, :], N_EXPERTS)
    next_e = jnp.min(later_used, axis=1)
    next_e = jnp.where(next_e < N_EXPERTS, next_e, -1).astype(jnp.int32)
    blk_first = ((blk_row == lookup(p_start, blk_e)) & (blk_row < p_end[-1])).astype(jnp.int32)
    blk_rows = jnp.clip(lookup(p_start + cnt, blk_e) - blk_row, 0, TM).astype(jnp.int32)
    blk_slot = lookup(slot_e, blk_e).astype(jnp.int32)
    blk_next = lookup(next_e, blk_e).astype(jnp.int32)


    assert seq % T_DISPATCH == 0 and n_sample % T_DISPATCH == 0
    npd = n_prompt // T_DISPATCH
    per_seq = seq // T_DISPATCH
    pd = lambda i: jnp.minimum(i, npd - 1)
    w_gate = w_ple_gate[0].astype(BF16)
    w_ple_b = w_ple[0].astype(BF16)
    x_sorted, base_all = pl.pallas_call(
        functools.partial(_dispatch_body, n_blocks=n_blocks, n_prompt_tiles=npd),
        grid_spec=pltpu.PrefetchScalarGridSpec(
            num_scalar_prefetch=3,
            grid=(n_tok // T_DISPATCH,),
            in_specs=[pl.BlockSpec((None, 1, T_DISPATCH * TOP_K), lambda i, *_: (i, 0, 0), memory_space=pltpu.SMEM),
                      pl.BlockSpec((T_DISPATCH * SUBLANES, LANES), lambda i, *_: (i, 0)),
                      pl.BlockSpec((None, T_DISPATCH, PLE_DIM), lambda i, *_: (pd(i) // per_seq, pd(i) % per_seq, 0)),
                      pl.BlockSpec((T_DISPATCH, PLE_DIM), lambda i, *_: (jnp.maximum(i - npd, 0), 0)),
                      _full(w_gate.shape), _full(w_ple_b.shape)],
            out_specs=[pl.BlockSpec(memory_space=pl.ANY),
                       pl.BlockSpec((T_DISPATCH, D_MODEL), lambda i, *_: (i, 0))],
            scratch_shapes=[pltpu.VMEM((TM // 2 * SUBLANES, LANES), F32), pltpu.SemaphoreType.DMA,
                            pltpu.SemaphoreType.DMA]),
        out_shape=(jax.ShapeDtypeStruct((cap * SUBLANES, LANES), F32), jax.ShapeDtypeStruct((n_tok, D_MODEL), F32)),
        compiler_params=cparams(("arbitrary",)),
        name="dispatch",
    )(pad_off, pad_n, n_used, dest_tiles, h_all, p_prompt[0], p_sample[0].reshape(n_sample, PLE_DIM), w_gate, w_ple_b)

    last = lambda i, nu: jnp.minimum(i, nu[0] - 1)
    y_sorted = pl.pallas_call(
        _expert_body,
        grid_spec=pltpu.PrefetchScalarGridSpec(
            num_scalar_prefetch=6,
            grid=(n_blocks,),
            in_specs=[pl.BlockSpec((TM * SUBLANES, LANES), lambda i, be, bf, bs, bn, br, nu: (last(i, nu), 0)),
                      pl.BlockSpec((None, 1, 2 * D_FF), lambda i, be, bf, bs, bn, br, nu: (be[last(i, nu)], 0, 0)),
                      pl.BlockSpec((None, 1, D_MODEL), lambda i, be, bf, bs, bn, br, nu: (be[last(i, nu)], 0, 0)),
                      pl.BlockSpec(memory_space=pl.ANY), pl.BlockSpec(memory_space=pl.ANY)],
            out_specs=pl.BlockSpec((TM * SUBLANES, LANES), lambda i, be, bf, bs, bn, br, nu: (i, 0)),
            scratch_shapes=[pltpu.VMEM((2, D_MODEL, 2 * D_FF), F32), pltpu.VMEM((2, D_FF, D_MODEL), F32),
                            pltpu.VMEM((D_MODEL, 2 * D_FF), BF16), pltpu.VMEM((D_FF, D_MODEL), BF16),
                            pltpu.SemaphoreType.DMA((2,))]),
        out_shape=jax.ShapeDtypeStruct((cap * SUBLANES, LANES), F32),
        compiler_params=pltpu.CompilerParams(dimension_semantics=("arbitrary",), vmem_limit_bytes=BIG_VMEM_LIMIT),
        name="experts",
    )(blk_e, blk_first, blk_slot, blk_next, blk_rows, n_used, x_sorted, b_up[0].astype(F32)[:, None, :],
      b_down[0].astype(F32)[:, None, :], w_up[0].astype(F32), w_down[0].astype(F32))

    npt = n_prompt // T_ROWS
    n_tiles = n_tok // T_ROWS
    per = T_DISPATCH // T_ROWS
    dest_spec = pl.BlockSpec((None, 1, T_DISPATCH * TOP_K), lambda i: (i // per, 0, 0), memory_space=pltpu.SMEM)
    dest_next_spec = pl.BlockSpec((None, 1, T_DISPATCH * TOP_K),
                                  lambda i: (jnp.minimum(i + COMBINE_RING - 1, n_tiles - 1) // per, 0, 0),
                                  memory_space=pltpu.SMEM)
    assert per >= COMBINE_RING - 1
    out_p, out_s = pl.pallas_call(
        functools.partial(_combine_body, n_prompt_tiles=npt, n_tiles=n_tiles),
        grid=(n_tiles,),
        in_specs=[dest_spec, dest_next_spec, pl.BlockSpec(memory_space=pl.ANY),
                  pl.BlockSpec((T_ROWS, D_MODEL), lambda i: (i, 0)), pl.BlockSpec((T_ROWS, LANES), lambda i: (i, 0)),
                  _full((1, D_MODEL)), _full((1, D_MODEL))],
        out_specs=[pl.BlockSpec((T_ROWS, D_MODEL), lambda i: (jnp.minimum(i, npt - 1), 0)),
                   pl.BlockSpec((T_ROWS, D_MODEL), lambda i: (jnp.maximum(i - npt, 0), 0))],
        out_shape=(jax.ShapeDtypeStruct((n_prompt, D_MODEL), F32), jax.ShapeDtypeStruct((n_sample, D_MODEL), F32)),
        scratch_shapes=[pltpu.VMEM((COMBINE_RING * TOP_K, T_ROWS * SUBLANES, LANES), F32),
                        pltpu.SemaphoreType.DMA((COMBINE_RING,))],
        compiler_params=cparams(("arbitrary",)),
        name="combine",
    )(dest_tiles, dest_tiles, y_sorted, base_all, gate_all, row(ln2_g[0]), row(ln2_b[0]))

    y_prompt = out_p.reshape(batch, seq, D_MODEL)
    y_sample = out_s.reshape(dec_batch, dec_seq, D_MODEL)
    conv_state_prompt = u_tail[None, :, HIST - (CONV_W - 1):, :]
    u_s4 = u_s.reshape(1, dec_batch, dec_seq, C_CONV)
    conv_state_sample = jnp.concatenate([cache_conv[:, :, dec_seq:, :].astype(F32), u_s4], axis=2)
    chunk_v_prompt = v_chunk[None]
    chunk_v_sample = v_s.reshape(1, dec_batch, dec_seq, C_GMLP)
    return (y_prompt, y_sample, conv_state_prompt, conv_state_sample, chunk_v_prompt, chunk_v_sample)
```
